```python
import jax
import jax.numpy as jnp
from jax import lax
import numpy as np

D_MODEL = 1024
BATCH = 4
SEQ = 8192
DEPTH = 4

GRID_W = 64
CTX_LEN = 256
HEAD_DIM = 64
N_NA_HEADS = 8
N_HG_HEADS = 8
NA_WIDTH = N_NA_HEADS * HEAD_DIM
HG_WIDTH = N_HG_HEADS * HEAD_DIM
N_EVEN_PARTS = 8
EVEN_IN_WIDTH = 3 * NA_WIDTH + 5 * HG_WIDTH
MIX_WIDTH = NA_WIDTH + HG_WIDTH
NA_WIN_ROWS = 8
NA_WIN_COLS = 16
HG_CHUNK = 64
FORGET_FLOOR = 1e-30
LRU_WIDTH = D_MODEL
LRU_BLOCKS = 16
LRU_BLOCK_DIM = LRU_WIDTH // LRU_BLOCKS
LRU_C = 8.0
CONV_W = 4
CONV_PAD_LEFT = 2
N_EXPERTS = 64
TOP_K = 6
EXPERT_HIDDEN = 256
SHARED_HIDDEN = 256
ROUTED_SCALE = 2.5
MOE_BLOCK = 128
N_EVEN = (DEPTH + 1) // 2
N_ODD = DEPTH // 2
DEEPNORM_ALPHA = (2 * DEPTH) ** 0.25
DEEPNORM_BETA = (8 * DEPTH) ** -0.25
LN_EPS = 1e-5
RMS_EPS = 1e-6

kernel_name = 'hybrid_na_hgrn2_rglru_moe_diffusion_trunk'


def layer_norm(x, g, b):
    x32 = x.astype(jnp.float32)
    mu = jnp.mean(x32, axis=-1, keepdims=True)
    var = jnp.mean(jnp.square(x32 - mu), axis=-1, keepdims=True)
    y = (x32 - mu) * lax.rsqrt(var + LN_EPS)
    return (y * g.astype(jnp.float32) + b.astype(jnp.float32)).astype(x.dtype)


def to_heads(t):
    b, n, _ = t.shape
    return t.reshape(b, n, -1, HEAD_DIM).transpose(0, 2, 1, 3)


def from_heads(t):
    b, h, n, d = t.shape
    return t.transpose(0, 2, 1, 3).reshape(b, n, h * d)


def neighbourhood_attention(q, k, v, k_ctx, v_ctx, rpb):
    b, h, n, dh = q.shape
    rows = n // GRID_W
    kr = min(NA_WIN_ROWS, rows)
    kc = NA_WIN_COLS
    q = q.reshape(b, h, rows, GRID_W, dh)
    k = k.reshape(b, h, rows, GRID_W, dh)
    v = v.reshape(b, h, rows, GRID_W, dh)
    col = jnp.arange(GRID_W)
    col_idx = jnp.clip(col - kc // 2, 0, GRID_W - kc)[:, None] + jnp.arange(kc)[None, :]
    col_bias_idx = col_idx - col[:, None] + (NA_WIN_COLS - 1)

    def one_row(r):
        r0 = jnp.clip(r - kr // 2, 0, rows - kr)
        k_band = lax.dynamic_slice_in_dim(k, r0, kr, axis=2)
        v_band = lax.dynamic_slice_in_dim(v, r0, kr, axis=2)
        k_win = k_band[:, :, :, col_idx, :]
        v_win = v_band[:, :, :, col_idx, :]
        q_r = lax.dynamic_index_in_dim(q, r, axis=2, keepdims=False)
        row_bias_idx = r0 + jnp.arange(kr) - r + (NA_WIN_ROWS - 1)
        bias = rpb[:, row_bias_idx[None, :, None], col_bias_idx[:, None, :]]
        s_loc = jnp.einsum('bhwd,bhrwkd->bhwrk', q_r, k_win) + bias[None]
        s_loc = s_loc.reshape(b, h, GRID_W, kr * kc)
        s_ctx = jnp.einsum('bhwd,bhcd->bhwc', q_r, k_ctx)
        p = jax.nn.softmax(jnp.concatenate([s_loc, s_ctx], axis=-1).astype(jnp.float32), axis=-1).astype(v.dtype)
        p_loc = p[..., :kr * kc].reshape(b, h, GRID_W, kr, kc)
        p_ctx = p[..., kr * kc:]
        return (jnp.einsum('bhwrk,bhrwkd->bhwd', p_loc, v_win)
                + jnp.einsum('bhwc,bhcd->bhwd', p_ctx, v_ctx))

    o = lax.map(one_row, jnp.arange(rows))
    return o.transpose(1, 2, 0, 3, 4).reshape(b, h, n, dh)


def context_attention(q, k, v):
    s = jnp.einsum('bhqd,bhkd->bhqk', q, k).astype(jnp.float32)
    p = jax.nn.softmax(s, axis=-1).astype(v.dtype)
    return jnp.einsum('bhqk,bhkd->bhqd', p, v)


def log_forget(z, lb):
    f = lb.astype(jnp.float32) + (1.0 - lb.astype(jnp.float32)) * jax.nn.sigmoid(z.astype(jnp.float32))
    return jnp.log(jnp.maximum(f, FORGET_FLOOR))


def hgrn2_chunk_scan(q, k, v, g, s0):
    b, h, n, dk = q.shape
    dv = v.shape[-1]
    nc = n // HG_CHUNK

    def chunks(t):
        return jnp.moveaxis(t.reshape(b, h, nc, HG_CHUNK, t.shape[-1]), 2, 0)

    causal = jnp.tril(jnp.ones((HG_CHUNK, HG_CHUNK), dtype=bool))[:, :, None]

    def step(state, blk):
        qb, kb, vb, gb = blk
        qb = qb.astype(jnp.float32)
        kb = kb.astype(jnp.float32)
        vb = vb.astype(jnp.float32)
        cum = jnp.cumsum(gb.astype(jnp.float32), axis=2)
        rel = cum[:, :, :, None, :] - cum[:, :, None, :, :]
        decay = jnp.where(causal, jnp.exp(jnp.minimum(rel, 0.0)), 0.0)
        scores = jnp.einsum('bhtk,bhsk,bhtsk->bhts', qb, kb, decay)
        out = (jnp.einsum('bhts,bhsv->bhtv', scores, vb)
               + jnp.einsum('bhtk,bhkv->bhtv', qb * jnp.exp(cum), state))
        tail = cum[:, :, -1:, :]
        new_state = (jnp.exp(tail[:, :, 0, :])[..., None] * state
                     + jnp.einsum('bhsk,bhsv->bhkv', kb * jnp.exp(tail - cum), vb))
        return new_state, out

    s_final, o = lax.scan(step, s0, (chunks(q), chunks(k), chunks(v), chunks(g)))
    o = jnp.moveaxis(o, 0, 2).reshape(b, h, n, dv)
    return o.astype(v.dtype), s_final


def hgrn2_direction(q, k, v, g, s0, reverse):
    if reverse:
        q, k, v, g = [jnp.flip(t, axis=2) for t in (q, k, v, g)]
    o, s = hgrn2_chunk_scan(q, k, v, g, s0)
    if reverse:
        o = jnp.flip(o, axis=2)
    return o, s


def hgrn2_readout(o, gate, norm_g):
    o32 = o.astype(jnp.float32)
    o32 = o32 * lax.rsqrt(jnp.mean(jnp.square(o32), axis=-1, keepdims=True) + RMS_EPS) * norm_g.astype(jnp.float32)
    return from_heads((o32 * jax.nn.silu(gate.astype(jnp.float32))).astype(gate.dtype))


def even_mixer(hx, hc, w_in, w_out, rpb, lb, norm_g, need_ctx):
    bsz = hx.shape[0]
    scale = HEAD_DIM ** -0.5
    qx, kx, vx, hq_x, ff_x, fb_x, hi_x, hg_x = [to_heads(t) for t in jnp.split(hx @ w_in, N_EVEN_PARTS, axis=-1)]
    qc, kc, vc, hq_c, ff_c, fb_c, hi_c, hg_c = [to_heads(t) for t in jnp.split(hc @ w_in, N_EVEN_PARTS, axis=-1)]
    na_x = neighbourhood_attention(qx * scale, kx, vx, kc, vc, rpb)
    hqx = jax.nn.silu(hq_x) * scale
    hqc = jax.nn.silu(hq_c) * scale
    o_x = 0.0
    o_c = 0.0
    for d, (fz_x, fz_c) in enumerate(((ff_x, ff_c), (fb_x, fb_c))):
        lb_d = lb[d].reshape(N_HG_HEADS, 1, HEAD_DIM)
        g_c = log_forget(fz_c, lb_d)
        g_x = log_forget(fz_x, lb_d)
        s0 = jnp.zeros((bsz, N_HG_HEADS, HEAD_DIM, HEAD_DIM), jnp.float32)
        oc_d, s_ctx = hgrn2_direction(hqc, -jnp.expm1(g_c), hi_c, g_c, s0, d == 1)
        ox_d, _ = hgrn2_direction(hqx, -jnp.expm1(g_x), hi_x, g_x, s_ctx, d == 1)
        o_x = o_x + ox_d
        o_c = o_c + oc_d
    out_x = jnp.concatenate([from_heads(na_x), hgrn2_readout(o_x, hg_x, norm_g)], axis=-1) @ w_out
    if not need_ctx:
        return out_x, None
    na_c = context_attention(qc * scale, kc, vc)
    out_c = jnp.concatenate([from_heads(na_c), hgrn2_readout(o_c, hg_c, norm_g)], axis=-1) @ w_out
    return out_x, out_c


def depthwise_conv(u, w, b):
    y = lax.conv_general_dilated(
        u, w[:, None, :].astype(u.dtype), window_strides=(1,),
        padding=[(CONV_PAD_LEFT, CONV_W - 1 - CONV_PAD_LEFT)],
        dimension_numbers=('NWC', 'WIO', 'NWC'), feature_group_count=u.shape[-1])
    return y + b


def linear_combine(e1, e2):
    a1, b1 = e1
    a2, b2 = e2
    return a1 * a2, a2 * b1 + b2


def rglru_scan(u, w_a, b_a, w_x, b_x, lam, h0, reverse):
    bsz, n, w = u.shape
    ub = u.reshape(bsz, n, LRU_BLOCKS, LRU_BLOCK_DIM)
    r = jax.nn.sigmoid((jnp.einsum('bnkd,kde->bnke', ub, w_a).reshape(bsz, n, w) + b_a).astype(jnp.float32))
    i = jax.nn.sigmoid((jnp.einsum('bnkd,kde->bnke', ub, w_x).reshape(bsz, n, w) + b_x).astype(jnp.float32))
    log_a = -LRU_C * r * jax.nn.softplus(-lam.astype(jnp.float32))
    a = jnp.exp(log_a)
    x_in = jnp.sqrt(-jnp.expm1(2.0 * log_a)) * i * u.astype(jnp.float32)
    first = n - 1 if reverse else 0
    last = 0 if reverse else n - 1
    x_in = x_in.at[:, first].add(a[:, first] * h0)
    _, hs = lax.associative_scan(linear_combine, (a, x_in), axis=1, reverse=reverse)
    return hs, hs[:, last]


def rglru_mixer(hx, hc, w_in, conv_w, conv_b, w_a, b_a, w_x, b_x, lam, w_out, need_ctx):
    y_x, u_x = jnp.split(hx @ w_in, 2, axis=-1)
    y_c, u_c = jnp.split(hc @ w_in, 2, axis=-1)
    u_x = depthwise_conv(u_x, conv_w, conv_b)
    u_c = depthwise_conv(u_c, conv_w, conv_b)
    h0 = jnp.zeros((hc.shape[0], LRU_WIDTH), jnp.float32)
    hs_x = 0.0
    hs_c = 0.0
    for d in range(2):
        h_c, h_c_final = rglru_scan(u_c, w_a[d], b_a[d], w_x[d], b_x[d], lam[d], h0, d == 1)
        h_x, _ = rglru_scan(u_x, w_a[d], b_a[d], w_x[d], b_x[d], lam[d], h_c_final, d == 1)
        hs_x = hs_x + h_x
        hs_c = hs_c + h_c
    out_x = (hs_x * jax.nn.gelu(y_x.astype(jnp.float32))).astype(hx.dtype) @ w_out
    if not need_ctx:
        return out_x, None
    out_c = (hs_c * jax.nn.gelu(y_c.astype(jnp.float32))).astype(hc.dtype) @ w_out
    return out_x, out_c


def moe(h, router_w, router_bias, w_gate, w_up, w_down, s_gate, s_up, s_down):
    t_count, d = h.shape
    scores = jax.nn.sigmoid((h @ router_w).astype(jnp.float32))
    _, idx = lax.top_k(scores + router_bias.astype(jnp.float32), TOP_K)
    sel = jnp.take_along_axis(scores, idx, axis=1)
    wts = sel / jnp.sum(sel, axis=-1, keepdims=True) * ROUTED_SCALE
    tk = t_count * TOP_K
    flat_e = idx.reshape(-1)
    flat_t = jnp.repeat(jnp.arange(t_count, dtype=jnp.int32), TOP_K)
    flat_w = wts.reshape(-1)
    order = jnp.argsort(flat_e)
    se = flat_e[order]
    counts = jnp.bincount(flat_e, length=N_EXPERTS)
    padded = (counts + MOE_BLOCK - 1) // MOE_BLOCK * MOE_BLOCK
    starts = jnp.cumsum(counts) - counts
    pends = jnp.cumsum(padded)
    pstarts = pends - padded
    dest = pstarts[se] + jnp.arange(tk) - starts[se]
    n_blocks = -(-tk // MOE_BLOCK) + N_EXPERTS
    n_slots = n_blocks * MOE_BLOCK
    slot_tok = jnp.full((n_slots,), t_count, jnp.int32).at[dest].set(flat_t[order])
    slot_w = jnp.zeros((n_slots,), jnp.float32).at[dest].set(flat_w[order])
    block_exp = jnp.clip(jnp.searchsorted(pends, jnp.arange(n_blocks) * MOE_BLOCK, side='right'), 0, N_EXPERTS - 1)
    h_pad = jnp.concatenate([h, jnp.zeros((1, d), h.dtype)], axis=0)

    def run_block(args):
        toks, e = args
        xb = h_pad[toks]
        return (jax.nn.silu(xb @ w_gate[e]) * (xb @ w_up[e])) @ w_down[e]

    out = lax.map(run_block, (slot_tok.reshape(n_blocks, MOE_BLOCK), block_exp))
    out = out.reshape(n_slots, d) * slot_w[:, None]
    routed = jax.ops.segment_sum(out, slot_tok, num_segments=t_count + 1)[:t_count]
    shared = (jax.nn.silu(h @ s_gate) * (h @ s_up)) @ s_down
    return (routed + shared).astype(h.dtype)


def setup_inputs(seed: int = 0) -> dict:
    key = jax.random.key(seed)
    ks = jax.random.split(key, 32)
    f32 = jnp.float32
    D = D_MODEL

    def nrm(k, shape, std):
        return jax.random.normal(k, shape, f32) * std

    u = jax.random.uniform(ks[20], (N_ODD, 2, LRU_WIDTH), f32, minval=0.9, maxval=0.999)
    a0 = u ** (1.0 / LRU_C)
    return {
        'x': nrm(ks[0], (BATCH, SEQ, D), 1.0),
        'c': nrm(ks[1], (BATCH, D), 1.0),
        'ctx': nrm(ks[2], (BATCH, CTX_LEN, D), 1.0),
        'c_ctx': nrm(ks[3], (D,), 1.0),
        'ada_w': nrm(ks[4], (DEPTH, D, 6 * D), 0.5 * D ** -0.5),
        'ada_b': nrm(ks[5], (DEPTH, 6 * D), 0.01),
        'ln_g': 1.0 + nrm(ks[6], (DEPTH, 2, D), 0.01),
        'ln_b': nrm(ks[7], (DEPTH, 2, D), 0.01),
        'ev_w_in': nrm(ks[8], (N_EVEN, D, EVEN_IN_WIDTH), D ** -0.5),
        'ev_w_out': nrm(ks[9], (N_EVEN, MIX_WIDTH, D), DEEPNORM_BETA * MIX_WIDTH ** -0.5),
        'na_rpb': nrm(ks[10], (N_EVEN, N_NA_HEADS, 2 * NA_WIN_ROWS - 1, 2 * NA_WIN_COLS - 1), 0.1),
        'hg_lb_raw': nrm(ks[11], (2, N_EVEN, HG_WIDTH), 0.5),
        'hg_norm_g': 1.0 + nrm(ks[12], (N_EVEN, HEAD_DIM), 0.01),
        'od_w_in': nrm(ks[13], (N_ODD, D, 2 * LRU_WIDTH), D ** -0.5),
        'od_conv_w': nrm(ks[14], (N_ODD, CONV_W, LRU_WIDTH), CONV_W ** -0.5),
        'od_conv_b': nrm(ks[15], (N_ODD, LRU_WIDTH), 0.01),
        'lru_w_a': nrm(ks[16], (N_ODD, 2, LRU_BLOCKS, LRU_BLOCK_DIM, LRU_BLOCK_DIM), LRU_BLOCK_DIM ** -0.5),
        'lru_b_a': nrm(ks[17], (N_ODD, 2, LRU_WIDTH), 0.01),
        'lru_w_x': nrm(ks[18], (N_ODD, 2, LRU_BLOCKS, LRU_BLOCK_DIM, LRU_BLOCK_DIM), LRU_BLOCK_DIM ** -0.5),
        'lru_b_x': nrm(ks[19], (N_ODD, 2, LRU_WIDTH), 0.01),
        'lru_lam': jnp.log(a0) - jnp.log1p(-a0),
        'od_w_out': nrm(ks[21], (N_ODD, LRU_WIDTH, D), DEEPNORM_BETA * LRU_WIDTH ** -0.5),
        'router_w': nrm(ks[22], (DEPTH, D, N_EXPERTS), D ** -0.5),
        'router_bias': nrm(ks[23], (DEPTH, N_EXPERTS), 0.01),
        'exp_w_gate': nrm(ks[24], (DEPTH, N_EXPERTS, D, EXPERT_HIDDEN), D ** -0.5),
        'exp_w_up': nrm(ks[25], (DEPTH, N_EXPERTS, D, EXPERT_HIDDEN), D ** -0.5),
        'exp_w_down': nrm(ks[26], (DEPTH, N_EXPERTS, EXPERT_HIDDEN, D), DEEPNORM_BETA * EXPERT_HIDDEN ** -0.5),
        'sh_w_gate': nrm(ks[27], (DEPTH, D, SHARED_HIDDEN), D ** -0.5),
        'sh_w_up': nrm(ks[28], (DEPTH, D, SHARED_HIDDEN), D ** -0.5),
        'sh_w_down': nrm(ks[29], (DEPTH, SHARED_HIDDEN, D), DEEPNORM_BETA * SHARED_HIDDEN ** -0.5),
    }


def reference(x, c, ctx, c_ctx, ada_w, ada_b, ln_g, ln_b, ev_w_in, ev_w_out, na_rpb, hg_lb_raw, hg_norm_g,
              od_w_in, od_conv_w, od_conv_b, lru_w_a, lru_b_a, lru_w_x, lru_b_x, lru_lam, od_w_out,
              router_w, router_bias, exp_w_gate, exp_w_up, exp_w_down, sh_w_gate, sh_w_up, sh_w_down):
    bsz, n, d = x.shape
    n_ctx_tok = ctx.shape[0] * ctx.shape[1]
    p_lb = jax.nn.softmax(hg_lb_raw.astype(jnp.float32), axis=1)
    hg_lb = jnp.cumsum(p_lb, axis=1) - p_lb[:, :1]
    silu_c = jax.nn.silu(c)
    silu_cc = jax.nn.silu(c_ctx)
    x_lat = x
    x_ctx = ctx
    for l in range(DEPTH):
        last = l == DEPTH - 1
        j = l // 2
        mod_x = jnp.split((silu_c @ ada_w[l] + ada_b[l])[:, None, :], 6, axis=-1)
        mod_c = jnp.split(silu_cc @ ada_w[l] + ada_b[l], 6, axis=-1)
        hx = x_lat * (1.0 + mod_x[1]) + mod_x[0]
        hc = x_ctx * (1.0 + mod_c[1]) + mod_c[0]
        if l % 2 == 0:
            ox, oc = even_mixer(hx, hc, ev_w_in[j], ev_w_out[j], na_rpb[j], hg_lb[:, j], hg_norm_g[j], not last)
        else:
            ox, oc = rglru_mixer(hx, hc, od_w_in[j], od_conv_w[j], od_conv_b[j], lru_w_a[j], lru_b_a[j],
                                 lru_w_x[j], lru_b_x[j], lru_lam[j], od_w_out[j], not last)
        x_lat = layer_norm(DEEPNORM_ALPHA * x_lat + mod_x[2] * ox, ln_g[l, 0], ln_b[l, 0])
        hx = x_lat * (1.0 + mod_x[4]) + mod_x[3]
        if last:
            y = moe(hx.reshape(-1, d), router_w[l], router_bias[l], exp_w_gate[l], exp_w_up[l], exp_w_down[l],
                    sh_w_gate[l], sh_w_up[l], sh_w_down[l])
        else:
            x_ctx = layer_norm(DEEPNORM_ALPHA * x_ctx + mod_c[2] * oc, ln_g[l, 0], ln_b[l, 0])
            hc = x_ctx * (1.0 + mod_c[4]) + mod_c[3]
            y_all = moe(jnp.concatenate([hc.reshape(-1, d), hx.reshape(-1, d)], axis=0), router_w[l],
                        router_bias[l], exp_w_gate[l], exp_w_up[l], exp_w_down[l],
                        sh_w_gate[l], sh_w_up[l], sh_w_down[l])
            x_ctx = layer_norm(DEEPNORM_ALPHA * x_ctx + mod_c[5] * y_all[:n_ctx_tok].reshape(x_ctx.shape),
                               ln_g[l, 1], ln_b[l, 1])
            y = y_all[n_ctx_tok:]
        x_lat = layer_norm(DEEPNORM_ALPHA * x_lat + mod_x[5] * y.reshape(bsz, n, d), ln_g[l, 1], ln_b[l, 1])
    return x_lat
```

```python
import functools

import numpy as np
import jax
import jax.numpy as jnp
from jax import lax
from jax.experimental import pallas as pl
from jax.experimental.pallas import tpu as pltpu

F32 = jnp.float32
BF16 = jnp.bfloat16
I32 = jnp.int32

HEAD_DIM = 64
GRID_W = 64
NA_WIN_ROWS = 8
NA_WIN_COLS = 16
NA_QROWS = 8
NA_KROWS = 16
HG_CHUNK = 64
HG_HEADS_PER_GROUP = 4
FORGET_FLOOR = 1e-30
HG_SAFE_EXP = 80.0
HG_CLIP_EXP = 85.0
LRU_C = 8.0
LRU_BLOCKS = 16
CONV_W = 4
TOP_K = 6
ROUTED_SCALE = 2.5
LN_EPS = 1e-5
RMS_EPS = 1e-6
SEQ_BLOCK = 256
MOE_BLOCK = 256
MASK_VALUE = -1e30
VMEM_LIMIT = 56 * 1024 * 1024


def _cparams(sem):
    return pltpu.CompilerParams(dimension_semantics=sem, vmem_limit_bytes=VMEM_LIMIT)


def _split(a):
    hi = a.astype(BF16)
    lo = (a - hi.astype(F32)).astype(BF16)
    return hi, lo


def _dot(a, b):
    return jnp.dot(a, b, preferred_element_type=F32)


def _dot_nt(a, b):
    return lax.dot_general(a, b, (((1,), (1,)), ((), ())), preferred_element_type=F32)


def _dot_tn(a, b):
    return lax.dot_general(a, b, (((0,), (0,)), ((), ())), preferred_element_type=F32)


def _dot3(a, b):
    ah, al = _split(a)
    bh, bl = _split(b)
    return _dot(ah, bh) + _dot(al, bh) + _dot(ah, bl)


def _silu(v):
    return v * jax.nn.sigmoid(v)


def _gelu_tanh(v):
    return 0.5 * v * (1.0 + jnp.tanh(0.7978845608028654 * (v + 0.044715 * v * v * v)))


def _layer_norm(z, g, b):
    mu = jnp.mean(z, axis=-1, keepdims=True)
    zc = z - mu
    var = jnp.mean(zc * zc, axis=-1, keepdims=True)
    return zc * lax.rsqrt(var + LN_EPS) * g + b


def _pick_tile(n, cands):
    for c in cands:
        if n % c == 0:
            return c
    raise ValueError(f"no tile for {n}")


class _Dims:
    def __init__(self, B, N, C, D):
        self.B, self.N, self.C, self.D = B, N, C, D
        self.T_ctx = B * C
        self.T = B * C + B * N
        assert C == SEQ_BLOCK and N % SEQ_BLOCK == 0
        assert N % GRID_W == 0 and (N // GRID_W) % NA_QROWS == 0 and N // GRID_W >= NA_KROWS
        assert self.T_ctx % 512 == 0
        self.TM = _pick_tile(self.T_ctx, (512, 256))
        assert N % self.TM == 0

    def group(self, i, tile):
        start = i * tile
        return jnp.where(start < self.T_ctx, self.B, (start - self.T_ctx) // self.N)


def _mod_kernel(c_ref, w_ref, b_ref, o_ref):
    o_ref[...] = _dot3(_silu(c_ref[...]), w_ref[...]) + b_ref[...]


def _modulation(cvec, ada_w, ada_b):
    L, D, W6 = ada_w.shape
    nc = 1536
    return pl.pallas_call(
        _mod_kernel,
        grid=(L, W6 // nc),
        in_specs=[
            pl.BlockSpec((8, D), lambda l, j: (0, 0)),
            pl.BlockSpec((None, D, nc), lambda l, j: (l, 0, j)),
            pl.BlockSpec((None, 1, nc), lambda l, j: (l, 0, j)),
        ],
        out_specs=pl.BlockSpec((None, 8, nc), lambda l, j: (l, 0, j)),
        out_shape=jax.ShapeDtypeStruct((L, 8, W6), F32),
        compiler_params=_cparams(("arbitrary", "arbitrary")),
        name="adaln_modulation",
    )(cvec, ada_w, ada_b.reshape(L, 1, W6))


def _inproj_even_kernel(x_ref, sc_ref, sh_ref, w_ref, q_ref, k_ref, v_ref, hq_ref, ff_ref, fb_ref, hi_ref, hg_ref):
    h = (x_ref[...] * (1.0 + sc_ref[...]) + sh_ref[...]).astype(BF16)
    wd = q_ref.shape[1]

    def part(j):
        return _dot(h, w_ref[:, j * wd:(j + 1) * wd])

    scale = HEAD_DIM ** -0.5
    q_ref[...] = (part(0) * scale).astype(BF16)
    k_ref[...] = part(1).astype(BF16)
    v_ref[...] = part(2).astype(BF16)
    hq_ref[...] = _silu(part(3)) * scale
    ff_ref[...] = part(4)
    fb_ref[...] = part(5)
    hi_ref[...] = part(6)
    hg_ref[...] = part(7)


def _inproj_odd_kernel(x_ref, sc_ref, sh_ref, w_ref, y_ref, u_ref):
    h = (x_ref[...] * (1.0 + sc_ref[...]) + sh_ref[...]).astype(BF16)
    wd = y_ref.shape[1]
    y_ref[...] = _dot(h, w_ref[:, :wd])
    u_ref[...] = _dot(h, w_ref[:, wd:])


def _inproj(dm, xs, sc, sh, w, even):
    T, D, TM = dm.T, dm.D, dm.TM
    wtot = w.shape[1]
    mod_spec = pl.BlockSpec((None, 1, D), lambda i: (dm.group(i, TM), 0, 0))
    in_specs = [pl.BlockSpec((TM, D), lambda i: (i, 0)), mod_spec, mod_spec,
                pl.BlockSpec((D, wtot), lambda i: (0, 0))]
    if even:
        wd = wtot // 8
        dts = [BF16] * 3 + [F32] * 5
        kern = _inproj_even_kernel
    else:
        wd = wtot // 2
        dts = [F32] * 2
        kern = _inproj_odd_kernel
    return pl.pallas_call(
        kern,
        grid=(T // TM,),
        in_specs=in_specs,
        out_specs=[pl.BlockSpec((TM, wd), lambda i: (i, 0)) for _ in dts],
        out_shape=[jax.ShapeDtypeStruct((T, wd), dt) for dt in dts],
        compiler_params=_cparams(("parallel",)),
        name="inproj_even" if even else "inproj_odd",
    )(xs, sc, sh, w)


def _na_bias_tables(rpb, rows):
    W = GRID_W
    tabs = []
    for rbase, kb in ((0, 0), (NA_QROWS, NA_QROWS - NA_WIN_ROWS // 2), (rows - NA_QROWS, rows - NA_KROWS)):
        r = rbase + np.arange(NA_QROWS)[:, None, None, None]
        c = np.arange(W)[None, :, None, None]
        kr = kb + np.arange(NA_KROWS)[None, None, :, None]
        kc = np.arange(W)[None, None, None, :]
        r0 = np.clip(r - NA_WIN_ROWS // 2, 0, rows - NA_WIN_ROWS)
        c0 = np.clip(c - NA_WIN_COLS // 2, 0, W - NA_WIN_COLS)
        valid = (kr >= r0) & (kr < r0 + NA_WIN_ROWS) & (kc >= c0) & (kc < c0 + NA_WIN_COLS)
        dr = np.clip(kr - r + NA_WIN_ROWS - 1, 0, 2 * NA_WIN_ROWS - 2)
        dc = np.clip(kc - c + NA_WIN_COLS - 1, 0, 2 * NA_WIN_COLS - 2)
        shape = (NA_QROWS, W, NA_KROWS, W)
        valid = np.broadcast_to(valid, shape).reshape(NA_QROWS * W, NA_KROWS * W)
        dr = np.broadcast_to(dr, shape).reshape(NA_QROWS * W, NA_KROWS * W)
        dc = np.broadcast_to(dc, shape).reshape(NA_QROWS * W, NA_KROWS * W)
        tabs.append(jnp.where(valid[None], rpb[:, dr, dc], MASK_VALUE))
    return jnp.stack(tabs).astype(F32)


def _na_kernel(q_ref, k0, k1, k2, k3, v0, v1, v2, v3, kc_ref, vc_ref, bias_ref, o_ref):
    q2 = q_ref[...]
    lane = lax.broadcasted_iota(I32, (1, q2.shape[1]), 1)
    first = lane < HEAD_DIM
    ks = [r[...] for r in (k0, k1, k2, k3)]
    vs = [r[...] for r in (v0, v1, v2, v3)]
    kc = kc_ref[...]
    vc = vc_ref[...]
    kb = ks[0].shape[0]
    outs = []
    for hh in range(2):
        sel = first if hh == 0 else jnp.logical_not(first)
        qh = jnp.where(sel, q2, jnp.zeros_like(q2))
        s_loc = jnp.concatenate([_dot_nt(qh, kj) for kj in ks], axis=1) + bias_ref[hh]
        s_ctx = _dot_nt(qh, kc)
        m = jnp.maximum(jnp.max(s_loc, axis=1, keepdims=True), jnp.max(s_ctx, axis=1, keepdims=True))
        p_loc = jnp.exp(s_loc - m)
        p_ctx = jnp.exp(s_ctx - m)
        denom = jnp.sum(p_loc, axis=1, keepdims=True) + jnp.sum(p_ctx, axis=1, keepdims=True)
        o = _dot(p_ctx.astype(BF16), vc)
        for j in range(4):
            o = o + _dot(p_loc[:, j * kb:(j + 1) * kb].astype(BF16), vs[j])
        outs.append(o / denom)
    o_ref[...] = jnp.where(first, outs[0], outs[1]).astype(o_ref.dtype)


def _neighbourhood_attention(dm, q, k, v, bias):
    B, N, C, T = dm.B, dm.N, dm.C, dm.T
    H2 = q.shape[1] // (2 * HEAD_DIM)
    rows = N // GRID_W
    nqb = rows // NA_QROWS
    QB = NA_QROWS * GRID_W
    KB = QB // 2
    ngroups = N // KB
    lat_q0 = dm.T_ctx // QB
    lat_k0 = dm.T_ctx // KB

    def q_map(p, i, b):
        return (lat_q0 + b * (N // QB) + i, p)

    def o_map(p, i, b):
        return (b * (N // QB) + i, p)

    def kv_map(j):
        def f(p, i, b):
            gs = jnp.clip(2 * i - 1, 0, ngroups - 4)
            return (lat_k0 + b * ngroups + gs + j, p)
        return f

    def ctx_map(p, i, b):
        return (b, p)

    def bias_map(p, i, b):
        var = jnp.where(i == 0, 0, jnp.where(i == nqb - 1, 2, 1))
        return (var, p, 0, 0)

    lanes = 2 * HEAD_DIM
    kv_specs = [pl.BlockSpec((KB, lanes), kv_map(j)) for j in range(4)]
    return pl.pallas_call(
        _na_kernel,
        grid=(H2, nqb, B),
        in_specs=[pl.BlockSpec((QB, lanes), q_map)] + kv_specs + kv_specs
        + [pl.BlockSpec((C, lanes), ctx_map), pl.BlockSpec((C, lanes), ctx_map),
           pl.BlockSpec((None, 2, QB, NA_KROWS * GRID_W), bias_map)],
        out_specs=pl.BlockSpec((QB, lanes), o_map),
        out_shape=jax.ShapeDtypeStruct((B * N, q.shape[1]), BF16),
        compiler_params=_cparams(("arbitrary", "arbitrary", "arbitrary")),
        name="neighbourhood_attention",
    )(q, k, k, k, k, v, v, v, v, k, v, bias)


def _ctx_attn_kernel(q_ref, k_ref, v_ref, o_ref):
    q2 = q_ref[...]
    k2 = k_ref[...]
    v2 = v_ref[...]
    lane = lax.broadcasted_iota(I32, (1, q2.shape[1]), 1)
    first = lane < HEAD_DIM
    outs = []
    for hh in range(2):
        sel = first if hh == 0 else jnp.logical_not(first)
        s = _dot_nt(jnp.where(sel, q2, jnp.zeros_like(q2)), k2)
        p = jnp.exp(s - jnp.max(s, axis=1, keepdims=True))
        outs.append(_dot(p.astype(BF16), v2) / jnp.sum(p, axis=1, keepdims=True))
    o_ref[...] = jnp.where(first, outs[0], outs[1]).astype(o_ref.dtype)


def _context_attention(dm, q, k, v):
    lanes = 2 * HEAD_DIM
    spec = pl.BlockSpec((dm.C, lanes), lambda p, b: (b, p))
    return pl.pallas_call(
        _ctx_attn_kernel,
        grid=(q.shape[1] // lanes, dm.B),
        in_specs=[spec, spec, spec],
        out_specs=spec,
        out_shape=jax.ShapeDtypeStruct((dm.T_ctx, q.shape[1]), BF16),
        compiler_params=_cparams(("arbitrary", "arbitrary")),
        name="context_attention",
    )(q, k, v)


def _seq_block_maps(dm):
    nl = dm.N // SEQ_BLOCK
    base = dm.T_ctx // SEQ_BLOCK

    def fwd(b, j):
        return jnp.where(j == 0, b, base + b * nl + j - 1)

    def bwd(b, j):
        return jnp.where(j == 0, b, base + b * nl + nl - j)

    return fwd, bwd


def _hgrn_prepare(hq, z, lb, tri, last_row, mid_row):
    f = lb + (1.0 - lb) * jax.nn.sigmoid(z)
    fm = jnp.maximum(f, FORGET_FLOOR)
    g = jnp.log(fm)
    kk = 1.0 - fm
    gh, gl = _split(g)
    cum = _dot(tri, gh) + _dot(tri, gl)
    tail = cum[last_row:last_row + 1, :]
    e = cum - cum[mid_row:mid_row + 1, :]
    emax = jnp.max(jnp.max(jnp.abs(e), axis=1, keepdims=True), axis=0, keepdims=True)
    ec = jnp.clip(e, -HG_CLIP_EXP, HG_CLIP_EXP)
    qe = (hq * jnp.exp(ec)).astype(BF16)
    ke = (kk * jnp.exp(-ec)).astype(BF16)
    return kk, cum, tail, emax, qe, ke


def _hgrn_kernel(hqf_ref, zf_ref, vf_ref, hqb_ref, zb_ref, vb_ref, lbf_ref, lbb_ref,
                 of_ref, ob_ref, st_ref, a_ref, cum_ref, k_ref):
    j = pl.program_id(1)

    @pl.when(j == 0)
    def _():
        st_ref[...] = jnp.zeros_like(st_ref)

    CH = HG_CHUNK
    GW = HG_HEADS_PER_GROUP * HEAD_DIM
    ngroups = hqf_ref.shape[1] // GW
    nchunks = hqf_ref.shape[0] // CH
    t_i = lax.broadcasted_iota(I32, (CH, CH), 0)
    u_i = lax.broadcasted_iota(I32, (CH, CH), 1)
    tris = ((u_i <= t_i).astype(BF16), (u_i >= t_i).astype(BF16))
    t_w = lax.broadcasted_iota(I32, (CH, GW), 0)
    s_w = lax.broadcasted_iota(I32, (CH, GW), 1) % CH
    cmasks = (s_w <= t_w, s_w >= t_w)
    r_b = lax.broadcasted_iota(I32, (GW, GW), 0)
    c_b = lax.broadcasted_iota(I32, (GW, GW), 1)
    bm = (r_b // HEAD_DIM) == (c_b // HEAD_DIM)
    hsel_base = (r_b // HEAD_DIM) * HEAD_DIM
    dirs = ((hqf_ref, zf_ref, vf_ref, lbf_ref, of_ref), (hqb_ref, zb_ref, vb_ref, lbb_ref, ob_ref))
    chains = [(d, g) for d in range(2) for g in range(ngroups)]

    def expand(m):
        return jnp.where(bm, jnp.concatenate([m] * HG_HEADS_PER_GROUP, axis=0), jnp.zeros((GW, GW), m.dtype))

    def chunk_body(c, carry):
        rows = (pl.multiple_of(c * CH, CH), pl.multiple_of((nchunks - 1 - c) * CH, CH))
        prepared = []
        worst = jnp.zeros((1, 1), F32)
        for n, (d, g) in enumerate(chains):
            hq_r, z_r, v_r, lb_r, _ = dirs[d]
            ls = slice(g * GW, (g + 1) * GW)
            hq = hq_r[pl.ds(rows[d], CH), ls]
            z = z_r[pl.ds(rows[d], CH), ls]
            v = v_r[pl.ds(rows[d], CH), ls]
            last_row = CH - 1 if d == 0 else 0
            kk, cum, tail, emax, qe, ke = _hgrn_prepare(hq, z, lb_r[:, ls], tris[d], last_row, CH // 2)
            worst = jnp.maximum(worst, emax)
            a_ref[n] = _dot_nt(qe, expand(ke))
            cum_ref[n] = cum
            k_ref[n] = kk
            prepared.append((hq, v, kk, cum, tail))

        @pl.when(worst[0, 0] > HG_SAFE_EXP)
        def _():
            for n, (d, g) in enumerate(chains):
                hq_r = dirs[d][0]
                hq = hq_r[pl.ds(rows[d], CH), g * GW:(g + 1) * GW]
                cum = cum_ref[n]

                def key_body(s, acc):
                    cs = cum_ref[n, pl.ds(s, 1), :]
                    ksr = k_ref[n, pl.ds(s, 1), :]
                    p = hq * ksr * jnp.exp(jnp.minimum(cum - cs, 0.0))
                    hsel = jnp.where(c_b == hsel_base + s, 1.0, 0.0).astype(BF16)
                    return acc + _dot(p.astype(BF16), hsel)

                a_ref[n] = lax.fori_loop(0, CH, key_body, jnp.zeros((CH, GW), F32))

        for n, (d, g) in enumerate(chains):
            hq, v, kk, cum, tail = prepared[n]
            o_r = dirs[d][4]
            st = st_ref[n]
            a = jnp.where(cmasks[d], a_ref[n], 0.0).astype(BF16)
            vb = v.astype(BF16)
            o = _dot(a, expand(vb)) + _dot_nt((hq * jnp.exp(cum)).astype(BF16), st.astype(BF16))
            o_r[pl.ds(rows[d], CH), g * GW:(g + 1) * GW] = o
            k2 = (kk * jnp.exp(tail - cum)).astype(BF16)
            st_ref[n] = jnp.exp(tail) * st + jnp.where(bm, _dot_tn(vb, k2), 0.0)
        return carry

    lax.fori_loop(0, nchunks, chunk_body, 0)


def _hgrn2(dm, hq, ff, fb, hi, lb):
    B, T = dm.B, dm.T
    Wd = hq.shape[1]
    GW = HG_HEADS_PER_GROUP * HEAD_DIM
    nchain = 2 * (Wd // GW)
    nblk = 1 + dm.N // SEQ_BLOCK
    fwd, bwd = _seq_block_maps(dm)
    fspec = pl.BlockSpec((SEQ_BLOCK, Wd), lambda b, j: (fwd(b, j), 0))
    bspec = pl.BlockSpec((SEQ_BLOCK, Wd), lambda b, j: (bwd(b, j), 0))
    lbspec = pl.BlockSpec((1, Wd), lambda b, j: (0, 0))
    return pl.pallas_call(
        _hgrn_kernel,
        grid=(B, nblk),
        in_specs=[fspec, fspec, fspec, bspec, bspec, bspec, lbspec, lbspec],
        out_specs=[fspec, bspec],
        out_shape=[jax.ShapeDtypeStruct((T, Wd), F32)] * 2,
        scratch_shapes=[pltpu.VMEM((nchain, GW, GW), F32),
                        pltpu.VMEM((nchain, HG_CHUNK, GW), F32),
                        pltpu.VMEM((nchain, HG_CHUNK, GW), F32),
                        pltpu.VMEM((nchain, HG_CHUNK, GW), F32)],
        compiler_params=_cparams(("arbitrary", "arbitrary")),
        name="hgrn2_bidirectional",
    )(hq, ff, hi, hq, fb, hi, lb[0:1], lb[1:2])


def _lru_scan(a, x, reverse):
    n = a.shape[0]
    row = lax.broadcasted_iota(I32, a.shape, 0)
    s = 1
    while s < n:
        shift = (n - s) if reverse else s
        a_sh = pltpu.roll(a, shift, 0)
        x_sh = pltpu.roll(x, shift, 0)
        valid = (row < n - s) if reverse else (row >= s)
        x = jnp.where(valid, a * x_sh + x, x)
        a = jnp.where(valid, a * a_sh, a)
        s *= 2
    return a, x


def _lru_kernel(ucf_ref, upf_ref, unf_ref, ucb_ref, upb_ref, unb_ref, cw_ref, cb_ref, wbd_ref,
                ba_ref, bx_ref, lam_ref, hf_ref, hb_ref, ext_ref, carry_ref):
    j = pl.program_id(1)
    nblk = pl.num_programs(1)

    @pl.when(j == 0)
    def _():
        carry_ref[...] = jnp.zeros_like(carry_ref)

    TT, W = ucf_ref.shape
    GW = wbd_ref.shape[2]
    halo = upf_ref.shape[0]
    dirs = ((ucf_ref, upf_ref, unf_ref, hf_ref), (ucb_ref, upb_ref, unb_ref, hb_ref))
    for d, (uc_r, up_r, un_r, out_r) in enumerate(dirs):
        pos = j if d == 0 else jnp.where(j == 0, 0, nblk - j)
        keep_prev = jnp.where((pos == 0) | (pos == 1), 0.0, 1.0)
        keep_next = jnp.where((pos == 0) | (pos == nblk - 1), 0.0, 1.0)
        ext_ref[d, 0:halo, :] = up_r[...] * keep_prev
        ext_ref[d, halo:halo + TT, :] = uc_r[...]
        ext_ref[d, halo + TT:halo + TT + halo, :] = un_r[...] * keep_next
        for g in range(W // GW):
            ls = slice(g * GW, (g + 1) * GW)
            u = cb_ref[:, ls]
            for tap in range(CONV_W):
                off = halo - 2 + tap
                u = u + cw_ref[tap:tap + 1, ls] * ext_ref[d, off:off + TT, ls]
            gates = _dot(u.astype(BF16), wbd_ref[d, g])
            r = jax.nn.sigmoid(gates[:, :GW] + ba_ref[d:d + 1, ls])
            ig = jax.nn.sigmoid(gates[:, GW:] + bx_ref[d:d + 1, ls])
            nl = -lam_ref[d:d + 1, ls]
            softplus = jnp.maximum(nl, 0.0) + jnp.log1p(jnp.exp(-jnp.abs(nl)))
            log_a = -LRU_C * r * softplus
            a = jnp.exp(log_a)
            x_in = jnp.sqrt(1.0 - jnp.exp(2.0 * log_a)) * ig * u
            a_acc, h = _lru_scan(a, x_in, d == 1)
            h = h + a_acc * carry_ref[d, 0:1, ls]
            out_r[:, ls] = h
            last = 0 if d == 1 else TT - 1
            carry_ref[d, 0:1, ls] = h[last:last + 1, :]


def _block_diag_gates(w_a, w_x, group):
    ndir, K, d, _ = w_a.shape
    per = group // d
    eye = jnp.eye(per, dtype=w_a.dtype)

    def bd(w):
        w = w.reshape(ndir, K // per, per, d, d)
        full = jnp.einsum('ngkde,kl->ngkdle', w, eye)
        return full.reshape(ndir, K // per, group, group)

    return jnp.concatenate([bd(w_a), bd(w_x)], axis=-1)


def _rglru(dm, u, conv_w, conv_b, w_a, b_a, w_x, b_x, lam):
    B, T = dm.B, dm.T
    W = u.shape[1]
    GW = 256
    halo = 8
    nblk = 1 + dm.N // SEQ_BLOCK
    fwd, bwd = _seq_block_maps(dm)
    hb = SEQ_BLOCK // halo
    nh = T // halo
    wbd = _block_diag_gates(w_a, w_x, GW).astype(BF16)

    def cur(m):
        return pl.BlockSpec((SEQ_BLOCK, W), lambda b, j: (m(b, j), 0))

    def prev(m):
        return pl.BlockSpec((halo, W), lambda b, j: (jnp.maximum(m(b, j) * hb - 1, 0), 0))

    def nxt(m):
        return pl.BlockSpec((halo, W), lambda b, j: (jnp.minimum((m(b, j) + 1) * hb, nh - 1), 0))

    def full(a):
        nd = a.ndim
        return pl.BlockSpec(a.shape, lambda b, j: (0,) * nd)

    consts = (conv_w, conv_b.reshape(1, W), wbd, b_a, b_x, lam)
    return pl.pallas_call(
        _lru_kernel,
        grid=(B, nblk),
        in_specs=[cur(fwd), prev(fwd), nxt(fwd), cur(bwd), prev(bwd), nxt(bwd)] + [full(a) for a in consts],
        out_specs=[cur(fwd), cur(bwd)],
        out_shape=[jax.ShapeDtypeStruct((T, W), F32)] * 2,
        scratch_shapes=[pltpu.VMEM((2, SEQ_BLOCK + 2 * halo, W), F32), pltpu.VMEM((2, 8, W), F32)],
        compiler_params=_cparams(("arbitrary", "arbitrary")),
        name="rglru_bidirectional",
    )(u, u, u, u, u, u, *consts)


def _post_mixer_epilogue(o, x_ref, m2_ref, m3_ref, m4_ref, lng_ref, lnb_ref, rwh_ref, rwl_ref,
                         xo_ref, h2_ref, lg_ref, alpha):
    z = alpha * x_ref[...] + m2_ref[...] * o
    xn = _layer_norm(z, lng_ref[...], lnb_ref[...])
    xo_ref[...] = xn
    h2 = xn * (1.0 + m4_ref[...]) + m3_ref[...]
    h2_ref[...] = h2
    hh, hl = _split(h2)
    lg_ref[...] = _dot_nt(rwh_ref[...], hh) + _dot_nt(rwh_ref[...], hl) + _dot_nt(rwl_ref[...], hh)


def _post_even_kernel(nac_ref, nal_ref, of_ref, ob_ref, hg_ref, ng_ref, w_ref, *rest, alpha, ctx_tiles):
    na = jnp.where(pl.program_id(0) < ctx_tiles, nac_ref[...], nal_ref[...])
    o = of_ref[...] + ob_ref[...]
    wd = o.shape[1]
    r_i = lax.broadcasted_iota(I32, (wd, wd), 0) // HEAD_DIM
    c_i = lax.broadcasted_iota(I32, (wd, wd), 1) // HEAD_DIM
    avg = jnp.where(r_i == c_i, 1.0 / HEAD_DIM, 0.0).astype(BF16)
    sh, sl = _split(o * o)
    ms = _dot(sh, avg) + _dot(sl, avg)
    r = o * lax.rsqrt(ms + RMS_EPS) * ng_ref[...] * _silu(hg_ref[...])
    mix = _dot(na, w_ref[:wd, :]) + _dot(r.astype(BF16), w_ref[wd:, :])
    _post_mixer_epilogue(mix, *rest, alpha=alpha)


def _post_odd_kernel(hf_ref, hb_ref, y_ref, w_ref, *rest, alpha):
    m = ((hf_ref[...] + hb_ref[...]) * _gelu_tanh(y_ref[...])).astype(BF16)
    _post_mixer_epilogue(_dot(m, w_ref[...]), *rest, alpha=alpha)


def _post_mixer(dm, mixer_inputs, w_out, xs, m2, m3, m4, lng, lnb, rwh, rwl, alpha, even):
    T, D, TM = dm.T, dm.D, dm.TM
    E = rwh.shape[0]
    row = lambda a: pl.BlockSpec((TM, a.shape[1]), lambda i: (i, 0))
    const = lambda a: pl.BlockSpec(a.shape, lambda i: (0,) * a.ndim)
    mod_spec = pl.BlockSpec((None, 1, D), lambda i: (dm.group(i, TM), 0, 0))
    if even:
        nac, nal, of, ob, hg, ng = mixer_inputs
        nct = dm.T_ctx // TM
        ins = [nac, nal, of, ob, hg, ng, w_out]
        specs = [pl.BlockSpec((TM, nac.shape[1]), lambda i: (jnp.minimum(i, nct - 1), 0)),
                 pl.BlockSpec((TM, nal.shape[1]), lambda i: (jnp.maximum(i - nct, 0), 0)),
                 row(of), row(ob), row(hg), const(ng), const(w_out)]
        kern = functools.partial(_post_even_kernel, alpha=alpha, ctx_tiles=nct)
    else:
        hf, hb, y = mixer_inputs
        ins = [hf, hb, y, w_out]
        specs = [row(hf), row(hb), row(y), const(w_out)]
        kern = functools.partial(_post_odd_kernel, alpha=alpha)
    ins += [xs, m2, m3, m4, lng, lnb, rwh, rwl]
    specs += [row(xs), mod_spec, mod_spec, mod_spec, const(lng), const(lnb), const(rwh), const(rwl)]
    return pl.pallas_call(
        kern,
        grid=(T // TM,),
        in_specs=specs,
        out_specs=[pl.BlockSpec((TM, D), lambda i: (i, 0)), pl.BlockSpec((TM, D), lambda i: (i, 0)),
                   pl.BlockSpec((E, TM), lambda i: (0, i))],
        out_shape=[jax.ShapeDtypeStruct((T, D), F32), jax.ShapeDtypeStruct((T, D), F32),
                   jax.ShapeDtypeStruct((E, T), F32)],
        compiler_params=_cparams(("parallel",)),
        name="post_mixer_even" if even else "post_mixer_odd",
    )(*ins)


def _route_kernel(lg_ref, bias_ref, tri_ref, idx_ref, rank_ref, w_ref, cnt_ref, carry_ref):
    i = pl.program_id(0)

    @pl.when(i == 0)
    def _():
        carry_ref[...] = jnp.zeros_like(carry_ref)

    E, TK = lg_ref.shape
    s = jax.nn.sigmoid(lg_ref[...])
    work = s + bias_ref[...]
    eio = lax.broadcasted_iota(I32, (E, TK), 0).astype(F32)
    picked = jnp.zeros((E, TK), F32)
    sels, idxs = [], []
    for _ in range(TOP_K):
        m = jnp.max(work, axis=0, keepdims=True)
        ik = jnp.min(jnp.where(work == m, eio, float(E)), axis=0, keepdims=True)
        oh = eio == ik
        sels.append(jnp.sum(jnp.where(oh, s, 0.0), axis=0, keepdims=True))
        idxs.append(ik)
        picked = jnp.where(oh, 1.0, picked)
        work = jnp.where(oh, -jnp.inf, work)
    total = sels[0]
    for sk in sels[1:]:
        total = total + sk
    carry = carry_ref[:, 0:1]
    ranks = carry + _dot(picked.astype(BF16), tri_ref[...])
    out_rows = idx_ref.shape[0]
    rio = lax.broadcasted_iota(I32, (out_rows, TK), 0)
    idx_o = jnp.zeros((out_rows, TK), I32)
    rank_o = jnp.zeros((out_rows, TK), I32)
    w_o = jnp.zeros((out_rows, TK), F32)
    for kx in range(TOP_K):
        rk = jnp.sum(jnp.where(eio == idxs[kx], ranks, 0.0), axis=0, keepdims=True).astype(I32)
        idx_o = jnp.where(rio == kx, idxs[kx].astype(I32), idx_o)
        rank_o = jnp.where(rio == kx, rk, rank_o)
        w_o = jnp.where(rio == kx, sels[kx] / total * ROUTED_SCALE, w_o)
    idx_ref[...] = idx_o
    rank_ref[...] = rank_o
    w_ref[...] = w_o
    new_carry = carry + jnp.sum(picked, axis=1, keepdims=True)
    carry_ref[...] = jnp.broadcast_to(new_carry, carry_ref.shape)
    cnt_ref[...] = jnp.broadcast_to(new_carry, cnt_ref.shape)


def _route(logits_t, router_bias, t0):
    E, T = logits_t.shape
    Tm = T - t0
    TK = _pick_tile(int(np.gcd(Tm, t0)), (1024, 512, 256))
    off = t0 // TK
    tri = jnp.asarray(np.triu(np.ones((TK, TK), np.float32), 1), BF16)
    tok = lambda: pl.BlockSpec((8, TK), lambda i: (0, i))
    idx, rank, w, cnt = pl.pallas_call(
        _route_kernel,
        grid=(Tm // TK,),
        in_specs=[pl.BlockSpec((E, TK), lambda i: (0, i + off)),
                  pl.BlockSpec((E, 1), lambda i: (0, 0)),
                  pl.BlockSpec((TK, TK), lambda i: (0, 0))],
        out_specs=[tok(), tok(), tok(), pl.BlockSpec((E, 128), lambda i: (0, 0))],
        out_shape=[jax.ShapeDtypeStruct((8, Tm), I32), jax.ShapeDtypeStruct((8, Tm), I32),
                   jax.ShapeDtypeStruct((8, Tm), F32), jax.ShapeDtypeStruct((E, 128), F32)],
        scratch_shapes=[pltpu.VMEM((E, 128), F32)],
        compiler_params=_cparams(("arbitrary",)),
        name="moe_route",
    )(logits_t, router_bias.reshape(E, 1), tri)
    return idx, rank, w, cnt[:, 0].astype(I32)


def _dispatch_kernel(dest_hbm, h_ref, xin_hbm, xs_hbm, dsm, sem_d, sem):
    del xin_hbm
    i = pl.program_id(0)
    TMd = h_ref.shape[0]
    n = TMd * TOP_K
    cp = pltpu.make_async_copy(dest_hbm.at[pl.ds(pl.multiple_of(i * n, 1024), n)], dsm, sem_d)
    cp.start()
    cp.wait()

    def row_copy(t, d):
        return pltpu.make_async_copy(h_ref.at[pl.ds(t, 1)], xs_hbm.at[pl.ds(d, 1)], sem)

    def start_body(t, c):
        for kx in range(TOP_K):
            row_copy(t, dsm[t * TOP_K + kx]).start()
        return c

    lax.fori_loop(0, TMd, start_body, 0)

    def wait_body(t, c):
        for kx in range(TOP_K):
            row_copy(t, dsm[t * TOP_K + kx]).wait()
        return c

    lax.fori_loop(0, TMd, wait_body, 0)


def _dispatch(h2, dest_flat, t0, n_slots):
    T, D = h2.shape
    TMd = 512
    Tm = T - t0
    assert Tm % TMd == 0 and t0 % TMd == 0
    off = t0 // TMd
    n = TMd * TOP_K
    return pl.pallas_call(
        _dispatch_kernel,
        grid=(Tm // TMd,),
        in_specs=[pl.BlockSpec(memory_space=pl.ANY),
                  pl.BlockSpec((TMd, D), lambda i: (i + off, 0)),
                  pl.BlockSpec(memory_space=pl.ANY)],
        out_specs=pl.BlockSpec(memory_space=pl.ANY),
        out_shape=jax.ShapeDtypeStruct((n_slots, D), F32),
        scratch_shapes=[pltpu.SMEM((n,), I32), pltpu.SemaphoreType.DMA, pltpu.SemaphoreType.DMA],
        input_output_aliases={2: 0},
        compiler_params=_cparams(("arbitrary",)),
        name="moe_dispatch",
    )(dest_flat, h2, jnp.zeros((n_slots, D), F32))


def _gmm_kernel(bexp_ref, nused_ref, x_ref, wgu_ref, wd_ref, y_ref):
    del bexp_ref
    i = pl.program_id(0)
    hid = wd_ref.shape[0]

    @pl.when(i < nused_ref[0])
    def _():
        gu = _dot(x_ref[...].astype(BF16), wgu_ref[...])
        act = _silu(gu[:, :hid]) * gu[:, hid:]
        y_ref[...] = _dot(act.astype(BF16), wd_ref[...])

    @pl.when(i >= nused_ref[0])
    def _():
        y_ref[...] = jnp.zeros_like(y_ref)


def _grouped_experts(xsorted, block_exp, nused, wgu, wd):
    NS, D = xsorted.shape
    nb = NS // MOE_BLOCK
    hid = wd.shape[1]
    grid_spec = pltpu.PrefetchScalarGridSpec(
        num_scalar_prefetch=2,
        grid=(nb,),
        in_specs=[pl.BlockSpec((MOE_BLOCK, D), lambda i, be, nu: (i, 0)),
                  pl.BlockSpec((None, D, 2 * hid), lambda i, be, nu: (be[i], 0, 0)),
                  pl.BlockSpec((None, hid, D), lambda i, be, nu: (be[i], 0, 0))],
        out_specs=pl.BlockSpec((MOE_BLOCK, D), lambda i, be, nu: (i, 0)),
    )
    return pl.pallas_call(
        _gmm_kernel,
        grid_spec=grid_spec,
        out_shape=jax.ShapeDtypeStruct((NS, D), F32),
        compiler_params=_cparams(("arbitrary",)),
        name="moe_grouped_experts",
    )(block_exp, nused, xsorted, wgu, wd)


def _combine_kernel(dest_hbm, y_hbm, w_ref, h_ref, sgu_ref, sd_ref, x_ref, m5_ref, lng_ref, lnb_ref,
                    o_ref, dsm, gbuf, sem_d, sem, *, alpha):
    i = pl.program_id(0)
    TMc = h_ref.shape[0]
    n = TMc * TOP_K
    cp = pltpu.make_async_copy(dest_hbm.at[pl.ds(pl.multiple_of(i * n, 1024), n)], dsm, sem_d)
    cp.start()
    cp.wait()

    def row_copy(t, kx, d):
        return pltpu.make_async_copy(y_hbm.at[pl.ds(d, 1)], gbuf.at[kx, pl.ds(t, 1)], sem)

    def start_body(t, c):
        for kx in range(TOP_K):
            row_copy(t, kx, dsm[t * TOP_K + kx]).start()
        return c

    lax.fori_loop(0, TMc, start_body, 0)

    hid = sd_ref.shape[0]
    gu = _dot(h_ref[...].astype(BF16), sgu_ref[...])
    y = _dot((_silu(gu[:, :hid]) * gu[:, hid:]).astype(BF16), sd_ref[...])

    def wait_body(t, c):
        for kx in range(TOP_K):
            row_copy(t, kx, dsm[t * TOP_K + kx]).wait()
        return c

    lax.fori_loop(0, TMc, wait_body, 0)
    w = w_ref[...]
    for kx in range(TOP_K):
        y = y + w[:, kx:kx + 1] * gbuf[kx]
    z = alpha * x_ref[...] + m5_ref[...] * y
    o_ref[...] = _layer_norm(z, lng_ref[...], lnb_ref[...])


def _combine(dm, dest_flat, ysorted, w_tm, h2, sgu, sd, xs, m5, lng, lnb, alpha, t0):
    T, D = h2.shape
    TMc = 512
    Tm = T - t0
    off = t0 // TMc
    n = TMc * TOP_K
    const = lambda a: pl.BlockSpec(a.shape, lambda i: (0,) * a.ndim)
    row = lambda a: pl.BlockSpec((TMc, a.shape[1]), lambda i: (i + off, 0))
    return pl.pallas_call(
        functools.partial(_combine_kernel, alpha=alpha),
        grid=(Tm // TMc,),
        in_specs=[pl.BlockSpec(memory_space=pl.ANY), pl.BlockSpec(memory_space=pl.ANY),
                  pl.BlockSpec((TMc, 8), lambda i: (i, 0)), row(h2), const(sgu), const(sd), row(xs),
                  pl.BlockSpec((None, 1, D), lambda i: (dm.group(i + off, TMc), 0, 0)),
                  const(lng), const(lnb)],
        out_specs=pl.BlockSpec((TMc, D), lambda i: (i, 0)),
        out_shape=jax.ShapeDtypeStruct((Tm, D), F32),
        scratch_shapes=[pltpu.SMEM((n,), I32), pltpu.VMEM((TOP_K, TMc, D), F32),
                        pltpu.SemaphoreType.DMA, pltpu.SemaphoreType.DMA],
        compiler_params=_cparams(("arbitrary",)),
        name="moe_combine",
    )(dest_flat, ysorted, w_tm, h2, sgu, sd, xs, m5, lng, lnb)


def _moe(dm, h2, logits_t, xs, m5, lng, lnb, router_bias, wgu, wd, sgu, sd, alpha, t0):
    T, D = h2.shape
    E = wgu.shape[0]
    Tm = T - t0
    idx, rank, w, counts = _route(logits_t, router_bias, t0)
    nb = -(-(Tm * TOP_K) // MOE_BLOCK) + E
    padded = (counts + MOE_BLOCK - 1) // MOE_BLOCK * MOE_BLOCK
    pends = jnp.cumsum(padded)
    pstarts = pends - padded
    nused = (pends[-1:] // MOE_BLOCK).astype(I32)
    block_exp = jnp.clip(jnp.searchsorted(pends, jnp.arange(nb, dtype=I32) * MOE_BLOCK, side='right'),
                         0, E - 1).astype(I32)
    dest = pstarts[idx[:TOP_K]] + rank[:TOP_K]
    dest_flat = dest.T.reshape(-1).astype(I32)
    w_tm = jnp.pad(w[:TOP_K].T, ((0, 0), (0, 8 - TOP_K)))
    xsorted = _dispatch(h2, dest_flat, t0, nb * MOE_BLOCK)
    ysorted = _grouped_experts(xsorted, block_exp, nused, wgu, wd)
    return _combine(dm, dest_flat, ysorted, w_tm, h2, sgu, sd, xs, m5, lng, lnb, alpha, t0)


def kernel(x, c, ctx, c_ctx, ada_w, ada_b, ln_g, ln_b, ev_w_in, ev_w_out, na_rpb, hg_lb_raw, hg_norm_g, od_w_in, od_conv_w, od_conv_b, lru_w_a, lru_b_a, lru_w_x, lru_b_x, lru_lam, od_w_out, router_w, router_bias, exp_w_gate, exp_w_up, exp_w_down, sh_w_gate, sh_w_up, sh_w_down):
    B, N, D = x.shape
    C = ctx.shape[1]
    depth = ada_w.shape[0]
    dm = _Dims(B, N, C, D)
    alpha = float((2 * depth) ** 0.25)
    assert B + 1 <= 8

    cvec = jnp.zeros((8, D), F32).at[:B].set(c).at[B].set(c_ctx)
    mod = _modulation(cvec, ada_w, ada_b).reshape(depth, 8, 6, 1, D)

    p_lb = jax.nn.softmax(hg_lb_raw.astype(F32), axis=1)
    hg_lb = jnp.cumsum(p_lb, axis=1) - p_lb[:, :1]

    xs = jnp.concatenate([ctx.reshape(B * C, D), x.reshape(B * N, D)], axis=0)
    for l in range(depth):
        jl = l // 2
        last = l == depth - 1
        m = [mod[l, :, t] for t in range(6)]
        rw_t = router_w[l].T
        rwh, rwl = _split(rw_t)
        lng = ln_g[l][:, None, :]
        lnb = ln_b[l][:, None, :]
        if l % 2 == 0:
            q, k, v, hq, ff, fb, hi, hg = _inproj(dm, xs, m[1], m[0], ev_w_in[jl].astype(BF16), True)
            bias = _na_bias_tables(na_rpb[jl], N // GRID_W)
            na_lat = _neighbourhood_attention(dm, q, k, v, bias)
            na_ctx = _context_attention(dm, q, k, v)
            o_f, o_b = _hgrn2(dm, hq, ff, fb, hi, hg_lb[:, jl])
            ng = jnp.tile(hg_norm_g[jl], hq.shape[1] // HEAD_DIM)[None, :]
            mixer_inputs = (na_ctx, na_lat, o_f, o_b, hg, ng)
            w_out = ev_w_out[jl].astype(BF16)
        else:
            y, u = _inproj(dm, xs, m[1], m[0], od_w_in[jl].astype(BF16), False)
            h_f, h_b = _rglru(dm, u, od_conv_w[jl], od_conv_b[jl], lru_w_a[jl], lru_b_a[jl],
                              lru_w_x[jl], lru_b_x[jl], lru_lam[jl])
            mixer_inputs = (h_f, h_b, y)
            w_out = od_w_out[jl].astype(BF16)
        xs, h2, logits_t = _post_mixer(dm, mixer_inputs, w_out, xs, m[2], m[3], m[4], lng[0], lnb[0],
                                       rwh, rwl, alpha, l % 2 == 0)
        wgu = jnp.concatenate([exp_w_gate[l], exp_w_up[l]], axis=-1).astype(BF16)
        wd = exp_w_down[l].astype(BF16)
        sgu = jnp.concatenate([sh_w_gate[l], sh_w_up[l]], axis=-1).astype(BF16)
        sd = sh_w_down[l].astype(BF16)
        t0 = dm.T_ctx if last else 0
        xs = _moe(dm, h2, logits_t, xs, m[5], lng[1], lnb[1], router_bias[l], wgu, wd, sgu, sd, alpha, t0)
    return xs.reshape(B, N, D)
```

```python
import functools

import numpy as np
import jax
import jax.numpy as jnp
from jax import lax
from jax.experimental import pallas as pl
from jax.experimental.pallas import tpu as pltpu

F32 = jnp.float32
BF16 = jnp.bfloat16
I32 = jnp.int32

HEAD_DIM = 64
GRID_W = 64
NA_WIN_ROWS = 8
NA_WIN_COLS = 16
NA_QROWS = 8
NA_KROWS = 16
HG_CHUNK = 64
HG_HEADS_PER_GROUP = 4
FORGET_FLOOR = 1e-30
HG_SAFE_EXP = 80.0
HG_CLIP_EXP = 85.0
LRU_C = 8.0
LRU_BLOCKS = 16
CONV_W = 4
TOP_K = 6
ROUTED_SCALE = 2.5
LN_EPS = 1e-5
RMS_EPS = 1e-6
SEQ_BLOCK = 256
MOE_BLOCK = 256
MASK_VALUE = -1e30
VMEM_LIMIT = 56 * 1024 * 1024


def _cparams(sem):
    return pltpu.CompilerParams(dimension_semantics=sem, vmem_limit_bytes=VMEM_LIMIT)


def _split(a):
    hi = a.astype(BF16)
    lo = (a - hi.astype(F32)).astype(BF16)
    return hi, lo


def _dot(a, b):
    return jnp.dot(a, b, preferred_element_type=F32)


def _dot_nt(a, b):
    return lax.dot_general(a, b, (((1,), (1,)), ((), ())), preferred_element_type=F32)


def _dot_tn(a, b):
    return lax.dot_general(a, b, (((0,), (0,)), ((), ())), preferred_element_type=F32)


def _dot3(a, b):
    ah, al = _split(a)
    bh, bl = _split(b)
    return _dot(ah, bh) + _dot(al, bh) + _dot(ah, bl)


def _silu(v):
    return v * jax.nn.sigmoid(v)


def _gelu_tanh(v):
    return 0.5 * v * (1.0 + jnp.tanh(0.7978845608028654 * (v + 0.044715 * v * v * v)))


def _layer_norm(z, g, b):
    mu = jnp.mean(z, axis=-1, keepdims=True)
    zc = z - mu
    var = jnp.mean(zc * zc, axis=-1, keepdims=True)
    return zc * lax.rsqrt(var + LN_EPS) * g + b


def _pick_tile(n, cands):
    for c in cands:
        if n % c == 0:
            return c
    raise ValueError(f"no tile for {n}")


class _Dims:
    def __init__(self, B, N, C, D):
        self.B, self.N, self.C, self.D = B, N, C, D
        self.T_ctx = B * C
        self.T = B * C + B * N
        assert C == SEQ_BLOCK and N % SEQ_BLOCK == 0
        assert N % GRID_W == 0 and (N // GRID_W) % NA_QROWS == 0 and N // GRID_W >= NA_KROWS
        assert self.T_ctx % 512 == 0
        self.TM = _pick_tile(self.T_ctx, (512, 256))
        assert N % self.TM == 0

    def group(self, i, tile):
        start = i * tile
        return jnp.where(start < self.T_ctx, self.B, (start - self.T_ctx) // self.N)


def _mod_kernel(c_ref, w_ref, b_ref, o_ref):
    o_ref[...] = _dot3(_silu(c_ref[...]), w_ref[...]) + b_ref[...]


def _modulation(cvec, ada_w, ada_b):
    L, D, W6 = ada_w.shape
    nc = 1536
    return pl.pallas_call(
        _mod_kernel,
        grid=(L, W6 // nc),
        in_specs=[
            pl.BlockSpec((8, D), lambda l, j: (0, 0)),
            pl.BlockSpec((None, D, nc), lambda l, j: (l, 0, j)),
            pl.BlockSpec((None, 1, nc), lambda l, j: (l, 0, j)),
        ],
        out_specs=pl.BlockSpec((None, 8, nc), lambda l, j: (l, 0, j)),
        out_shape=jax.ShapeDtypeStruct((L, 8, W6), F32),
        compiler_params=_cparams(("arbitrary", "arbitrary")),
        name="adaln_modulation",
    )(cvec, ada_w, ada_b.reshape(L, 1, W6))


def _inproj_even_kernel(x_ref, sc_ref, sh_ref, w_ref, q_ref, k_ref, v_ref, hq_ref, ff_ref, fb_ref, hi_ref, hg_ref):
    h = (x_ref[...] * (1.0 + sc_ref[...]) + sh_ref[...]).astype(BF16)
    wd = q_ref.shape[1]

    def part(j):
        return _dot(h, w_ref[:, j * wd:(j + 1) * wd])

    scale = HEAD_DIM ** -0.5
    q_ref[...] = (part(0) * scale).astype(BF16)
    k_ref[...] = part(1).astype(BF16)
    v_ref[...] = part(2).astype(BF16)
    hq_ref[...] = _silu(part(3)) * scale
    ff_ref[...] = part(4)
    fb_ref[...] = part(5)
    hi_ref[...] = part(6)
    hg_ref[...] = part(7)


def _inproj_odd_kernel(x_ref, sc_ref, sh_ref, w_ref, y_ref, u_ref):
    h = (x_ref[...] * (1.0 + sc_ref[...]) + sh_ref[...]).astype(BF16)
    wd = y_ref.shape[1]
    y_ref[...] = _dot(h, w_ref[:, :wd])
    u_ref[...] = _dot(h, w_ref[:, wd:])


def _inproj(dm, xs, sc, sh, w, even):
    T, D, TM = dm.T, dm.D, dm.TM
    wtot = w.shape[1]
    mod_spec = pl.BlockSpec((None, 1, D), lambda i: (dm.group(i, TM), 0, 0))
    in_specs = [pl.BlockSpec((TM, D), lambda i: (i, 0)), mod_spec, mod_spec,
                pl.BlockSpec((D, wtot), lambda i: (0, 0))]
    if even:
        wd = wtot // 8
        dts = [BF16] * 3 + [F32] * 5
        kern = _inproj_even_kernel
    else:
        wd = wtot // 2
        dts = [F32] * 2
        kern = _inproj_odd_kernel
    return pl.pallas_call(
        kern,
        grid=(T // TM,),
        in_specs=in_specs,
        out_specs=[pl.BlockSpec((TM, wd), lambda i: (i, 0)) for _ in dts],
        out_shape=[jax.ShapeDtypeStruct((T, wd), dt) for dt in dts],
        compiler_params=_cparams(("parallel",)),
        name="inproj_even" if even else "inproj_odd",
    )(xs, sc, sh, w)


def _na_bias_tables(rpb, rows):
    W = GRID_W
    H = rpb.shape[0]
    nr, nc = 2 * NA_WIN_ROWS - 1, 2 * NA_WIN_COLS - 1
    hp = lax.Precision.HIGHEST
    c = np.arange(W)[:, None]
    kc = np.arange(W)[None, :]
    c0 = np.clip(c - NA_WIN_COLS // 2, 0, W - NA_WIN_COLS)
    valid_c = (kc >= c0) & (kc < c0 + NA_WIN_COLS)
    dc = np.clip(kc - c + NA_WIN_COLS - 1, 0, nc - 1)
    oh_c = (dc.reshape(-1, 1) == np.arange(nc)[None, :]).astype(np.float32)
    by_col = jnp.einsum('hrs,xs->hrx', rpb.astype(F32), jnp.asarray(oh_c), precision=hp)
    tabs = []
    for rbase, kb in ((0, 0), (NA_QROWS, NA_QROWS - NA_WIN_ROWS // 2), (rows - NA_QROWS, rows - NA_KROWS)):
        r = rbase + np.arange(NA_QROWS)[:, None]
        kr = kb + np.arange(NA_KROWS)[None, :]
        r0 = np.clip(r - NA_WIN_ROWS // 2, 0, rows - NA_WIN_ROWS)
        valid_r = (kr >= r0) & (kr < r0 + NA_WIN_ROWS)
        dr = np.clip(kr - r + NA_WIN_ROWS - 1, 0, nr - 1)
        oh_r = (dr.reshape(-1, 1) == np.arange(nr)[None, :]).astype(np.float32)
        t = jnp.einsum('yr,hrx->hyx', jnp.asarray(oh_r), by_col, precision=hp)
        t = t.reshape(H, NA_QROWS, NA_KROWS, W, W).transpose(0, 1, 3, 2, 4)
        valid = valid_r[:, None, :, None] & valid_c[None, :, None, :]
        t = jnp.where(jnp.asarray(valid)[None], t, MASK_VALUE)
        tabs.append(t.reshape(H, NA_QROWS * W, NA_KROWS * W))
    return jnp.stack(tabs).astype(F32)


def _na_kernel(q_ref, k0, k1, k2, k3, v0, v1, v2, v3, kc_ref, vc_ref, bias_ref, o_ref):
    q2 = q_ref[...]
    lane = lax.broadcasted_iota(I32, (1, q2.shape[1]), 1)
    first = lane < HEAD_DIM
    ks = [r[...] for r in (k0, k1, k2, k3)]
    vs = [r[...] for r in (v0, v1, v2, v3)]
    kc = kc_ref[...]
    vc = vc_ref[...]
    kb = ks[0].shape[0]
    outs = []
    for hh in range(2):
        sel = first if hh == 0 else jnp.logical_not(first)
        qh = jnp.where(sel, q2, jnp.zeros_like(q2))
        s_loc = jnp.concatenate([_dot_nt(qh, kj) for kj in ks], axis=1) + bias_ref[hh]
        s_ctx = _dot_nt(qh, kc)
        m = jnp.maximum(jnp.max(s_loc, axis=1, keepdims=True), jnp.max(s_ctx, axis=1, keepdims=True))
        p_loc = jnp.exp(s_loc - m)
        p_ctx = jnp.exp(s_ctx - m)
        denom = jnp.sum(p_loc, axis=1, keepdims=True) + jnp.sum(p_ctx, axis=1, keepdims=True)
        o = _dot(p_ctx.astype(BF16), vc)
        for j in range(4):
            o = o + _dot(p_loc[:, j * kb:(j + 1) * kb].astype(BF16), vs[j])
        outs.append(o / denom)
    o_ref[...] = jnp.where(first, outs[0], outs[1]).astype(o_ref.dtype)


def _neighbourhood_attention(dm, q, k, v, bias):
    B, N, C, T = dm.B, dm.N, dm.C, dm.T
    H2 = q.shape[1] // (2 * HEAD_DIM)
    rows = N // GRID_W
    nqb = rows // NA_QROWS
    QB = NA_QROWS * GRID_W
    KB = QB // 2
    ngroups = N // KB
    lat_q0 = dm.T_ctx // QB
    lat_k0 = dm.T_ctx // KB

    def q_map(p, i, b):
        return (lat_q0 + b * (N // QB) + i, p)

    def o_map(p, i, b):
        return (b * (N // QB) + i, p)

    def kv_map(j):
        def f(p, i, b):
            gs = jnp.clip(2 * i - 1, 0, ngroups - 4)
            return (lat_k0 + b * ngroups + gs + j, p)
        return f

    def ctx_map(p, i, b):
        return (b, p)

    def bias_map(p, i, b):
        var = jnp.where(i == 0, 0, jnp.where(i == nqb - 1, 2, 1))
        return (var, p, 0, 0)

    lanes = 2 * HEAD_DIM
    kv_specs = [pl.BlockSpec((KB, lanes), kv_map(j)) for j in range(4)]
    return pl.pallas_call(
        _na_kernel,
        grid=(H2, nqb, B),
        in_specs=[pl.BlockSpec((QB, lanes), q_map)] + kv_specs + kv_specs
        + [pl.BlockSpec((C, lanes), ctx_map), pl.BlockSpec((C, lanes), ctx_map),
           pl.BlockSpec((None, 2, QB, NA_KROWS * GRID_W), bias_map)],
        out_specs=pl.BlockSpec((QB, lanes), o_map),
        out_shape=jax.ShapeDtypeStruct((B * N, q.shape[1]), BF16),
        compiler_params=_cparams(("arbitrary", "arbitrary", "arbitrary")),
        name="neighbourhood_attention",
    )(q, k, k, k, k, v, v, v, v, k, v, bias)


def _ctx_attn_kernel(q_ref, k_ref, v_ref, o_ref):
    q2 = q_ref[...]
    k2 = k_ref[...]
    v2 = v_ref[...]
    lane = lax.broadcasted_iota(I32, (1, q2.shape[1]), 1)
    first = lane < HEAD_DIM
    outs = []
    for hh in range(2):
        sel = first if hh == 0 else jnp.logical_not(first)
        s = _dot_nt(jnp.where(sel, q2, jnp.zeros_like(q2)), k2)
        p = jnp.exp(s - jnp.max(s, axis=1, keepdims=True))
        outs.append(_dot(p.astype(BF16), v2) / jnp.sum(p, axis=1, keepdims=True))
    o_ref[...] = jnp.where(first, outs[0], outs[1]).astype(o_ref.dtype)


def _context_attention(dm, q, k, v):
    lanes = 2 * HEAD_DIM
    spec = pl.BlockSpec((dm.C, lanes), lambda p, b: (b, p))
    return pl.pallas_call(
        _ctx_attn_kernel,
        grid=(q.shape[1] // lanes, dm.B),
        in_specs=[spec, spec, spec],
        out_specs=spec,
        out_shape=jax.ShapeDtypeStruct((dm.T_ctx, q.shape[1]), BF16),
        compiler_params=_cparams(("arbitrary", "arbitrary")),
        name="context_attention",
    )(q, k, v)


def _seq_block_maps(dm):
    nl = dm.N // SEQ_BLOCK
    base = dm.T_ctx // SEQ_BLOCK

    def fwd(b, j):
        return jnp.where(j == 0, b, base + b * nl + j - 1)

    def bwd(b, j):
        return jnp.where(j == 0, b, base + b * nl + nl - j)

    return fwd, bwd


def _hgrn_prepare(hq, z, lb, tri, last_row, mid_row):
    f = lb + (1.0 - lb) * jax.nn.sigmoid(z)
    fm = jnp.maximum(f, FORGET_FLOOR)
    g = jnp.log(fm)
    kk = 1.0 - fm
    gh, gl = _split(g)
    cum = _dot(tri, gh) + _dot(tri, gl)
    tail = cum[last_row:last_row + 1, :]
    e = cum - cum[mid_row:mid_row + 1, :]
    emax = jnp.max(jnp.max(jnp.abs(e), axis=1, keepdims=True), axis=0, keepdims=True)
    ec = jnp.clip(e, -HG_CLIP_EXP, HG_CLIP_EXP)
    qe = (hq * jnp.exp(ec)).astype(BF16)
    ke = (kk * jnp.exp(-ec)).astype(BF16)
    return kk, cum, tail, emax, qe, ke


def _hgrn_kernel(hqf_ref, zf_ref, vf_ref, hqb_ref, zb_ref, vb_ref, lbf_ref, lbb_ref,
                 of_ref, ob_ref, st_ref, a_ref, cum_ref, k_ref):
    j = pl.program_id(1)

    @pl.when(j == 0)
    def _():
        st_ref[...] = jnp.zeros_like(st_ref)

    CH = HG_CHUNK
    GW = HG_HEADS_PER_GROUP * HEAD_DIM
    ngroups = hqf_ref.shape[1] // GW
    nchunks = hqf_ref.shape[0] // CH
    t_i = lax.broadcasted_iota(I32, (CH, CH), 0)
    u_i = lax.broadcasted_iota(I32, (CH, CH), 1)
    tris = ((u_i <= t_i).astype(BF16), (u_i >= t_i).astype(BF16))
    t_w = lax.broadcasted_iota(I32, (CH, GW), 0)
    s_w = lax.broadcasted_iota(I32, (CH, GW), 1) % CH
    cmasks = (s_w <= t_w, s_w >= t_w)
    r_b = lax.broadcasted_iota(I32, (GW, GW), 0)
    c_b = lax.broadcasted_iota(I32, (GW, GW), 1)
    bm = (r_b // HEAD_DIM) == (c_b // HEAD_DIM)
    hsel_base = (r_b // HEAD_DIM) * HEAD_DIM
    dirs = ((hqf_ref, zf_ref, vf_ref, lbf_ref, of_ref), (hqb_ref, zb_ref, vb_ref, lbb_ref, ob_ref))
    chains = [(d, g) for d in range(2) for g in range(ngroups)]

    def expand(m):
        return jnp.where(bm, jnp.concatenate([m] * HG_HEADS_PER_GROUP, axis=0), jnp.zeros((GW, GW), m.dtype))

    def chunk_body(c, carry):
        rows = (pl.multiple_of(c * CH, CH), pl.multiple_of((nchunks - 1 - c) * CH, CH))
        prepared = []
        worst = jnp.zeros((1, 1), F32)
        for n, (d, g) in enumerate(chains):
            hq_r, z_r, v_r, lb_r, _ = dirs[d]
            ls = slice(g * GW, (g + 1) * GW)
            hq = hq_r[pl.ds(rows[d], CH), ls]
            z = z_r[pl.ds(rows[d], CH), ls]
            v = v_r[pl.ds(rows[d], CH), ls]
            last_row = CH - 1 if d == 0 else 0
            kk, cum, tail, emax, qe, ke = _hgrn_prepare(hq, z, lb_r[:, ls], tris[d], last_row, CH // 2)
            worst = jnp.maximum(worst, emax)
            a_ref[n] = _dot_nt(qe, expand(ke))
            cum_ref[n] = cum
            k_ref[n] = kk
            prepared.append((hq, v, kk, cum, tail))

        @pl.when(worst[0, 0] > HG_SAFE_EXP)
        def _():
            for n, (d, g) in enumerate(chains):
                hq_r = dirs[d][0]
                hq = hq_r[pl.ds(rows[d], CH), g * GW:(g + 1) * GW]
                cum = cum_ref[n]

                def key_body(s, acc):
                    cs = cum_ref[n, pl.ds(s, 1), :]
                    ksr = k_ref[n, pl.ds(s, 1), :]
                    p = hq * ksr * jnp.exp(jnp.minimum(cum - cs, 0.0))
                    hsel = jnp.where(c_b == hsel_base + s, 1.0, 0.0).astype(BF16)
                    return acc + _dot(p.astype(BF16), hsel)

                a_ref[n] = lax.fori_loop(0, CH, key_body, jnp.zeros((CH, GW), F32))

        for n, (d, g) in enumerate(chains):
            hq, v, kk, cum, tail = prepared[n]
            o_r = dirs[d][4]
            st = st_ref[n]
            a = jnp.where(cmasks[d], a_ref[n], 0.0).astype(BF16)
            vb = v.astype(BF16)
            o = _dot(a, expand(vb)) + _dot_nt((hq * jnp.exp(cum)).astype(BF16), st.astype(BF16))
            o_r[pl.ds(rows[d], CH), g * GW:(g + 1) * GW] = o
            k2 = (kk * jnp.exp(tail - cum)).astype(BF16)
            st_ref[n] = jnp.exp(tail) * st + jnp.where(bm, _dot_tn(vb, k2), 0.0)
        return carry

    lax.fori_loop(0, nchunks, chunk_body, 0)


def _hgrn2(dm, hq, ff, fb, hi, lb):
    B, T = dm.B, dm.T
    Wd = hq.shape[1]
    GW = HG_HEADS_PER_GROUP * HEAD_DIM
    nchain = 2 * (Wd // GW)
    nblk = 1 + dm.N // SEQ_BLOCK
    fwd, bwd = _seq_block_maps(dm)
    fspec = pl.BlockSpec((SEQ_BLOCK, Wd), lambda b, j: (fwd(b, j), 0))
    bspec = pl.BlockSpec((SEQ_BLOCK, Wd), lambda b, j: (bwd(b, j), 0))
    lbspec = pl.BlockSpec((1, Wd), lambda b, j: (0, 0))
    return pl.pallas_call(
        _hgrn_kernel,
        grid=(B, nblk),
        in_specs=[fspec, fspec, fspec, bspec, bspec, bspec, lbspec, lbspec],
        out_specs=[fspec, bspec],
        out_shape=[jax.ShapeDtypeStruct((T, Wd), F32)] * 2,
        scratch_shapes=[pltpu.VMEM((nchain, GW, GW), F32),
                        pltpu.VMEM((nchain, HG_CHUNK, GW), F32),
                        pltpu.VMEM((nchain, HG_CHUNK, GW), F32),
                        pltpu.VMEM((nchain, HG_CHUNK, GW), F32)],
        compiler_params=_cparams(("arbitrary", "arbitrary")),
        name="hgrn2_bidirectional",
    )(hq, ff, hi, hq, fb, hi, lb[0:1], lb[1:2])


def _lru_scan(a, x, reverse):
    n = a.shape[0]
    row = lax.broadcasted_iota(I32, a.shape, 0)
    s = 1
    while s < n:
        shift = (n - s) if reverse else s
        a_sh = pltpu.roll(a, shift, 0)
        x_sh = pltpu.roll(x, shift, 0)
        valid = (row < n - s) if reverse else (row >= s)
        x = jnp.where(valid, a * x_sh + x, x)
        a = jnp.where(valid, a * a_sh, a)
        s *= 2
    return a, x


def _lru_kernel(ucf_ref, upf_ref, unf_ref, ucb_ref, upb_ref, unb_ref, cw_ref, cb_ref, wbd_ref,
                ba_ref, bx_ref, lam_ref, hf_ref, hb_ref, ext_ref, carry_ref):
    j = pl.program_id(1)
    nblk = pl.num_programs(1)

    @pl.when(j == 0)
    def _():
        carry_ref[...] = jnp.zeros_like(carry_ref)

    TT, W = ucf_ref.shape
    GW = wbd_ref.shape[2]
    halo = upf_ref.shape[0]
    dirs = ((ucf_ref, upf_ref, unf_ref, hf_ref), (ucb_ref, upb_ref, unb_ref, hb_ref))
    for d, (uc_r, up_r, un_r, out_r) in enumerate(dirs):
        pos = j if d == 0 else jnp.where(j == 0, 0, nblk - j)
        keep_prev = jnp.where((pos == 0) | (pos == 1), 0.0, 1.0)
        keep_next = jnp.where((pos == 0) | (pos == nblk - 1), 0.0, 1.0)
        ext_ref[d, 0:halo, :] = up_r[...] * keep_prev
        ext_ref[d, halo:halo + TT, :] = uc_r[...]
        ext_ref[d, halo + TT:halo + TT + halo, :] = un_r[...] * keep_next
        for g in range(W // GW):
            ls = slice(g * GW, (g + 1) * GW)
            u = cb_ref[:, ls]
            for tap in range(CONV_W):
                off = halo - 2 + tap
                u = u + cw_ref[tap:tap + 1, ls] * ext_ref[d, off:off + TT, ls]
            gates = _dot(u.astype(BF16), wbd_ref[d, g])
            r = jax.nn.sigmoid(gates[:, :GW] + ba_ref[d:d + 1, ls])
            ig = jax.nn.sigmoid(gates[:, GW:] + bx_ref[d:d + 1, ls])
            nl = -lam_ref[d:d + 1, ls]
            softplus = jnp.maximum(nl, 0.0) + jnp.log1p(jnp.exp(-jnp.abs(nl)))
            log_a = -LRU_C * r * softplus
            a = jnp.exp(log_a)
            x_in = jnp.sqrt(1.0 - jnp.exp(2.0 * log_a)) * ig * u
            a_acc, h = _lru_scan(a, x_in, d == 1)
            h = h + a_acc * carry_ref[d, 0:1, ls]
            out_r[:, ls] = h
            last = 0 if d == 1 else TT - 1
            carry_ref[d, 0:1, ls] = h[last:last + 1, :]


def _block_diag_gates(w_a, w_x, group):
    ndir, K, d, _ = w_a.shape
    per = group // d
    eye = jnp.eye(per, dtype=w_a.dtype)

    def bd(w):
        w = w.reshape(ndir, K // per, per, d, d)
        full = jnp.einsum('ngkde,kl->ngkdle', w, eye)
        return full.reshape(ndir, K // per, group, group)

    return jnp.concatenate([bd(w_a), bd(w_x)], axis=-1)


def _rglru(dm, u, conv_w, conv_b, w_a, b_a, w_x, b_x, lam):
    B, T = dm.B, dm.T
    W = u.shape[1]
    GW = 256
    halo = 8
    nblk = 1 + dm.N // SEQ_BLOCK
    fwd, bwd = _seq_block_maps(dm)
    hb = SEQ_BLOCK // halo
    nh = T // halo
    wbd = _block_diag_gates(w_a, w_x, GW).astype(BF16)

    def cur(m):
        return pl.BlockSpec((SEQ_BLOCK, W), lambda b, j: (m(b, j), 0))

    def prev(m):
        return pl.BlockSpec((halo, W), lambda b, j: (jnp.maximum(m(b, j) * hb - 1, 0), 0))

    def nxt(m):
        return pl.BlockSpec((halo, W), lambda b, j: (jnp.minimum((m(b, j) + 1) * hb, nh - 1), 0))

    def full(a):
        nd = a.ndim
        return pl.BlockSpec(a.shape, lambda b, j: (0,) * nd)

    consts = (conv_w, conv_b.reshape(1, W), wbd, b_a, b_x, lam)
    return pl.pallas_call(
        _lru_kernel,
        grid=(B, nblk),
        in_specs=[cur(fwd), prev(fwd), nxt(fwd), cur(bwd), prev(bwd), nxt(bwd)] + [full(a) for a in consts],
        out_specs=[cur(fwd), cur(bwd)],
        out_shape=[jax.ShapeDtypeStruct((T, W), F32)] * 2,
        scratch_shapes=[pltpu.VMEM((2, SEQ_BLOCK + 2 * halo, W), F32), pltpu.VMEM((2, 8, W), F32)],
        compiler_params=_cparams(("arbitrary", "arbitrary")),
        name="rglru_bidirectional",
    )(u, u, u, u, u, u, *consts)


def _post_mixer_epilogue(o, x_ref, m2_ref, m3_ref, m4_ref, lng_ref, lnb_ref, rwh_ref, rwl_ref,
                         xo_ref, h2_ref, lg_ref, alpha):
    z = alpha * x_ref[...] + m2_ref[...] * o
    xn = _layer_norm(z, lng_ref[...], lnb_ref[...])
    xo_ref[...] = xn
    h2 = xn * (1.0 + m4_ref[...]) + m3_ref[...]
    h2_ref[...] = h2
    hh, hl = _split(h2)
    lg_ref[...] = _dot_nt(rwh_ref[...], hh) + _dot_nt(rwh_ref[...], hl) + _dot_nt(rwl_ref[...], hh)


def _post_even_kernel(nac_ref, nal_ref, of_ref, ob_ref, hg_ref, ng_ref, w_ref, *rest, alpha, ctx_tiles):
    na = jnp.where(pl.program_id(0) < ctx_tiles, nac_ref[...], nal_ref[...])
    o = of_ref[...] + ob_ref[...]
    wd = o.shape[1]
    r_i = lax.broadcasted_iota(I32, (wd, wd), 0) // HEAD_DIM
    c_i = lax.broadcasted_iota(I32, (wd, wd), 1) // HEAD_DIM
    avg = jnp.where(r_i == c_i, 1.0 / HEAD_DIM, 0.0).astype(BF16)
    sh, sl = _split(o * o)
    ms = _dot(sh, avg) + _dot(sl, avg)
    r = o * lax.rsqrt(ms + RMS_EPS) * ng_ref[...] * _silu(hg_ref[...])
    mix = _dot(na, w_ref[:wd, :]) + _dot(r.astype(BF16), w_ref[wd:, :])
    _post_mixer_epilogue(mix, *rest, alpha=alpha)


def _post_odd_kernel(hf_ref, hb_ref, y_ref, w_ref, *rest, alpha):
    m = ((hf_ref[...] + hb_ref[...]) * _gelu_tanh(y_ref[...])).astype(BF16)
    _post_mixer_epilogue(_dot(m, w_ref[...]), *rest, alpha=alpha)


def _post_mixer(dm, mixer_inputs, w_out, xs, m2, m3, m4, lng, lnb, rwh, rwl, alpha, even):
    T, D, TM = dm.T, dm.D, dm.TM
    E = rwh.shape[0]
    row = lambda a: pl.BlockSpec((TM, a.shape[1]), lambda i: (i, 0))
    const = lambda a: pl.BlockSpec(a.shape, lambda i: (0,) * a.ndim)
    mod_spec = pl.BlockSpec((None, 1, D), lambda i: (dm.group(i, TM), 0, 0))
    if even:
        nac, nal, of, ob, hg, ng = mixer_inputs
        nct = dm.T_ctx // TM
        ins = [nac, nal, of, ob, hg, ng, w_out]
        specs = [pl.BlockSpec((TM, nac.shape[1]), lambda i: (jnp.minimum(i, nct - 1), 0)),
                 pl.BlockSpec((TM, nal.shape[1]), lambda i: (jnp.maximum(i - nct, 0), 0)),
                 row(of), row(ob), row(hg), const(ng), const(w_out)]
        kern = functools.partial(_post_even_kernel, alpha=alpha, ctx_tiles=nct)
    else:
        hf, hb, y = mixer_inputs
        ins = [hf, hb, y, w_out]
        specs = [row(hf), row(hb), row(y), const(w_out)]
        kern = functools.partial(_post_odd_kernel, alpha=alpha)
    ins += [xs, m2, m3, m4, lng, lnb, rwh, rwl]
    specs += [row(xs), mod_spec, mod_spec, mod_spec, const(lng), const(lnb), const(rwh), const(rwl)]
    return pl.pallas_call(
        kern,
        grid=(T // TM,),
        in_specs=specs,
        out_specs=[pl.BlockSpec((TM, D), lambda i: (i, 0)), pl.BlockSpec((TM, D), lambda i: (i, 0)),
                   pl.BlockSpec((E, TM), lambda i: (0, i))],
        out_shape=[jax.ShapeDtypeStruct((T, D), F32), jax.ShapeDtypeStruct((T, D), F32),
                   jax.ShapeDtypeStruct((E, T), F32)],
        compiler_params=_cparams(("parallel",)),
        name="post_mixer_even" if even else "post_mixer_odd",
    )(*ins)


def _route_kernel(lg_ref, bias_ref, tri_ref, idx_ref, rank_ref, w_ref, cnt_ref, carry_ref):
    i = pl.program_id(0)

    @pl.when(i == 0)
    def _():
        carry_ref[...] = jnp.zeros_like(carry_ref)

    E, TK = lg_ref.shape
    s = jax.nn.sigmoid(lg_ref[...])
    work = s + bias_ref[...]
    eio = lax.broadcasted_iota(I32, (E, TK), 0).astype(F32)
    picked = jnp.zeros((E, TK), F32)
    sels, idxs = [], []
    for _ in range(TOP_K):
        m = jnp.max(work, axis=0, keepdims=True)
        ik = jnp.min(jnp.where(work == m, eio, float(E)), axis=0, keepdims=True)
        oh = eio == ik
        sels.append(jnp.sum(jnp.where(oh, s, 0.0), axis=0, keepdims=True))
        idxs.append(ik)
        picked = jnp.where(oh, 1.0, picked)
        work = jnp.where(oh, -jnp.inf, work)
    total = sels[0]
    for sk in sels[1:]:
        total = total + sk
    carry = carry_ref[:, 0:1]
    ranks = carry + _dot(picked.astype(BF16), tri_ref[...])
    out_rows = idx_ref.shape[0]
    rio = lax.broadcasted_iota(I32, (out_rows, TK), 0)
    idx_o = jnp.zeros((out_rows, TK), I32)
    rank_o = jnp.zeros((out_rows, TK), I32)
    w_o = jnp.zeros((out_rows, TK), F32)
    for kx in range(TOP_K):
        rk = jnp.sum(jnp.where(eio == idxs[kx], ranks, 0.0), axis=0, keepdims=True).astype(I32)
        idx_o = jnp.where(rio == kx, idxs[kx].astype(I32), idx_o)
        rank_o = jnp.where(rio == kx, rk, rank_o)
        w_o = jnp.where(rio == kx, sels[kx] / total * ROUTED_SCALE, w_o)
    idx_ref[...] = idx_o
    rank_ref[...] = rank_o
    w_ref[...] = w_o
    new_carry = carry + jnp.sum(picked, axis=1, keepdims=True)
    carry_ref[...] = jnp.broadcast_to(new_carry, carry_ref.shape)
    cnt_ref[...] = jnp.broadcast_to(new_carry, cnt_ref.shape)


def _route(logits_t, router_bias, t0):
    E, T = logits_t.shape
    Tm = T - t0
    TK = _pick_tile(int(np.gcd(Tm, t0)), (1024, 512, 256))
    off = t0 // TK
    tri = jnp.asarray(np.triu(np.ones((TK, TK), np.float32), 1), BF16)
    tok = lambda: pl.BlockSpec((8, TK), lambda i: (0, i))
    idx, rank, w, cnt = pl.pallas_call(
        _route_kernel,
        grid=(Tm // TK,),
        in_specs=[pl.BlockSpec((E, TK), lambda i: (0, i + off)),
                  pl.BlockSpec((E, 1), lambda i: (0, 0)),
                  pl.BlockSpec((TK, TK), lambda i: (0, 0))],
        out_specs=[tok(), tok(), tok(), pl.BlockSpec((E, 128), lambda i: (0, 0))],
        out_shape=[jax.ShapeDtypeStruct((8, Tm), I32), jax.ShapeDtypeStruct((8, Tm), I32),
                   jax.ShapeDtypeStruct((8, Tm), F32), jax.ShapeDtypeStruct((E, 128), F32)],
        scratch_shapes=[pltpu.VMEM((E, 128), F32)],
        compiler_params=_cparams(("arbitrary",)),
        name="moe_route",
    )(logits_t, router_bias.reshape(E, 1), tri)
    return idx, rank, w, cnt[:, 0].astype(I32)


def _dispatch_kernel(dest_hbm, h_ref, xs_hbm, dsm, sem_d, sem):
    i = pl.program_id(0)
    TMd = h_ref.shape[0]
    n = TMd * TOP_K
    cp = pltpu.make_async_copy(dest_hbm.at[pl.ds(pl.multiple_of(i * n, 1024), n)], dsm, sem_d)
    cp.start()
    cp.wait()

    def start_body(t, c):
        for kx in range(TOP_K):
            d = dsm[t * TOP_K + kx]
            pltpu.make_async_copy(h_ref.at[pl.ds(t, 1)], xs_hbm.at[pl.ds(d, 1)], sem).start(priority=kx % 2)
        return c

    lax.fori_loop(0, TMd, start_body, 0, unroll=2)
    for _ in range(TOP_K):
        pltpu.make_async_copy(h_ref, xs_hbm.at[pl.ds(0, TMd)], sem).wait()


def _dispatch(h2, dest_flat, t0, n_slots):
    T, D = h2.shape
    TMd = 512
    Tm = T - t0
    assert Tm % TMd == 0 and t0 % TMd == 0
    off = t0 // TMd
    n = TMd * TOP_K
    return pl.pallas_call(
        _dispatch_kernel,
        grid=(Tm // TMd,),
        in_specs=[pl.BlockSpec(memory_space=pl.ANY),
                  pl.BlockSpec((TMd, D), lambda i: (i + off, 0))],
        out_specs=pl.BlockSpec(memory_space=pl.ANY),
        out_shape=jax.ShapeDtypeStruct((n_slots, D), F32),
        scratch_shapes=[pltpu.SMEM((n,), I32), pltpu.SemaphoreType.DMA, pltpu.SemaphoreType.DMA],
        compiler_params=_cparams(("arbitrary",)),
        name="moe_dispatch",
    )(dest_flat, h2)


def _gmm_kernel(bexp_ref, bvalid_ref, x_ref, wgu_ref, wd_ref, y_ref):
    del bexp_ref
    i = pl.program_id(0)
    hid = wd_ref.shape[0]
    nvalid = bvalid_ref[i]

    @pl.when(nvalid > 0)
    def _():
        row = lax.broadcasted_iota(I32, x_ref.shape, 0)
        xb = jnp.where(row < nvalid, x_ref[...], 0.0).astype(BF16)
        gu = _dot(xb, wgu_ref[...])
        act = _silu(gu[:, :hid]) * gu[:, hid:]
        y_ref[...] = _dot(act.astype(BF16), wd_ref[...])

    @pl.when(nvalid == 0)
    def _():
        y_ref[...] = jnp.zeros_like(y_ref)


def _grouped_experts(xsorted, block_exp, block_valid, wgu, wd):
    NS, D = xsorted.shape
    nb = NS // MOE_BLOCK
    hid = wd.shape[1]
    grid_spec = pltpu.PrefetchScalarGridSpec(
        num_scalar_prefetch=2,
        grid=(nb,),
        in_specs=[pl.BlockSpec((MOE_BLOCK, D), lambda i, be, bv: (i, 0)),
                  pl.BlockSpec((None, D, 2 * hid), lambda i, be, bv: (be[i], 0, 0)),
                  pl.BlockSpec((None, hid, D), lambda i, be, bv: (be[i], 0, 0))],
        out_specs=pl.BlockSpec((MOE_BLOCK, D), lambda i, be, bv: (i, 0)),
    )
    return pl.pallas_call(
        _gmm_kernel,
        grid_spec=grid_spec,
        out_shape=jax.ShapeDtypeStruct((NS, D), F32),
        compiler_params=_cparams(("arbitrary",)),
        name="moe_grouped_experts",
    )(block_exp, block_valid, xsorted, wgu, wd)


def _combine_kernel(dest_hbm, y_hbm, w_ref, h_ref, sgu_ref, sd_ref, x_ref, m5_ref, lng_ref, lnb_ref,
                    o_ref, dsm, gbuf, sem_d, sem, *, alpha):
    i = pl.program_id(0)
    TMc = h_ref.shape[0]
    n = TMc * TOP_K
    cp = pltpu.make_async_copy(dest_hbm.at[pl.ds(pl.multiple_of(i * n, 1024), n)], dsm, sem_d)
    cp.start()
    cp.wait()

    def start_body(t, c):
        for kx in range(TOP_K):
            d = dsm[t * TOP_K + kx]
            pltpu.make_async_copy(y_hbm.at[pl.ds(d, 1)], gbuf.at[kx, pl.ds(t, 1)], sem).start(priority=kx % 2)
        return c

    lax.fori_loop(0, TMc, start_body, 0, unroll=2)

    hid = sd_ref.shape[0]
    gu = _dot(h_ref[...].astype(BF16), sgu_ref[...])
    y = _dot((_silu(gu[:, :hid]) * gu[:, hid:]).astype(BF16), sd_ref[...])

    for kx in range(TOP_K):
        pltpu.make_async_copy(y_hbm.at[pl.ds(0, TMc)], gbuf.at[kx], sem).wait()
    w = w_ref[...]
    for kx in range(TOP_K):
        y = y + w[:, kx:kx + 1] * gbuf[kx]
    z = alpha * x_ref[...] + m5_ref[...] * y
    o_ref[...] = _layer_norm(z, lng_ref[...], lnb_ref[...])


def _combine(dm, dest_flat, ysorted, w_tm, h2, sgu, sd, xs, m5, lng, lnb, alpha, t0):
    T, D = h2.shape
    TMc = 512
    Tm = T - t0
    off = t0 // TMc
    n = TMc * TOP_K
    const = lambda a: pl.BlockSpec(a.shape, lambda i: (0,) * a.ndim)
    row = lambda a: pl.BlockSpec((TMc, a.shape[1]), lambda i: (i + off, 0))
    return pl.pallas_call(
        functools.partial(_combine_kernel, alpha=alpha),
        grid=(Tm // TMc,),
        in_specs=[pl.BlockSpec(memory_space=pl.ANY), pl.BlockSpec(memory_space=pl.ANY),
                  pl.BlockSpec((TMc, 8), lambda i: (i, 0)), row(h2), const(sgu), const(sd), row(xs),
                  pl.BlockSpec((None, 1, D), lambda i: (dm.group(i + off, TMc), 0, 0)),
                  const(lng), const(lnb)],
        out_specs=pl.BlockSpec((TMc, D), lambda i: (i, 0)),
        out_shape=jax.ShapeDtypeStruct((Tm, D), F32),
        scratch_shapes=[pltpu.SMEM((n,), I32), pltpu.VMEM((TOP_K, TMc, D), F32),
                        pltpu.SemaphoreType.DMA, pltpu.SemaphoreType.DMA],
        compiler_params=_cparams(("arbitrary",)),
        name="moe_combine",
    )(dest_flat, ysorted, w_tm, h2, sgu, sd, xs, m5, lng, lnb)


def _moe(dm, h2, logits_t, xs, m5, lng, lnb, router_bias, wgu, wd, sgu, sd, alpha, t0):
    T, D = h2.shape
    E = wgu.shape[0]
    Tm = T - t0
    idx, rank, w, counts = _route(logits_t, router_bias, t0)
    nb = -(-(Tm * TOP_K) // MOE_BLOCK) + E
    padded = (counts + MOE_BLOCK - 1) // MOE_BLOCK * MOE_BLOCK
    pends = jnp.cumsum(padded)
    pstarts = pends - padded
    eids = jnp.arange(E, dtype=I32)
    bstart = jnp.arange(nb, dtype=I32) * MOE_BLOCK
    block_exp = jnp.minimum(jnp.sum((pends[None, :] <= bstart[:, None]).astype(I32), axis=1), E - 1)
    of_block = block_exp[:, None] == eids[None, :]
    seg_end = jnp.sum(jnp.where(of_block, (pstarts + counts)[None, :], 0), axis=1)
    block_valid = jnp.clip(seg_end - bstart, 0, MOE_BLOCK).astype(I32)
    of_tok = idx[:TOP_K, :, None] == eids[None, None, :]
    dest = rank[:TOP_K] + jnp.sum(jnp.where(of_tok, pstarts[None, None, :], 0), axis=2)
    dest_flat = dest.T.reshape(-1).astype(I32)
    w_tm = jnp.pad(w[:TOP_K].T, ((0, 0), (0, 8 - TOP_K)))
    xsorted = _dispatch(h2, dest_flat, t0, nb * MOE_BLOCK)
    ysorted = _grouped_experts(xsorted, block_exp, block_valid, wgu, wd)
    return _combine(dm, dest_flat, ysorted, w_tm, h2, sgu, sd, xs, m5, lng, lnb, alpha, t0)


def kernel(x, c, ctx, c_ctx, ada_w, ada_b, ln_g, ln_b, ev_w_in, ev_w_out, na_rpb, hg_lb_raw, hg_norm_g, od_w_in, od_conv_w, od_conv_b, lru_w_a, lru_b_a, lru_w_x, lru_b_x, lru_lam, od_w_out, router_w, router_bias, exp_w_gate, exp_w_up, exp_w_down, sh_w_gate, sh_w_up, sh_w_down):
    B, N, D = x.shape
    C = ctx.shape[1]
    depth = ada_w.shape[0]
    dm = _Dims(B, N, C, D)
    alpha = float((2 * depth) ** 0.25)
    assert B + 1 <= 8

    cvec = jnp.zeros((8, D), F32).at[:B].set(c).at[B].set(c_ctx)
    mod = _modulation(cvec, ada_w, ada_b).reshape(depth, 8, 6, 1, D)

    p_lb = jax.nn.softmax(hg_lb_raw.astype(F32), axis=1)
    hg_lb = jnp.cumsum(p_lb, axis=1) - p_lb[:, :1]

    xs = jnp.concatenate([ctx.reshape(B * C, D), x.reshape(B * N, D)], axis=0)
    for l in range(depth):
        jl = l // 2
        last = l == depth - 1
        m = [mod[l, :, t] for t in range(6)]
        rw_t = router_w[l].T
        rwh, rwl = _split(rw_t)
        lng = ln_g[l][:, None, :]
        lnb = ln_b[l][:, None, :]
        if l % 2 == 0:
            q, k, v, hq, ff, fb, hi, hg = _inproj(dm, xs, m[1], m[0], ev_w_in[jl].astype(BF16), True)
            bias = _na_bias_tables(na_rpb[jl], N // GRID_W)
            na_lat = _neighbourhood_attention(dm, q, k, v, bias)
            na_ctx = _context_attention(dm, q, k, v)
            o_f, o_b = _hgrn2(dm, hq, ff, fb, hi, hg_lb[:, jl])
            ng = jnp.tile(hg_norm_g[jl], hq.shape[1] // HEAD_DIM)[None, :]
            mixer_inputs = (na_ctx, na_lat, o_f, o_b, hg, ng)
            w_out = ev_w_out[jl].astype(BF16)
        else:
            y, u = _inproj(dm, xs, m[1], m[0], od_w_in[jl].astype(BF16), False)
            h_f, h_b = _rglru(dm, u, od_conv_w[jl], od_conv_b[jl], lru_w_a[jl], lru_b_a[jl],
                              lru_w_x[jl], lru_b_x[jl], lru_lam[jl])
            mixer_inputs = (h_f, h_b, y)
            w_out = od_w_out[jl].astype(BF16)
        xs, h2, logits_t = _post_mixer(dm, mixer_inputs, w_out, xs, m[2], m[3], m[4], lng[0], lnb[0],
                                       rwh, rwl, alpha, l % 2 == 0)
        wgu = jnp.concatenate([exp_w_gate[l], exp_w_up[l]], axis=-1).astype(BF16)
        wd = exp_w_down[l].astype(BF16)
        sgu = jnp.concatenate([sh_w_gate[l], sh_w_up[l]], axis=-1).astype(BF16)
        sd = sh_w_down[l].astype(BF16)
        t0 = dm.T_ctx if last else 0
        xs = _moe(dm, h2, logits_t, xs, m[5], lng[1], lnb[1], router_bias[l], wgu, wd, sgu, sd, alpha, t0)
    return xs.reshape(B, N, D)
```

```python
import functools

import numpy as np
import jax
import jax.numpy as jnp
from jax import lax
from jax.experimental import pallas as pl
from jax.experimental.pallas import tpu as pltpu

F32 = jnp.float32
BF16 = jnp.bfloat16
I32 = jnp.int32

HEAD_DIM = 64
GRID_W = 64
NA_WIN_ROWS = 8
NA_WIN_COLS = 16
NA_QROWS = 8
NA_KROWS = 16
HG_CHUNK = 64
HG_HEADS_PER_GROUP = 4
FORGET_FLOOR = 1e-30
HG_SAFE_EXP = 80.0
HG_CLIP_EXP = 85.0
LRU_C = 8.0
LRU_BLOCKS = 16
CONV_W = 4
TOP_K = 6
ROUTED_SCALE = 2.5
LN_EPS = 1e-5
RMS_EPS = 1e-6
SEQ_BLOCK = 256
MOE_BLOCK = 512
MASK_VALUE = -1e30
VMEM_LIMIT = 56 * 1024 * 1024


def _cparams(sem):
    return pltpu.CompilerParams(dimension_semantics=sem, vmem_limit_bytes=VMEM_LIMIT)


def _split(a):
    hi = a.astype(BF16)
    lo = (a - hi.astype(F32)).astype(BF16)
    return hi, lo


def _dot(a, b):
    return jnp.dot(a, b, preferred_element_type=F32)


def _dot_nt(a, b):
    return lax.dot_general(a, b, (((1,), (1,)), ((), ())), preferred_element_type=F32)


def _dot_tn(a, b):
    return lax.dot_general(a, b, (((0,), (0,)), ((), ())), preferred_element_type=F32)


def _dot3(a, b):
    ah, al = _split(a)
    bh, bl = _split(b)
    return _dot(ah, bh) + _dot(al, bh) + _dot(ah, bl)


def _pack_rows(h):
    half = h.shape[1] // 2
    bits = pltpu.bitcast(h.astype(BF16).astype(F32), jnp.uint32)
    return (bits[:, :half] >> 16) | (bits[:, half:] & jnp.uint32(0xFFFF0000))


def _unpack_rows(p):
    lo = pltpu.bitcast(p << 16, F32)
    hi = pltpu.bitcast(p & jnp.uint32(0xFFFF0000), F32)
    return jnp.concatenate([lo, hi], axis=1).astype(BF16)


def _silu(v):
    return v * jax.nn.sigmoid(v)


def _gelu_tanh(v):
    return 0.5 * v * (1.0 + jnp.tanh(0.7978845608028654 * (v + 0.044715 * v * v * v)))


def _layer_norm(z, g, b):
    mu = jnp.mean(z, axis=-1, keepdims=True)
    zc = z - mu
    var = jnp.mean(zc * zc, axis=-1, keepdims=True)
    return zc * lax.rsqrt(var + LN_EPS) * g + b


def _pick_tile(n, cands):
    for c in cands:
        if n % c == 0:
            return c
    raise ValueError(f"no tile for {n}")


class _Dims:
    def __init__(self, B, N, C, D):
        self.B, self.N, self.C, self.D = B, N, C, D
        self.T_ctx = B * C
        self.T = B * C + B * N
        assert C == SEQ_BLOCK and N % SEQ_BLOCK == 0
        assert N % GRID_W == 0 and (N // GRID_W) % NA_QROWS == 0 and N // GRID_W >= NA_KROWS
        assert self.T_ctx % 512 == 0
        self.TM = _pick_tile(self.T_ctx, (512, 256))
        assert N % self.TM == 0

    def group(self, i, tile):
        start = i * tile
        return jnp.where(start < self.T_ctx, self.B, (start - self.T_ctx) // self.N)


def _mod_kernel(c_ref, w_ref, b_ref, o_ref):
    o_ref[...] = _dot3(_silu(c_ref[...]), w_ref[...]) + b_ref[...]


def _modulation(cvec, ada_w, ada_b):
    L, D, W6 = ada_w.shape
    nc = 1536
    return pl.pallas_call(
        _mod_kernel,
        grid=(L, W6 // nc),
        in_specs=[
            pl.BlockSpec((8, D), lambda l, j: (0, 0)),
            pl.BlockSpec((None, D, nc), lambda l, j: (l, 0, j)),
            pl.BlockSpec((None, 1, nc), lambda l, j: (l, 0, j)),
        ],
        out_specs=pl.BlockSpec((None, 8, nc), lambda l, j: (l, 0, j)),
        out_shape=jax.ShapeDtypeStruct((L, 8, W6), F32),
        compiler_params=_cparams(("arbitrary", "arbitrary")),
        name="adaln_modulation",
    )(cvec, ada_w, ada_b.reshape(L, 1, W6))


def _inproj_even_kernel(x_ref, sc_ref, sh_ref, w_ref, q_ref, k_ref, v_ref, hq_ref, ff_ref, fb_ref, hi_ref, hg_ref):
    h = (x_ref[...] * (1.0 + sc_ref[...]) + sh_ref[...]).astype(BF16)
    wd = q_ref.shape[1]

    def part(j):
        return _dot(h, w_ref[:, j * wd:(j + 1) * wd])

    scale = HEAD_DIM ** -0.5
    q_ref[...] = (part(0) * scale).astype(BF16)
    k_ref[...] = part(1).astype(BF16)
    v_ref[...] = part(2).astype(BF16)
    hq_ref[...] = _silu(part(3)) * scale
    ff_ref[...] = part(4)
    fb_ref[...] = part(5)
    hi_ref[...] = part(6)
    hg_ref[...] = part(7)


def _inproj_odd_kernel(x_ref, sc_ref, sh_ref, w_ref, y_ref, u_ref):
    h = (x_ref[...] * (1.0 + sc_ref[...]) + sh_ref[...]).astype(BF16)
    wd = y_ref.shape[1]
    y_ref[...] = _dot(h, w_ref[:, :wd])
    u_ref[...] = _dot(h, w_ref[:, wd:])


def _inproj(dm, xs, sc, sh, w, even):
    T, D, TM = dm.T, dm.D, dm.TM
    wtot = w.shape[1]
    mod_spec = pl.BlockSpec((None, 1, D), lambda i: (dm.group(i, TM), 0, 0))
    in_specs = [pl.BlockSpec((TM, D), lambda i: (i, 0)), mod_spec, mod_spec,
                pl.BlockSpec((D, wtot), lambda i: (0, 0))]
    if even:
        wd = wtot // 8
        dts = [BF16] * 3 + [F32] * 5
        kern = _inproj_even_kernel
    else:
        wd = wtot // 2
        dts = [F32] * 2
        kern = _inproj_odd_kernel
    return pl.pallas_call(
        kern,
        grid=(T // TM,),
        in_specs=in_specs,
        out_specs=[pl.BlockSpec((TM, wd), lambda i: (i, 0)) for _ in dts],
        out_shape=[jax.ShapeDtypeStruct((T, wd), dt) for dt in dts],
        compiler_params=_cparams(("parallel",)),
        name="inproj_even" if even else "inproj_odd",
    )(xs, sc, sh, w)


def _na_bias_tables(rpb, rows):
    W = GRID_W
    H = rpb.shape[0]
    nr, nc = 2 * NA_WIN_ROWS - 1, 2 * NA_WIN_COLS - 1
    hp = lax.Precision.HIGHEST
    c = np.arange(W)[:, None]
    kc = np.arange(W)[None, :]
    c0 = np.clip(c - NA_WIN_COLS // 2, 0, W - NA_WIN_COLS)
    valid_c = (kc >= c0) & (kc < c0 + NA_WIN_COLS)
    dc = np.clip(kc - c + NA_WIN_COLS - 1, 0, nc - 1)
    oh_c = (dc.reshape(-1, 1) == np.arange(nc)[None, :]).astype(np.float32)
    by_col = jnp.einsum('hrs,xs->hrx', rpb.astype(F32), jnp.asarray(oh_c), precision=hp)
    tabs = []
    for rbase, kb in ((0, 0), (NA_QROWS, NA_QROWS - NA_WIN_ROWS // 2), (rows - NA_QROWS, rows - NA_KROWS)):
        r = rbase + np.arange(NA_QROWS)[:, None]
        kr = kb + np.arange(NA_KROWS)[None, :]
        r0 = np.clip(r - NA_WIN_ROWS // 2, 0, rows - NA_WIN_ROWS)
        valid_r = (kr >= r0) & (kr < r0 + NA_WIN_ROWS)
        dr = np.clip(kr - r + NA_WIN_ROWS - 1, 0, nr - 1)
        oh_r = (dr.reshape(-1, 1) == np.arange(nr)[None, :]).astype(np.float32)
        t = jnp.einsum('yr,hrx->hyx', jnp.asarray(oh_r), by_col, precision=hp)
        t = t.reshape(H, NA_QROWS, NA_KROWS, W, W).transpose(0, 1, 3, 2, 4)
        valid = valid_r[:, None, :, None] & valid_c[None, :, None, :]
        t = jnp.where(jnp.asarray(valid)[None], t, MASK_VALUE)
        tabs.append(t.reshape(H, NA_QROWS * W, NA_KROWS * W))
    return jnp.stack(tabs).astype(F32)


def _na_kernel(q_ref, k0, k1, k2, k3, v0, v1, v2, v3, kc_ref, vc_ref, bias_ref, o_ref):
    q2 = q_ref[...]
    lane = lax.broadcasted_iota(I32, (1, q2.shape[1]), 1)
    first = lane < HEAD_DIM
    ks = [r[...] for r in (k0, k1, k2, k3)]
    vs = [r[...] for r in (v0, v1, v2, v3)]
    kc = kc_ref[...]
    vc = vc_ref[...]
    kb = ks[0].shape[0]
    outs = []
    for hh in range(2):
        sel = first if hh == 0 else jnp.logical_not(first)
        qh = jnp.where(sel, q2, jnp.zeros_like(q2))
        s_loc = jnp.concatenate([_dot_nt(qh, kj) for kj in ks], axis=1) + bias_ref[hh]
        s_ctx = _dot_nt(qh, kc)
        m = jnp.maximum(jnp.max(s_loc, axis=1, keepdims=True), jnp.max(s_ctx, axis=1, keepdims=True))
        p_loc = jnp.exp(s_loc - m)
        p_ctx = jnp.exp(s_ctx - m)
        denom = jnp.sum(p_loc, axis=1, keepdims=True) + jnp.sum(p_ctx, axis=1, keepdims=True)
        o = _dot(p_ctx.astype(BF16), vc)
        for j in range(4):
            o = o + _dot(p_loc[:, j * kb:(j + 1) * kb].astype(BF16), vs[j])
        outs.append(o / denom)
    o_ref[...] = jnp.where(first, outs[0], outs[1]).astype(o_ref.dtype)


def _neighbourhood_attention(dm, q, k, v, bias):
    B, N, C, T = dm.B, dm.N, dm.C, dm.T
    H2 = q.shape[1] // (2 * HEAD_DIM)
    rows = N // GRID_W
    nqb = rows // NA_QROWS
    QB = NA_QROWS * GRID_W
    KB = QB // 2
    ngroups = N // KB
    lat_q0 = dm.T_ctx // QB
    lat_k0 = dm.T_ctx // KB

    def q_map(p, i, b):
        return (lat_q0 + b * (N // QB) + i, p)

    def o_map(p, i, b):
        return (b * (N // QB) + i, p)

    def kv_map(j):
        def f(p, i, b):
            gs = jnp.clip(2 * i - 1, 0, ngroups - 4)
            return (lat_k0 + b * ngroups + gs + j, p)
        return f

    def ctx_map(p, i, b):
        return (b, p)

    def bias_map(p, i, b):
        var = jnp.where(i == 0, 0, jnp.where(i == nqb - 1, 2, 1))
        return (var, p, 0, 0)

    lanes = 2 * HEAD_DIM
    kv_specs = [pl.BlockSpec((KB, lanes), kv_map(j)) for j in range(4)]
    return pl.pallas_call(
        _na_kernel,
        grid=(H2, nqb, B),
        in_specs=[pl.BlockSpec((QB, lanes), q_map)] + kv_specs + kv_specs
        + [pl.BlockSpec((C, lanes), ctx_map), pl.BlockSpec((C, lanes), ctx_map),
           pl.BlockSpec((None, 2, QB, NA_KROWS * GRID_W), bias_map)],
        out_specs=pl.BlockSpec((QB, lanes), o_map),
        out_shape=jax.ShapeDtypeStruct((B * N, q.shape[1]), BF16),
        compiler_params=_cparams(("arbitrary", "arbitrary", "arbitrary")),
        name="neighbourhood_attention",
    )(q, k, k, k, k, v, v, v, v, k, v, bias)


def _ctx_attn_kernel(q_ref, k_ref, v_ref, o_ref):
    q2 = q_ref[...]
    k2 = k_ref[...]
    v2 = v_ref[...]
    lane = lax.broadcasted_iota(I32, (1, q2.shape[1]), 1)
    first = lane < HEAD_DIM
    outs = []
    for hh in range(2):
        sel = first if hh == 0 else jnp.logical_not(first)
        s = _dot_nt(jnp.where(sel, q2, jnp.zeros_like(q2)), k2)
        p = jnp.exp(s - jnp.max(s, axis=1, keepdims=True))
        outs.append(_dot(p.astype(BF16), v2) / jnp.sum(p, axis=1, keepdims=True))
    o_ref[...] = jnp.where(first, outs[0], outs[1]).astype(o_ref.dtype)


def _context_attention(dm, q, k, v):
    lanes = 2 * HEAD_DIM
    spec = pl.BlockSpec((dm.C, lanes), lambda p, b: (b, p))
    return pl.pallas_call(
        _ctx_attn_kernel,
        grid=(q.shape[1] // lanes, dm.B),
        in_specs=[spec, spec, spec],
        out_specs=spec,
        out_shape=jax.ShapeDtypeStruct((dm.T_ctx, q.shape[1]), BF16),
        compiler_params=_cparams(("arbitrary", "arbitrary")),
        name="context_attention",
    )(q, k, v)


def _seq_block_maps(dm):
    nl = dm.N // SEQ_BLOCK
    base = dm.T_ctx // SEQ_BLOCK

    def fwd(b, j):
        return jnp.where(j == 0, b, base + b * nl + j - 1)

    def bwd(b, j):
        return jnp.where(j == 0, b, base + b * nl + nl - j)

    return fwd, bwd


def _hgrn_prepare(hq, z, lb, tri, last_row, mid_row):
    f = lb + (1.0 - lb) * jax.nn.sigmoid(z)
    fm = jnp.maximum(f, FORGET_FLOOR)
    g = jnp.log(fm)
    kk = 1.0 - fm
    gh, gl = _split(g)
    cum = _dot(tri, gh) + _dot(tri, gl)
    tail = cum[last_row:last_row + 1, :]
    e = cum - cum[mid_row:mid_row + 1, :]
    emax = jnp.max(jnp.max(jnp.abs(e), axis=1, keepdims=True), axis=0, keepdims=True)
    ec = jnp.clip(e, -HG_CLIP_EXP, HG_CLIP_EXP)
    qe = (hq * jnp.exp(ec)).astype(BF16)
    ke = (kk * jnp.exp(-ec)).astype(BF16)
    return kk, cum, tail, emax, qe, ke


def _hgrn_kernel(hqf_ref, zf_ref, vf_ref, hqb_ref, zb_ref, vb_ref, lbf_ref, lbb_ref,
                 of_ref, ob_ref, st_ref, a_ref, cum_ref, k_ref):
    j = pl.program_id(1)

    @pl.when(j == 0)
    def _():
        st_ref[...] = jnp.zeros_like(st_ref)

    CH = HG_CHUNK
    GW = HG_HEADS_PER_GROUP * HEAD_DIM
    ngroups = hqf_ref.shape[1] // GW
    nchunks = hqf_ref.shape[0] // CH
    t_i = lax.broadcasted_iota(I32, (CH, CH), 0)
    u_i = lax.broadcasted_iota(I32, (CH, CH), 1)
    tris = ((u_i <= t_i).astype(BF16), (u_i >= t_i).astype(BF16))
    t_w = lax.broadcasted_iota(I32, (CH, GW), 0)
    s_w = lax.broadcasted_iota(I32, (CH, GW), 1) % CH
    cmasks = (s_w <= t_w, s_w >= t_w)
    r_b = lax.broadcasted_iota(I32, (GW, GW), 0)
    c_b = lax.broadcasted_iota(I32, (GW, GW), 1)
    bm = (r_b // HEAD_DIM) == (c_b // HEAD_DIM)
    hsel_base = (r_b // HEAD_DIM) * HEAD_DIM
    dirs = ((hqf_ref, zf_ref, vf_ref, lbf_ref, of_ref), (hqb_ref, zb_ref, vb_ref, lbb_ref, ob_ref))
    chains = [(d, g) for d in range(2) for g in range(ngroups)]

    def expand(m):
        return jnp.where(bm, jnp.concatenate([m] * HG_HEADS_PER_GROUP, axis=0), jnp.zeros((GW, GW), m.dtype))

    def chunk_body(c, carry):
        rows = (pl.multiple_of(c * CH, CH), pl.multiple_of((nchunks - 1 - c) * CH, CH))
        prepared = []
        worst = jnp.zeros((1, 1), F32)
        for n, (d, g) in enumerate(chains):
            hq_r, z_r, v_r, lb_r, _ = dirs[d]
            ls = slice(g * GW, (g + 1) * GW)
            hq = hq_r[pl.ds(rows[d], CH), ls]
            z = z_r[pl.ds(rows[d], CH), ls]
            v = v_r[pl.ds(rows[d], CH), ls]
            last_row = CH - 1 if d == 0 else 0
            kk, cum, tail, emax, qe, ke = _hgrn_prepare(hq, z, lb_r[:, ls], tris[d], last_row, CH // 2)
            worst = jnp.maximum(worst, emax)
            a_ref[n] = _dot_nt(qe, expand(ke))
            cum_ref[n] = cum
            k_ref[n] = kk
            prepared.append((hq, v, kk, cum, tail))

        @pl.when(worst[0, 0] > HG_SAFE_EXP)
        def _():
            for n, (d, g) in enumerate(chains):
                hq_r = dirs[d][0]
                hq = hq_r[pl.ds(rows[d], CH), g * GW:(g + 1) * GW]
                cum = cum_ref[n]

                def key_body(s, acc):
                    cs = cum_ref[n, pl.ds(s, 1), :]
                    ksr = k_ref[n, pl.ds(s, 1), :]
                    p = hq * ksr * jnp.exp(jnp.minimum(cum - cs, 0.0))
                    hsel = jnp.where(c_b == hsel_base + s, 1.0, 0.0).astype(BF16)
                    return acc + _dot(p.astype(BF16), hsel)

                a_ref[n] = lax.fori_loop(0, CH, key_body, jnp.zeros((CH, GW), F32))

        for n, (d, g) in enumerate(chains):
            hq, v, kk, cum, tail = prepared[n]
            o_r = dirs[d][4]
            st = st_ref[n]
            a = jnp.where(cmasks[d], a_ref[n], 0.0).astype(BF16)
            vb = v.astype(BF16)
            o = _dot(a, expand(vb)) + _dot_nt((hq * jnp.exp(cum)).astype(BF16), st.astype(BF16))
            o_r[pl.ds(rows[d], CH), g * GW:(g + 1) * GW] = o
            k2 = (kk * jnp.exp(tail - cum)).astype(BF16)
            st_ref[n] = jnp.exp(tail) * st + jnp.where(bm, _dot_tn(vb, k2), 0.0)
        return carry

    lax.fori_loop(0, nchunks, chunk_body, 0)


def _hgrn2(dm, hq, ff, fb, hi, lb):
    B, T = dm.B, dm.T
    Wd = hq.shape[1]
    GW = HG_HEADS_PER_GROUP * HEAD_DIM
    nchain = 2 * (Wd // GW)
    nblk = 1 + dm.N // SEQ_BLOCK
    fwd, bwd = _seq_block_maps(dm)
    fspec = pl.BlockSpec((SEQ_BLOCK, Wd), lambda b, j: (fwd(b, j), 0))
    bspec = pl.BlockSpec((SEQ_BLOCK, Wd), lambda b, j: (bwd(b, j), 0))
    lbspec = pl.BlockSpec((1, Wd), lambda b, j: (0, 0))
    return pl.pallas_call(
        _hgrn_kernel,
        grid=(B, nblk),
        in_specs=[fspec, fspec, fspec, bspec, bspec, bspec, lbspec, lbspec],
        out_specs=[fspec, bspec],
        out_shape=[jax.ShapeDtypeStruct((T, Wd), F32)] * 2,
        scratch_shapes=[pltpu.VMEM((nchain, GW, GW), F32),
                        pltpu.VMEM((nchain, HG_CHUNK, GW), F32),
                        pltpu.VMEM((nchain, HG_CHUNK, GW), F32),
                        pltpu.VMEM((nchain, HG_CHUNK, GW), F32)],
        compiler_params=_cparams(("arbitrary", "arbitrary")),
        name="hgrn2_bidirectional",
    )(hq, ff, hi, hq, fb, hi, lb[0:1], lb[1:2])


def _lru_scan(a, x, reverse):
    n = a.shape[0]
    row = lax.broadcasted_iota(I32, a.shape, 0)
    s = 1
    while s < n:
        shift = (n - s) if reverse else s
        a_sh = pltpu.roll(a, shift, 0)
        x_sh = pltpu.roll(x, shift, 0)
        valid = (row < n - s) if reverse else (row >= s)
        x = jnp.where(valid, a * x_sh + x, x)
        a = jnp.where(valid, a * a_sh, a)
        s *= 2
    return a, x


def _lru_kernel(ucf_ref, upf_ref, unf_ref, ucb_ref, upb_ref, unb_ref, cw_ref, cb_ref, wbd_ref,
                ba_ref, bx_ref, lam_ref, hf_ref, hb_ref, ext_ref, carry_ref):
    j = pl.program_id(1)
    nblk = pl.num_programs(1)

    @pl.when(j == 0)
    def _():
        carry_ref[...] = jnp.zeros_like(carry_ref)

    TT, W = ucf_ref.shape
    GW = wbd_ref.shape[2]
    halo = upf_ref.shape[0]
    dirs = ((ucf_ref, upf_ref, unf_ref, hf_ref), (ucb_ref, upb_ref, unb_ref, hb_ref))
    for d, (uc_r, up_r, un_r, out_r) in enumerate(dirs):
        pos = j if d == 0 else jnp.where(j == 0, 0, nblk - j)
        keep_prev = jnp.where((pos == 0) | (pos == 1), 0.0, 1.0)
        keep_next = jnp.where((pos == 0) | (pos == nblk - 1), 0.0, 1.0)
        ext_ref[d, 0:halo, :] = up_r[...] * keep_prev
        ext_ref[d, halo:halo + TT, :] = uc_r[...]
        ext_ref[d, halo + TT:halo + TT + halo, :] = un_r[...] * keep_next
        for g in range(W // GW):
            ls = slice(g * GW, (g + 1) * GW)
            u = cb_ref[:, ls]
            for tap in range(CONV_W):
                off = halo - 2 + tap
                u = u + cw_ref[tap:tap + 1, ls] * ext_ref[d, off:off + TT, ls]
            gates = _dot(u.astype(BF16), wbd_ref[d, g])
            r = jax.nn.sigmoid(gates[:, :GW] + ba_ref[d:d + 1, ls])
            ig = jax.nn.sigmoid(gates[:, GW:] + bx_ref[d:d + 1, ls])
            nl = -lam_ref[d:d + 1, ls]
            softplus = jnp.maximum(nl, 0.0) + jnp.log1p(jnp.exp(-jnp.abs(nl)))
            log_a = -LRU_C * r * softplus
            a = jnp.exp(log_a)
            x_in = jnp.sqrt(1.0 - jnp.exp(2.0 * log_a)) * ig * u
            a_acc, h = _lru_scan(a, x_in, d == 1)
            h = h + a_acc * carry_ref[d, 0:1, ls]
            out_r[:, ls] = h
            last = 0 if d == 1 else TT - 1
            carry_ref[d, 0:1, ls] = h[last:last + 1, :]


def _block_diag_gates(w_a, w_x, group):
    ndir, K, d, _ = w_a.shape
    per = group // d
    eye = jnp.eye(per, dtype=w_a.dtype)

    def bd(w):
        w = w.reshape(ndir, K // per, per, d, d)
        full = jnp.einsum('ngkde,kl->ngkdle', w, eye)
        return full.reshape(ndir, K // per, group, group)

    return jnp.concatenate([bd(w_a), bd(w_x)], axis=-1)


def _rglru(dm, u, conv_w, conv_b, w_a, b_a, w_x, b_x, lam):
    B, T = dm.B, dm.T
    W = u.shape[1]
    GW = 256
    halo = 8
    nblk = 1 + dm.N // SEQ_BLOCK
    fwd, bwd = _seq_block_maps(dm)
    hb = SEQ_BLOCK // halo
    nh = T // halo
    wbd = _block_diag_gates(w_a, w_x, GW).astype(BF16)

    def cur(m):
        return pl.BlockSpec((SEQ_BLOCK, W), lambda b, j: (m(b, j), 0))

    def prev(m):
        return pl.BlockSpec((halo, W), lambda b, j: (jnp.maximum(m(b, j) * hb - 1, 0), 0))

    def nxt(m):
        return pl.BlockSpec((halo, W), lambda b, j: (jnp.minimum((m(b, j) + 1) * hb, nh - 1), 0))

    def full(a):
        nd = a.ndim
        return pl.BlockSpec(a.shape, lambda b, j: (0,) * nd)

    consts = (conv_w, conv_b.reshape(1, W), wbd, b_a, b_x, lam)
    return pl.pallas_call(
        _lru_kernel,
        grid=(B, nblk),
        in_specs=[cur(fwd), prev(fwd), nxt(fwd), cur(bwd), prev(bwd), nxt(bwd)] + [full(a) for a in consts],
        out_specs=[cur(fwd), cur(bwd)],
        out_shape=[jax.ShapeDtypeStruct((T, W), F32)] * 2,
        scratch_shapes=[pltpu.VMEM((2, SEQ_BLOCK + 2 * halo, W), F32), pltpu.VMEM((2, 8, W), F32)],
        compiler_params=_cparams(("arbitrary", "arbitrary")),
        name="rglru_bidirectional",
    )(u, u, u, u, u, u, *consts)


def _post_mixer_epilogue(o, x_ref, m2_ref, m3_ref, m4_ref, lng_ref, lnb_ref, rwh_ref, rwl_ref,
                         xo_ref, h2_ref, lg_ref, alpha):
    z = alpha * x_ref[...] + m2_ref[...] * o
    xn = _layer_norm(z, lng_ref[...], lnb_ref[...])
    xo_ref[...] = xn
    h2 = xn * (1.0 + m4_ref[...]) + m3_ref[...]
    h2_ref[...] = _pack_rows(h2)
    hh, hl = _split(h2)
    lg_ref[...] = _dot_nt(rwh_ref[...], hh) + _dot_nt(rwh_ref[...], hl) + _dot_nt(rwl_ref[...], hh)


def _post_even_kernel(nac_ref, nal_ref, of_ref, ob_ref, hg_ref, ng_ref, w_ref, *rest, alpha, ctx_tiles):
    na = jnp.where(pl.program_id(0) < ctx_tiles, nac_ref[...], nal_ref[...])
    o = of_ref[...] + ob_ref[...]
    wd = o.shape[1]
    r_i = lax.broadcasted_iota(I32, (wd, wd), 0) // HEAD_DIM
    c_i = lax.broadcasted_iota(I32, (wd, wd), 1) // HEAD_DIM
    avg = jnp.where(r_i == c_i, 1.0 / HEAD_DIM, 0.0).astype(BF16)
    sh, sl = _split(o * o)
    ms = _dot(sh, avg) + _dot(sl, avg)
    r = o * lax.rsqrt(ms + RMS_EPS) * ng_ref[...] * _silu(hg_ref[...])
    mix = _dot(na, w_ref[:wd, :]) + _dot(r.astype(BF16), w_ref[wd:, :])
    _post_mixer_epilogue(mix, *rest, alpha=alpha)


def _post_odd_kernel(hf_ref, hb_ref, y_ref, w_ref, *rest, alpha):
    m = ((hf_ref[...] + hb_ref[...]) * _gelu_tanh(y_ref[...])).astype(BF16)
    _post_mixer_epilogue(_dot(m, w_ref[...]), *rest, alpha=alpha)


def _post_mixer(dm, mixer_inputs, w_out, xs, m2, m3, m4, lng, lnb, rwh, rwl, alpha, even):
    T, D, TM = dm.T, dm.D, dm.TM
    E = rwh.shape[0]
    row = lambda a: pl.BlockSpec((TM, a.shape[1]), lambda i: (i, 0))
    const = lambda a: pl.BlockSpec(a.shape, lambda i: (0,) * a.ndim)
    mod_spec = pl.BlockSpec((None, 1, D), lambda i: (dm.group(i, TM), 0, 0))
    if even:
        nac, nal, of, ob, hg, ng = mixer_inputs
        nct = dm.T_ctx // TM
        ins = [nac, nal, of, ob, hg, ng, w_out]
        specs = [pl.BlockSpec((TM, nac.shape[1]), lambda i: (jnp.minimum(i, nct - 1), 0)),
                 pl.BlockSpec((TM, nal.shape[1]), lambda i: (jnp.maximum(i - nct, 0), 0)),
                 row(of), row(ob), row(hg), const(ng), const(w_out)]
        kern = functools.partial(_post_even_kernel, alpha=alpha, ctx_tiles=nct)
    else:
        hf, hb, y = mixer_inputs
        ins = [hf, hb, y, w_out]
        specs = [row(hf), row(hb), row(y), const(w_out)]
        kern = functools.partial(_post_odd_kernel, alpha=alpha)
    ins += [xs, m2, m3, m4, lng, lnb, rwh, rwl]
    specs += [row(xs), mod_spec, mod_spec, mod_spec, const(lng), const(lnb), const(rwh), const(rwl)]
    return pl.pallas_call(
        kern,
        grid=(T // TM,),
        in_specs=specs,
        out_specs=[pl.BlockSpec((TM, D), lambda i: (i, 0)), pl.BlockSpec((TM, D // 2), lambda i: (i, 0)),
                   pl.BlockSpec((E, TM), lambda i: (0, i))],
        out_shape=[jax.ShapeDtypeStruct((T, D), F32), jax.ShapeDtypeStruct((T, D // 2), jnp.uint32),
                   jax.ShapeDtypeStruct((E, T), F32)],
        compiler_params=_cparams(("parallel",)),
        name="post_mixer_even" if even else "post_mixer_odd",
    )(*ins)


def _route_kernel(lg_ref, bias_ref, tri_ref, idx_ref, rank_ref, w_ref, cnt_ref, carry_ref):
    i = pl.program_id(0)

    @pl.when(i == 0)
    def _():
        carry_ref[...] = jnp.zeros_like(carry_ref)

    E, TK = lg_ref.shape
    s = jax.nn.sigmoid(lg_ref[...])
    work = s + bias_ref[...]
    eio = lax.broadcasted_iota(I32, (E, TK), 0).astype(F32)
    picked = jnp.zeros((E, TK), F32)
    sels, idxs = [], []
    for _ in range(TOP_K):
        m = jnp.max(work, axis=0, keepdims=True)
        ik = jnp.min(jnp.where(work == m, eio, float(E)), axis=0, keepdims=True)
        oh = eio == ik
        sels.append(jnp.sum(jnp.where(oh, s, 0.0), axis=0, keepdims=True))
        idxs.append(ik)
        picked = jnp.where(oh, 1.0, picked)
        work = jnp.where(oh, -jnp.inf, work)
    total = sels[0]
    for sk in sels[1:]:
        total = total + sk
    carry = carry_ref[:, 0:1]
    ranks = carry + _dot(picked.astype(BF16), tri_ref[...])
    out_rows = idx_ref.shape[0]
    rio = lax.broadcasted_iota(I32, (out_rows, TK), 0)
    idx_o = jnp.zeros((out_rows, TK), I32)
    rank_o = jnp.zeros((out_rows, TK), I32)
    w_o = jnp.zeros((out_rows, TK), F32)
    for kx in range(TOP_K):
        rk = jnp.sum(jnp.where(eio == idxs[kx], ranks, 0.0), axis=0, keepdims=True).astype(I32)
        idx_o = jnp.where(rio == kx, idxs[kx].astype(I32), idx_o)
        rank_o = jnp.where(rio == kx, rk, rank_o)
        w_o = jnp.where(rio == kx, sels[kx] / total * ROUTED_SCALE, w_o)
    idx_ref[...] = idx_o
    rank_ref[...] = rank_o
    w_ref[...] = w_o
    new_carry = carry + jnp.sum(picked, axis=1, keepdims=True)
    carry_ref[...] = jnp.broadcast_to(new_carry, carry_ref.shape)
    cnt_ref[...] = jnp.broadcast_to(new_carry, cnt_ref.shape)


def _route(logits_t, router_bias, t0):
    E, T = logits_t.shape
    Tm = T - t0
    TK = _pick_tile(int(np.gcd(Tm, t0)), (1024, 512, 256))
    off = t0 // TK
    tri = jnp.asarray(np.triu(np.ones((TK, TK), np.float32), 1), BF16)
    tok = lambda: pl.BlockSpec((8, TK), lambda i: (0, i))
    idx, rank, w, cnt = pl.pallas_call(
        _route_kernel,
        grid=(Tm // TK,),
        in_specs=[pl.BlockSpec((E, TK), lambda i: (0, i + off)),
                  pl.BlockSpec((E, 1), lambda i: (0, 0)),
                  pl.BlockSpec((TK, TK), lambda i: (0, 0))],
        out_specs=[tok(), tok(), tok(), pl.BlockSpec((E, 128), lambda i: (0, 0))],
        out_shape=[jax.ShapeDtypeStruct((8, Tm), I32), jax.ShapeDtypeStruct((8, Tm), I32),
                   jax.ShapeDtypeStruct((8, Tm), F32), jax.ShapeDtypeStruct((E, 128), F32)],
        scratch_shapes=[pltpu.VMEM((E, 128), F32)],
        compiler_params=_cparams(("arbitrary",)),
        name="moe_route",
    )(logits_t, router_bias.reshape(E, 1), tri)
    return idx, rank, w, cnt[:, 0].astype(I32)


def _dispatch_kernel(dest_hbm, h_ref, xs_hbm, dsm, sem_d, sem):
    i = pl.program_id(0)
    TMd = h_ref.shape[0]
    n = TMd * TOP_K
    cp = pltpu.make_async_copy(dest_hbm.at[pl.ds(pl.multiple_of(i * n, 1024), n)], dsm, sem_d)
    cp.start()
    cp.wait()

    def start_body(t, c):
        for kx in range(TOP_K):
            d = dsm[t * TOP_K + kx]
            pltpu.make_async_copy(h_ref.at[pl.ds(t, 1)], xs_hbm.at[pl.ds(d, 1)], sem).start(priority=kx % 2)
        return c

    lax.fori_loop(0, TMd, start_body, 0, unroll=2)
    for _ in range(TOP_K):
        pltpu.make_async_copy(h_ref, xs_hbm.at[pl.ds(0, TMd)], sem).wait()


def _dispatch(h2, dest_flat, t0, n_slots):
    T, D = h2.shape
    TMd = 512
    Tm = T - t0
    assert Tm % TMd == 0 and t0 % TMd == 0
    off = t0 // TMd
    n = TMd * TOP_K
    return pl.pallas_call(
        _dispatch_kernel,
        grid=(Tm // TMd,),
        in_specs=[pl.BlockSpec(memory_space=pl.ANY),
                  pl.BlockSpec((TMd, D), lambda i: (i + off, 0))],
        out_specs=pl.BlockSpec(memory_space=pl.ANY),
        out_shape=jax.ShapeDtypeStruct((n_slots, D), h2.dtype),
        scratch_shapes=[pltpu.SMEM((n,), I32), pltpu.SemaphoreType.DMA, pltpu.SemaphoreType.DMA],
        compiler_params=_cparams(("arbitrary",)),
        name="moe_dispatch",
    )(dest_flat, h2)


def _gmm_kernel(bexp_ref, bvalid_ref, x_ref, wg_ref, wu_ref, wd_ref, y_ref):
    del bexp_ref
    i = pl.program_id(0)
    nvalid = bvalid_ref[i]

    @pl.when(nvalid > 0)
    def _():
        row = lax.broadcasted_iota(I32, x_ref.shape, 0)
        xb = _unpack_rows(jnp.where(row < nvalid, x_ref[...], jnp.uint32(0)))
        act = _silu(_dot(xb, wg_ref[...].astype(BF16))) * _dot(xb, wu_ref[...].astype(BF16))
        y_ref[...] = _dot(act.astype(BF16), wd_ref[...].astype(BF16))

    @pl.when(nvalid == 0)
    def _():
        y_ref[...] = jnp.zeros_like(y_ref)


def _grouped_experts(xsorted, block_exp, block_valid, wg, wu, wd):
    NS, DP = xsorted.shape
    nb = NS // MOE_BLOCK
    _, D, hid = wg.shape
    grid_spec = pltpu.PrefetchScalarGridSpec(
        num_scalar_prefetch=2,
        grid=(nb,),
        in_specs=[pl.BlockSpec((MOE_BLOCK, DP), lambda i, be, bv: (i, 0)),
                  pl.BlockSpec((None, D, hid), lambda i, be, bv: (be[i], 0, 0)),
                  pl.BlockSpec((None, D, hid), lambda i, be, bv: (be[i], 0, 0)),
                  pl.BlockSpec((None, hid, D), lambda i, be, bv: (be[i], 0, 0))],
        out_specs=pl.BlockSpec((MOE_BLOCK, D), lambda i, be, bv: (i, 0)),
    )
    return pl.pallas_call(
        _gmm_kernel,
        grid_spec=grid_spec,
        out_shape=jax.ShapeDtypeStruct((NS, D), F32),
        compiler_params=_cparams(("arbitrary",)),
        name="moe_grouped_experts",
    )(block_exp, block_valid, xsorted, wg, wu, wd)


def _combine_kernel(dest_hbm, y_hbm, w_ref, h_ref, sgu_ref, sd_ref, x_ref, m5_ref, lng_ref, lnb_ref,
                    o_ref, dsm, gbuf, sem_d, sem, *, alpha):
    i = pl.program_id(0)
    TMc = h_ref.shape[0]
    n = TMc * TOP_K
    cp = pltpu.make_async_copy(dest_hbm.at[pl.ds(pl.multiple_of(i * n, 1024), n)], dsm, sem_d)
    cp.start()
    cp.wait()

    def start_body(t, c):
        for kx in range(TOP_K):
            d = dsm[t * TOP_K + kx]
            pltpu.make_async_copy(y_hbm.at[pl.ds(d, 1)], gbuf.at[kx, pl.ds(t, 1)], sem).start(priority=kx % 2)
        return c

    lax.fori_loop(0, TMc, start_body, 0, unroll=2)

    hid = sd_ref.shape[0]
    gu = _dot(_unpack_rows(h_ref[...]), sgu_ref[...])
    y = _dot((_silu(gu[:, :hid]) * gu[:, hid:]).astype(BF16), sd_ref[...])

    for kx in range(TOP_K):
        pltpu.make_async_copy(y_hbm.at[pl.ds(0, TMc)], gbuf.at[kx], sem).wait()
    w = w_ref[...]
    for kx in range(TOP_K):
        y = y + w[:, kx:kx + 1] * gbuf[kx]
    z = alpha * x_ref[...] + m5_ref[...] * y
    o_ref[...] = _layer_norm(z, lng_ref[...], lnb_ref[...])


def _combine(dm, dest_flat, ysorted, w_tm, h2, sgu, sd, xs, m5, lng, lnb, alpha, t0):
    T, D = xs.shape
    TMc = 512
    Tm = T - t0
    off = t0 // TMc
    n = TMc * TOP_K
    const = lambda a: pl.BlockSpec(a.shape, lambda i: (0,) * a.ndim)
    row = lambda a: pl.BlockSpec((TMc, a.shape[1]), lambda i: (i + off, 0))
    return pl.pallas_call(
        functools.partial(_combine_kernel, alpha=alpha),
        grid=(Tm // TMc,),
        in_specs=[pl.BlockSpec(memory_space=pl.ANY), pl.BlockSpec(memory_space=pl.ANY),
                  pl.BlockSpec((TMc, 8), lambda i: (i, 0)), row(h2), const(sgu), const(sd), row(xs),
                  pl.BlockSpec((None, 1, D), lambda i: (dm.group(i + off, TMc), 0, 0)),
                  const(lng), const(lnb)],
        out_specs=pl.BlockSpec((TMc, D), lambda i: (i, 0)),
        out_shape=jax.ShapeDtypeStruct((Tm, D), F32),
        scratch_shapes=[pltpu.SMEM((n,), I32), pltpu.VMEM((TOP_K, TMc, D), F32),
                        pltpu.SemaphoreType.DMA, pltpu.SemaphoreType.DMA],
        compiler_params=_cparams(("arbitrary",)),
        name="moe_combine",
    )(dest_flat, ysorted, w_tm, h2, sgu, sd, xs, m5, lng, lnb)


def _moe(dm, h2, logits_t, xs, m5, lng, lnb, router_bias, wg, wu, wd, sgu, sd, alpha, t0):
    T = h2.shape[0]
    E = wg.shape[0]
    Tm = T - t0
    idx, rank, w, counts = _route(logits_t, router_bias, t0)
    nb = -(-(Tm * TOP_K) // MOE_BLOCK) + E
    padded = (counts + MOE_BLOCK - 1) // MOE_BLOCK * MOE_BLOCK
    pends = jnp.cumsum(padded)
    pstarts = pends - padded
    eids = jnp.arange(E, dtype=I32)
    bstart = jnp.arange(nb, dtype=I32) * MOE_BLOCK
    block_exp = jnp.minimum(jnp.sum((pends[None, :] <= bstart[:, None]).astype(I32), axis=1), E - 1)
    of_block = block_exp[:, None] == eids[None, :]
    seg_end = jnp.sum(jnp.where(of_block, (pstarts + counts)[None, :], 0), axis=1)
    block_valid = jnp.clip(seg_end - bstart, 0, MOE_BLOCK).astype(I32)
    of_tok = idx[:TOP_K, :, None] == eids[None, None, :]
    dest = rank[:TOP_K] + jnp.sum(jnp.where(of_tok, pstarts[None, None, :], 0), axis=2)
    dest_flat = dest.T.reshape(-1).astype(I32)
    w_tm = jnp.pad(w[:TOP_K].T, ((0, 0), (0, 8 - TOP_K)))
    xsorted = _dispatch(h2, dest_flat, t0, nb * MOE_BLOCK)
    ysorted = _grouped_experts(xsorted, block_exp, block_valid, wg, wu, wd)
    return _combine(dm, dest_flat, ysorted, w_tm, h2, sgu, sd, xs, m5, lng, lnb, alpha, t0)


def kernel(x, c, ctx, c_ctx, ada_w, ada_b, ln_g, ln_b, ev_w_in, ev_w_out, na_rpb, hg_lb_raw, hg_norm_g, od_w_in, od_conv_w, od_conv_b, lru_w_a, lru_b_a, lru_w_x, lru_b_x, lru_lam, od_w_out, router_w, router_bias, exp_w_gate, exp_w_up, exp_w_down, sh_w_gate, sh_w_up, sh_w_down):
    B, N, D = x.shape
    C = ctx.shape[1]
    depth = ada_w.shape[0]
    dm = _Dims(B, N, C, D)
    alpha = float((2 * depth) ** 0.25)
    assert B + 1 <= 8

    cvec = jnp.zeros((8, D), F32).at[:B].set(c).at[B].set(c_ctx)
    mod = _modulation(cvec, ada_w, ada_b).reshape(depth, 8, 6, 1, D)

    p_lb = jax.nn.softmax(hg_lb_raw.astype(F32), axis=1)
    hg_lb = jnp.cumsum(p_lb, axis=1) - p_lb[:, :1]

    xs = jnp.concatenate([ctx.reshape(B * C, D), x.reshape(B * N, D)], axis=0)
    for l in range(depth):
        jl = l // 2
        last = l == depth - 1
        m = [mod[l, :, t] for t in range(6)]
        rw_t = router_w[l].T
        rwh, rwl = _split(rw_t)
        lng = ln_g[l][:, None, :]
        lnb = ln_b[l][:, None, :]
        if l % 2 == 0:
            q, k, v, hq, ff, fb, hi, hg = _inproj(dm, xs, m[1], m[0], ev_w_in[jl].astype(BF16), True)
            bias = _na_bias_tables(na_rpb[jl], N // GRID_W)
            na_lat = _neighbourhood_attention(dm, q, k, v, bias)
            na_ctx = _context_attention(dm, q, k, v)
            o_f, o_b = _hgrn2(dm, hq, ff, fb, hi, hg_lb[:, jl])
            ng = jnp.tile(hg_norm_g[jl], hq.shape[1] // HEAD_DIM)[None, :]
            mixer_inputs = (na_ctx, na_lat, o_f, o_b, hg, ng)
            w_out = ev_w_out[jl].astype(BF16)
        else:
            y, u = _inproj(dm, xs, m[1], m[0], od_w_in[jl].astype(BF16), False)
            h_f, h_b = _rglru(dm, u, od_conv_w[jl], od_conv_b[jl], lru_w_a[jl], lru_b_a[jl],
                              lru_w_x[jl], lru_b_x[jl], lru_lam[jl])
            mixer_inputs = (h_f, h_b, y)
            w_out = od_w_out[jl].astype(BF16)
        xs, h2, logits_t = _post_mixer(dm, mixer_inputs, w_out, xs, m[2], m[3], m[4], lng[0], lnb[0],
                                       rwh, rwl, alpha, l % 2 == 0)
        sgu = jnp.concatenate([sh_w_gate[l], sh_w_up[l]], axis=-1).astype(BF16)
        sd = sh_w_down[l].astype(BF16)
        t0 = dm.T_ctx if last else 0
        xs = _moe(dm, h2, logits_t, xs, m[5], lng[1], lnb[1], router_bias[l], exp_w_gate[l], exp_w_up[l],
                  exp_w_down[l], sgu, sd, alpha, t0)
    return xs.reshape(B, N, D)
```

```python
import functools

import numpy as np
import jax
import jax.numpy as jnp
from jax import lax
from jax.experimental import pallas as pl
from jax.experimental.pallas import tpu as pltpu

F32 = jnp.float32
BF16 = jnp.bfloat16
I32 = jnp.int32

HEAD_DIM = 64
GRID_W = 64
NA_WIN_ROWS = 8
NA_WIN_COLS = 16
NA_QROWS = 8
NA_KROWS = 16
HG_CHUNK = 64
HG_HEADS_PER_GROUP = 4
FORGET_FLOOR = 1e-30
HG_SAFE_EXP = 80.0
HG_CLIP_EXP = 85.0
LRU_C = 8.0
LRU_BLOCKS = 16
CONV_W = 4
TOP_K = 6
ROUTED_SCALE = 2.5
LN_EPS = 1e-5
RMS_EPS = 1e-6
SEQ_BLOCK = 256
MOE_BLOCK = 512
MOE_TILE = 256
ROW_ALIGN = 8
DISP_CHUNK = 32
COMB_CHUNK = ROW_ALIGN
MOE_SLACK = DISP_CHUNK
MASK_VALUE = -1e30
VMEM_LIMIT = 56 * 1024 * 1024


def _cparams(sem):
    return pltpu.CompilerParams(dimension_semantics=sem, vmem_limit_bytes=VMEM_LIMIT)


def _split(a):
    hi = a.astype(BF16)
    lo = (a - hi.astype(F32)).astype(BF16)
    return hi, lo


def _dot(a, b):
    return jnp.dot(a, b, preferred_element_type=F32)


def _dot_nt(a, b):
    return lax.dot_general(a, b, (((1,), (1,)), ((), ())), preferred_element_type=F32)


def _dot_tn(a, b):
    return lax.dot_general(a, b, (((0,), (0,)), ((), ())), preferred_element_type=F32)


def _dot3(a, b):
    ah, al = _split(a)
    bh, bl = _split(b)
    return _dot(ah, bh) + _dot(al, bh) + _dot(ah, bl)


def _pack_rows(h):
    half = h.shape[1] // 2
    bits = pltpu.bitcast(h.astype(BF16).astype(F32), jnp.uint32)
    return (bits[:, :half] >> 16) | (bits[:, half:] & jnp.uint32(0xFFFF0000))


def _unpack_rows(p):
    lo = pltpu.bitcast(p << 16, F32)
    hi = pltpu.bitcast(p & jnp.uint32(0xFFFF0000), F32)
    return jnp.concatenate([lo, hi], axis=1).astype(BF16)


def _silu(v):
    return v * jax.nn.sigmoid(v)


def _gelu_tanh(v):
    return 0.5 * v * (1.0 + jnp.tanh(0.7978845608028654 * (v + 0.044715 * v * v * v)))


def _layer_norm(z, g, b):
    mu = jnp.mean(z, axis=-1, keepdims=True)
    zc = z - mu
    var = jnp.mean(zc * zc, axis=-1, keepdims=True)
    return zc * lax.rsqrt(var + LN_EPS) * g + b


def _pick_tile(n, cands):
    for c in cands:
        if n % c == 0:
            return c
    raise ValueError(f"no tile for {n}")


class _Dims:
    def __init__(self, B, N, C, D):
        self.B, self.N, self.C, self.D = B, N, C, D
        self.T_ctx = B * C
        self.T = B * C + B * N
        assert C == SEQ_BLOCK and N % SEQ_BLOCK == 0
        assert N % GRID_W == 0 and (N // GRID_W) % NA_QROWS == 0 and N // GRID_W >= NA_KROWS
        assert self.T_ctx % 512 == 0
        self.TM = _pick_tile(self.T_ctx, (512, 256))
        assert N % self.TM == 0

    def group(self, i, tile):
        start = i * tile
        return jnp.where(start < self.T_ctx, self.B, (start - self.T_ctx) // self.N)


def _mod_kernel(c_ref, w_ref, b_ref, o_ref):
    o_ref[...] = _dot3(_silu(c_ref[...]), w_ref[...]) + b_ref[...]


def _modulation(cvec, ada_w, ada_b):
    L, D, W6 = ada_w.shape
    nc = 1536
    return pl.pallas_call(
        _mod_kernel,
        grid=(L, W6 // nc),
        in_specs=[
            pl.BlockSpec((8, D), lambda l, j: (0, 0)),
            pl.BlockSpec((None, D, nc), lambda l, j: (l, 0, j)),
            pl.BlockSpec((None, 1, nc), lambda l, j: (l, 0, j)),
        ],
        out_specs=pl.BlockSpec((None, 8, nc), lambda l, j: (l, 0, j)),
        out_shape=jax.ShapeDtypeStruct((L, 8, W6), F32),
        compiler_params=_cparams(("arbitrary", "arbitrary")),
        name="adaln_modulation",
    )(cvec, ada_w, ada_b.reshape(L, 1, W6))


def _inproj_even_kernel(x_ref, sc_ref, sh_ref, w_ref, q_ref, k_ref, v_ref, hq_ref, ff_ref, fb_ref, hi_ref, hg_ref):
    h = (x_ref[...] * (1.0 + sc_ref[...]) + sh_ref[...]).astype(BF16)
    wd = q_ref.shape[1]

    def part(j):
        return _dot(h, w_ref[:, j * wd:(j + 1) * wd])

    scale = HEAD_DIM ** -0.5
    q_ref[...] = (part(0) * scale).astype(BF16)
    k_ref[...] = part(1).astype(BF16)
    v_ref[...] = part(2).astype(BF16)
    hq_ref[...] = _silu(part(3)) * scale
    ff_ref[...] = part(4)
    fb_ref[...] = part(5)
    hi_ref[...] = part(6)
    hg_ref[...] = part(7)


def _inproj_odd_kernel(x_ref, sc_ref, sh_ref, w_ref, y_ref, u_ref):
    h = (x_ref[...] * (1.0 + sc_ref[...]) + sh_ref[...]).astype(BF16)
    wd = y_ref.shape[1]
    y_ref[...] = _dot(h, w_ref[:, :wd])
    u_ref[...] = _dot(h, w_ref[:, wd:])


def _inproj(dm, xs, sc, sh, w, even):
    T, D, TM = dm.T, dm.D, dm.TM
    wtot = w.shape[1]
    mod_spec = pl.BlockSpec((None, 1, D), lambda i: (dm.group(i, TM), 0, 0))
    in_specs = [pl.BlockSpec((TM, D), lambda i: (i, 0)), mod_spec, mod_spec,
                pl.BlockSpec((D, wtot), lambda i: (0, 0))]
    if even:
        wd = wtot // 8
        dts = [BF16] * 3 + [F32] * 5
        kern = _inproj_even_kernel
    else:
        wd = wtot // 2
        dts = [F32] * 2
        kern = _inproj_odd_kernel
    return pl.pallas_call(
        kern,
        grid=(T // TM,),
        in_specs=in_specs,
        out_specs=[pl.BlockSpec((TM, wd), lambda i: (i, 0)) for _ in dts],
        out_shape=[jax.ShapeDtypeStruct((T, wd), dt) for dt in dts],
        compiler_params=_cparams(("parallel",)),
        name="inproj_even" if even else "inproj_odd",
    )(xs, sc, sh, w)


def _na_bias_tables(rpb, rows):
    W = GRID_W
    H = rpb.shape[0]
    nr, nc = 2 * NA_WIN_ROWS - 1, 2 * NA_WIN_COLS - 1
    hp = lax.Precision.HIGHEST
    c = np.arange(W)[:, None]
    kc = np.arange(W)[None, :]
    c0 = np.clip(c - NA_WIN_COLS // 2, 0, W - NA_WIN_COLS)
    valid_c = (kc >= c0) & (kc < c0 + NA_WIN_COLS)
    dc = np.clip(kc - c + NA_WIN_COLS - 1, 0, nc - 1)
    oh_c = (dc.reshape(-1, 1) == np.arange(nc)[None, :]).astype(np.float32)
    by_col = jnp.einsum('hrs,xs->hrx', rpb.astype(F32), jnp.asarray(oh_c), precision=hp)
    tabs = []
    for rbase, kb in ((0, 0), (NA_QROWS, NA_QROWS - NA_WIN_ROWS // 2), (rows - NA_QROWS, rows - NA_KROWS)):
        r = rbase + np.arange(NA_QROWS)[:, None]
        kr = kb + np.arange(NA_KROWS)[None, :]
        r0 = np.clip(r - NA_WIN_ROWS // 2, 0, rows - NA_WIN_ROWS)
        valid_r = (kr >= r0) & (kr < r0 + NA_WIN_ROWS)
        dr = np.clip(kr - r + NA_WIN_ROWS - 1, 0, nr - 1)
        oh_r = (dr.reshape(-1, 1) == np.arange(nr)[None, :]).astype(np.float32)
        t = jnp.einsum('yr,hrx->hyx', jnp.asarray(oh_r), by_col, precision=hp)
        t = t.reshape(H, NA_QROWS, NA_KROWS, W, W).transpose(0, 1, 3, 2, 4)
        valid = valid_r[:, None, :, None] & valid_c[None, :, None, :]
        t = jnp.where(jnp.asarray(valid)[None], t, MASK_VALUE)
        tabs.append(t.reshape(H, NA_QROWS * W, NA_KROWS * W))
    return jnp.stack(tabs).astype(F32)


def _na_kernel(q_ref, k0, k1, k2, k3, v0, v1, v2, v3, kc_ref, vc_ref, bias_ref, o_ref):
    q2 = q_ref[...]
    lane = lax.broadcasted_iota(I32, (1, q2.shape[1]), 1)
    first = lane < HEAD_DIM
    ks = [r[...] for r in (k0, k1, k2, k3)]
    vs = [r[...] for r in (v0, v1, v2, v3)]
    kc = kc_ref[...]
    vc = vc_ref[...]
    kb = ks[0].shape[0]
    outs = []
    for hh in range(2):
        sel = first if hh == 0 else jnp.logical_not(first)
        qh = jnp.where(sel, q2, jnp.zeros_like(q2))
        s_loc = jnp.concatenate([_dot_nt(qh, kj) for kj in ks], axis=1) + bias_ref[hh]
        s_ctx = _dot_nt(qh, kc)
        m = jnp.maximum(jnp.max(s_loc, axis=1, keepdims=True), jnp.max(s_ctx, axis=1, keepdims=True))
        p_loc = jnp.exp(s_loc - m)
        p_ctx = jnp.exp(s_ctx - m)
        denom = jnp.sum(p_loc, axis=1, keepdims=True) + jnp.sum(p_ctx, axis=1, keepdims=True)
        o = _dot(p_ctx.astype(BF16), vc)
        for j in range(4):
            o = o + _dot(p_loc[:, j * kb:(j + 1) * kb].astype(BF16), vs[j])
        outs.append(o / denom)
    o_ref[...] = jnp.where(first, outs[0], outs[1]).astype(o_ref.dtype)


def _neighbourhood_attention(dm, q, k, v, bias):
    B, N, C, T = dm.B, dm.N, dm.C, dm.T
    H2 = q.shape[1] // (2 * HEAD_DIM)
    rows = N // GRID_W
    nqb = rows // NA_QROWS
    QB = NA_QROWS * GRID_W
    KB = QB // 2
    ngroups = N // KB
    lat_q0 = dm.T_ctx // QB
    lat_k0 = dm.T_ctx // KB

    def q_map(p, i, b):
        return (lat_q0 + b * (N // QB) + i, p)

    def o_map(p, i, b):
        return (b * (N // QB) + i, p)

    def kv_map(j):
        def f(p, i, b):
            gs = jnp.clip(2 * i - 1, 0, ngroups - 4)
            return (lat_k0 + b * ngroups + gs + j, p)
        return f

    def ctx_map(p, i, b):
        return (b, p)

    def bias_map(p, i, b):
        var = jnp.where(i == 0, 0, jnp.where(i == nqb - 1, 2, 1))
        return (var, p, 0, 0)

    lanes = 2 * HEAD_DIM
    kv_specs = [pl.BlockSpec((KB, lanes), kv_map(j)) for j in range(4)]
    return pl.pallas_call(
        _na_kernel,
        grid=(H2, nqb, B),
        in_specs=[pl.BlockSpec((QB, lanes), q_map)] + kv_specs + kv_specs
        + [pl.BlockSpec((C, lanes), ctx_map), pl.BlockSpec((C, lanes), ctx_map),
           pl.BlockSpec((None, 2, QB, NA_KROWS * GRID_W), bias_map)],
        out_specs=pl.BlockSpec((QB, lanes), o_map),
        out_shape=jax.ShapeDtypeStruct((B * N, q.shape[1]), BF16),
        compiler_params=_cparams(("arbitrary", "arbitrary", "arbitrary")),
        name="neighbourhood_attention",
    )(q, k, k, k, k, v, v, v, v, k, v, bias)


def _ctx_attn_kernel(q_ref, k_ref, v_ref, o_ref):
    q2 = q_ref[...]
    k2 = k_ref[...]
    v2 = v_ref[...]
    lane = lax.broadcasted_iota(I32, (1, q2.shape[1]), 1)
    first = lane < HEAD_DIM
    outs = []
    for hh in range(2):
        sel = first if hh == 0 else jnp.logical_not(first)
        s = _dot_nt(jnp.where(sel, q2, jnp.zeros_like(q2)), k2)
        p = jnp.exp(s - jnp.max(s, axis=1, keepdims=True))
        outs.append(_dot(p.astype(BF16), v2) / jnp.sum(p, axis=1, keepdims=True))
    o_ref[...] = jnp.where(first, outs[0], outs[1]).astype(o_ref.dtype)


def _context_attention(dm, q, k, v):
    lanes = 2 * HEAD_DIM
    spec = pl.BlockSpec((dm.C, lanes), lambda p, b: (b, p))
    return pl.pallas_call(
        _ctx_attn_kernel,
        grid=(q.shape[1] // lanes, dm.B),
        in_specs=[spec, spec, spec],
        out_specs=spec,
        out_shape=jax.ShapeDtypeStruct((dm.T_ctx, q.shape[1]), BF16),
        compiler_params=_cparams(("arbitrary", "arbitrary")),
        name="context_attention",
    )(q, k, v)


def _seq_block_maps(dm):
    nl = dm.N // SEQ_BLOCK
    base = dm.T_ctx // SEQ_BLOCK

    def fwd(b, j):
        return jnp.where(j == 0, b, base + b * nl + j - 1)

    def bwd(b, j):
        return jnp.where(j == 0, b, base + b * nl + nl - j)

    return fwd, bwd


def _hgrn_prepare(hq, z, lb, tri, last_row, mid_row):
    f = lb + (1.0 - lb) * jax.nn.sigmoid(z)
    fm = jnp.maximum(f, FORGET_FLOOR)
    g = jnp.log(fm)
    kk = 1.0 - fm
    gh, gl = _split(g)
    cum = _dot(tri, gh) + _dot(tri, gl)
    tail = cum[last_row:last_row + 1, :]
    e = cum - cum[mid_row:mid_row + 1, :]
    emax = jnp.max(jnp.max(jnp.abs(e), axis=1, keepdims=True), axis=0, keepdims=True)
    ec = jnp.clip(e, -HG_CLIP_EXP, HG_CLIP_EXP)
    qe = (hq * jnp.exp(ec)).astype(BF16)
    ke = (kk * jnp.exp(-ec)).astype(BF16)
    return kk, cum, tail, emax, qe, ke


def _hgrn_kernel(hqf_ref, zf_ref, vf_ref, hqb_ref, zb_ref, vb_ref, lbf_ref, lbb_ref,
                 of_ref, ob_ref, st_ref, a_ref, cum_ref, k_ref):
    j = pl.program_id(1)

    @pl.when(j == 0)
    def _():
        st_ref[...] = jnp.zeros_like(st_ref)

    CH = HG_CHUNK
    GW = HG_HEADS_PER_GROUP * HEAD_DIM
    ngroups = hqf_ref.shape[1] // GW
    nchunks = hqf_ref.shape[0] // CH
    t_i = lax.broadcasted_iota(I32, (CH, CH), 0)
    u_i = lax.broadcasted_iota(I32, (CH, CH), 1)
    tris = ((u_i <= t_i).astype(BF16), (u_i >= t_i).astype(BF16))
    t_w = lax.broadcasted_iota(I32, (CH, GW), 0)
    s_w = lax.broadcasted_iota(I32, (CH, GW), 1) % CH
    cmasks = (s_w <= t_w, s_w >= t_w)
    r_b = lax.broadcasted_iota(I32, (GW, GW), 0)
    c_b = lax.broadcasted_iota(I32, (GW, GW), 1)
    bm = (r_b // HEAD_DIM) == (c_b // HEAD_DIM)
    hsel_base = (r_b // HEAD_DIM) * HEAD_DIM
    dirs = ((hqf_ref, zf_ref, vf_ref, lbf_ref, of_ref), (hqb_ref, zb_ref, vb_ref, lbb_ref, ob_ref))
    chains = [(d, g) for d in range(2) for g in range(ngroups)]

    def expand(m):
        return jnp.where(bm, jnp.concatenate([m] * HG_HEADS_PER_GROUP, axis=0), jnp.zeros((GW, GW), m.dtype))

    def chunk_body(c, carry):
        rows = (pl.multiple_of(c * CH, CH), pl.multiple_of((nchunks - 1 - c) * CH, CH))
        prepared = []
        worst = jnp.zeros((1, 1), F32)
        for n, (d, g) in enumerate(chains):
            hq_r, z_r, v_r, lb_r, _ = dirs[d]
            ls = slice(g * GW, (g + 1) * GW)
            hq = hq_r[pl.ds(rows[d], CH), ls]
            z = z_r[pl.ds(rows[d], CH), ls]
            v = v_r[pl.ds(rows[d], CH), ls]
            last_row = CH - 1 if d == 0 else 0
            kk, cum, tail, emax, qe, ke = _hgrn_prepare(hq, z, lb_r[:, ls], tris[d], last_row, CH // 2)
            worst = jnp.maximum(worst, emax)
            a_ref[n] = _dot_nt(qe, expand(ke))
            cum_ref[n] = cum
            k_ref[n] = kk
            prepared.append((hq, v, kk, cum, tail))

        @pl.when(worst[0, 0] > HG_SAFE_EXP)
        def _():
            for n, (d, g) in enumerate(chains):
                hq_r = dirs[d][0]
                hq = hq_r[pl.ds(rows[d], CH), g * GW:(g + 1) * GW]
                cum = cum_ref[n]

                def key_body(s, acc):
                    cs = cum_ref[n, pl.ds(s, 1), :]
                    ksr = k_ref[n, pl.ds(s, 1), :]
                    p = hq * ksr * jnp.exp(jnp.minimum(cum - cs, 0.0))
                    hsel = jnp.where(c_b == hsel_base + s, 1.0, 0.0).astype(BF16)
                    return acc + _dot(p.astype(BF16), hsel)

                a_ref[n] = lax.fori_loop(0, CH, key_body, jnp.zeros((CH, GW), F32))

        for n, (d, g) in enumerate(chains):
            hq, v, kk, cum, tail = prepared[n]
            o_r = dirs[d][4]
            st = st_ref[n]
            a = jnp.where(cmasks[d], a_ref[n], 0.0).astype(BF16)
            vb = v.astype(BF16)
            o = _dot(a, expand(vb)) + _dot_nt((hq * jnp.exp(cum)).astype(BF16), st.astype(BF16))
            o_r[pl.ds(rows[d], CH), g * GW:(g + 1) * GW] = o
            k2 = (kk * jnp.exp(tail - cum)).astype(BF16)
            st_ref[n] = jnp.exp(tail) * st + jnp.where(bm, _dot_tn(vb, k2), 0.0)
        return carry

    lax.fori_loop(0, nchunks, chunk_body, 0)


def _hgrn2(dm, hq, ff, fb, hi, lb):
    B, T = dm.B, dm.T
    Wd = hq.shape[1]
    GW = HG_HEADS_PER_GROUP * HEAD_DIM
    nchain = 2 * (Wd // GW)
    nblk = 1 + dm.N // SEQ_BLOCK
    fwd, bwd = _seq_block_maps(dm)
    fspec = pl.BlockSpec((SEQ_BLOCK, Wd), lambda b, j: (fwd(b, j), 0))
    bspec = pl.BlockSpec((SEQ_BLOCK, Wd), lambda b, j: (bwd(b, j), 0))
    lbspec = pl.BlockSpec((1, Wd), lambda b, j: (0, 0))
    return pl.pallas_call(
        _hgrn_kernel,
        grid=(B, nblk),
        in_specs=[fspec, fspec, fspec, bspec, bspec, bspec, lbspec, lbspec],
        out_specs=[fspec, bspec],
        out_shape=[jax.ShapeDtypeStruct((T, Wd), F32)] * 2,
        scratch_shapes=[pltpu.VMEM((nchain, GW, GW), F32),
                        pltpu.VMEM((nchain, HG_CHUNK, GW), F32),
                        pltpu.VMEM((nchain, HG_CHUNK, GW), F32),
                        pltpu.VMEM((nchain, HG_CHUNK, GW), F32)],
        compiler_params=_cparams(("arbitrary", "arbitrary")),
        name="hgrn2_bidirectional",
    )(hq, ff, hi, hq, fb, hi, lb[0:1], lb[1:2])


def _lru_scan(a, x, reverse):
    n = a.shape[0]
    row = lax.broadcasted_iota(I32, a.shape, 0)
    s = 1
    while s < n:
        shift = (n - s) if reverse else s
        a_sh = pltpu.roll(a, shift, 0)
        x_sh = pltpu.roll(x, shift, 0)
        valid = (row < n - s) if reverse else (row >= s)
        x = jnp.where(valid, a * x_sh + x, x)
        a = jnp.where(valid, a * a_sh, a)
        s *= 2
    return a, x


def _lru_kernel(ucf_ref, upf_ref, unf_ref, ucb_ref, upb_ref, unb_ref, cw_ref, cb_ref, wbd_ref,
                ba_ref, bx_ref, lam_ref, hf_ref, hb_ref, ext_ref, carry_ref):
    j = pl.program_id(1)
    nblk = pl.num_programs(1)

    @pl.when(j == 0)
    def _():
        carry_ref[...] = jnp.zeros_like(carry_ref)

    TT, W = ucf_ref.shape
    GW = wbd_ref.shape[2]
    halo = upf_ref.shape[0]
    dirs = ((ucf_ref, upf_ref, unf_ref, hf_ref), (ucb_ref, upb_ref, unb_ref, hb_ref))
    for d, (uc_r, up_r, un_r, out_r) in enumerate(dirs):
        pos = j if d == 0 else jnp.where(j == 0, 0, nblk - j)
        keep_prev = jnp.where((pos == 0) | (pos == 1), 0.0, 1.0)
        keep_next = jnp.where((pos == 0) | (pos == nblk - 1), 0.0, 1.0)
        ext_ref[d, 0:halo, :] = up_r[...] * keep_prev
        ext_ref[d, halo:halo + TT, :] = uc_r[...]
        ext_ref[d, halo + TT:halo + TT + halo, :] = un_r[...] * keep_next
        for g in range(W // GW):
            ls = slice(g * GW, (g + 1) * GW)
            u = cb_ref[:, ls]
            for tap in range(CONV_W):
                off = halo - 2 + tap
                u = u + cw_ref[tap:tap + 1, ls] * ext_ref[d, off:off + TT, ls]
            gates = _dot(u.astype(BF16), wbd_ref[d, g])
            r = jax.nn.sigmoid(gates[:, :GW] + ba_ref[d:d + 1, ls])
            ig = jax.nn.sigmoid(gates[:, GW:] + bx_ref[d:d + 1, ls])
            nl = -lam_ref[d:d + 1, ls]
            softplus = jnp.maximum(nl, 0.0) + jnp.log1p(jnp.exp(-jnp.abs(nl)))
            log_a = -LRU_C * r * softplus
            a = jnp.exp(log_a)
            x_in = jnp.sqrt(1.0 - jnp.exp(2.0 * log_a)) * ig * u
            a_acc, h = _lru_scan(a, x_in, d == 1)
            h = h + a_acc * carry_ref[d, 0:1, ls]
            out_r[:, ls] = h
            last = 0 if d == 1 else TT - 1
            carry_ref[d, 0:1, ls] = h[last:last + 1, :]


def _block_diag_gates(w_a, w_x, group):
    ndir, K, d, _ = w_a.shape
    per = group // d
    eye = jnp.eye(per, dtype=w_a.dtype)

    def bd(w):
        w = w.reshape(ndir, K // per, per, d, d)
        full = jnp.einsum('ngkde,kl->ngkdle', w, eye)
        return full.reshape(ndir, K // per, group, group)

    return jnp.concatenate([bd(w_a), bd(w_x)], axis=-1)


def _rglru(dm, u, conv_w, conv_b, w_a, b_a, w_x, b_x, lam):
    B, T = dm.B, dm.T
    W = u.shape[1]
    GW = 256
    halo = 8
    nblk = 1 + dm.N // SEQ_BLOCK
    fwd, bwd = _seq_block_maps(dm)
    hb = SEQ_BLOCK // halo
    nh = T // halo
    wbd = _block_diag_gates(w_a, w_x, GW).astype(BF16)

    def cur(m):
        return pl.BlockSpec((SEQ_BLOCK, W), lambda b, j: (m(b, j), 0))

    def prev(m):
        return pl.BlockSpec((halo, W), lambda b, j: (jnp.maximum(m(b, j) * hb - 1, 0), 0))

    def nxt(m):
        return pl.BlockSpec((halo, W), lambda b, j: (jnp.minimum((m(b, j) + 1) * hb, nh - 1), 0))

    def full(a):
        nd = a.ndim
        return pl.BlockSpec(a.shape, lambda b, j: (0,) * nd)

    consts = (conv_w, conv_b.reshape(1, W), wbd, b_a, b_x, lam)
    return pl.pallas_call(
        _lru_kernel,
        grid=(B, nblk),
        in_specs=[cur(fwd), prev(fwd), nxt(fwd), cur(bwd), prev(bwd), nxt(bwd)] + [full(a) for a in consts],
        out_specs=[cur(fwd), cur(bwd)],
        out_shape=[jax.ShapeDtypeStruct((T, W), F32)] * 2,
        scratch_shapes=[pltpu.VMEM((2, SEQ_BLOCK + 2 * halo, W), F32), pltpu.VMEM((2, 8, W), F32)],
        compiler_params=_cparams(("arbitrary", "arbitrary")),
        name="rglru_bidirectional",
    )(u, u, u, u, u, u, *consts)


def _post_mixer_epilogue(o, x_ref, m2_ref, m3_ref, m4_ref, lng_ref, lnb_ref, rwh_ref, rwl_ref,
                         xo_ref, h2_ref, lg_ref, alpha):
    z = alpha * x_ref[...] + m2_ref[...] * o
    xn = _layer_norm(z, lng_ref[...], lnb_ref[...])
    xo_ref[...] = xn
    h2 = xn * (1.0 + m4_ref[...]) + m3_ref[...]
    h2_ref[...] = _pack_rows(h2)
    hh, hl = _split(h2)
    lg_ref[...] = _dot_nt(rwh_ref[...], hh) + _dot_nt(rwh_ref[...], hl) + _dot_nt(rwl_ref[...], hh)


def _post_even_kernel(nac_ref, nal_ref, of_ref, ob_ref, hg_ref, ng_ref, w_ref, *rest, alpha, ctx_tiles):
    na = jnp.where(pl.program_id(0) < ctx_tiles, nac_ref[...], nal_ref[...])
    o = of_ref[...] + ob_ref[...]
    wd = o.shape[1]
    r_i = lax.broadcasted_iota(I32, (wd, wd), 0) // HEAD_DIM
    c_i = lax.broadcasted_iota(I32, (wd, wd), 1) // HEAD_DIM
    avg = jnp.where(r_i == c_i, 1.0 / HEAD_DIM, 0.0).astype(BF16)
    sh, sl = _split(o * o)
    ms = _dot(sh, avg) + _dot(sl, avg)
    r = o * lax.rsqrt(ms + RMS_EPS) * ng_ref[...] * _silu(hg_ref[...])
    mix = _dot(na, w_ref[:wd, :]) + _dot(r.astype(BF16), w_ref[wd:, :])
    _post_mixer_epilogue(mix, *rest, alpha=alpha)


def _post_odd_kernel(hf_ref, hb_ref, y_ref, w_ref, *rest, alpha):
    m = ((hf_ref[...] + hb_ref[...]) * _gelu_tanh(y_ref[...])).astype(BF16)
    _post_mixer_epilogue(_dot(m, w_ref[...]), *rest, alpha=alpha)


def _post_mixer(dm, mixer_inputs, w_out, xs, m2, m3, m4, lng, lnb, rwh, rwl, alpha, even):
    T, D, TM = dm.T, dm.D, dm.TM
    E = rwh.shape[0]
    row = lambda a: pl.BlockSpec((TM, a.shape[1]), lambda i: (i, 0))
    const = lambda a: pl.BlockSpec(a.shape, lambda i: (0,) * a.ndim)
    mod_spec = pl.BlockSpec((None, 1, D), lambda i: (dm.group(i, TM), 0, 0))
    if even:
        nac, nal, of, ob, hg, ng = mixer_inputs
        nct = dm.T_ctx // TM
        ins = [nac, nal, of, ob, hg, ng, w_out]
        specs = [pl.BlockSpec((TM, nac.shape[1]), lambda i: (jnp.minimum(i, nct - 1), 0)),
                 pl.BlockSpec((TM, nal.shape[1]), lambda i: (jnp.maximum(i - nct, 0), 0)),
                 row(of), row(ob), row(hg), const(ng), const(w_out)]
        kern = functools.partial(_post_even_kernel, alpha=alpha, ctx_tiles=nct)
    else:
        hf, hb, y = mixer_inputs
        ins = [hf, hb, y, w_out]
        specs = [row(hf), row(hb), row(y), const(w_out)]
        kern = functools.partial(_post_odd_kernel, alpha=alpha)
    ins += [xs, m2, m3, m4, lng, lnb, rwh, rwl]
    specs += [row(xs), mod_spec, mod_spec, mod_spec, const(lng), const(lnb), const(rwh), const(rwl)]
    return pl.pallas_call(
        kern,
        grid=(T // TM,),
        in_specs=specs,
        out_specs=[pl.BlockSpec((TM, D), lambda i: (i, 0)), pl.BlockSpec((TM, D // 2), lambda i: (i, 0)),
                   pl.BlockSpec((E, TM), lambda i: (0, i))],
        out_shape=[jax.ShapeDtypeStruct((T, D), F32), jax.ShapeDtypeStruct((T, D // 2), jnp.uint32),
                   jax.ShapeDtypeStruct((E, T), F32)],
        compiler_params=_cparams(("parallel",)),
        name="post_mixer_even" if even else "post_mixer_odd",
    )(*ins)


def _route_kernel(lg_ref, bias_ref, tri_ref, idx_ref, rank_ref, w_ref, cnt_ref, carry_ref):
    i = pl.program_id(0)

    @pl.when(i == 0)
    def _():
        carry_ref[...] = jnp.zeros_like(carry_ref)

    E, TK = lg_ref.shape
    s = jax.nn.sigmoid(lg_ref[...])
    work = s + bias_ref[...]
    eio = lax.broadcasted_iota(I32, (E, TK), 0).astype(F32)
    picked = jnp.zeros((E, TK), F32)
    sels, idxs = [], []
    for _ in range(TOP_K):
        m = jnp.max(work, axis=0, keepdims=True)
        ik = jnp.min(jnp.where(work == m, eio, float(E)), axis=0, keepdims=True)
        oh = eio == ik
        sels.append(jnp.sum(jnp.where(oh, s, 0.0), axis=0, keepdims=True))
        idxs.append(ik)
        picked = jnp.where(oh, 1.0, picked)
        work = jnp.where(oh, -jnp.inf, work)
    total = sels[0]
    for sk in sels[1:]:
        total = total + sk
    carry = carry_ref[:, 0:1]
    ranks = carry + _dot(picked.astype(BF16), tri_ref[...])
    out_rows = idx_ref.shape[0]
    rio = lax.broadcasted_iota(I32, (out_rows, TK), 0)
    idx_o = jnp.zeros((out_rows, TK), I32)
    rank_o = jnp.zeros((out_rows, TK), I32)
    w_o = jnp.zeros((out_rows, TK), F32)
    for kx in range(TOP_K):
        rk = jnp.sum(jnp.where(eio == idxs[kx], ranks, 0.0), axis=0, keepdims=True).astype(I32)
        idx_o = jnp.where(rio == kx, idxs[kx].astype(I32), idx_o)
        rank_o = jnp.where(rio == kx, rk, rank_o)
        w_o = jnp.where(rio == kx, sels[kx] / total * ROUTED_SCALE, w_o)
    idx_ref[...] = idx_o
    rank_ref[...] = rank_o
    w_ref[...] = w_o
    new_carry = carry + jnp.sum(picked, axis=1, keepdims=True)
    carry_ref[...] = jnp.broadcast_to(new_carry, carry_ref.shape)
    cnt_ref[...] = jnp.broadcast_to(new_carry, cnt_ref.shape)


def _route(logits_t, router_bias, t0):
    E, T = logits_t.shape
    Tm = T - t0
    TK = _pick_tile(int(np.gcd(Tm, t0)), (1024, 512, 256))
    off = t0 // TK
    tri = jnp.asarray(np.triu(np.ones((TK, TK), np.float32), 1), BF16)
    tok = lambda: pl.BlockSpec((8, TK), lambda i: (0, i))
    idx, rank, w, cnt = pl.pallas_call(
        _route_kernel,
        grid=(Tm // TK,),
        in_specs=[pl.BlockSpec((E, TK), lambda i: (0, i + off)),
                  pl.BlockSpec((E, 1), lambda i: (0, 0)),
                  pl.BlockSpec((TK, TK), lambda i: (0, 0))],
        out_specs=[tok(), tok(), tok(), pl.BlockSpec((E, 128), lambda i: (0, 0))],
        out_shape=[jax.ShapeDtypeStruct((8, Tm), I32), jax.ShapeDtypeStruct((8, Tm), I32),
                   jax.ShapeDtypeStruct((8, Tm), F32), jax.ShapeDtypeStruct((E, 128), F32)],
        scratch_shapes=[pltpu.VMEM((E, 128), F32)],
        compiler_params=_cparams(("arbitrary",)),
        name="moe_route",
    )(logits_t, router_bias.reshape(E, 1), tri)
    return idx, rank, w, cnt[:, 0].astype(I32)


def _sorted_tile_rows(n_exp):
    return -(-(MOE_TILE * TOP_K + n_exp * (ROW_ALIGN - 1)) // 16) * 16


def _run_copies(cnt_ref, src_ref, dst_ref, base, n_exp, chunk, make_copy):
    shift = chunk.bit_length() - 1

    def expert_body(e, total):
        n = cnt_ref[base + e]
        src = src_ref[base + e]
        dst = dst_ref[base + e]
        nch = lax.shift_right_logical(n + (chunk - 1), shift)

        def chunk_body(c, carry):
            make_copy(pl.multiple_of(src + c * chunk, ROW_ALIGN), pl.multiple_of(dst + c * chunk, ROW_ALIGN)).start()
            return carry

        lax.fori_loop(0, nch, chunk_body, 0)
        return total + nch

    return lax.fori_loop(0, n_exp, expert_body, jnp.int32(0))


def _dispatch_kernel(cnt_ref, ls_ref, gb_ref, pos_ref, h_ref, xs_hbm, stage, sem, *, n_exp):
    i = pl.program_id(0)
    TMd = h_ref.shape[0]
    R = stage.shape[0] - DISP_CHUNK

    @pl.when(i == 0)
    def _():
        stage[R:, :] = jnp.zeros((stage.shape[0] - R, stage.shape[1]), stage.dtype)

    rio = lax.broadcasted_iota(I32, (R, TMd), 0)
    perm = jnp.zeros((R, TMd), F32)
    for kx in range(TOP_K):
        perm = jnp.where(rio == pos_ref[kx:kx + 1, :], 1.0, perm)
    stage[0:R, :] = _pack_rows(_dot(perm.astype(BF16), _unpack_rows(h_ref[...])))

    def make_copy(src, dst):
        return pltpu.make_async_copy(stage.at[pl.ds(src, DISP_CHUNK)], xs_hbm.at[pl.ds(dst, DISP_CHUNK)], sem)

    total = _run_copies(cnt_ref, ls_ref, gb_ref, i * n_exp, n_exp, DISP_CHUNK, make_copy)

    def wait_body(c, carry):
        make_copy(0, 0).wait()
        return carry

    lax.fori_loop(0, total, wait_body, 0)


def _dispatch(h2, plan, t0, n_slots, n_exp):
    cnt, ls, gb, pos = plan
    T, DP = h2.shape
    TMd = MOE_TILE
    Tm = T - t0
    assert Tm % TMd == 0 and t0 % TMd == 0
    off = t0 // TMd
    grid_spec = pltpu.PrefetchScalarGridSpec(
        num_scalar_prefetch=3,
        grid=(Tm // TMd,),
        in_specs=[pl.BlockSpec((8, TMd), lambda i, *_: (0, i)),
                  pl.BlockSpec((TMd, DP), lambda i, *_: (i + off, 0))],
        out_specs=pl.BlockSpec(memory_space=pl.ANY),
        scratch_shapes=[pltpu.VMEM((_sorted_tile_rows(n_exp) + DISP_CHUNK, DP), h2.dtype),
                        pltpu.SemaphoreType.DMA],
    )
    return pl.pallas_call(
        functools.partial(_dispatch_kernel, n_exp=n_exp),
        grid_spec=grid_spec,
        out_shape=jax.ShapeDtypeStruct((n_slots, DP), h2.dtype),
        compiler_params=_cparams(("arbitrary",)),
        name="moe_dispatch",
    )(cnt, ls, gb, pos, h2)


def _gmm_kernel(bexp_ref, bvalid_ref, x_ref, wg_ref, wu_ref, wd_ref, y_ref):
    del bexp_ref
    i = pl.program_id(0)
    nvalid = bvalid_ref[i]

    @pl.when(nvalid > 0)
    def _():
        row = lax.broadcasted_iota(I32, x_ref.shape, 0)
        xb = _unpack_rows(jnp.where(row < nvalid, x_ref[...], jnp.uint32(0)))
        act = _silu(_dot(xb, wg_ref[...].astype(BF16))) * _dot(xb, wu_ref[...].astype(BF16))
        y_ref[...] = _pack_rows(_dot(act.astype(BF16), wd_ref[...].astype(BF16)))

    @pl.when(nvalid == 0)
    def _():
        y_ref[...] = jnp.zeros_like(y_ref)


def _grouped_experts(xsorted, block_exp, block_valid, wg, wu, wd, layer):
    NS, DP = xsorted.shape
    nb = NS // MOE_BLOCK
    _, _, D, hid = wg.shape
    grid_spec = pltpu.PrefetchScalarGridSpec(
        num_scalar_prefetch=2,
        grid=(nb,),
        in_specs=[pl.BlockSpec((MOE_BLOCK, DP), lambda i, be, bv: (i, 0)),
                  pl.BlockSpec((None, None, D, hid), lambda i, be, bv: (layer, be[i], 0, 0)),
                  pl.BlockSpec((None, None, D, hid), lambda i, be, bv: (layer, be[i], 0, 0)),
                  pl.BlockSpec((None, None, hid, D), lambda i, be, bv: (layer, be[i], 0, 0))],
        out_specs=pl.BlockSpec((MOE_BLOCK, DP), lambda i, be, bv: (i, 0)),
    )
    return pl.pallas_call(
        _gmm_kernel,
        grid_spec=grid_spec,
        out_shape=jax.ShapeDtypeStruct((NS, DP), jnp.uint32),
        compiler_params=_cparams(("arbitrary",)),
        name="moe_grouped_experts",
    )(block_exp, block_valid, xsorted, wg, wu, wd)


def _combine_kernel(cnt_ref, so_ref, gb_ref, y_hbm, qpos_ref, w_ref, h_ref, sgu_ref, sd_ref, x_ref, m5_ref,
                    lng_ref, lnb_ref, o_ref, stage, sem, *, alpha, n_exp):
    i = pl.program_id(0)

    @pl.when(i == 0)
    def _():
        stage[...] = jnp.zeros_like(stage)

    def make_copy(dst, src):
        return pltpu.make_async_copy(y_hbm.at[pl.ds(src, COMB_CHUNK)], stage.at[pl.ds(dst, COMB_CHUNK)], sem)

    total = _run_copies(cnt_ref, so_ref, gb_ref, i * n_exp, n_exp, COMB_CHUNK, make_copy)

    hid = sd_ref.shape[0]
    gu = _dot(_unpack_rows(h_ref[...]), sgu_ref[...])
    y = _dot((_silu(gu[:, :hid]) * gu[:, hid:]).astype(BF16), sd_ref[...])

    def wait_body(c, carry):
        make_copy(0, 0).wait()
        return carry

    lax.fori_loop(0, total, wait_body, 0)

    TMc, S = h_ref.shape[0], stage.shape[0]
    sio = lax.broadcasted_iota(I32, (TMc, S), 1)
    q = jnp.zeros((TMc, S), F32)
    for kx in range(TOP_K):
        q = jnp.where(sio == qpos_ref[:, kx:kx + 1], w_ref[:, kx:kx + 1], q)
    y = y + _dot(q.astype(BF16), _unpack_rows(stage[...]))
    z = alpha * x_ref[...] + m5_ref[...] * y
    o_ref[...] = _layer_norm(z, lng_ref[...], lnb_ref[...])


def _combine(dm, plan, ysorted, qpos_tm, w_tm, h2, sgu, sd, xs, m5, lng, lnb, alpha, t0, n_exp):
    cnt, so, gb = plan
    T, D = xs.shape
    TMc = MOE_TILE
    Tm = T - t0
    off = t0 // TMc
    const = lambda a: pl.BlockSpec(a.shape, lambda i, *_: (0,) * a.ndim)
    row = lambda a: pl.BlockSpec((TMc, a.shape[1]), lambda i, *_: (i + off, 0))
    grid_spec = pltpu.PrefetchScalarGridSpec(
        num_scalar_prefetch=3,
        grid=(Tm // TMc,),
        in_specs=[pl.BlockSpec(memory_space=pl.ANY),
                  pl.BlockSpec((TMc, 8), lambda i, *_: (i, 0)), pl.BlockSpec((TMc, 8), lambda i, *_: (i, 0)),
                  row(h2), const(sgu), const(sd), row(xs),
                  pl.BlockSpec((None, 1, D), lambda i, *_: (dm.group(i + off, TMc), 0, 0)),
                  const(lng), const(lnb)],
        out_specs=pl.BlockSpec((TMc, D), lambda i, *_: (i, 0)),
        scratch_shapes=[pltpu.VMEM((_sorted_tile_rows(n_exp), ysorted.shape[1]), ysorted.dtype),
                        pltpu.SemaphoreType.DMA],
    )
    return pl.pallas_call(
        functools.partial(_combine_kernel, alpha=alpha, n_exp=n_exp),
        grid_spec=grid_spec,
        out_shape=jax.ShapeDtypeStruct((Tm, D), F32),
        compiler_params=_cparams(("arbitrary",)),
        name="moe_combine",
    )(cnt, so, gb, ysorted, qpos_tm, w_tm, h2, sgu, sd, xs, m5, lng, lnb)


def _moe(dm, h2, logits_t, xs, m5, lng, lnb, router_bias, wg, wu, wd, layer, sgu, sd, alpha, t0):
    T = h2.shape[0]
    E = wg.shape[1]
    Tm = T - t0
    nt = Tm // MOE_TILE
    idx, rank, w, _ = _route(logits_t, router_bias, t0)
    idx, rank, w = idx[:TOP_K], rank[:TOP_K], w[:TOP_K]
    eids = jnp.arange(E, dtype=I32)
    onehot = (idx[:, :, None] == eids).reshape(TOP_K, nt, MOE_TILE, E)
    cnt = jnp.sum(onehot, axis=(0, 2), dtype=I32)
    rank_base = jnp.cumsum(cnt, axis=0) - cnt
    run = (cnt + ROW_ALIGN - 1) // ROW_ALIGN * ROW_ALIGN
    tile_base = jnp.cumsum(run, axis=0) - run
    local_start = jnp.cumsum(run, axis=1) - run
    counts = jnp.sum(run, axis=0)
    nb = -(-(Tm * TOP_K + E * (nt * (ROW_ALIGN - 1) + MOE_SLACK)) // MOE_BLOCK) + E
    padded = (counts + MOE_SLACK + MOE_BLOCK - 1) // MOE_BLOCK * MOE_BLOCK
    pends = jnp.cumsum(padded)
    pstarts = pends - padded
    global_base = pstarts[None, :] + tile_base
    bstart = jnp.arange(nb, dtype=I32) * MOE_BLOCK
    block_exp = jnp.minimum(jnp.sum((pends[None, :] <= bstart[:, None]).astype(I32), axis=1), E - 1)
    of_block = block_exp[:, None] == eids[None, :]
    seg_end = jnp.sum(jnp.where(of_block, (pstarts + counts)[None, :], 0), axis=1)
    block_valid = jnp.clip(seg_end - bstart, 0, MOE_BLOCK).astype(I32)
    look = lambda tab: jnp.sum(jnp.where(onehot, tab[None, :, None, :], 0), axis=3).reshape(TOP_K, Tm)
    pos = rank - look(rank_base) + look(local_start)
    pos8 = jnp.pad(pos, ((0, 8 - TOP_K), (0, 0)), constant_values=-1).astype(I32)
    pos_tm = jnp.pad(pos.T, ((0, 0), (0, 8 - TOP_K)), constant_values=-1).astype(I32)
    w_tm = jnp.pad(w.T, ((0, 0), (0, 8 - TOP_K)))
    flat = lambda a: a.reshape(-1).astype(I32)
    plan = (flat(run), flat(local_start), flat(global_base))
    xsorted = _dispatch(h2, plan + (pos8,), t0, nb * MOE_BLOCK, E)
    ysorted = _grouped_experts(xsorted, block_exp, block_valid, wg, wu, wd, layer)
    return _combine(dm, plan, ysorted, pos_tm, w_tm, h2, sgu, sd, xs, m5, lng, lnb, alpha, t0, E)


def kernel(x, c, ctx, c_ctx, ada_w, ada_b, ln_g, ln_b, ev_w_in, ev_w_out, na_rpb, hg_lb_raw, hg_norm_g, od_w_in, od_conv_w, od_conv_b, lru_w_a, lru_b_a, lru_w_x, lru_b_x, lru_lam, od_w_out, router_w, router_bias, exp_w_gate, exp_w_up, exp_w_down, sh_w_gate, sh_w_up, sh_w_down):
    B, N, D = x.shape
    C = ctx.shape[1]
    depth = ada_w.shape[0]
    dm = _Dims(B, N, C, D)
    alpha = float((2 * depth) ** 0.25)
    assert B + 1 <= 8

    cvec = jnp.zeros((8, D), F32).at[:B].set(c).at[B].set(c_ctx)
    mod = _modulation(cvec, ada_w, ada_b).reshape(depth, 8, 6, 1, D)

    p_lb = jax.nn.softmax(hg_lb_raw.astype(F32), axis=1)
    hg_lb = jnp.cumsum(p_lb, axis=1) - p_lb[:, :1]

    xs = jnp.concatenate([ctx.reshape(B * C, D), x.reshape(B * N, D)], axis=0)
    for l in range(depth):
        jl = l // 2
        last = l == depth - 1
        m = [mod[l, :, t] for t in range(6)]
        rw_t = router_w[l].T
        rwh, rwl = _split(rw_t)
        lng = ln_g[l][:, None, :]
        lnb = ln_b[l][:, None, :]
        if l % 2 == 0:
            q, k, v, hq, ff, fb, hi, hg = _inproj(dm, xs, m[1], m[0], ev_w_in[jl].astype(BF16), True)
            bias = _na_bias_tables(na_rpb[jl], N // GRID_W)
            na_lat = _neighbourhood_attention(dm, q, k, v, bias)
            na_ctx = _context_attention(dm, q, k, v)
            o_f, o_b = _hgrn2(dm, hq, ff, fb, hi, hg_lb[:, jl])
            ng = jnp.tile(hg_norm_g[jl], hq.shape[1] // HEAD_DIM)[None, :]
            mixer_inputs = (na_ctx, na_lat, o_f, o_b, hg, ng)
            w_out = ev_w_out[jl].astype(BF16)
        else:
            y, u = _inproj(dm, xs, m[1], m[0], od_w_in[jl].astype(BF16), False)
            h_f, h_b = _rglru(dm, u, od_conv_w[jl], od_conv_b[jl], lru_w_a[jl], lru_b_a[jl],
                              lru_w_x[jl], lru_b_x[jl], lru_lam[jl])
            mixer_inputs = (h_f, h_b, y)
            w_out = od_w_out[jl].astype(BF16)
        xs, h2, logits_t = _post_mixer(dm, mixer_inputs, w_out, xs, m[2], m[3], m[4], lng[0], lnb[0],
                                       rwh, rwl, alpha, l % 2 == 0)
        sgu = jnp.concatenate([sh_w_gate[l], sh_w_up[l]], axis=-1).astype(BF16)
        sd = sh_w_down[l].astype(BF16)
        t0 = dm.T_ctx if last else 0
        xs = _moe(dm, h2, logits_t, xs, m[5], lng[1], lnb[1], router_bias[l], exp_w_gate, exp_w_up,
                  exp_w_down, l, sgu, sd, alpha, t0)
    return xs.reshape(B, N, D)
```

```python
import functools

import numpy as np
import jax
import jax.numpy as jnp
from jax import lax
from jax.experimental import pallas as pl
from jax.experimental.pallas import tpu as pltpu

F32 = jnp.float32
BF16 = jnp.bfloat16
I32 = jnp.int32

HEAD_DIM = 64
GRID_W = 64
NA_WIN_ROWS = 8
NA_WIN_COLS = 16
NA_QROWS = 8
NA_KROWS = 16
HG_CHUNK = 64
HG_HEADS_PER_GROUP = 4
FORGET_FLOOR = 1e-30
HG_SAFE_EXP = 80.0
HG_CLIP_EXP = 85.0
LRU_C = 8.0
LRU_BLOCKS = 16
CONV_W = 4
TOP_K = 6
ROUTED_SCALE = 2.5
LN_EPS = 1e-5
RMS_EPS = 1e-6
SEQ_BLOCK = 256
MOE_BLOCK = 512
MOE_TILE = 256
ROW_ALIGN = 8
DISP_CHUNK = 32
COMB_CHUNK = ROW_ALIGN
MOE_SLACK = DISP_CHUNK
MASK_VALUE = -1e30
VMEM_LIMIT = 56 * 1024 * 1024


def _cparams(sem):
    return pltpu.CompilerParams(dimension_semantics=sem, vmem_limit_bytes=VMEM_LIMIT)


def _split(a):
    hi = a.astype(BF16)
    lo = (a - hi.astype(F32)).astype(BF16)
    return hi, lo


def _dot(a, b):
    return jnp.dot(a, b, preferred_element_type=F32)


def _dot_nt(a, b):
    return lax.dot_general(a, b, (((1,), (1,)), ((), ())), preferred_element_type=F32)


def _dot_tn(a, b):
    return lax.dot_general(a, b, (((0,), (0,)), ((), ())), preferred_element_type=F32)


def _dot3(a, b):
    ah, al = _split(a)
    bh, bl = _split(b)
    return _dot(ah, bh) + _dot(al, bh) + _dot(ah, bl)


def _pack_rows(h):
    half = h.shape[1] // 2
    bits = pltpu.bitcast(h.astype(BF16).astype(F32), jnp.uint32)
    return (bits[:, :half] >> 16) | (bits[:, half:] & jnp.uint32(0xFFFF0000))


def _unpack_rows(p):
    lo = pltpu.bitcast(p << 16, F32)
    hi = pltpu.bitcast(p & jnp.uint32(0xFFFF0000), F32)
    return jnp.concatenate([lo, hi], axis=1).astype(BF16)


def _silu(v):
    return v * jax.nn.sigmoid(v)


def _gelu_tanh(v):
    return 0.5 * v * (1.0 + jnp.tanh(0.7978845608028654 * (v + 0.044715 * v * v * v)))


def _layer_norm(z, g, b):
    mu = jnp.mean(z, axis=-1, keepdims=True)
    zc = z - mu
    var = jnp.mean(zc * zc, axis=-1, keepdims=True)
    return zc * lax.rsqrt(var + LN_EPS) * g + b


def _pick_tile(n, cands):
    for c in cands:
        if n % c == 0:
            return c
    raise ValueError(f"no tile for {n}")


class _Dims:
    def __init__(self, B, N, C, D):
        self.B, self.N, self.C, self.D = B, N, C, D
        self.T_ctx = B * C
        self.T = B * C + B * N
        assert C == SEQ_BLOCK and N % SEQ_BLOCK == 0
        assert N % GRID_W == 0 and (N // GRID_W) % NA_QROWS == 0 and N // GRID_W >= NA_KROWS
        assert self.T_ctx % 512 == 0
        self.TM = _pick_tile(self.T_ctx, (512, 256))
        assert N % self.TM == 0

    def group(self, i, tile):
        start = i * tile
        return jnp.where(start < self.T_ctx, self.B, (start - self.T_ctx) // self.N)


def _mod_kernel(c_ref, w_ref, b_ref, o_ref):
    o_ref[...] = _dot3(_silu(c_ref[...]), w_ref[...]) + b_ref[...]


def _modulation(cvec, ada_w, ada_b):
    L, D, W6 = ada_w.shape
    nc = 1536
    return pl.pallas_call(
        _mod_kernel,
        grid=(L, W6 // nc),
        in_specs=[
            pl.BlockSpec((8, D), lambda l, j: (0, 0)),
            pl.BlockSpec((None, D, nc), lambda l, j: (l, 0, j)),
            pl.BlockSpec((None, 1, nc), lambda l, j: (l, 0, j)),
        ],
        out_specs=pl.BlockSpec((None, 8, nc), lambda l, j: (l, 0, j)),
        out_shape=jax.ShapeDtypeStruct((L, 8, W6), F32),
        compiler_params=_cparams(("arbitrary", "arbitrary")),
        name="adaln_modulation",
    )(cvec, ada_w, ada_b.reshape(L, 1, W6))


def _inproj_even_kernel(x_ref, sc_ref, sh_ref, w_ref, q_ref, k_ref, v_ref, hq_ref, ff_ref, fb_ref, hi_ref, hg_ref):
    h = (x_ref[...] * (1.0 + sc_ref[...]) + sh_ref[...]).astype(BF16)
    wd = q_ref.shape[1]

    def part(j):
        return _dot(h, w_ref[:, j * wd:(j + 1) * wd])

    scale = HEAD_DIM ** -0.5
    q_ref[...] = (part(0) * scale).astype(BF16)
    k_ref[...] = part(1).astype(BF16)
    v_ref[...] = part(2).astype(BF16)
    hq_ref[...] = _silu(part(3)) * scale
    ff_ref[...] = part(4)
    fb_ref[...] = part(5)
    hi_ref[...] = part(6)
    hg_ref[...] = part(7)


def _inproj_odd_kernel(x_ref, sc_ref, sh_ref, w_ref, y_ref, u_ref):
    h = (x_ref[...] * (1.0 + sc_ref[...]) + sh_ref[...]).astype(BF16)
    wd = y_ref.shape[1]
    y_ref[...] = _dot(h, w_ref[:, :wd])
    u_ref[...] = _dot(h, w_ref[:, wd:])


def _inproj(dm, xs, sc, sh, w, even):
    T, D, TM = dm.T, dm.D, dm.TM
    wtot = w.shape[1]
    mod_spec = pl.BlockSpec((None, 1, D), lambda i: (dm.group(i, TM), 0, 0))
    in_specs = [pl.BlockSpec((TM, D), lambda i: (i, 0)), mod_spec, mod_spec,
                pl.BlockSpec((D, wtot), lambda i: (0, 0))]
    if even:
        wd = wtot // 8
        dts = [BF16] * 3 + [F32] * 5
        kern = _inproj_even_kernel
    else:
        wd = wtot // 2
        dts = [F32] * 2
        kern = _inproj_odd_kernel
    return pl.pallas_call(
        kern,
        grid=(T // TM,),
        in_specs=in_specs,
        out_specs=[pl.BlockSpec((TM, wd), lambda i: (i, 0)) for _ in dts],
        out_shape=[jax.ShapeDtypeStruct((T, wd), dt) for dt in dts],
        compiler_params=_cparams(("parallel",)),
        name="inproj_even" if even else "inproj_odd",
    )(xs, sc, sh, w)


def _na_bias_tables(rpb, rows):
    W = GRID_W
    H = rpb.shape[0]
    nr, nc = 2 * NA_WIN_ROWS - 1, 2 * NA_WIN_COLS - 1
    hp = lax.Precision.HIGHEST
    c = np.arange(W)[:, None]
    kc = np.arange(W)[None, :]
    c0 = np.clip(c - NA_WIN_COLS // 2, 0, W - NA_WIN_COLS)
    valid_c = (kc >= c0) & (kc < c0 + NA_WIN_COLS)
    dc = np.clip(kc - c + NA_WIN_COLS - 1, 0, nc - 1)
    oh_c = (dc.reshape(-1, 1) == np.arange(nc)[None, :]).astype(np.float32)
    by_col = jnp.einsum('hrs,xs->hrx', rpb.astype(F32), jnp.asarray(oh_c), precision=hp)
    tabs = []
    for rbase, kb in ((0, 0), (NA_QROWS, NA_QROWS - NA_WIN_ROWS // 2), (rows - NA_QROWS, rows - NA_KROWS)):
        r = rbase + np.arange(NA_QROWS)[:, None]
        kr = kb + np.arange(NA_KROWS)[None, :]
        r0 = np.clip(r - NA_WIN_ROWS // 2, 0, rows - NA_WIN_ROWS)
        valid_r = (kr >= r0) & (kr < r0 + NA_WIN_ROWS)
        dr = np.clip(kr - r + NA_WIN_ROWS - 1, 0, nr - 1)
        oh_r = (dr.reshape(-1, 1) == np.arange(nr)[None, :]).astype(np.float32)
        t = jnp.einsum('yr,hrx->hyx', jnp.asarray(oh_r), by_col, precision=hp)
        t = t.reshape(H, NA_QROWS, NA_KROWS, W, W).transpose(0, 1, 3, 2, 4)
        valid = valid_r[:, None, :, None] & valid_c[None, :, None, :]
        t = jnp.where(jnp.asarray(valid)[None], t, MASK_VALUE)
        tabs.append(t.reshape(H, NA_QROWS * W, NA_KROWS * W))
    return jnp.stack(tabs).astype(F32)


def _na_kernel(q_ref, k0, k1, k2, k3, v0, v1, v2, v3, kc_ref, vc_ref, bias_ref, o_ref):
    q2 = q_ref[...]
    lane = lax.broadcasted_iota(I32, (1, q2.shape[1]), 1)
    first = lane < HEAD_DIM
    ks = [r[...] for r in (k0, k1, k2, k3)]
    vs = [r[...] for r in (v0, v1, v2, v3)]
    kc = kc_ref[...]
    vc = vc_ref[...]
    kb = ks[0].shape[0]
    outs = []
    for hh in range(2):
        sel = first if hh == 0 else jnp.logical_not(first)
        qh = jnp.where(sel, q2, jnp.zeros_like(q2))
        s_loc = jnp.concatenate([_dot_nt(qh, kj) for kj in ks], axis=1) + bias_ref[hh]
        s_ctx = _dot_nt(qh, kc)
        m = jnp.maximum(jnp.max(s_loc, axis=1, keepdims=True), jnp.max(s_ctx, axis=1, keepdims=True))
        p_loc = jnp.exp(s_loc - m)
        p_ctx = jnp.exp(s_ctx - m)
        denom = jnp.sum(p_loc, axis=1, keepdims=True) + jnp.sum(p_ctx, axis=1, keepdims=True)
        o = _dot(p_ctx.astype(BF16), vc)
        for j in range(4):
            o = o + _dot(p_loc[:, j * kb:(j + 1) * kb].astype(BF16), vs[j])
        outs.append(o / denom)
    o_ref[...] = jnp.where(first, outs[0], outs[1]).astype(o_ref.dtype)


def _neighbourhood_attention(dm, q, k, v, bias):
    B, N, C, T = dm.B, dm.N, dm.C, dm.T
    H2 = q.shape[1] // (2 * HEAD_DIM)
    rows = N // GRID_W
    nqb = rows // NA_QROWS
    QB = NA_QROWS * GRID_W
    KB = QB // 2
    ngroups = N // KB
    lat_q0 = dm.T_ctx // QB
    lat_k0 = dm.T_ctx // KB

    def q_map(p, i, b):
        return (lat_q0 + b * (N // QB) + i, p)

    def o_map(p, i, b):
        return (b * (N // QB) + i, p)

    def kv_map(j):
        def f(p, i, b):
            gs = jnp.clip(2 * i - 1, 0, ngroups - 4)
            return (lat_k0 + b * ngroups + gs + j, p)
        return f

    def ctx_map(p, i, b):
        return (b, p)

    def bias_map(p, i, b):
        var = jnp.where(i == 0, 0, jnp.where(i == nqb - 1, 2, 1))
        return (var, p, 0, 0)

    lanes = 2 * HEAD_DIM
    kv_specs = [pl.BlockSpec((KB, lanes), kv_map(j)) for j in range(4)]
    return pl.pallas_call(
        _na_kernel,
        grid=(H2, nqb, B),
        in_specs=[pl.BlockSpec((QB, lanes), q_map)] + kv_specs + kv_specs
        + [pl.BlockSpec((C, lanes), ctx_map), pl.BlockSpec((C, lanes), ctx_map),
           pl.BlockSpec((None, 2, QB, NA_KROWS * GRID_W), bias_map)],
        out_specs=pl.BlockSpec((QB, lanes), o_map),
        out_shape=jax.ShapeDtypeStruct((B * N, q.shape[1]), BF16),
        compiler_params=_cparams(("arbitrary", "arbitrary", "arbitrary")),
        name="neighbourhood_attention",
    )(q, k, k, k, k, v, v, v, v, k, v, bias)


def _ctx_attn_kernel(q_ref, k_ref, v_ref, o_ref):
    q2 = q_ref[...]
    k2 = k_ref[...]
    v2 = v_ref[...]
    lane = lax.broadcasted_iota(I32, (1, q2.shape[1]), 1)
    first = lane < HEAD_DIM
    outs = []
    for hh in range(2):
        sel = first if hh == 0 else jnp.logical_not(first)
        s = _dot_nt(jnp.where(sel, q2, jnp.zeros_like(q2)), k2)
        p = jnp.exp(s - jnp.max(s, axis=1, keepdims=True))
        outs.append(_dot(p.astype(BF16), v2) / jnp.sum(p, axis=1, keepdims=True))
    o_ref[...] = jnp.where(first, outs[0], outs[1]).astype(o_ref.dtype)


def _context_attention(dm, q, k, v):
    lanes = 2 * HEAD_DIM
    spec = pl.BlockSpec((dm.C, lanes), lambda p, b: (b, p))
    return pl.pallas_call(
        _ctx_attn_kernel,
        grid=(q.shape[1] // lanes, dm.B),
        in_specs=[spec, spec, spec],
        out_specs=spec,
        out_shape=jax.ShapeDtypeStruct((dm.T_ctx, q.shape[1]), BF16),
        compiler_params=_cparams(("arbitrary", "arbitrary")),
        name="context_attention",
    )(q, k, v)


def _seq_block_maps(dm):
    nl = dm.N // SEQ_BLOCK
    base = dm.T_ctx // SEQ_BLOCK

    def fwd(b, j):
        return jnp.where(j == 0, b, base + b * nl + j - 1)

    def bwd(b, j):
        return jnp.where(j == 0, b, base + b * nl + nl - j)

    return fwd, bwd


def _hgrn_prepare(hq, z, lb, tri, last_row, mid_row):
    f = lb + (1.0 - lb) * jax.nn.sigmoid(z)
    fm = jnp.maximum(f, FORGET_FLOOR)
    g = jnp.log(fm)
    kk = 1.0 - fm
    gh, gl = _split(g)
    cum = _dot(tri, gh) + _dot(tri, gl)
    tail = cum[last_row:last_row + 1, :]
    e = cum - cum[mid_row:mid_row + 1, :]
    emax = jnp.max(jnp.max(jnp.abs(e), axis=1, keepdims=True), axis=0, keepdims=True)
    ec = jnp.clip(e, -HG_CLIP_EXP, HG_CLIP_EXP)
    qe = (hq * jnp.exp(ec)).astype(BF16)
    ke = (kk * jnp.exp(-ec)).astype(BF16)
    return kk, cum, tail, emax, qe, ke


def _hgrn_kernel(hqf_ref, zf_ref, vf_ref, hqb_ref, zb_ref, vb_ref, lbf_ref, lbb_ref,
                 of_ref, ob_ref, st_ref, a_ref, cum_ref, k_ref):
    j = pl.program_id(1)

    @pl.when(j == 0)
    def _():
        st_ref[...] = jnp.zeros_like(st_ref)

    CH = HG_CHUNK
    GW = HG_HEADS_PER_GROUP * HEAD_DIM
    ngroups = hqf_ref.shape[1] // GW
    nchunks = hqf_ref.shape[0] // CH
    t_i = lax.broadcasted_iota(I32, (CH, CH), 0)
    u_i = lax.broadcasted_iota(I32, (CH, CH), 1)
    tris = ((u_i <= t_i).astype(BF16), (u_i >= t_i).astype(BF16))
    t_w = lax.broadcasted_iota(I32, (CH, GW), 0)
    s_w = lax.broadcasted_iota(I32, (CH, GW), 1) % CH
    cmasks = (s_w <= t_w, s_w >= t_w)
    r_b = lax.broadcasted_iota(I32, (GW, GW), 0)
    c_b = lax.broadcasted_iota(I32, (GW, GW), 1)
    bm = (r_b // HEAD_DIM) == (c_b // HEAD_DIM)
    hsel_base = (r_b // HEAD_DIM) * HEAD_DIM
    dirs = ((hqf_ref, zf_ref, vf_ref, lbf_ref, of_ref), (hqb_ref, zb_ref, vb_ref, lbb_ref, ob_ref))
    chains = [(d, g) for d in range(2) for g in range(ngroups)]

    def expand(m):
        return jnp.where(bm, jnp.concatenate([m] * HG_HEADS_PER_GROUP, axis=0), jnp.zeros((GW, GW), m.dtype))

    def chunk_body(c, carry):
        rows = (pl.multiple_of(c * CH, CH), pl.multiple_of((nchunks - 1 - c) * CH, CH))
        prepared = []
        worst = jnp.zeros((1, 1), F32)
        for n, (d, g) in enumerate(chains):
            hq_r, z_r, v_r, lb_r, _ = dirs[d]
            ls = slice(g * GW, (g + 1) * GW)
            hq = hq_r[pl.ds(rows[d], CH), ls]
            z = z_r[pl.ds(rows[d], CH), ls]
            v = v_r[pl.ds(rows[d], CH), ls]
            last_row = CH - 1 if d == 0 else 0
            kk, cum, tail, emax, qe, ke = _hgrn_prepare(hq, z, lb_r[:, ls], tris[d], last_row, CH // 2)
            worst = jnp.maximum(worst, emax)
            a_ref[n] = _dot_nt(qe, expand(ke))
            cum_ref[n] = cum
            k_ref[n] = kk
            prepared.append((hq, v, kk, cum, tail))

        @pl.when(worst[0, 0] > HG_SAFE_EXP)
        def _():
            for n, (d, g) in enumerate(chains):
                hq_r = dirs[d][0]
                hq = hq_r[pl.ds(rows[d], CH), g * GW:(g + 1) * GW]
                cum = cum_ref[n]

                def key_body(s, acc):
                    cs = cum_ref[n, pl.ds(s, 1), :]
                    ksr = k_ref[n, pl.ds(s, 1), :]
                    p = hq * ksr * jnp.exp(jnp.minimum(cum - cs, 0.0))
                    hsel = jnp.where(c_b == hsel_base + s, 1.0, 0.0).astype(BF16)
                    return acc + _dot(p.astype(BF16), hsel)

                a_ref[n] = lax.fori_loop(0, CH, key_body, jnp.zeros((CH, GW), F32))

        for n, (d, g) in enumerate(chains):
            hq, v, kk, cum, tail = prepared[n]
            o_r = dirs[d][4]
            st = st_ref[n]
            a = jnp.where(cmasks[d], a_ref[n], 0.0).astype(BF16)
            vb = v.astype(BF16)
            o = _dot(a, expand(vb)) + _dot_nt((hq * jnp.exp(cum)).astype(BF16), st.astype(BF16))
            o_r[pl.ds(rows[d], CH), g * GW:(g + 1) * GW] = o
            k2 = (kk * jnp.exp(tail - cum)).astype(BF16)
            st_ref[n] = jnp.exp(tail) * st + jnp.where(bm, _dot_tn(vb, k2), 0.0)
        return carry

    lax.fori_loop(0, nchunks, chunk_body, 0)


def _hgrn2(dm, hq, ff, fb, hi, lb):
    B, T = dm.B, dm.T
    Wd = hq.shape[1]
    GW = HG_HEADS_PER_GROUP * HEAD_DIM
    nchain = 2 * (Wd // GW)
    nblk = 1 + dm.N // SEQ_BLOCK
    fwd, bwd = _seq_block_maps(dm)
    fspec = pl.BlockSpec((SEQ_BLOCK, Wd), lambda b, j: (fwd(b, j), 0))
    bspec = pl.BlockSpec((SEQ_BLOCK, Wd), lambda b, j: (bwd(b, j), 0))
    lbspec = pl.BlockSpec((1, Wd), lambda b, j: (0, 0))
    return pl.pallas_call(
        _hgrn_kernel,
        grid=(B, nblk),
        in_specs=[fspec, fspec, fspec, bspec, bspec, bspec, lbspec, lbspec],
        out_specs=[fspec, bspec],
        out_shape=[jax.ShapeDtypeStruct((T, Wd), F32)] * 2,
        scratch_shapes=[pltpu.VMEM((nchain, GW, GW), F32),
                        pltpu.VMEM((nchain, HG_CHUNK, GW), F32),
                        pltpu.VMEM((nchain, HG_CHUNK, GW), F32),
                        pltpu.VMEM((nchain, HG_CHUNK, GW), F32)],
        compiler_params=_cparams(("arbitrary", "arbitrary")),
        name="hgrn2_bidirectional",
    )(hq, ff, hi, hq, fb, hi, lb[0:1], lb[1:2])


LRU_GROUP = 8


def _lru_group_scan(a, x, reverse):
    n, w = a.shape
    a = a.reshape(n // LRU_GROUP, LRU_GROUP, w)
    x = x.reshape(n // LRU_GROUP, LRU_GROUP, w)
    row = lax.broadcasted_iota(I32, a.shape, 1)
    s = 1
    while s < LRU_GROUP:
        shift = (LRU_GROUP - s) if reverse else s
        a_sh = pltpu.roll(a, shift, 1)
        x_sh = pltpu.roll(x, shift, 1)
        valid = (row < LRU_GROUP - s) if reverse else (row >= s)
        x = jnp.where(valid, a * x_sh + x, x)
        a = jnp.where(valid, a * a_sh, a)
        s *= 2
    return a.reshape(n, w), x.reshape(n, w)


def _lru_kernel(ucf_ref, upf_ref, unf_ref, ucb_ref, upb_ref, unb_ref, cw_ref, cb_ref, wbd_ref,
                ba_ref, bx_ref, lam_ref, hf_ref, hb_ref, ext_ref, carry_ref, sa_ref, sx_ref):
    j = pl.program_id(1)
    nblk = pl.num_programs(1)

    @pl.when(j == 0)
    def _():
        carry_ref[...] = jnp.zeros_like(carry_ref)

    TT, W = ucf_ref.shape
    GW = wbd_ref.shape[2]
    halo = upf_ref.shape[0]
    dirs = ((ucf_ref, upf_ref, unf_ref, hf_ref), (ucb_ref, upb_ref, unb_ref, hb_ref))
    for d, (uc_r, up_r, un_r, out_r) in enumerate(dirs):
        pos = j if d == 0 else jnp.where(j == 0, 0, nblk - j)
        keep_prev = jnp.where((pos == 0) | (pos == 1), 0.0, 1.0)
        keep_next = jnp.where((pos == 0) | (pos == nblk - 1), 0.0, 1.0)
        ext_ref[d, 0:halo, :] = up_r[...] * keep_prev
        ext_ref[d, halo:halo + TT, :] = uc_r[...]
        ext_ref[d, halo + TT:halo + TT + halo, :] = un_r[...] * keep_next
        for g in range(W // GW):
            ls = slice(g * GW, (g + 1) * GW)
            u = cb_ref[:, ls]
            for tap in range(CONV_W):
                off = halo - 2 + tap
                u = u + cw_ref[tap:tap + 1, ls] * ext_ref[d, off:off + TT, ls]
            gates = _dot(u.astype(BF16), wbd_ref[d, g])
            r = jax.nn.sigmoid(gates[:, :GW] + ba_ref[d:d + 1, ls])
            ig = jax.nn.sigmoid(gates[:, GW:] + bx_ref[d:d + 1, ls])
            nl = -lam_ref[d:d + 1, ls]
            softplus = jnp.maximum(nl, 0.0) + jnp.log1p(jnp.exp(-jnp.abs(nl)))
            log_a = -LRU_C * r * softplus
            a = jnp.exp(log_a)
            gain2 = 1.0 - jnp.exp(2.0 * log_a)
            gain = jnp.where(gain2 > 0.0, gain2 * lax.rsqrt(gain2), 0.0)
            x_in = gain * ig * u
            a_acc, h = _lru_group_scan(a, x_in, d == 1)
            sa_ref[d, :, ls] = a_acc
            sx_ref[d, :, ls] = h

    ngroups = TT // LRU_GROUP

    def group_body(gi, carry):
        hf, hb = carry
        rf = pl.multiple_of(gi * LRU_GROUP, LRU_GROUP)
        rb = pl.multiple_of((ngroups - 1 - gi) * LRU_GROUP, LRU_GROUP)
        of = sx_ref[0, pl.ds(rf, LRU_GROUP), :] + sa_ref[0, pl.ds(rf, LRU_GROUP), :] * hf
        ob = sx_ref[1, pl.ds(rb, LRU_GROUP), :] + sa_ref[1, pl.ds(rb, LRU_GROUP), :] * hb
        hf_ref[pl.ds(rf, LRU_GROUP), :] = of
        hb_ref[pl.ds(rb, LRU_GROUP), :] = ob
        return of[LRU_GROUP - 1:LRU_GROUP, :], ob[0:1, :]

    hf, hb = lax.fori_loop(0, ngroups, group_body, (carry_ref[0, 0:1, :], carry_ref[1, 0:1, :]))
    carry_ref[0, 0:1, :] = hf
    carry_ref[1, 0:1, :] = hb


def _block_diag_gates(w_a, w_x, group):
    ndir, K, d, _ = w_a.shape
    per = group // d
    eye = jnp.eye(per, dtype=w_a.dtype)

    def bd(w):
        w = w.reshape(ndir, K // per, per, d, d)
        full = jnp.einsum('ngkde,kl->ngkdle', w, eye)
        return full.reshape(ndir, K // per, group, group)

    return jnp.concatenate([bd(w_a), bd(w_x)], axis=-1)


def _rglru(dm, u, conv_w, conv_b, w_a, b_a, w_x, b_x, lam):
    B, T = dm.B, dm.T
    W = u.shape[1]
    GW = 256
    halo = 8
    nblk = 1 + dm.N // SEQ_BLOCK
    fwd, bwd = _seq_block_maps(dm)
    hb = SEQ_BLOCK // halo
    nh = T // halo
    wbd = _block_diag_gates(w_a, w_x, GW).astype(BF16)

    def cur(m):
        return pl.BlockSpec((SEQ_BLOCK, W), lambda b, j: (m(b, j), 0))

    def prev(m):
        return pl.BlockSpec((halo, W), lambda b, j: (jnp.maximum(m(b, j) * hb - 1, 0), 0))

    def nxt(m):
        return pl.BlockSpec((halo, W), lambda b, j: (jnp.minimum((m(b, j) + 1) * hb, nh - 1), 0))

    def full(a):
        nd = a.ndim
        return pl.BlockSpec(a.shape, lambda b, j: (0,) * nd)

    consts = (conv_w, conv_b.reshape(1, W), wbd, b_a, b_x, lam)
    return pl.pallas_call(
        _lru_kernel,
        grid=(B, nblk),
        in_specs=[cur(fwd), prev(fwd), nxt(fwd), cur(bwd), prev(bwd), nxt(bwd)] + [full(a) for a in consts],
        out_specs=[cur(fwd), cur(bwd)],
        out_shape=[jax.ShapeDtypeStruct((T, W), F32)] * 2,
        scratch_shapes=[pltpu.VMEM((2, SEQ_BLOCK + 2 * halo, W), F32), pltpu.VMEM((2, 8, W), F32),
                        pltpu.VMEM((2, SEQ_BLOCK, W), F32), pltpu.VMEM((2, SEQ_BLOCK, W), F32)],
        compiler_params=_cparams(("arbitrary", "arbitrary")),
        name="rglru_bidirectional",
    )(u, u, u, u, u, u, *consts)


def _post_mixer_epilogue(o, x_ref, m2_ref, m3_ref, m4_ref, lng_ref, lnb_ref, rwh_ref, rwl_ref,
                         xo_ref, h2_ref, lg_ref, alpha):
    z = alpha * x_ref[...] + m2_ref[...] * o
    xn = _layer_norm(z, lng_ref[...], lnb_ref[...])
    xo_ref[...] = xn
    h2 = xn * (1.0 + m4_ref[...]) + m3_ref[...]
    h2_ref[...] = _pack_rows(h2)
    hh, hl = _split(h2)
    lg_ref[...] = _dot_nt(rwh_ref[...], hh) + _dot_nt(rwh_ref[...], hl) + _dot_nt(rwl_ref[...], hh)


def _post_even_kernel(nac_ref, nal_ref, of_ref, ob_ref, hg_ref, ng_ref, w_ref, *rest, alpha, ctx_tiles):
    na = jnp.where(pl.program_id(0) < ctx_tiles, nac_ref[...], nal_ref[...])
    o = of_ref[...] + ob_ref[...]
    wd = o.shape[1]
    r_i = lax.broadcasted_iota(I32, (wd, wd), 0) // HEAD_DIM
    c_i = lax.broadcasted_iota(I32, (wd, wd), 1) // HEAD_DIM
    avg = jnp.where(r_i == c_i, 1.0 / HEAD_DIM, 0.0).astype(BF16)
    sh, sl = _split(o * o)
    ms = _dot(sh, avg) + _dot(sl, avg)
    r = o * lax.rsqrt(ms + RMS_EPS) * ng_ref[...] * _silu(hg_ref[...])
    mix = _dot(na, w_ref[:wd, :]) + _dot(r.astype(BF16), w_ref[wd:, :])
    _post_mixer_epilogue(mix, *rest, alpha=alpha)


def _post_odd_kernel(hf_ref, hb_ref, y_ref, w_ref, *rest, alpha):
    m = ((hf_ref[...] + hb_ref[...]) * _gelu_tanh(y_ref[...])).astype(BF16)
    _post_mixer_epilogue(_dot(m, w_ref[...]), *rest, alpha=alpha)


def _post_mixer(dm, mixer_inputs, w_out, xs, m2, m3, m4, lng, lnb, rwh, rwl, alpha, even):
    T, D, TM = dm.T, dm.D, dm.TM
    E = rwh.shape[0]
    row = lambda a: pl.BlockSpec((TM, a.shape[1]), lambda i: (i, 0))
    const = lambda a: pl.BlockSpec(a.shape, lambda i: (0,) * a.ndim)
    mod_spec = pl.BlockSpec((None, 1, D), lambda i: (dm.group(i, TM), 0, 0))
    if even:
        nac, nal, of, ob, hg, ng = mixer_inputs
        nct = dm.T_ctx // TM
        ins = [nac, nal, of, ob, hg, ng, w_out]
        specs = [pl.BlockSpec((TM, nac.shape[1]), lambda i: (jnp.minimum(i, nct - 1), 0)),
                 pl.BlockSpec((TM, nal.shape[1]), lambda i: (jnp.maximum(i - nct, 0), 0)),
                 row(of), row(ob), row(hg), const(ng), const(w_out)]
        kern = functools.partial(_post_even_kernel, alpha=alpha, ctx_tiles=nct)
    else:
        hf, hb, y = mixer_inputs
        ins = [hf, hb, y, w_out]
        specs = [row(hf), row(hb), row(y), const(w_out)]
        kern = functools.partial(_post_odd_kernel, alpha=alpha)
    ins += [xs, m2, m3, m4, lng, lnb, rwh, rwl]
    specs += [row(xs), mod_spec, mod_spec, mod_spec, const(lng), const(lnb), const(rwh), const(rwl)]
    return pl.pallas_call(
        kern,
        grid=(T // TM,),
        in_specs=specs,
        out_specs=[pl.BlockSpec((TM, D), lambda i: (i, 0)), pl.BlockSpec((TM, D // 2), lambda i: (i, 0)),
                   pl.BlockSpec((E, TM), lambda i: (0, i))],
        out_shape=[jax.ShapeDtypeStruct((T, D), F32), jax.ShapeDtypeStruct((T, D // 2), jnp.uint32),
                   jax.ShapeDtypeStruct((E, T), F32)],
        compiler_params=_cparams(("parallel",)),
        name="post_mixer_even" if even else "post_mixer_odd",
    )(*ins)


def _route_kernel(lg_ref, bias_ref, tri_ref, idx_ref, rank_ref, w_ref, cnt_ref, carry_ref):
    i = pl.program_id(0)

    @pl.when(i == 0)
    def _():
        carry_ref[...] = jnp.zeros_like(carry_ref)

    E, TK = lg_ref.shape
    s = jax.nn.sigmoid(lg_ref[...])
    work = s + bias_ref[...]
    eio = lax.broadcasted_iota(I32, (E, TK), 0).astype(F32)
    picked = jnp.zeros((E, TK), F32)
    sels, idxs = [], []
    for _ in range(TOP_K):
        m = jnp.max(work, axis=0, keepdims=True)
        ik = jnp.min(jnp.where(work == m, eio, float(E)), axis=0, keepdims=True)
        oh = eio == ik
        sels.append(jnp.sum(jnp.where(oh, s, 0.0), axis=0, keepdims=True))
        idxs.append(ik)
        picked = jnp.where(oh, 1.0, picked)
        work = jnp.where(oh, -jnp.inf, work)
    total = sels[0]
    for sk in sels[1:]:
        total = total + sk
    carry = carry_ref[:, 0:1]
    ranks = carry + _dot(picked.astype(BF16), tri_ref[...])
    out_rows = idx_ref.shape[0]
    rio = lax.broadcasted_iota(I32, (out_rows, TK), 0)
    idx_o = jnp.zeros((out_rows, TK), I32)
    rank_o = jnp.zeros((out_rows, TK), I32)
    w_o = jnp.zeros((out_rows, TK), F32)
    for kx in range(TOP_K):
        rk = jnp.sum(jnp.where(eio == idxs[kx], ranks, 0.0), axis=0, keepdims=True).astype(I32)
        idx_o = jnp.where(rio == kx, idxs[kx].astype(I32), idx_o)
        rank_o = jnp.where(rio == kx, rk, rank_o)
        w_o = jnp.where(rio == kx, sels[kx] / total * ROUTED_SCALE, w_o)
    idx_ref[...] = idx_o
    rank_ref[...] = rank_o
    w_ref[...] = w_o
    new_carry = carry + jnp.sum(picked, axis=1, keepdims=True)
    carry_ref[...] = jnp.broadcast_to(new_carry, carry_ref.shape)
    cnt_ref[...] = jnp.broadcast_to(new_carry, cnt_ref.shape)


def _route(logits_t, router_bias, t0):
    E, T = logits_t.shape
    Tm = T - t0
    TK = _pick_tile(int(np.gcd(Tm, t0)), (1024, 512, 256))
    off = t0 // TK
    tri = jnp.asarray(np.triu(np.ones((TK, TK), np.float32), 1), BF16)
    tok = lambda: pl.BlockSpec((8, TK), lambda i: (0, i))
    idx, rank, w, cnt = pl.pallas_call(
        _route_kernel,
        grid=(Tm // TK,),
        in_specs=[pl.BlockSpec((E, TK), lambda i: (0, i + off)),
                  pl.BlockSpec((E, 1), lambda i: (0, 0)),
                  pl.BlockSpec((TK, TK), lambda i: (0, 0))],
        out_specs=[tok(), tok(), tok(), pl.BlockSpec((E, 128), lambda i: (0, 0))],
        out_shape=[jax.ShapeDtypeStruct((8, Tm), I32), jax.ShapeDtypeStruct((8, Tm), I32),
                   jax.ShapeDtypeStruct((8, Tm), F32), jax.ShapeDtypeStruct((E, 128), F32)],
        scratch_shapes=[pltpu.VMEM((E, 128), F32)],
        compiler_params=_cparams(("arbitrary",)),
        name="moe_route",
    )(logits_t, router_bias.reshape(E, 1), tri)
    return idx, rank, w, cnt[:, 0].astype(I32)


def _sorted_tile_rows(n_exp):
    return -(-(MOE_TILE * TOP_K + n_exp * (ROW_ALIGN - 1)) // 16) * 16


def _run_copies(cnt_ref, src_ref, dst_ref, base, n_exp, chunk, make_copy):
    shift = chunk.bit_length() - 1

    def expert_body(e, total):
        n = cnt_ref[base + e]
        src = src_ref[base + e]
        dst = dst_ref[base + e]
        nch = lax.shift_right_logical(n + (chunk - 1), shift)

        def chunk_body(c, carry):
            make_copy(pl.multiple_of(src + c * chunk, ROW_ALIGN), pl.multiple_of(dst + c * chunk, ROW_ALIGN)).start()
            return carry

        lax.fori_loop(0, nch, chunk_body, 0)
        return total + nch

    return lax.fori_loop(0, n_exp, expert_body, jnp.int32(0))


def _dispatch_kernel(cnt_ref, ls_ref, gb_ref, pos_ref, h_ref, xs_hbm, stage, pending, sem, *, n_exp):
    i = pl.program_id(0)
    slot = i % 2
    TMd = h_ref.shape[0]
    R = stage.shape[1] - DISP_CHUNK

    @pl.when(i == 0)
    def _():
        pending[0] = 0
        for s in range(2):
            stage[s, R:, :] = jnp.zeros((stage.shape[1] - R, stage.shape[2]), stage.dtype)

    rio = lax.broadcasted_iota(I32, (R, TMd), 0)
    perm = jnp.zeros((R, TMd), F32)
    for kx in range(TOP_K):
        perm = jnp.where(rio == pos_ref[kx:kx + 1, :], 1.0, perm)
    stage[slot, 0:R, :] = _pack_rows(_dot(perm.astype(BF16), _unpack_rows(h_ref[...])))

    def make_copy(src, dst):
        return pltpu.make_async_copy(stage.at[slot, pl.ds(src, DISP_CHUNK)], xs_hbm.at[pl.ds(dst, DISP_CHUNK)], sem)

    def wait_copies(n):
        def wait_body(c, carry):
            make_copy(0, 0).wait()
            return carry
        lax.fori_loop(0, n, wait_body, 0)

    wait_copies(pending[0])
    pending[0] = _run_copies(cnt_ref, ls_ref, gb_ref, i * n_exp, n_exp, DISP_CHUNK, make_copy)

    @pl.when(i == pl.num_programs(0) - 1)
    def _():
        wait_copies(pending[0])


def _dispatch(h2, plan, t0, n_slots, n_exp):
    cnt, ls, gb, pos = plan
    T, DP = h2.shape
    TMd = MOE_TILE
    Tm = T - t0
    assert Tm % TMd == 0 and t0 % TMd == 0
    off = t0 // TMd
    grid_spec = pltpu.PrefetchScalarGridSpec(
        num_scalar_prefetch=3,
        grid=(Tm // TMd,),
        in_specs=[pl.BlockSpec((8, TMd), lambda i, *_: (0, i)),
                  pl.BlockSpec((TMd, DP), lambda i, *_: (i + off, 0))],
        out_specs=pl.BlockSpec(memory_space=pl.ANY),
        scratch_shapes=[pltpu.VMEM((2, _sorted_tile_rows(n_exp) + DISP_CHUNK, DP), h2.dtype),
                        pltpu.SMEM((1,), I32), pltpu.SemaphoreType.DMA],
    )
    return pl.pallas_call(
        functools.partial(_dispatch_kernel, n_exp=n_exp),
        grid_spec=grid_spec,
        out_shape=jax.ShapeDtypeStruct((n_slots, DP), h2.dtype),
        compiler_params=_cparams(("arbitrary",)),
        name="moe_dispatch",
    )(cnt, ls, gb, pos, h2)


def _gmm_kernel(bexp_ref, bvalid_ref, x_ref, wg_ref, wu_ref, wd_ref, y_ref):
    del bexp_ref
    i = pl.program_id(0)
    nvalid = bvalid_ref[i]

    @pl.when(nvalid > 0)
    def _():
        row = lax.broadcasted_iota(I32, x_ref.shape, 0)
        xb = _unpack_rows(jnp.where(row < nvalid, x_ref[...], jnp.uint32(0)))
        act = _silu(_dot(xb, wg_ref[...].astype(BF16))) * _dot(xb, wu_ref[...].astype(BF16))
        y_ref[...] = _pack_rows(_dot(act.astype(BF16), wd_ref[...].astype(BF16)))

    @pl.when(nvalid == 0)
    def _():
        y_ref[...] = jnp.zeros_like(y_ref)


def _grouped_experts(xsorted, block_exp, block_valid, wg, wu, wd, layer):
    NS, DP = xsorted.shape
    nb = NS // MOE_BLOCK
    _, _, D, hid = wg.shape
    grid_spec = pltpu.PrefetchScalarGridSpec(
        num_scalar_prefetch=2,
        grid=(nb,),
        in_specs=[pl.BlockSpec((MOE_BLOCK, DP), lambda i, be, bv: (i, 0)),
                  pl.BlockSpec((None, None, D, hid), lambda i, be, bv: (layer, be[i], 0, 0)),
                  pl.BlockSpec((None, None, D, hid), lambda i, be, bv: (layer, be[i], 0, 0)),
                  pl.BlockSpec((None, None, hid, D), lambda i, be, bv: (layer, be[i], 0, 0))],
        out_specs=pl.BlockSpec((MOE_BLOCK, DP), lambda i, be, bv: (i, 0)),
    )
    return pl.pallas_call(
        _gmm_kernel,
        grid_spec=grid_spec,
        out_shape=jax.ShapeDtypeStruct((NS, DP), jnp.uint32),
        compiler_params=_cparams(("arbitrary",)),
        name="moe_grouped_experts",
    )(block_exp, block_valid, xsorted, wg, wu, wd)


def _combine_kernel(cnt_ref, so_ref, gb_ref, y_hbm, qpos_ref, w_ref, h_ref, sgu_ref, sd_ref, x_ref, m5_ref,
                    lng_ref, lnb_ref, o_ref, stage, pending, sems, *, alpha, n_exp):
    i = pl.program_id(0)
    slot = i % 2

    def make_copy(s):
        def f(dst, src):
            return pltpu.make_async_copy(y_hbm.at[pl.ds(src, COMB_CHUNK)], stage.at[s, pl.ds(dst, COMB_CHUNK)],
                                         sems.at[s])
        return f

    def fetch(tile, s):
        pending[s] = _run_copies(cnt_ref, so_ref, gb_ref, tile * n_exp, n_exp, COMB_CHUNK, make_copy(s))

    @pl.when(i == 0)
    def _():
        stage[...] = jnp.zeros_like(stage)
        fetch(0, 0)

    @pl.when(i + 1 < pl.num_programs(0))
    def _():
        fetch(i + 1, 1 - slot)

    hid = sd_ref.shape[0]
    gu = _dot(_unpack_rows(h_ref[...]), sgu_ref[...])
    y = _dot((_silu(gu[:, :hid]) * gu[:, hid:]).astype(BF16), sd_ref[...])

    def wait_body(c, carry):
        make_copy(slot)(0, 0).wait()
        return carry

    lax.fori_loop(0, pending[slot], wait_body, 0)

    TMc, S = h_ref.shape[0], stage.shape[1]
    sio = lax.broadcasted_iota(I32, (TMc, S), 1)
    q = jnp.zeros((TMc, S), F32)
    for kx in range(TOP_K):
        q = jnp.where(sio == qpos_ref[:, kx:kx + 1], w_ref[:, kx:kx + 1], q)
    y = y + _dot(q.astype(BF16), _unpack_rows(stage[slot]))
    z = alpha * x_ref[...] + m5_ref[...] * y
    o_ref[...] = _layer_norm(z, lng_ref[...], lnb_ref[...])


def _combine(dm, plan, ysorted, qpos_tm, w_tm, h2, sgu, sd, xs, m5, lng, lnb, alpha, t0, n_exp):
    cnt, so, gb = plan
    T, D = xs.shape
    TMc = MOE_TILE
    Tm = T - t0
    off = t0 // TMc
    const = lambda a: pl.BlockSpec(a.shape, lambda i, *_: (0,) * a.ndim)
    row = lambda a: pl.BlockSpec((TMc, a.shape[1]), lambda i, *_: (i + off, 0))
    grid_spec = pltpu.PrefetchScalarGridSpec(
        num_scalar_prefetch=3,
        grid=(Tm // TMc,),
        in_specs=[pl.BlockSpec(memory_space=pl.ANY),
                  pl.BlockSpec((TMc, 8), lambda i, *_: (i, 0)), pl.BlockSpec((TMc, 8), lambda i, *_: (i, 0)),
                  row(h2), const(sgu), const(sd), row(xs),
                  pl.BlockSpec((None, 1, D), lambda i, *_: (dm.group(i + off, TMc), 0, 0)),
                  const(lng), const(lnb)],
        out_specs=pl.BlockSpec((TMc, D), lambda i, *_: (i, 0)),
        scratch_shapes=[pltpu.VMEM((2, _sorted_tile_rows(n_exp), ysorted.shape[1]), ysorted.dtype),
                        pltpu.SMEM((2,), I32), pltpu.SemaphoreType.DMA((2,))],
    )
    return pl.pallas_call(
        functools.partial(_combine_kernel, alpha=alpha, n_exp=n_exp),
        grid_spec=grid_spec,
        out_shape=jax.ShapeDtypeStruct((Tm, D), F32),
        compiler_params=_cparams(("arbitrary",)),
        name="moe_combine",
    )(cnt, so, gb, ysorted, qpos_tm, w_tm, h2, sgu, sd, xs, m5, lng, lnb)


def _moe(dm, h2, logits_t, xs, m5, lng, lnb, router_bias, wg, wu, wd, layer, sgu, sd, alpha, t0):
    T = h2.shape[0]
    E = wg.shape[1]
    Tm = T - t0
    nt = Tm // MOE_TILE
    idx, rank, w, _ = _route(logits_t, router_bias, t0)
    idx, rank, w = idx[:TOP_K], rank[:TOP_K], w[:TOP_K]
    eids = jnp.arange(E, dtype=I32)
    onehot = (idx[:, :, None] == eids).reshape(TOP_K, nt, MOE_TILE, E)
    cnt = jnp.sum(onehot, axis=(0, 2), dtype=I32)
    rank_base = jnp.cumsum(cnt, axis=0) - cnt
    run = (cnt + ROW_ALIGN - 1) // ROW_ALIGN * ROW_ALIGN
    tile_base = jnp.cumsum(run, axis=0) - run
    local_start = jnp.cumsum(run, axis=1) - run
    counts = jnp.sum(run, axis=0)
    nb = -(-(Tm * TOP_K + E * (nt * (ROW_ALIGN - 1) + MOE_SLACK)) // MOE_BLOCK) + E
    padded = (counts + MOE_SLACK + MOE_BLOCK - 1) // MOE_BLOCK * MOE_BLOCK
    pends = jnp.cumsum(padded)
    pstarts = pends - padded
    global_base = pstarts[None, :] + tile_base
    bstart = jnp.arange(nb, dtype=I32) * MOE_BLOCK
    block_exp = jnp.minimum(jnp.sum((pends[None, :] <= bstart[:, None]).astype(I32), axis=1), E - 1)
    of_block = block_exp[:, None] == eids[None, :]
    seg_end = jnp.sum(jnp.where(of_block, (pstarts + counts)[None, :], 0), axis=1)
    block_valid = jnp.clip(seg_end - bstart, 0, MOE_BLOCK).astype(I32)
    look = lambda tab: jnp.sum(jnp.where(onehot, tab[None, :, None, :], 0), axis=3).reshape(TOP_K, Tm)
    pos = rank - look(rank_base) + look(local_start)
    pos8 = jnp.pad(pos, ((0, 8 - TOP_K), (0, 0)), constant_values=-1).astype(I32)
    pos_tm = jnp.pad(pos.T, ((0, 0), (0, 8 - TOP_K)), constant_values=-1).astype(I32)
    w_tm = jnp.pad(w.T, ((0, 0), (0, 8 - TOP_K)))
    flat = lambda a: a.reshape(-1).astype(I32)
    plan = (flat(run), flat(local_start), flat(global_base))
    xsorted = _dispatch(h2, plan + (pos8,), t0, nb * MOE_BLOCK, E)
    ysorted = _grouped_experts(xsorted, block_exp, block_valid, wg, wu, wd, layer)
    return _combine(dm, plan, ysorted, pos_tm, w_tm, h2, sgu, sd, xs, m5, lng, lnb, alpha, t0, E)


def kernel(x, c, ctx, c_ctx, ada_w, ada_b, ln_g, ln_b, ev_w_in, ev_w_out, na_rpb, hg_lb_raw, hg_norm_g, od_w_in, od_conv_w, od_conv_b, lru_w_a, lru_b_a, lru_w_x, lru_b_x, lru_lam, od_w_out, router_w, router_bias, exp_w_gate, exp_w_up, exp_w_down, sh_w_gate, sh_w_up, sh_w_down):
    B, N, D = x.shape
    C = ctx.shape[1]
    depth = ada_w.shape[0]
    dm = _Dims(B, N, C, D)
    alpha = float((2 * depth) ** 0.25)
    assert B + 1 <= 8

    cvec = jnp.zeros((8, D), F32).at[:B].set(c).at[B].set(c_ctx)
    mod = _modulation(cvec, ada_w, ada_b).reshape(depth, 8, 6, 1, D)

    p_lb = jax.nn.softmax(hg_lb_raw.astype(F32), axis=1)
    hg_lb = jnp.cumsum(p_lb, axis=1) - p_lb[:, :1]

    xs = jnp.concatenate([ctx.reshape(B * C, D), x.reshape(B * N, D)], axis=0)
    for l in range(depth):
        jl = l // 2
        last = l == depth - 1
        m = [mod[l, :, t] for t in range(6)]
        rw_t = router_w[l].T
        rwh, rwl = _split(rw_t)
        lng = ln_g[l][:, None, :]
        lnb = ln_b[l][:, None, :]
        if l % 2 == 0:
            q, k, v, hq, ff, fb, hi, hg = _inproj(dm, xs, m[1], m[0], ev_w_in[jl].astype(BF16), True)
            bias = _na_bias_tables(na_rpb[jl], N // GRID_W)
            na_lat = _neighbourhood_attention(dm, q, k, v, bias)
            na_ctx = _context_attention(dm, q, k, v)
            o_f, o_b = _hgrn2(dm, hq, ff, fb, hi, hg_lb[:, jl])
            ng = jnp.tile(hg_norm_g[jl], hq.shape[1] // HEAD_DIM)[None, :]
            mixer_inputs = (na_ctx, na_lat, o_f, o_b, hg, ng)
            w_out = ev_w_out[jl].astype(BF16)
        else:
            y, u = _inproj(dm, xs, m[1], m[0], od_w_in[jl].astype(BF16), False)
            h_f, h_b = _rglru(dm, u, od_conv_w[jl], od_conv_b[jl], lru_w_a[jl], lru_b_a[jl],
                              lru_w_x[jl], lru_b_x[jl], lru_lam[jl])
            mixer_inputs = (h_f, h_b, y)
            w_out = od_w_out[jl].astype(BF16)
        xs, h2, logits_t = _post_mixer(dm, mixer_inputs, w_out, xs, m[2], m[3], m[4], lng[0], lnb[0],
                                       rwh, rwl, alpha, l % 2 == 0)
        sgu = jnp.concatenate([sh_w_gate[l], sh_w_up[l]], axis=-1).astype(BF16)
        sd = sh_w_down[l].astype(BF16)
        t0 = dm.T_ctx if last else 0
        xs = _moe(dm, h2, logits_t, xs, m[5], lng[1], lnb[1], router_bias[l], exp_w_gate, exp_w_up,
                  exp_w_down, l, sgu, sd, alpha, t0)
    return xs.reshape(B, N, D)
```

```python
import functools

import numpy as np
import jax
import jax.numpy as jnp
from jax import lax
from jax.experimental import pallas as pl
from jax.experimental.pallas import tpu as pltpu

F32 = jnp.float32
BF16 = jnp.bfloat16
I32 = jnp.int32

HEAD_DIM = 64
GRID_W = 64
NA_WIN_ROWS = 8
NA_WIN_COLS = 16
NA_QROWS = 8
NA_KROWS = 16
HG_CHUNK = 64
HG_HEADS_PER_GROUP = 4
FORGET_FLOOR = 1e-30
HG_SAFE_EXP = 80.0
HG_CLIP_EXP = 85.0
LRU_C = 8.0
LRU_BLOCKS = 16
CONV_W = 4
TOP_K = 6
ROUTED_SCALE = 2.5
LN_EPS = 1e-5
RMS_EPS = 1e-6
SEQ_BLOCK = 256
MOE_BLOCK = 512
MOE_TILE = 256
ROW_ALIGN = 8
DISP_CHUNK = 32
COMB_CHUNK = ROW_ALIGN
MOE_SLACK = DISP_CHUNK
MASK_VALUE = -1e30
VMEM_LIMIT = 56 * 1024 * 1024


def _cparams(sem):
    return pltpu.CompilerParams(dimension_semantics=sem, vmem_limit_bytes=VMEM_LIMIT)


def _split(a):
    hi = a.astype(BF16)
    lo = (a - hi.astype(F32)).astype(BF16)
    return hi, lo


def _dot(a, b):
    return jnp.dot(a, b, preferred_element_type=F32)


def _dot_nt(a, b):
    return lax.dot_general(a, b, (((1,), (1,)), ((), ())), preferred_element_type=F32)


def _dot_tn(a, b):
    return lax.dot_general(a, b, (((0,), (0,)), ((), ())), preferred_element_type=F32)


def _dot3(a, b):
    ah, al = _split(a)
    bh, bl = _split(b)
    return _dot(ah, bh) + _dot(al, bh) + _dot(ah, bl)


def _pack_rows(h):
    half = h.shape[1] // 2
    bits = pltpu.bitcast(h.astype(BF16).astype(F32), jnp.uint32)
    return (bits[:, :half] >> 16) | (bits[:, half:] & jnp.uint32(0xFFFF0000))


def _unpack_rows(p):
    lo = pltpu.bitcast(p << 16, F32)
    hi = pltpu.bitcast(p & jnp.uint32(0xFFFF0000), F32)
    return jnp.concatenate([lo, hi], axis=1).astype(BF16)


def _silu(v):
    return v * jax.nn.sigmoid(v)


def _gelu_tanh(v):
    return 0.5 * v * (1.0 + jnp.tanh(0.7978845608028654 * (v + 0.044715 * v * v * v)))


def _layer_norm(z, g, b):
    mu = jnp.mean(z, axis=-1, keepdims=True)
    zc = z - mu
    var = jnp.mean(zc * zc, axis=-1, keepdims=True)
    return zc * lax.rsqrt(var + LN_EPS) * g + b


def _pick_tile(n, cands):
    for c in cands:
        if n % c == 0:
            return c
    raise ValueError(f"no tile for {n}")


class _Dims:
    def __init__(self, B, N, C, D):
        self.B, self.N, self.C, self.D = B, N, C, D
        self.T_ctx = B * C
        self.T = B * C + B * N
        assert C == SEQ_BLOCK and N % SEQ_BLOCK == 0
        assert N % GRID_W == 0 and (N // GRID_W) % NA_QROWS == 0 and N // GRID_W >= NA_KROWS
        assert self.T_ctx % 512 == 0
        self.TM = _pick_tile(self.T_ctx, (512, 256))
        assert N % self.TM == 0

    def group(self, i, tile):
        start = i * tile
        return jnp.where(start < self.T_ctx, self.B, (start - self.T_ctx) // self.N)


def _mod_kernel(c_ref, w_ref, b_ref, o_ref):
    o_ref[...] = _dot3(_silu(c_ref[...]), w_ref[...]) + b_ref[...]


def _modulation(cvec, ada_w, ada_b):
    L, D, W6 = ada_w.shape
    nc = 1536
    return pl.pallas_call(
        _mod_kernel,
        grid=(L, W6 // nc),
        in_specs=[
            pl.BlockSpec((8, D), lambda l, j: (0, 0)),
            pl.BlockSpec((None, D, nc), lambda l, j: (l, 0, j)),
            pl.BlockSpec((None, 1, nc), lambda l, j: (l, 0, j)),
        ],
        out_specs=pl.BlockSpec((None, 8, nc), lambda l, j: (l, 0, j)),
        out_shape=jax.ShapeDtypeStruct((L, 8, W6), F32),
        compiler_params=_cparams(("arbitrary", "arbitrary")),
        name="adaln_modulation",
    )(cvec, ada_w, ada_b.reshape(L, 1, W6))


def _inproj_even_kernel(x_ref, sc_ref, sh_ref, w_ref, q_ref, k_ref, v_ref, hq_ref, ff_ref, fb_ref, hi_ref, hg_ref):
    h = (x_ref[...] * (1.0 + sc_ref[...]) + sh_ref[...]).astype(BF16)
    wd = q_ref.shape[1]

    def part(j):
        return _dot(h, w_ref[:, j * wd:(j + 1) * wd])

    scale = HEAD_DIM ** -0.5
    q_ref[...] = (part(0) * scale).astype(BF16)
    k_ref[...] = part(1).astype(BF16)
    v_ref[...] = part(2).astype(BF16)
    hq_ref[...] = _silu(part(3)) * scale
    ff_ref[...] = part(4)
    fb_ref[...] = part(5)
    hi_ref[...] = part(6)
    hg_ref[...] = part(7)


def _inproj_odd_kernel(x_ref, sc_ref, sh_ref, w_ref, y_ref, u_ref):
    h = (x_ref[...] * (1.0 + sc_ref[...]) + sh_ref[...]).astype(BF16)
    wd = y_ref.shape[1]
    y_ref[...] = _dot(h, w_ref[:, :wd])
    u_ref[...] = _dot(h, w_ref[:, wd:])


def _inproj(dm, xs, sc, sh, w, even):
    T, D, TM = dm.T, dm.D, dm.TM
    wtot = w.shape[1]
    mod_spec = pl.BlockSpec((None, 1, D), lambda i: (dm.group(i, TM), 0, 0))
    in_specs = [pl.BlockSpec((TM, D), lambda i: (i, 0)), mod_spec, mod_spec,
                pl.BlockSpec((D, wtot), lambda i: (0, 0))]
    if even:
        wd = wtot // 8
        dts = [BF16] * 3 + [F32] * 5
        kern = _inproj_even_kernel
    else:
        wd = wtot // 2
        dts = [F32] * 2
        kern = _inproj_odd_kernel
    return pl.pallas_call(
        kern,
        grid=(T // TM,),
        in_specs=in_specs,
        out_specs=[pl.BlockSpec((TM, wd), lambda i: (i, 0)) for _ in dts],
        out_shape=[jax.ShapeDtypeStruct((T, wd), dt) for dt in dts],
        compiler_params=_cparams(("parallel",)),
        name="inproj_even" if even else "inproj_odd",
    )(xs, sc, sh, w)


def _na_bias_tables(rpb, rows):
    W = GRID_W
    H = rpb.shape[0]
    nr, nc = 2 * NA_WIN_ROWS - 1, 2 * NA_WIN_COLS - 1
    hp = lax.Precision.HIGHEST
    c = np.arange(W)[:, None]
    kc = np.arange(W)[None, :]
    c0 = np.clip(c - NA_WIN_COLS // 2, 0, W - NA_WIN_COLS)
    valid_c = (kc >= c0) & (kc < c0 + NA_WIN_COLS)
    dc = np.clip(kc - c + NA_WIN_COLS - 1, 0, nc - 1)
    oh_c = (dc.reshape(-1, 1) == np.arange(nc)[None, :]).astype(np.float32)
    by_col = jnp.einsum('hrs,xs->hrx', rpb.astype(F32), jnp.asarray(oh_c), precision=hp)
    tabs = []
    for rbase, kb in ((0, 0), (NA_QROWS, NA_QROWS - NA_WIN_ROWS // 2), (rows - NA_QROWS, rows - NA_KROWS)):
        r = rbase + np.arange(NA_QROWS)[:, None]
        kr = kb + np.arange(NA_KROWS)[None, :]
        r0 = np.clip(r - NA_WIN_ROWS // 2, 0, rows - NA_WIN_ROWS)
        valid_r = (kr >= r0) & (kr < r0 + NA_WIN_ROWS)
        dr = np.clip(kr - r + NA_WIN_ROWS - 1, 0, nr - 1)
        oh_r = (dr.reshape(-1, 1) == np.arange(nr)[None, :]).astype(np.float32)
        t = jnp.einsum('yr,hrx->hyx', jnp.asarray(oh_r), by_col, precision=hp)
        t = t.reshape(H, NA_QROWS, NA_KROWS, W, W).transpose(0, 1, 3, 2, 4)
        valid = valid_r[:, None, :, None] & valid_c[None, :, None, :]
        t = jnp.where(jnp.asarray(valid)[None], t, MASK_VALUE)
        tabs.append(t.reshape(H, NA_QROWS * W, NA_KROWS * W))
    return jnp.stack(tabs).astype(F32)


def _na_kernel(q_ref, k0, k1, k2, k3, v0, v1, v2, v3, kc_ref, vc_ref, bias_ref, o_ref):
    q2 = q_ref[...]
    lane = lax.broadcasted_iota(I32, (1, q2.shape[1]), 1)
    first = lane < HEAD_DIM
    ks = [r[...] for r in (k0, k1, k2, k3)]
    vs = [r[...] for r in (v0, v1, v2, v3)]
    kc = kc_ref[...]
    vc = vc_ref[...]
    kb = ks[0].shape[0]
    outs = []
    for hh in range(2):
        sel = first if hh == 0 else jnp.logical_not(first)
        qh = jnp.where(sel, q2, jnp.zeros_like(q2))
        s_loc = jnp.concatenate([_dot_nt(qh, kj) for kj in ks], axis=1) + bias_ref[hh]
        s_ctx = _dot_nt(qh, kc)
        m = jnp.maximum(jnp.max(s_loc, axis=1, keepdims=True), jnp.max(s_ctx, axis=1, keepdims=True))
        p_loc = jnp.exp(s_loc - m)
        p_ctx = jnp.exp(s_ctx - m)
        denom = jnp.sum(p_loc, axis=1, keepdims=True) + jnp.sum(p_ctx, axis=1, keepdims=True)
        o = _dot(p_ctx.astype(BF16), vc)
        for j in range(4):
            o = o + _dot(p_loc[:, j * kb:(j + 1) * kb].astype(BF16), vs[j])
        outs.append(o / denom)
    o_ref[...] = jnp.where(first, outs[0], outs[1]).astype(o_ref.dtype)


def _neighbourhood_attention(dm, q, k, v, bias):
    B, N, C, T = dm.B, dm.N, dm.C, dm.T
    H2 = q.shape[1] // (2 * HEAD_DIM)
    rows = N // GRID_W
    nqb = rows // NA_QROWS
    QB = NA_QROWS * GRID_W
    KB = QB // 2
    ngroups = N // KB
    lat_q0 = dm.T_ctx // QB
    lat_k0 = dm.T_ctx // KB

    def q_map(p, i, b):
        return (lat_q0 + b * (N // QB) + i, p)

    def o_map(p, i, b):
        return (b * (N // QB) + i, p)

    def kv_map(j):
        def f(p, i, b):
            gs = jnp.clip(2 * i - 1, 0, ngroups - 4)
            return (lat_k0 + b * ngroups + gs + j, p)
        return f

    def ctx_map(p, i, b):
        return (b, p)

    def bias_map(p, i, b):
        var = jnp.where(i == 0, 0, jnp.where(i == nqb - 1, 2, 1))
        return (var, p, 0, 0)

    lanes = 2 * HEAD_DIM
    kv_specs = [pl.BlockSpec((KB, lanes), kv_map(j)) for j in range(4)]
    return pl.pallas_call(
        _na_kernel,
        grid=(H2, nqb, B),
        in_specs=[pl.BlockSpec((QB, lanes), q_map)] + kv_specs + kv_specs
        + [pl.BlockSpec((C, lanes), ctx_map), pl.BlockSpec((C, lanes), ctx_map),
           pl.BlockSpec((None, 2, QB, NA_KROWS * GRID_W), bias_map)],
        out_specs=pl.BlockSpec((QB, lanes), o_map),
        out_shape=jax.ShapeDtypeStruct((B * N, q.shape[1]), BF16),
        compiler_params=_cparams(("arbitrary", "arbitrary", "arbitrary")),
        name="neighbourhood_attention",
    )(q, k, k, k, k, v, v, v, v, k, v, bias)


def _ctx_attn_kernel(q_ref, k_ref, v_ref, o_ref):
    q2 = q_ref[...]
    k2 = k_ref[...]
    v2 = v_ref[...]
    lane = lax.broadcasted_iota(I32, (1, q2.shape[1]), 1)
    first = lane < HEAD_DIM
    outs = []
    for hh in range(2):
        sel = first if hh == 0 else jnp.logical_not(first)
        s = _dot_nt(jnp.where(sel, q2, jnp.zeros_like(q2)), k2)
        p = jnp.exp(s - jnp.max(s, axis=1, keepdims=True))
        outs.append(_dot(p.astype(BF16), v2) / jnp.sum(p, axis=1, keepdims=True))
    o_ref[...] = jnp.where(first, outs[0], outs[1]).astype(o_ref.dtype)


def _context_attention(dm, q, k, v):
    lanes = 2 * HEAD_DIM
    spec = pl.BlockSpec((dm.C, lanes), lambda p, b: (b, p))
    return pl.pallas_call(
        _ctx_attn_kernel,
        grid=(q.shape[1] // lanes, dm.B),
        in_specs=[spec, spec, spec],
        out_specs=spec,
        out_shape=jax.ShapeDtypeStruct((dm.T_ctx, q.shape[1]), BF16),
        compiler_params=_cparams(("arbitrary", "arbitrary")),
        name="context_attention",
    )(q, k, v)


def _seq_block_maps(dm):
    nl = dm.N // SEQ_BLOCK
    base = dm.T_ctx // SEQ_BLOCK

    def fwd(b, j):
        return jnp.where(j == 0, b, base + b * nl + j - 1)

    def bwd(b, j):
        return jnp.where(j == 0, b, base + b * nl + nl - j)

    return fwd, bwd


def _hgrn_prepare(hq, z, lb, tri, last_row, mid_row):
    f = lb + (1.0 - lb) * jax.nn.sigmoid(z)
    fm = jnp.maximum(f, FORGET_FLOOR)
    g = jnp.log(fm)
    kk = 1.0 - fm
    gh, gl = _split(g)
    cum = _dot(tri, gh) + _dot(tri, gl)
    tail = cum[last_row:last_row + 1, :]
    e = cum - cum[mid_row:mid_row + 1, :]
    emax = jnp.max(jnp.max(jnp.abs(e), axis=1, keepdims=True), axis=0, keepdims=True)
    ec = jnp.clip(e, -HG_CLIP_EXP, HG_CLIP_EXP)
    qe = (hq * jnp.exp(ec)).astype(BF16)
    ke = (kk * jnp.exp(-ec)).astype(BF16)
    return kk, cum, tail, emax, qe, ke


def _hgrn_kernel(hqf_ref, zf_ref, vf_ref, hqb_ref, zb_ref, vb_ref, lbf_ref, lbb_ref,
                 of_ref, ob_ref, st_ref, a_ref, cum_ref, k_ref, qs_ref, k2_ref, et_ref):
    j = pl.program_id(1)

    @pl.when(j == 0)
    def _():
        st_ref[...] = jnp.zeros_like(st_ref)

    CH = HG_CHUNK
    GW = HG_HEADS_PER_GROUP * HEAD_DIM
    ngroups = hqf_ref.shape[1] // GW
    nchunks = hqf_ref.shape[0] // CH
    t_i = lax.broadcasted_iota(I32, (CH, CH), 0)
    u_i = lax.broadcasted_iota(I32, (CH, CH), 1)
    tris = ((u_i <= t_i).astype(BF16), (u_i >= t_i).astype(BF16))
    t_w = lax.broadcasted_iota(I32, (CH, GW), 0)
    s_w = lax.broadcasted_iota(I32, (CH, GW), 1) % CH
    cmasks = (s_w <= t_w, s_w >= t_w)
    r_b = lax.broadcasted_iota(I32, (GW, GW), 0)
    c_b = lax.broadcasted_iota(I32, (GW, GW), 1)
    bm = (r_b // HEAD_DIM) == (c_b // HEAD_DIM)
    hsel_base = (r_b // HEAD_DIM) * HEAD_DIM
    dirs = ((hqf_ref, zf_ref, vf_ref, lbf_ref, of_ref), (hqb_ref, zb_ref, vb_ref, lbb_ref, ob_ref))
    chains = [(d, g) for d in range(2) for g in range(ngroups)]

    def expand(m):
        return jnp.where(bm, jnp.concatenate([m] * HG_HEADS_PER_GROUP, axis=0), jnp.zeros((GW, GW), m.dtype))

    nchain = len(chains)

    def rows_of(c):
        return pl.multiple_of(c * CH, CH), pl.multiple_of((nchunks - 1 - c) * CH, CH)

    def prep_body(c, worst):
        rows = rows_of(c)
        for n, (d, g) in enumerate(chains):
            hq_r, z_r, _, lb_r, _ = dirs[d]
            ls = slice(g * GW, (g + 1) * GW)
            hq = hq_r[pl.ds(rows[d], CH), ls]
            z = z_r[pl.ds(rows[d], CH), ls]
            last_row = CH - 1 if d == 0 else 0
            kk, cum, tail, emax, qe, ke = _hgrn_prepare(hq, z, lb_r[:, ls], tris[d], last_row, CH // 2)
            m = c * nchain + n
            a_ref[m] = _dot_nt(qe, expand(ke))
            cum_ref[m] = cum
            k_ref[m] = kk
            qs_ref[m] = hq * jnp.exp(cum)
            k2_ref[m] = kk * jnp.exp(tail - cum)
            et_ref[m] = jnp.broadcast_to(jnp.exp(tail), et_ref.shape[1:])
            worst = jnp.maximum(worst, emax)
        return worst

    worst = lax.fori_loop(0, nchunks, prep_body, jnp.zeros((1, 1), F32))

    @pl.when(worst[0, 0] > HG_SAFE_EXP)
    def _():
        def slow_body(c, carry):
            rows = rows_of(c)
            for n, (d, g) in enumerate(chains):
                m = c * nchain + n
                hq = dirs[d][0][pl.ds(rows[d], CH), g * GW:(g + 1) * GW]
                cum = cum_ref[m]

                def key_body(s, acc):
                    cs = cum_ref[m, pl.ds(s, 1), :]
                    ksr = k_ref[m, pl.ds(s, 1), :]
                    p = hq * ksr * jnp.exp(jnp.minimum(cum - cs, 0.0))
                    hsel = jnp.where(c_b == hsel_base + s, 1.0, 0.0).astype(BF16)
                    return acc + _dot(p.astype(BF16), hsel)

                a_ref[m] = lax.fori_loop(0, CH, key_body, jnp.zeros((CH, GW), F32))
            return carry

        lax.fori_loop(0, nchunks, slow_body, 0)

    def state_body(c, carry):
        rows = rows_of(c)
        for n, (d, g) in enumerate(chains):
            m = c * nchain + n
            ls = slice(g * GW, (g + 1) * GW)
            vb = dirs[d][2][pl.ds(rows[d], CH), ls].astype(BF16)
            st = st_ref[n]
            a = jnp.where(cmasks[d], a_ref[m], 0.0).astype(BF16)
            o = _dot(a, expand(vb)) + _dot_nt(qs_ref[m].astype(BF16), st.astype(BF16))
            dirs[d][4][pl.ds(rows[d], CH), ls] = o
            st_ref[n] = et_ref[m, 0:1, :] * st + jnp.where(bm, _dot_tn(vb, k2_ref[m].astype(BF16)), 0.0)
        return carry

    lax.fori_loop(0, nchunks, state_body, 0)


def _hgrn2(dm, hq, ff, fb, hi, lb):
    B, T = dm.B, dm.T
    Wd = hq.shape[1]
    GW = HG_HEADS_PER_GROUP * HEAD_DIM
    nchain = 2 * (Wd // GW)
    nsteps = nchain * (SEQ_BLOCK // HG_CHUNK)
    nblk = 1 + dm.N // SEQ_BLOCK
    fwd, bwd = _seq_block_maps(dm)
    fspec = pl.BlockSpec((SEQ_BLOCK, Wd), lambda b, j: (fwd(b, j), 0))
    bspec = pl.BlockSpec((SEQ_BLOCK, Wd), lambda b, j: (bwd(b, j), 0))
    lbspec = pl.BlockSpec((1, Wd), lambda b, j: (0, 0))
    return pl.pallas_call(
        _hgrn_kernel,
        grid=(B, nblk),
        in_specs=[fspec, fspec, fspec, bspec, bspec, bspec, lbspec, lbspec],
        out_specs=[fspec, bspec],
        out_shape=[jax.ShapeDtypeStruct((T, Wd), F32)] * 2,
        scratch_shapes=[pltpu.VMEM((nchain, GW, GW), F32)]
        + [pltpu.VMEM((nsteps, HG_CHUNK, GW), F32)] * 5 + [pltpu.VMEM((nsteps, 8, GW), F32)],
        compiler_params=_cparams(("arbitrary", "arbitrary")),
        name="hgrn2_bidirectional",
    )(hq, ff, hi, hq, fb, hi, lb[0:1], lb[1:2])


LRU_GROUP = 8


def _lru_group_scan(a, x, reverse):
    n, w = a.shape
    a = a.reshape(n // LRU_GROUP, LRU_GROUP, w)
    x = x.reshape(n // LRU_GROUP, LRU_GROUP, w)
    row = lax.broadcasted_iota(I32, a.shape, 1)
    s = 1
    while s < LRU_GROUP:
        shift = (LRU_GROUP - s) if reverse else s
        a_sh = pltpu.roll(a, shift, 1)
        x_sh = pltpu.roll(x, shift, 1)
        valid = (row < LRU_GROUP - s) if reverse else (row >= s)
        x = jnp.where(valid, a * x_sh + x, x)
        a = jnp.where(valid, a * a_sh, a)
        s *= 2
    return a.reshape(n, w), x.reshape(n, w)


def _lru_kernel(ucf_ref, upf_ref, unf_ref, ucb_ref, upb_ref, unb_ref, cw_ref, cb_ref, wbd_ref,
                ba_ref, bx_ref, lam_ref, hf_ref, hb_ref, ext_ref, carry_ref, sa_ref, sx_ref):
    j = pl.program_id(1)
    nblk = pl.num_programs(1)

    @pl.when(j == 0)
    def _():
        carry_ref[...] = jnp.zeros_like(carry_ref)

    TT, W = ucf_ref.shape
    GW = wbd_ref.shape[2]
    halo = upf_ref.shape[0]
    dirs = ((ucf_ref, upf_ref, unf_ref, hf_ref), (ucb_ref, upb_ref, unb_ref, hb_ref))
    for d, (uc_r, up_r, un_r, out_r) in enumerate(dirs):
        pos = j if d == 0 else jnp.where(j == 0, 0, nblk - j)
        keep_prev = jnp.where((pos == 0) | (pos == 1), 0.0, 1.0)
        keep_next = jnp.where((pos == 0) | (pos == nblk - 1), 0.0, 1.0)
        ext_ref[d, 0:halo, :] = up_r[...] * keep_prev
        ext_ref[d, halo:halo + TT, :] = uc_r[...]
        ext_ref[d, halo + TT:halo + TT + halo, :] = un_r[...] * keep_next
        for g in range(W // GW):
            ls = slice(g * GW, (g + 1) * GW)
            u = cb_ref[:, ls]
            for tap in range(CONV_W):
                off = halo - 2 + tap
                u = u + cw_ref[tap:tap + 1, ls] * ext_ref[d, off:off + TT, ls]
            gates = _dot(u.astype(BF16), wbd_ref[d, g])
            r = jax.nn.sigmoid(gates[:, :GW] + ba_ref[d:d + 1, ls])
            ig = jax.nn.sigmoid(gates[:, GW:] + bx_ref[d:d + 1, ls])
            nl = -lam_ref[d:d + 1, ls]
            softplus = jnp.maximum(nl, 0.0) + jnp.log1p(jnp.exp(-jnp.abs(nl)))
            log_a = -LRU_C * r * softplus
            a = jnp.exp(log_a)
            gain2 = 1.0 - jnp.exp(2.0 * log_a)
            gain = jnp.where(gain2 > 0.0, gain2 * lax.rsqrt(gain2), 0.0)
            x_in = gain * ig * u
            a_acc, h = _lru_group_scan(a, x_in, d == 1)
            sa_ref[d, :, ls] = a_acc
            sx_ref[d, :, ls] = h

    ngroups = TT // LRU_GROUP

    def group_body(gi, carry):
        hf, hb = carry
        rf = pl.multiple_of(gi * LRU_GROUP, LRU_GROUP)
        rb = pl.multiple_of((ngroups - 1 - gi) * LRU_GROUP, LRU_GROUP)
        of = sx_ref[0, pl.ds(rf, LRU_GROUP), :] + sa_ref[0, pl.ds(rf, LRU_GROUP), :] * hf
        ob = sx_ref[1, pl.ds(rb, LRU_GROUP), :] + sa_ref[1, pl.ds(rb, LRU_GROUP), :] * hb
        hf_ref[pl.ds(rf, LRU_GROUP), :] = of
        hb_ref[pl.ds(rb, LRU_GROUP), :] = ob
        return of[LRU_GROUP - 1:LRU_GROUP, :], ob[0:1, :]

    hf, hb = lax.fori_loop(0, ngroups, group_body, (carry_ref[0, 0:1, :], carry_ref[1, 0:1, :]))
    carry_ref[0, 0:1, :] = hf
    carry_ref[1, 0:1, :] = hb


def _block_diag_gates(w_a, w_x, group):
    ndir, K, d, _ = w_a.shape
    per = group // d
    eye = jnp.eye(per, dtype=w_a.dtype)

    def bd(w):
        w = w.reshape(ndir, K // per, per, d, d)
        full = jnp.einsum('ngkde,kl->ngkdle', w, eye)
        return full.reshape(ndir, K // per, group, group)

    return jnp.concatenate([bd(w_a), bd(w_x)], axis=-1)


def _rglru(dm, u, conv_w, conv_b, w_a, b_a, w_x, b_x, lam):
    B, T = dm.B, dm.T
    W = u.shape[1]
    GW = 256
    halo = 8
    nblk = 1 + dm.N // SEQ_BLOCK
    fwd, bwd = _seq_block_maps(dm)
    hb = SEQ_BLOCK // halo
    nh = T // halo
    wbd = _block_diag_gates(w_a, w_x, GW).astype(BF16)

    def cur(m):
        return pl.BlockSpec((SEQ_BLOCK, W), lambda b, j: (m(b, j), 0))

    def prev(m):
        return pl.BlockSpec((halo, W), lambda b, j: (jnp.maximum(m(b, j) * hb - 1, 0), 0))

    def nxt(m):
        return pl.BlockSpec((halo, W), lambda b, j: (jnp.minimum((m(b, j) + 1) * hb, nh - 1), 0))

    def full(a):
        nd = a.ndim
        return pl.BlockSpec(a.shape, lambda b, j: (0,) * nd)

    consts = (conv_w, conv_b.reshape(1, W), wbd, b_a, b_x, lam)
    return pl.pallas_call(
        _lru_kernel,
        grid=(B, nblk),
        in_specs=[cur(fwd), prev(fwd), nxt(fwd), cur(bwd), prev(bwd), nxt(bwd)] + [full(a) for a in consts],
        out_specs=[cur(fwd), cur(bwd)],
        out_shape=[jax.ShapeDtypeStruct((T, W), F32)] * 2,
        scratch_shapes=[pltpu.VMEM((2, SEQ_BLOCK + 2 * halo, W), F32), pltpu.VMEM((2, 8, W), F32),
                        pltpu.VMEM((2, SEQ_BLOCK, W), F32), pltpu.VMEM((2, SEQ_BLOCK, W), F32)],
        compiler_params=_cparams(("arbitrary", "arbitrary")),
        name="rglru_bidirectional",
    )(u, u, u, u, u, u, *consts)


def _post_mixer_epilogue(o, x_ref, m2_ref, m3_ref, m4_ref, lng_ref, lnb_ref, rwh_ref, rwl_ref,
                         xo_ref, h2_ref, lg_ref, alpha):
    z = alpha * x_ref[...] + m2_ref[...] * o
    xn = _layer_norm(z, lng_ref[...], lnb_ref[...])
    xo_ref[...] = xn
    h2 = xn * (1.0 + m4_ref[...]) + m3_ref[...]
    h2_ref[...] = _pack_rows(h2)
    hh, hl = _split(h2)
    lg_ref[...] = _dot_nt(rwh_ref[...], hh) + _dot_nt(rwh_ref[...], hl) + _dot_nt(rwl_ref[...], hh)


def _post_even_kernel(nac_ref, nal_ref, of_ref, ob_ref, hg_ref, ng_ref, w_ref, *rest, alpha, ctx_tiles):
    na = jnp.where(pl.program_id(0) < ctx_tiles, nac_ref[...], nal_ref[...])
    o = of_ref[...] + ob_ref[...]
    wd = o.shape[1]
    r_i = lax.broadcasted_iota(I32, (wd, wd), 0) // HEAD_DIM
    c_i = lax.broadcasted_iota(I32, (wd, wd), 1) // HEAD_DIM
    avg = jnp.where(r_i == c_i, 1.0 / HEAD_DIM, 0.0).astype(BF16)
    sh, sl = _split(o * o)
    ms = _dot(sh, avg) + _dot(sl, avg)
    r = o * lax.rsqrt(ms + RMS_EPS) * ng_ref[...] * _silu(hg_ref[...])
    mix = _dot(na, w_ref[:wd, :]) + _dot(r.astype(BF16), w_ref[wd:, :])
    _post_mixer_epilogue(mix, *rest, alpha=alpha)


def _post_odd_kernel(hf_ref, hb_ref, y_ref, w_ref, *rest, alpha):
    m = ((hf_ref[...] + hb_ref[...]) * _gelu_tanh(y_ref[...])).astype(BF16)
    _post_mixer_epilogue(_dot(m, w_ref[...]), *rest, alpha=alpha)


def _post_mixer(dm, mixer_inputs, w_out, xs, m2, m3, m4, lng, lnb, rwh, rwl, alpha, even):
    T, D, TM = dm.T, dm.D, dm.TM
    E = rwh.shape[0]
    row = lambda a: pl.BlockSpec((TM, a.shape[1]), lambda i: (i, 0))
    const = lambda a: pl.BlockSpec(a.shape, lambda i: (0,) * a.ndim)
    mod_spec = pl.BlockSpec((None, 1, D), lambda i: (dm.group(i, TM), 0, 0))
    if even:
        nac, nal, of, ob, hg, ng = mixer_inputs
        nct = dm.T_ctx // TM
        ins = [nac, nal, of, ob, hg, ng, w_out]
        specs = [pl.BlockSpec((TM, nac.shape[1]), lambda i: (jnp.minimum(i, nct - 1), 0)),
                 pl.BlockSpec((TM, nal.shape[1]), lambda i: (jnp.maximum(i - nct, 0), 0)),
                 row(of), row(ob), row(hg), const(ng), const(w_out)]
        kern = functools.partial(_post_even_kernel, alpha=alpha, ctx_tiles=nct)
    else:
        hf, hb, y = mixer_inputs
        ins = [hf, hb, y, w_out]
        specs = [row(hf), row(hb), row(y), const(w_out)]
        kern = functools.partial(_post_odd_kernel, alpha=alpha)
    ins += [xs, m2, m3, m4, lng, lnb, rwh, rwl]
    specs += [row(xs), mod_spec, mod_spec, mod_spec, const(lng), const(lnb), const(rwh), const(rwl)]
    return pl.pallas_call(
        kern,
        grid=(T // TM,),
        in_specs=specs,
        out_specs=[pl.BlockSpec((TM, D), lambda i: (i, 0)), pl.BlockSpec((TM, D // 2), lambda i: (i, 0)),
                   pl.BlockSpec((E, TM), lambda i: (0, i))],
        out_shape=[jax.ShapeDtypeStruct((T, D), F32), jax.ShapeDtypeStruct((T, D // 2), jnp.uint32),
                   jax.ShapeDtypeStruct((E, T), F32)],
        compiler_params=_cparams(("parallel",)),
        name="post_mixer_even" if even else "post_mixer_odd",
    )(*ins)


def _route_kernel(lg_ref, bias_ref, tri_ref, idx_ref, rank_ref, w_ref, cnt_ref, carry_ref):
    i = pl.program_id(0)

    @pl.when(i == 0)
    def _():
        carry_ref[...] = jnp.zeros_like(carry_ref)

    E, TK = lg_ref.shape
    s = jax.nn.sigmoid(lg_ref[...])
    work = s + bias_ref[...]
    eio = lax.broadcasted_iota(I32, (E, TK), 0).astype(F32)
    picked = jnp.zeros((E, TK), F32)
    sels, idxs = [], []
    for _ in range(TOP_K):
        m = jnp.max(work, axis=0, keepdims=True)
        ik = jnp.min(jnp.where(work == m, eio, float(E)), axis=0, keepdims=True)
        oh = eio == ik
        sels.append(jnp.sum(jnp.where(oh, s, 0.0), axis=0, keepdims=True))
        idxs.append(ik)
        picked = jnp.where(oh, 1.0, picked)
        work = jnp.where(oh, -jnp.inf, work)
    total = sels[0]
    for sk in sels[1:]:
        total = total + sk
    carry = carry_ref[:, 0:1]
    ranks = carry + _dot(picked.astype(BF16), tri_ref[...])
    out_rows = idx_ref.shape[0]
    rio = lax.broadcasted_iota(I32, (out_rows, TK), 0)
    idx_o = jnp.zeros((out_rows, TK), I32)
    rank_o = jnp.zeros((out_rows, TK), I32)
    w_o = jnp.zeros((out_rows, TK), F32)
    for kx in range(TOP_K):
        rk = jnp.sum(jnp.where(eio == idxs[kx], ranks, 0.0), axis=0, keepdims=True).astype(I32)
        idx_o = jnp.where(rio == kx, idxs[kx].astype(I32), idx_o)
        rank_o = jnp.where(rio == kx, rk, rank_o)
        w_o = jnp.where(rio == kx, sels[kx] / total * ROUTED_SCALE, w_o)
    idx_ref[...] = idx_o
    rank_ref[...] = rank_o
    w_ref[...] = w_o
    new_carry = carry + jnp.sum(picked, axis=1, keepdims=True)
    carry_ref[...] = jnp.broadcast_to(new_carry, carry_ref.shape)
    cnt_ref[...] = jnp.broadcast_to(new_carry, cnt_ref.shape)


def _route(logits_t, router_bias, t0):
    E, T = logits_t.shape
    Tm = T - t0
    TK = _pick_tile(int(np.gcd(Tm, t0)), (1024, 512, 256))
    off = t0 // TK
    tri = jnp.asarray(np.triu(np.ones((TK, TK), np.float32), 1), BF16)
    tok = lambda: pl.BlockSpec((8, TK), lambda i: (0, i))
    idx, rank, w, cnt = pl.pallas_call(
        _route_kernel,
        grid=(Tm // TK,),
        in_specs=[pl.BlockSpec((E, TK), lambda i: (0, i + off)),
                  pl.BlockSpec((E, 1), lambda i: (0, 0)),
                  pl.BlockSpec((TK, TK), lambda i: (0, 0))],
        out_specs=[tok(), tok(), tok(), pl.BlockSpec((E, 128), lambda i: (0, 0))],
        out_shape=[jax.ShapeDtypeStruct((8, Tm), I32), jax.ShapeDtypeStruct((8, Tm), I32),
                   jax.ShapeDtypeStruct((8, Tm), F32), jax.ShapeDtypeStruct((E, 128), F32)],
        scratch_shapes=[pltpu.VMEM((E, 128), F32)],
        compiler_params=_cparams(("arbitrary",)),
        name="moe_route",
    )(logits_t, router_bias.reshape(E, 1), tri)
    return idx, rank, w, cnt[:, 0].astype(I32)


def _sorted_tile_rows(n_exp):
    return -(-(MOE_TILE * TOP_K + n_exp * (ROW_ALIGN - 1)) // 16) * 16


def _run_copies(cnt_ref, src_ref, dst_ref, base, n_exp, chunk, make_copy):
    shift = chunk.bit_length() - 1

    def expert_body(e, total):
        n = cnt_ref[base + e]
        src = src_ref[base + e]
        dst = dst_ref[base + e]
        nch = lax.shift_right_logical(n + (chunk - 1), shift)

        def chunk_body(c, carry):
            make_copy(pl.multiple_of(src + c * chunk, ROW_ALIGN), pl.multiple_of(dst + c * chunk, ROW_ALIGN)).start()
            return carry

        lax.fori_loop(0, nch, chunk_body, 0)
        return total + nch

    return lax.fori_loop(0, n_exp, expert_body, jnp.int32(0))


def _dispatch_kernel(cnt_ref, ls_ref, gb_ref, pos_ref, h_ref, xs_hbm, stage, pending, sem, *, n_exp):
    i = pl.program_id(0)
    slot = i % 2
    TMd = h_ref.shape[0]
    R = stage.shape[1] - DISP_CHUNK

    @pl.when(i == 0)
    def _():
        pending[0] = 0
        for s in range(2):
            stage[s, R:, :] = jnp.zeros((stage.shape[1] - R, stage.shape[2]), stage.dtype)

    rio = lax.broadcasted_iota(I32, (R, TMd), 0)
    perm = jnp.zeros((R, TMd), F32)
    for kx in range(TOP_K):
        perm = jnp.where(rio == pos_ref[kx:kx + 1, :], 1.0, perm)
    stage[slot, 0:R, :] = _pack_rows(_dot(perm.astype(BF16), _unpack_rows(h_ref[...])))

    def make_copy(src, dst):
        return pltpu.make_async_copy(stage.at[slot, pl.ds(src, DISP_CHUNK)], xs_hbm.at[pl.ds(dst, DISP_CHUNK)], sem)

    def wait_copies(n):
        def wait_body(c, carry):
            make_copy(0, 0).wait()
            return carry
        lax.fori_loop(0, n, wait_body, 0)

    wait_copies(pending[0])
    pending[0] = _run_copies(cnt_ref, ls_ref, gb_ref, i * n_exp, n_exp, DISP_CHUNK, make_copy)

    @pl.when(i == pl.num_programs(0) - 1)
    def _():
        wait_copies(pending[0])


def _dispatch(h2, plan, t0, n_slots, n_exp):
    cnt, ls, gb, pos = plan
    T, DP = h2.shape
    TMd = MOE_TILE
    Tm = T - t0
    assert Tm % TMd == 0 and t0 % TMd == 0
    off = t0 // TMd
    grid_spec = pltpu.PrefetchScalarGridSpec(
        num_scalar_prefetch=3,
        grid=(Tm // TMd,),
        in_specs=[pl.BlockSpec((8, TMd), lambda i, *_: (0, i)),
                  pl.BlockSpec((TMd, DP), lambda i, *_: (i + off, 0))],
        out_specs=pl.BlockSpec(memory_space=pl.ANY),
        scratch_shapes=[pltpu.VMEM((2, _sorted_tile_rows(n_exp) + DISP_CHUNK, DP), h2.dtype),
                        pltpu.SMEM((1,), I32), pltpu.SemaphoreType.DMA],
    )
    return pl.pallas_call(
        functools.partial(_dispatch_kernel, n_exp=n_exp),
        grid_spec=grid_spec,
        out_shape=jax.ShapeDtypeStruct((n_slots, DP), h2.dtype),
        compiler_params=_cparams(("arbitrary",)),
        name="moe_dispatch",
    )(cnt, ls, gb, pos, h2)


def _gmm_kernel(bexp_ref, bvalid_ref, x_ref, wg_ref, wu_ref, wd_ref, y_ref):
    del bexp_ref
    i = pl.program_id(0)
    nvalid = bvalid_ref[i]

    @pl.when(nvalid > 0)
    def _():
        row = lax.broadcasted_iota(I32, x_ref.shape, 0)
        xb = _unpack_rows(jnp.where(row < nvalid, x_ref[...], jnp.uint32(0)))
        act = _silu(_dot(xb, wg_ref[...].astype(BF16))) * _dot(xb, wu_ref[...].astype(BF16))
        y_ref[...] = _pack_rows(_dot(act.astype(BF16), wd_ref[...].astype(BF16)))

    @pl.when(nvalid == 0)
    def _():
        y_ref[...] = jnp.zeros_like(y_ref)


def _grouped_experts(xsorted, block_exp, block_valid, wg, wu, wd, layer):
    NS, DP = xsorted.shape
    nb = NS // MOE_BLOCK
    _, _, D, hid = wg.shape
    grid_spec = pltpu.PrefetchScalarGridSpec(
        num_scalar_prefetch=2,
        grid=(nb,),
        in_specs=[pl.BlockSpec((MOE_BLOCK, DP), lambda i, be, bv: (i, 0)),
                  pl.BlockSpec((None, None, D, hid), lambda i, be, bv: (layer, be[i], 0, 0)),
                  pl.BlockSpec((None, None, D, hid), lambda i, be, bv: (layer, be[i], 0, 0)),
                  pl.BlockSpec((None, None, hid, D), lambda i, be, bv: (layer, be[i], 0, 0))],
        out_specs=pl.BlockSpec((MOE_BLOCK, DP), lambda i, be, bv: (i, 0)),
    )
    return pl.pallas_call(
        _gmm_kernel,
        grid_spec=grid_spec,
        out_shape=jax.ShapeDtypeStruct((NS, DP), jnp.uint32),
        compiler_params=_cparams(("arbitrary",)),
        name="moe_grouped_experts",
    )(block_exp, block_valid, xsorted, wg, wu, wd)


def _combine_kernel(cnt_ref, so_ref, gb_ref, y_hbm, qpos_ref, w_ref, h_ref, sgu_ref, sd_ref, x_ref, m5_ref,
                    lng_ref, lnb_ref, o_ref, stage, pending, sems, *, alpha, n_exp):
    i = pl.program_id(0)
    slot = i % 2

    def make_copy(s):
        def f(dst, src):
            return pltpu.make_async_copy(y_hbm.at[pl.ds(src, COMB_CHUNK)], stage.at[s, pl.ds(dst, COMB_CHUNK)],
                                         sems.at[s])
        return f

    def fetch(tile, s):
        pending[s] = _run_copies(cnt_ref, so_ref, gb_ref, tile * n_exp, n_exp, COMB_CHUNK, make_copy(s))

    @pl.when(i == 0)
    def _():
        stage[...] = jnp.zeros_like(stage)
        fetch(0, 0)

    @pl.when(i + 1 < pl.num_programs(0))
    def _():
        fetch(i + 1, 1 - slot)

    hid = sd_ref.shape[0]
    gu = _dot(_unpack_rows(h_ref[...]), sgu_ref[...])
    y = _dot((_silu(gu[:, :hid]) * gu[:, hid:]).astype(BF16), sd_ref[...])

    def wait_body(c, carry):
        make_copy(slot)(0, 0).wait()
        return carry

    lax.fori_loop(0, pending[slot], wait_body, 0)

    TMc, S = h_ref.shape[0], stage.shape[1]
    sio = lax.broadcasted_iota(I32, (TMc, S), 1)
    q = jnp.zeros((TMc, S), F32)
    for kx in range(TOP_K):
        q = jnp.where(sio == qpos_ref[:, kx:kx + 1], w_ref[:, kx:kx + 1], q)
    y = y + _dot(q.astype(BF16), _unpack_rows(stage[slot]))
    z = alpha * x_ref[...] + m5_ref[...] * y
    o_ref[...] = _layer_norm(z, lng_ref[...], lnb_ref[...])


def _combine(dm, plan, ysorted, qpos_tm, w_tm, h2, sgu, sd, xs, m5, lng, lnb, alpha, t0, n_exp):
    cnt, so, gb = plan
    T, D = xs.shape
    TMc = MOE_TILE
    Tm = T - t0
    off = t0 // TMc
    const = lambda a: pl.BlockSpec(a.shape, lambda i, *_: (0,) * a.ndim)
    row = lambda a: pl.BlockSpec((TMc, a.shape[1]), lambda i, *_: (i + off, 0))
    grid_spec = pltpu.PrefetchScalarGridSpec(
        num_scalar_prefetch=3,
        grid=(Tm // TMc,),
        in_specs=[pl.BlockSpec(memory_space=pl.ANY),
                  pl.BlockSpec((TMc, 8), lambda i, *_: (i, 0)), pl.BlockSpec((TMc, 8), lambda i, *_: (i, 0)),
                  row(h2), const(sgu), const(sd), row(xs),
                  pl.BlockSpec((None, 1, D), lambda i, *_: (dm.group(i + off, TMc), 0, 0)),
                  const(lng), const(lnb)],
        out_specs=pl.BlockSpec((TMc, D), lambda i, *_: (i, 0)),
        scratch_shapes=[pltpu.VMEM((2, _sorted_tile_rows(n_exp), ysorted.shape[1]), ysorted.dtype),
                        pltpu.SMEM((2,), I32), pltpu.SemaphoreType.DMA((2,))],
    )
    return pl.pallas_call(
        functools.partial(_combine_kernel, alpha=alpha, n_exp=n_exp),
        grid_spec=grid_spec,
        out_shape=jax.ShapeDtypeStruct((Tm, D), F32),
        compiler_params=_cparams(("arbitrary",)),
        name="moe_combine",
    )(cnt, so, gb, ysorted, qpos_tm, w_tm, h2, sgu, sd, xs, m5, lng, lnb)


def _moe(dm, h2, logits_t, xs, m5, lng, lnb, router_bias, wg, wu, wd, layer, sgu, sd, alpha, t0):
    T = h2.shape[0]
    E = wg.shape[1]
    Tm = T - t0
    nt = Tm // MOE_TILE
    idx, rank, w, _ = _route(logits_t, router_bias, t0)
    idx, rank, w = idx[:TOP_K], rank[:TOP_K], w[:TOP_K]
    eids = jnp.arange(E, dtype=I32)
    onehot = (idx[:, :, None] == eids).reshape(TOP_K, nt, MOE_TILE, E)
    cnt = jnp.sum(onehot, axis=(0, 2), dtype=I32)
    rank_base = jnp.cumsum(cnt, axis=0) - cnt
    run = (cnt + ROW_ALIGN - 1) // ROW_ALIGN * ROW_ALIGN
    tile_base = jnp.cumsum(run, axis=0) - run
    local_start = jnp.cumsum(run, axis=1) - run
    counts = jnp.sum(run, axis=0)
    nb = -(-(Tm * TOP_K + E * (nt * (ROW_ALIGN - 1) + MOE_SLACK)) // MOE_BLOCK) + E
    padded = (counts + MOE_SLACK + MOE_BLOCK - 1) // MOE_BLOCK * MOE_BLOCK
    pends = jnp.cumsum(padded)
    pstarts = pends - padded
    global_base = pstarts[None, :] + tile_base
    bstart = jnp.arange(nb, dtype=I32) * MOE_BLOCK
    block_exp = jnp.minimum(jnp.sum((pends[None, :] <= bstart[:, None]).astype(I32), axis=1), E - 1)
    of_block = block_exp[:, None] == eids[None, :]
    seg_end = jnp.sum(jnp.where(of_block, (pstarts + counts)[None, :], 0), axis=1)
    block_valid = jnp.clip(seg_end - bstart, 0, MOE_BLOCK).astype(I32)
    look = lambda tab: jnp.sum(jnp.where(onehot, tab[None, :, None, :], 0), axis=3).reshape(TOP_K, Tm)
    pos = rank - look(rank_base) + look(local_start)
    pos8 = jnp.pad(pos, ((0, 8 - TOP_K), (0, 0)), constant_values=-1).astype(I32)
    pos_tm = jnp.pad(pos.T, ((0, 0), (0, 8 - TOP_K)), constant_values=-1).astype(I32)
    w_tm = jnp.pad(w.T, ((0, 0), (0, 8 - TOP_K)))
    flat = lambda a: a.reshape(-1).astype(I32)
    plan = (flat(run), flat(local_start), flat(global_base))
    xsorted = _dispatch(h2, plan + (pos8,), t0, nb * MOE_BLOCK, E)
    ysorted = _grouped_experts(xsorted, block_exp, block_valid, wg, wu, wd, layer)
    return _combine(dm, plan, ysorted, pos_tm, w_tm, h2, sgu, sd, xs, m5, lng, lnb, alpha, t0, E)


def kernel(x, c, ctx, c_ctx, ada_w, ada_b, ln_g, ln_b, ev_w_in, ev_w_out, na_rpb, hg_lb_raw, hg_norm_g, od_w_in, od_conv_w, od_conv_b, lru_w_a, lru_b_a, lru_w_x, lru_b_x, lru_lam, od_w_out, router_w, router_bias, exp_w_gate, exp_w_up, exp_w_down, sh_w_gate, sh_w_up, sh_w_down):
    B, N, D = x.shape
    C = ctx.shape[1]
    depth = ada_w.shape[0]
    dm = _Dims(B, N, C, D)
    alpha = float((2 * depth) ** 0.25)
    assert B + 1 <= 8

    cvec = jnp.zeros((8, D), F32).at[:B].set(c).at[B].set(c_ctx)
    mod = _modulation(cvec, ada_w, ada_b).reshape(depth, 8, 6, 1, D)

    p_lb = jax.nn.softmax(hg_lb_raw.astype(F32), axis=1)
    hg_lb = jnp.cumsum(p_lb, axis=1) - p_lb[:, :1]

    xs = jnp.concatenate([ctx.reshape(B * C, D), x.reshape(B * N, D)], axis=0)
    for l in range(depth):
        jl = l // 2
        last = l == depth - 1
        m = [mod[l, :, t] for t in range(6)]
        rw_t = router_w[l].T
        rwh, rwl = _split(rw_t)
        lng = ln_g[l][:, None, :]
        lnb = ln_b[l][:, None, :]
        if l % 2 == 0:
            q, k, v, hq, ff, fb, hi, hg = _inproj(dm, xs, m[1], m[0], ev_w_in[jl].astype(BF16), True)
            bias = _na_bias_tables(na_rpb[jl], N // GRID_W)
            na_lat = _neighbourhood_attention(dm, q, k, v, bias)
            na_ctx = _context_attention(dm, q, k, v)
            o_f, o_b = _hgrn2(dm, hq, ff, fb, hi, hg_lb[:, jl])
            ng = jnp.tile(hg_norm_g[jl], hq.shape[1] // HEAD_DIM)[None, :]
            mixer_inputs = (na_ctx, na_lat, o_f, o_b, hg, ng)
            w_out = ev_w_out[jl].astype(BF16)
        else:
            y, u = _inproj(dm, xs, m[1], m[0], od_w_in[jl].astype(BF16), False)
            h_f, h_b = _rglru(dm, u, od_conv_w[jl], od_conv_b[jl], lru_w_a[jl], lru_b_a[jl],
                              lru_w_x[jl], lru_b_x[jl], lru_lam[jl])
            mixer_inputs = (h_f, h_b, y)
            w_out = od_w_out[jl].astype(BF16)
        xs, h2, logits_t = _post_mixer(dm, mixer_inputs, w_out, xs, m[2], m[3], m[4], lng[0], lnb[0],
                                       rwh, rwl, alpha, l % 2 == 0)
        sgu = jnp.concatenate([sh_w_gate[l], sh_w_up[l]], axis=-1).astype(BF16)
        sd = sh_w_down[l].astype(BF16)
        t0 = dm.T_ctx if last else 0
        xs = _moe(dm, h2, logits_t, xs, m[5], lng[1], lnb[1], router_bias[l], exp_w_gate, exp_w_up,
                  exp_w_down, l, sgu, sd, alpha, t0)
    return xs.reshape(B, N, D)
```

```python
import functools

import numpy as np
import jax
import jax.numpy as jnp
from jax import lax
from jax.experimental import pallas as pl
from jax.experimental.pallas import tpu as pltpu

F32 = jnp.float32
BF16 = jnp.bfloat16
I32 = jnp.int32

HEAD_DIM = 64
GRID_W = 64
NA_WIN_ROWS = 8
NA_WIN_COLS = 16
NA_QROWS = 8
NA_KROWS = 16
HG_CHUNK = 64
HG_HEADS_PER_GROUP = 4
FORGET_FLOOR = 1e-30
HG_SAFE_EXP = 80.0
HG_CLIP_EXP = 85.0
LRU_C = 8.0
LRU_BLOCKS = 16
CONV_W = 4
TOP_K = 6
ROUTED_SCALE = 2.5
LN_EPS = 1e-5
RMS_EPS = 1e-6
SEQ_BLOCK = 256
MOE_BLOCK = 512
MOE_TILE = 256
ROW_ALIGN = 8
DISP_CHUNK = 32
COMB_CHUNK = ROW_ALIGN
MOE_SLACK = DISP_CHUNK
MASK_VALUE = -1e30
VMEM_LIMIT = 56 * 1024 * 1024


def _cparams(sem):
    return pltpu.CompilerParams(dimension_semantics=sem, vmem_limit_bytes=VMEM_LIMIT)


def _split(a):
    hi = a.astype(BF16)
    lo = (a - hi.astype(F32)).astype(BF16)
    return hi, lo


def _dot(a, b):
    return jnp.dot(a, b, preferred_element_type=F32)


def _dot_nt(a, b):
    return lax.dot_general(a, b, (((1,), (1,)), ((), ())), preferred_element_type=F32)


def _dot_tn(a, b):
    return lax.dot_general(a, b, (((0,), (0,)), ((), ())), preferred_element_type=F32)


def _dot3(a, b):
    ah, al = _split(a)
    bh, bl = _split(b)
    return _dot(ah, bh) + _dot(al, bh) + _dot(ah, bl)


def _pack_rows(h):
    half = h.shape[1] // 2
    bits = pltpu.bitcast(h.astype(BF16).astype(F32), jnp.uint32)
    return (bits[:, :half] >> 16) | (bits[:, half:] & jnp.uint32(0xFFFF0000))


def _unpack_rows(p):
    lo = pltpu.bitcast(p << 16, F32)
    hi = pltpu.bitcast(p & jnp.uint32(0xFFFF0000), F32)
    return jnp.concatenate([lo, hi], axis=1).astype(BF16)


def _silu(v):
    return v * jax.nn.sigmoid(v)


def _gelu_tanh(v):
    return 0.5 * v * (1.0 + jnp.tanh(0.7978845608028654 * (v + 0.044715 * v * v * v)))


def _layer_norm(z, g, b):
    mu = jnp.mean(z, axis=-1, keepdims=True)
    zc = z - mu
    var = jnp.mean(zc * zc, axis=-1, keepdims=True)
    return zc * lax.rsqrt(var + LN_EPS) * g + b


def _pick_tile(n, cands):
    for c in cands:
        if n % c == 0:
            return c
    raise ValueError(f"no tile for {n}")


class _Dims:
    def __init__(self, B, N, C, D):
        self.B, self.N, self.C, self.D = B, N, C, D
        self.T_ctx = B * C
        self.T = B * C + B * N
        assert C == SEQ_BLOCK and N % SEQ_BLOCK == 0
        assert N % GRID_W == 0 and (N // GRID_W) % NA_QROWS == 0 and N // GRID_W >= NA_KROWS
        assert self.T_ctx % 512 == 0
        self.TM = _pick_tile(self.T_ctx, (512, 256))
        assert N % self.TM == 0

    def group(self, i, tile):
        start = i * tile
        return jnp.where(start < self.T_ctx, self.B, (start - self.T_ctx) // self.N)


def _mod_kernel(c_ref, w_ref, b_ref, o_ref):
    o_ref[...] = _dot3(_silu(c_ref[...]), w_ref[...]) + b_ref[...]


def _modulation(cvec, ada_w, ada_b):
    L, D, W6 = ada_w.shape
    nc = 1536
    return pl.pallas_call(
        _mod_kernel,
        grid=(L, W6 // nc),
        in_specs=[
            pl.BlockSpec((8, D), lambda l, j: (0, 0)),
            pl.BlockSpec((None, D, nc), lambda l, j: (l, 0, j)),
            pl.BlockSpec((None, 1, nc), lambda l, j: (l, 0, j)),
        ],
        out_specs=pl.BlockSpec((None, 8, nc), lambda l, j: (l, 0, j)),
        out_shape=jax.ShapeDtypeStruct((L, 8, W6), F32),
        compiler_params=_cparams(("arbitrary", "arbitrary")),
        name="adaln_modulation",
    )(cvec, ada_w, ada_b.reshape(L, 1, W6))


def _inproj_even_kernel(x_ref, sc_ref, sh_ref, w_ref, q_ref, k_ref, v_ref, hq_ref, ff_ref, fb_ref, hi_ref, hg_ref):
    h = (x_ref[...] * (1.0 + sc_ref[...]) + sh_ref[...]).astype(BF16)
    wd = q_ref.shape[1]

    def part(j):
        return _dot(h, w_ref[:, j * wd:(j + 1) * wd])

    scale = HEAD_DIM ** -0.5
    q_ref[...] = (part(0) * scale).astype(BF16)
    k_ref[...] = part(1).astype(BF16)
    v_ref[...] = part(2).astype(BF16)
    hq_ref[...] = _silu(part(3)) * scale
    ff_ref[...] = part(4)
    fb_ref[...] = part(5)
    hi_ref[...] = part(6)
    hg_ref[...] = part(7)


def _inproj_odd_kernel(x_ref, sc_ref, sh_ref, w_ref, y_ref, u_ref):
    h = (x_ref[...] * (1.0 + sc_ref[...]) + sh_ref[...]).astype(BF16)
    wd = y_ref.shape[1]
    y_ref[...] = _dot(h, w_ref[:, :wd])
    u_ref[...] = _dot(h, w_ref[:, wd:])


def _inproj(dm, xs, sc, sh, w, even):
    T, D, TM = dm.T, dm.D, dm.TM
    wtot = w.shape[1]
    mod_spec = pl.BlockSpec((None, 1, D), lambda i: (dm.group(i, TM), 0, 0))
    in_specs = [pl.BlockSpec((TM, D), lambda i: (i, 0)), mod_spec, mod_spec,
                pl.BlockSpec((D, wtot), lambda i: (0, 0))]
    if even:
        wd = wtot // 8
        dts = [BF16] * 3 + [F32] * 5
        kern = _inproj_even_kernel
    else:
        wd = wtot // 2
        dts = [F32] * 2
        kern = _inproj_odd_kernel
    return pl.pallas_call(
        kern,
        grid=(T // TM,),
        in_specs=in_specs,
        out_specs=[pl.BlockSpec((TM, wd), lambda i: (i, 0)) for _ in dts],
        out_shape=[jax.ShapeDtypeStruct((T, wd), dt) for dt in dts],
        compiler_params=_cparams(("parallel",)),
        name="inproj_even" if even else "inproj_odd",
    )(xs, sc, sh, w)


def _na_bias_tables(rpb, rows):
    W = GRID_W
    H = rpb.shape[0]
    nr, nc = 2 * NA_WIN_ROWS - 1, 2 * NA_WIN_COLS - 1
    hp = lax.Precision.HIGHEST
    c = np.arange(W)[:, None]
    kc = np.arange(W)[None, :]
    c0 = np.clip(c - NA_WIN_COLS // 2, 0, W - NA_WIN_COLS)
    valid_c = (kc >= c0) & (kc < c0 + NA_WIN_COLS)
    dc = np.clip(kc - c + NA_WIN_COLS - 1, 0, nc - 1)
    oh_c = (dc.reshape(-1, 1) == np.arange(nc)[None, :]).astype(np.float32)
    by_col = jnp.einsum('hrs,xs->hrx', rpb.astype(F32), jnp.asarray(oh_c), precision=hp)
    tabs = []
    for rbase, kb in ((0, 0), (NA_QROWS, NA_QROWS - NA_WIN_ROWS // 2), (rows - NA_QROWS, rows - NA_KROWS)):
        r = rbase + np.arange(NA_QROWS)[:, None]
        kr = kb + np.arange(NA_KROWS)[None, :]
        r0 = np.clip(r - NA_WIN_ROWS // 2, 0, rows - NA_WIN_ROWS)
        valid_r = (kr >= r0) & (kr < r0 + NA_WIN_ROWS)
        dr = np.clip(kr - r + NA_WIN_ROWS - 1, 0, nr - 1)
        oh_r = (dr.reshape(-1, 1) == np.arange(nr)[None, :]).astype(np.float32)
        t = jnp.einsum('yr,hrx->hyx', jnp.asarray(oh_r), by_col, precision=hp)
        t = t.reshape(H, NA_QROWS, NA_KROWS, W, W).transpose(0, 1, 3, 2, 4)
        valid = valid_r[:, None, :, None] & valid_c[None, :, None, :]
        t = jnp.where(jnp.asarray(valid)[None], t, MASK_VALUE)
        tabs.append(t.reshape(H, NA_QROWS * W, NA_KROWS * W))
    return jnp.stack(tabs).astype(F32)


def _na_kernel(q_ref, k0, k1, k2, k3, v0, v1, v2, v3, kc_ref, vc_ref, bias_ref, o_ref):
    q2 = q_ref[...]
    lane = lax.broadcasted_iota(I32, (1, q2.shape[1]), 1)
    first = lane < HEAD_DIM
    ks = [r[...] for r in (k0, k1, k2, k3)]
    vs = [r[...] for r in (v0, v1, v2, v3)]
    kc = kc_ref[...]
    vc = vc_ref[...]
    kb = ks[0].shape[0]
    outs = []
    for hh in range(2):
        sel = first if hh == 0 else jnp.logical_not(first)
        qh = jnp.where(sel, q2, jnp.zeros_like(q2))
        s_loc = jnp.concatenate([_dot_nt(qh, kj) for kj in ks], axis=1) + bias_ref[hh]
        s_ctx = _dot_nt(qh, kc)
        m = jnp.maximum(jnp.max(s_loc, axis=1, keepdims=True), jnp.max(s_ctx, axis=1, keepdims=True))
        p_loc = jnp.exp(s_loc - m)
        p_ctx = jnp.exp(s_ctx - m)
        denom = jnp.sum(p_loc, axis=1, keepdims=True) + jnp.sum(p_ctx, axis=1, keepdims=True)
        o = _dot(p_ctx.astype(BF16), vc)
        for j in range(4):
            o = o + _dot(p_loc[:, j * kb:(j + 1) * kb].astype(BF16), vs[j])
        outs.append(o / denom)
    o_ref[...] = jnp.where(first, outs[0], outs[1]).astype(o_ref.dtype)


def _neighbourhood_attention(dm, q, k, v, bias):
    B, N, C, T = dm.B, dm.N, dm.C, dm.T
    H2 = q.shape[1] // (2 * HEAD_DIM)
    rows = N // GRID_W
    nqb = rows // NA_QROWS
    QB = NA_QROWS * GRID_W
    KB = QB // 2
    ngroups = N // KB
    lat_q0 = dm.T_ctx // QB
    lat_k0 = dm.T_ctx // KB

    def q_map(p, i, b):
        return (lat_q0 + b * (N // QB) + i, p)

    def o_map(p, i, b):
        return (b * (N // QB) + i, p)

    def kv_map(j):
        def f(p, i, b):
            gs = jnp.clip(2 * i - 1, 0, ngroups - 4)
            return (lat_k0 + b * ngroups + gs + j, p)
        return f

    def ctx_map(p, i, b):
        return (b, p)

    def bias_map(p, i, b):
        var = jnp.where(i == 0, 0, jnp.where(i == nqb - 1, 2, 1))
        return (var, p, 0, 0)

    lanes = 2 * HEAD_DIM
    kv_specs = [pl.BlockSpec((KB, lanes), kv_map(j)) for j in range(4)]
    return pl.pallas_call(
        _na_kernel,
        grid=(H2, nqb, B),
        in_specs=[pl.BlockSpec((QB, lanes), q_map)] + kv_specs + kv_specs
        + [pl.BlockSpec((C, lanes), ctx_map), pl.BlockSpec((C, lanes), ctx_map),
           pl.BlockSpec((None, 2, QB, NA_KROWS * GRID_W), bias_map)],
        out_specs=pl.BlockSpec((QB, lanes), o_map),
        out_shape=jax.ShapeDtypeStruct((B * N, q.shape[1]), BF16),
        compiler_params=_cparams(("arbitrary", "arbitrary", "arbitrary")),
        name="neighbourhood_attention",
    )(q, k, k, k, k, v, v, v, v, k, v, bias)


def _ctx_attn_kernel(q_ref, k_ref, v_ref, o_ref):
    q2 = q_ref[...]
    k2 = k_ref[...]
    v2 = v_ref[...]
    lane = lax.broadcasted_iota(I32, (1, q2.shape[1]), 1)
    first = lane < HEAD_DIM
    outs = []
    for hh in range(2):
        sel = first if hh == 0 else jnp.logical_not(first)
        s = _dot_nt(jnp.where(sel, q2, jnp.zeros_like(q2)), k2)
        p = jnp.exp(s - jnp.max(s, axis=1, keepdims=True))
        outs.append(_dot(p.astype(BF16), v2) / jnp.sum(p, axis=1, keepdims=True))
    o_ref[...] = jnp.where(first, outs[0], outs[1]).astype(o_ref.dtype)


def _context_attention(dm, q, k, v):
    lanes = 2 * HEAD_DIM
    spec = pl.BlockSpec((dm.C, lanes), lambda p, b: (b, p))
    return pl.pallas_call(
        _ctx_attn_kernel,
        grid=(q.shape[1] // lanes, dm.B),
        in_specs=[spec, spec, spec],
        out_specs=spec,
        out_shape=jax.ShapeDtypeStruct((dm.T_ctx, q.shape[1]), BF16),
        compiler_params=_cparams(("arbitrary", "arbitrary")),
        name="context_attention",
    )(q, k, v)


def _seq_block_maps(dm):
    nl = dm.N // SEQ_BLOCK
    base = dm.T_ctx // SEQ_BLOCK

    def fwd(b, j):
        return jnp.where(j == 0, b, base + b * nl + j - 1)

    def bwd(b, j):
        return jnp.where(j == 0, b, base + b * nl + nl - j)

    return fwd, bwd


def _hgrn_prepare(hq, z, lb, tri, last_row, mid_row):
    f = lb + (1.0 - lb) * jax.nn.sigmoid(z)
    fm = jnp.maximum(f, FORGET_FLOOR)
    g = jnp.log(fm)
    kk = 1.0 - fm
    gh, gl = _split(g)
    cum = _dot(tri, gh) + _dot(tri, gl)
    tail = cum[last_row:last_row + 1, :]
    e = cum - cum[mid_row:mid_row + 1, :]
    emax = jnp.max(jnp.max(jnp.abs(e), axis=1, keepdims=True), axis=0, keepdims=True)
    ec = jnp.clip(e, -HG_CLIP_EXP, HG_CLIP_EXP)
    qe = (hq * jnp.exp(ec)).astype(BF16)
    ke = (kk * jnp.exp(-ec)).astype(BF16)
    return kk, cum, tail, emax, qe, ke


def _hgrn_kernel(hqf_ref, zf_ref, vf_ref, hqb_ref, zb_ref, vb_ref, lbf_ref, lbb_ref,
                 of_ref, ob_ref, st_ref, a_ref, cum_ref, k_ref, qs_ref, k2_ref, et_ref):
    j = pl.program_id(1)

    @pl.when(j == 0)
    def _():
        st_ref[...] = jnp.zeros_like(st_ref)

    CH = HG_CHUNK
    GW = HG_HEADS_PER_GROUP * HEAD_DIM
    ngroups = hqf_ref.shape[1] // GW
    nchunks = hqf_ref.shape[0] // CH
    t_i = lax.broadcasted_iota(I32, (CH, CH), 0)
    u_i = lax.broadcasted_iota(I32, (CH, CH), 1)
    tris = ((u_i <= t_i).astype(BF16), (u_i >= t_i).astype(BF16))
    t_w = lax.broadcasted_iota(I32, (CH, GW), 0)
    s_w = lax.broadcasted_iota(I32, (CH, GW), 1) % CH
    cmasks = (s_w <= t_w, s_w >= t_w)
    r_b = lax.broadcasted_iota(I32, (GW, GW), 0)
    c_b = lax.broadcasted_iota(I32, (GW, GW), 1)
    bm = (r_b // HEAD_DIM) == (c_b // HEAD_DIM)
    hsel_base = (r_b // HEAD_DIM) * HEAD_DIM
    dirs = ((hqf_ref, zf_ref, vf_ref, lbf_ref, of_ref), (hqb_ref, zb_ref, vb_ref, lbb_ref, ob_ref))
    chains = [(d, g) for d in range(2) for g in range(ngroups)]

    def expand(m):
        return jnp.where(bm, jnp.concatenate([m] * HG_HEADS_PER_GROUP, axis=0), jnp.zeros((GW, GW), m.dtype))

    nchain = len(chains)

    def rows_of(c):
        return pl.multiple_of(c * CH, CH), pl.multiple_of((nchunks - 1 - c) * CH, CH)

    def prep_body(c, worst):
        rows = rows_of(c)
        for n, (d, g) in enumerate(chains):
            hq_r, z_r, _, lb_r, _ = dirs[d]
            ls = slice(g * GW, (g + 1) * GW)
            hq = hq_r[pl.ds(rows[d], CH), ls]
            z = z_r[pl.ds(rows[d], CH), ls]
            last_row = CH - 1 if d == 0 else 0
            kk, cum, tail, emax, qe, ke = _hgrn_prepare(hq, z, lb_r[:, ls], tris[d], last_row, CH // 2)
            m = c * nchain + n
            a_ref[m] = _dot_nt(qe, expand(ke))
            cum_ref[m] = cum
            k_ref[m] = kk
            qs_ref[m] = hq * jnp.exp(cum)
            k2_ref[m] = kk * jnp.exp(tail - cum)
            et_ref[m] = jnp.broadcast_to(jnp.exp(tail), et_ref.shape[1:])
            worst = jnp.maximum(worst, emax)
        return worst

    worst = lax.fori_loop(0, nchunks, prep_body, jnp.zeros((1, 1), F32), unroll=True)

    @pl.when(worst[0, 0] > HG_SAFE_EXP)
    def _():
        def slow_body(c, carry):
            rows = rows_of(c)
            for n, (d, g) in enumerate(chains):
                m = c * nchain + n
                hq = dirs[d][0][pl.ds(rows[d], CH), g * GW:(g + 1) * GW]
                cum = cum_ref[m]

                def key_body(s, acc):
                    cs = cum_ref[m, pl.ds(s, 1), :]
                    ksr = k_ref[m, pl.ds(s, 1), :]
                    p = hq * ksr * jnp.exp(jnp.minimum(cum - cs, 0.0))
                    hsel = jnp.where(c_b == hsel_base + s, 1.0, 0.0).astype(BF16)
                    return acc + _dot(p.astype(BF16), hsel)

                a_ref[m] = lax.fori_loop(0, CH, key_body, jnp.zeros((CH, GW), F32))
            return carry

        lax.fori_loop(0, nchunks, slow_body, 0)

    def state_body(c, carry):
        rows = rows_of(c)
        for n, (d, g) in enumerate(chains):
            m = c * nchain + n
            ls = slice(g * GW, (g + 1) * GW)
            vb = dirs[d][2][pl.ds(rows[d], CH), ls].astype(BF16)
            st = st_ref[n]
            a = jnp.where(cmasks[d], a_ref[m], 0.0).astype(BF16)
            o = _dot(a, expand(vb)) + _dot_nt(qs_ref[m].astype(BF16), st.astype(BF16))
            dirs[d][4][pl.ds(rows[d], CH), ls] = o
            st_ref[n] = et_ref[m, 0:1, :] * st + jnp.where(bm, _dot_tn(vb, k2_ref[m].astype(BF16)), 0.0)
        return carry

    lax.fori_loop(0, nchunks, state_body, 0)


def _hgrn2(dm, hq, ff, fb, hi, lb):
    B, T = dm.B, dm.T
    Wd = hq.shape[1]
    GW = HG_HEADS_PER_GROUP * HEAD_DIM
    nchain = 2 * (Wd // GW)
    nsteps = nchain * (SEQ_BLOCK // HG_CHUNK)
    nblk = 1 + dm.N // SEQ_BLOCK
    fwd, bwd = _seq_block_maps(dm)
    fspec = pl.BlockSpec((SEQ_BLOCK, Wd), lambda b, j: (fwd(b, j), 0))
    bspec = pl.BlockSpec((SEQ_BLOCK, Wd), lambda b, j: (bwd(b, j), 0))
    lbspec = pl.BlockSpec((1, Wd), lambda b, j: (0, 0))
    return pl.pallas_call(
        _hgrn_kernel,
        grid=(B, nblk),
        in_specs=[fspec, fspec, fspec, bspec, bspec, bspec, lbspec, lbspec],
        out_specs=[fspec, bspec],
        out_shape=[jax.ShapeDtypeStruct((T, Wd), F32)] * 2,
        scratch_shapes=[pltpu.VMEM((nchain, GW, GW), F32)]
        + [pltpu.VMEM((nsteps, HG_CHUNK, GW), F32)] * 5 + [pltpu.VMEM((nsteps, 8, GW), F32)],
        compiler_params=_cparams(("arbitrary", "arbitrary")),
        name="hgrn2_bidirectional",
    )(hq, ff, hi, hq, fb, hi, lb[0:1], lb[1:2])


LRU_GROUP = 8


def _lru_group_scan(a, x, reverse):
    n, w = a.shape
    a = a.reshape(n // LRU_GROUP, LRU_GROUP, w)
    x = x.reshape(n // LRU_GROUP, LRU_GROUP, w)
    row = lax.broadcasted_iota(I32, a.shape, 1)
    s = 1
    while s < LRU_GROUP:
        shift = (LRU_GROUP - s) if reverse else s
        a_sh = pltpu.roll(a, shift, 1)
        x_sh = pltpu.roll(x, shift, 1)
        valid = (row < LRU_GROUP - s) if reverse else (row >= s)
        x = jnp.where(valid, a * x_sh + x, x)
        a = jnp.where(valid, a * a_sh, a)
        s *= 2
    return a.reshape(n, w), x.reshape(n, w)


def _lru_kernel(ucf_ref, upf_ref, unf_ref, ucb_ref, upb_ref, unb_ref, cw_ref, cb_ref, wbd_ref,
                ba_ref, bx_ref, lam_ref, hf_ref, hb_ref, ext_ref, carry_ref, sa_ref, sx_ref):
    j = pl.program_id(1)
    nblk = pl.num_programs(1)

    @pl.when(j == 0)
    def _():
        carry_ref[...] = jnp.zeros_like(carry_ref)

    TT, W = ucf_ref.shape
    GW = wbd_ref.shape[2]
    halo = upf_ref.shape[0]
    dirs = ((ucf_ref, upf_ref, unf_ref, hf_ref), (ucb_ref, upb_ref, unb_ref, hb_ref))
    for d, (uc_r, up_r, un_r, out_r) in enumerate(dirs):
        pos = j if d == 0 else jnp.where(j == 0, 0, nblk - j)
        keep_prev = jnp.where((pos == 0) | (pos == 1), 0.0, 1.0)
        keep_next = jnp.where((pos == 0) | (pos == nblk - 1), 0.0, 1.0)
        ext_ref[d, 0:halo, :] = up_r[...] * keep_prev
        ext_ref[d, halo:halo + TT, :] = uc_r[...]
        ext_ref[d, halo + TT:halo + TT + halo, :] = un_r[...] * keep_next
        for g in range(W // GW):
            ls = slice(g * GW, (g + 1) * GW)
            u = cb_ref[:, ls]
            for tap in range(CONV_W):
                off = halo - 2 + tap
                u = u + cw_ref[tap:tap + 1, ls] * ext_ref[d, off:off + TT, ls]
            gates = _dot(u.astype(BF16), wbd_ref[d, g])
            r = jax.nn.sigmoid(gates[:, :GW] + ba_ref[d:d + 1, ls])
            ig = jax.nn.sigmoid(gates[:, GW:] + bx_ref[d:d + 1, ls])
            nl = -lam_ref[d:d + 1, ls]
            softplus = jnp.maximum(nl, 0.0) + jnp.log1p(jnp.exp(-jnp.abs(nl)))
            log_a = -LRU_C * r * softplus
            a = jnp.exp(log_a)
            gain2 = 1.0 - jnp.exp(2.0 * log_a)
            gain = jnp.where(gain2 > 0.0, gain2 * lax.rsqrt(gain2), 0.0)
            x_in = gain * ig * u
            a_acc, h = _lru_group_scan(a, x_in, d == 1)
            sa_ref[d, :, ls] = a_acc
            sx_ref[d, :, ls] = h

    ngroups = TT // LRU_GROUP

    def group_body(gi, carry):
        hf, hb = carry
        rf = pl.multiple_of(gi * LRU_GROUP, LRU_GROUP)
        rb = pl.multiple_of((ngroups - 1 - gi) * LRU_GROUP, LRU_GROUP)
        of = sx_ref[0, pl.ds(rf, LRU_GROUP), :] + sa_ref[0, pl.ds(rf, LRU_GROUP), :] * hf
        ob = sx_ref[1, pl.ds(rb, LRU_GROUP), :] + sa_ref[1, pl.ds(rb, LRU_GROUP), :] * hb
        hf_ref[pl.ds(rf, LRU_GROUP), :] = of
        hb_ref[pl.ds(rb, LRU_GROUP), :] = ob
        return of[LRU_GROUP - 1:LRU_GROUP, :], ob[0:1, :]

    hf, hb = lax.fori_loop(0, ngroups, group_body, (carry_ref[0, 0:1, :], carry_ref[1, 0:1, :]))
    carry_ref[0, 0:1, :] = hf
    carry_ref[1, 0:1, :] = hb


def _block_diag_gates(w_a, w_x, group):
    ndir, K, d, _ = w_a.shape
    per = group // d
    eye = jnp.eye(per, dtype=w_a.dtype)

    def bd(w):
        w = w.reshape(ndir, K // per, per, d, d)
        full = jnp.einsum('ngkde,kl->ngkdle', w, eye)
        return full.reshape(ndir, K // per, group, group)

    return jnp.concatenate([bd(w_a), bd(w_x)], axis=-1)


def _rglru(dm, u, conv_w, conv_b, w_a, b_a, w_x, b_x, lam):
    B, T = dm.B, dm.T
    W = u.shape[1]
    GW = 256
    halo = 8
    nblk = 1 + dm.N // SEQ_BLOCK
    fwd, bwd = _seq_block_maps(dm)
    hb = SEQ_BLOCK // halo
    nh = T // halo
    wbd = _block_diag_gates(w_a, w_x, GW).astype(BF16)

    def cur(m):
        return pl.BlockSpec((SEQ_BLOCK, W), lambda b, j: (m(b, j), 0))

    def prev(m):
        return pl.BlockSpec((halo, W), lambda b, j: (jnp.maximum(m(b, j) * hb - 1, 0), 0))

    def nxt(m):
        return pl.BlockSpec((halo, W), lambda b, j: (jnp.minimum((m(b, j) + 1) * hb, nh - 1), 0))

    def full(a):
        nd = a.ndim
        return pl.BlockSpec(a.shape, lambda b, j: (0,) * nd)

    consts = (conv_w, conv_b.reshape(1, W), wbd, b_a, b_x, lam)
    return pl.pallas_call(
        _lru_kernel,
        grid=(B, nblk),
        in_specs=[cur(fwd), prev(fwd), nxt(fwd), cur(bwd), prev(bwd), nxt(bwd)] + [full(a) for a in consts],
        out_specs=[cur(fwd), cur(bwd)],
        out_shape=[jax.ShapeDtypeStruct((T, W), F32)] * 2,
        scratch_shapes=[pltpu.VMEM((2, SEQ_BLOCK + 2 * halo, W), F32), pltpu.VMEM((2, 8, W), F32),
                        pltpu.VMEM((2, SEQ_BLOCK, W), F32), pltpu.VMEM((2, SEQ_BLOCK, W), F32)],
        compiler_params=_cparams(("arbitrary", "arbitrary")),
        name="rglru_bidirectional",
    )(u, u, u, u, u, u, *consts)


def _post_mixer_epilogue(o, x_ref, m2_ref, m3_ref, m4_ref, lng_ref, lnb_ref, rwh_ref, rwl_ref,
                         xo_ref, h2_ref, lg_ref, alpha):
    z = alpha * x_ref[...] + m2_ref[...] * o
    xn = _layer_norm(z, lng_ref[...], lnb_ref[...])
    xo_ref[...] = xn
    h2 = xn * (1.0 + m4_ref[...]) + m3_ref[...]
    h2_ref[...] = _pack_rows(h2)
    hh, hl = _split(h2)
    lg_ref[...] = _dot_nt(rwh_ref[...], hh) + _dot_nt(rwh_ref[...], hl) + _dot_nt(rwl_ref[...], hh)


def _post_even_kernel(nac_ref, nal_ref, of_ref, ob_ref, hg_ref, ng_ref, w_ref, *rest, alpha, ctx_tiles):
    na = jnp.where(pl.program_id(0) < ctx_tiles, nac_ref[...], nal_ref[...])
    o = of_ref[...] + ob_ref[...]
    wd = o.shape[1]
    r_i = lax.broadcasted_iota(I32, (wd, wd), 0) // HEAD_DIM
    c_i = lax.broadcasted_iota(I32, (wd, wd), 1) // HEAD_DIM
    avg = jnp.where(r_i == c_i, 1.0 / HEAD_DIM, 0.0).astype(BF16)
    sh, sl = _split(o * o)
    ms = _dot(sh, avg) + _dot(sl, avg)
    r = o * lax.rsqrt(ms + RMS_EPS) * ng_ref[...] * _silu(hg_ref[...])
    mix = _dot(na, w_ref[:wd, :]) + _dot(r.astype(BF16), w_ref[wd:, :])
    _post_mixer_epilogue(mix, *rest, alpha=alpha)


def _post_odd_kernel(hf_ref, hb_ref, y_ref, w_ref, *rest, alpha):
    m = ((hf_ref[...] + hb_ref[...]) * _gelu_tanh(y_ref[...])).astype(BF16)
    _post_mixer_epilogue(_dot(m, w_ref[...]), *rest, alpha=alpha)


def _post_mixer(dm, mixer_inputs, w_out, xs, m2, m3, m4, lng, lnb, rwh, rwl, alpha, even):
    T, D, TM = dm.T, dm.D, dm.TM
    E = rwh.shape[0]
    row = lambda a: pl.BlockSpec((TM, a.shape[1]), lambda i: (i, 0))
    const = lambda a: pl.BlockSpec(a.shape, lambda i: (0,) * a.ndim)
    mod_spec = pl.BlockSpec((None, 1, D), lambda i: (dm.group(i, TM), 0, 0))
    if even:
        nac, nal, of, ob, hg, ng = mixer_inputs
        nct = dm.T_ctx // TM
        ins = [nac, nal, of, ob, hg, ng, w_out]
        specs = [pl.BlockSpec((TM, nac.shape[1]), lambda i: (jnp.minimum(i, nct - 1), 0)),
                 pl.BlockSpec((TM, nal.shape[1]), lambda i: (jnp.maximum(i - nct, 0), 0)),
                 row(of), row(ob), row(hg), const(ng), const(w_out)]
        kern = functools.partial(_post_even_kernel, alpha=alpha, ctx_tiles=nct)
    else:
        hf, hb, y = mixer_inputs
        ins = [hf, hb, y, w_out]
        specs = [row(hf), row(hb), row(y), const(w_out)]
        kern = functools.partial(_post_odd_kernel, alpha=alpha)
    ins += [xs, m2, m3, m4, lng, lnb, rwh, rwl]
    specs += [row(xs), mod_spec, mod_spec, mod_spec, const(lng), const(lnb), const(rwh), const(rwl)]
    return pl.pallas_call(
        kern,
        grid=(T // TM,),
        in_specs=specs,
        out_specs=[pl.BlockSpec((TM, D), lambda i: (i, 0)), pl.BlockSpec((TM, D // 2), lambda i: (i, 0)),
                   pl.BlockSpec((E, TM), lambda i: (0, i))],
        out_shape=[jax.ShapeDtypeStruct((T, D), F32), jax.ShapeDtypeStruct((T, D // 2), jnp.uint32),
                   jax.ShapeDtypeStruct((E, T), F32)],
        compiler_params=_cparams(("parallel",)),
        name="post_mixer_even" if even else "post_mixer_odd",
    )(*ins)


def _route_kernel(lg_ref, bias_ref, tri_ref, idx_ref, rank_ref, w_ref, cnt_ref, carry_ref):
    i = pl.program_id(0)

    @pl.when(i == 0)
    def _():
        carry_ref[...] = jnp.zeros_like(carry_ref)

    E, TK = lg_ref.shape
    s = jax.nn.sigmoid(lg_ref[...])
    work = s + bias_ref[...]
    eio = lax.broadcasted_iota(I32, (E, TK), 0).astype(F32)
    picked = jnp.zeros((E, TK), F32)
    sels, idxs = [], []
    for _ in range(TOP_K):
        m = jnp.max(work, axis=0, keepdims=True)
        ik = jnp.min(jnp.where(work == m, eio, float(E)), axis=0, keepdims=True)
        oh = eio == ik
        sels.append(jnp.sum(jnp.where(oh, s, 0.0), axis=0, keepdims=True))
        idxs.append(ik)
        picked = jnp.where(oh, 1.0, picked)
        work = jnp.where(oh, -jnp.inf, work)
    total = sels[0]
    for sk in sels[1:]:
        total = total + sk
    carry = carry_ref[:, 0:1]
    ranks = carry + _dot(picked.astype(BF16), tri_ref[...])
    out_rows = idx_ref.shape[0]
    rio = lax.broadcasted_iota(I32, (out_rows, TK), 0)
    idx_o = jnp.zeros((out_rows, TK), I32)
    rank_o = jnp.zeros((out_rows, TK), I32)
    w_o = jnp.zeros((out_rows, TK), F32)
    for kx in range(TOP_K):
        rk = jnp.sum(jnp.where(eio == idxs[kx], ranks, 0.0), axis=0, keepdims=True).astype(I32)
        idx_o = jnp.where(rio == kx, idxs[kx].astype(I32), idx_o)
        rank_o = jnp.where(rio == kx, rk, rank_o)
        w_o = jnp.where(rio == kx, sels[kx] / total * ROUTED_SCALE, w_o)
    idx_ref[...] = idx_o
    rank_ref[...] = rank_o
    w_ref[...] = w_o
    new_carry = carry + jnp.sum(picked, axis=1, keepdims=True)
    carry_ref[...] = jnp.broadcast_to(new_carry, carry_ref.shape)
    cnt_ref[...] = jnp.broadcast_to(new_carry, cnt_ref.shape)


def _route(logits_t, router_bias, t0):
    E, T = logits_t.shape
    Tm = T - t0
    TK = _pick_tile(int(np.gcd(Tm, t0)), (1024, 512, 256))
    off = t0 // TK
    tri = jnp.asarray(np.triu(np.ones((TK, TK), np.float32), 1), BF16)
    tok = lambda: pl.BlockSpec((8, TK), lambda i: (0, i))
    idx, rank, w, cnt = pl.pallas_call(
        _route_kernel,
        grid=(Tm // TK,),
        in_specs=[pl.BlockSpec((E, TK), lambda i: (0, i + off)),
                  pl.BlockSpec((E, 1), lambda i: (0, 0)),
                  pl.BlockSpec((TK, TK), lambda i: (0, 0))],
        out_specs=[tok(), tok(), tok(), pl.BlockSpec((E, 128), lambda i: (0, 0))],
        out_shape=[jax.ShapeDtypeStruct((8, Tm), I32), jax.ShapeDtypeStruct((8, Tm), I32),
                   jax.ShapeDtypeStruct((8, Tm), F32), jax.ShapeDtypeStruct((E, 128), F32)],
        scratch_shapes=[pltpu.VMEM((E, 128), F32)],
        compiler_params=_cparams(("arbitrary",)),
        name="moe_route",
    )(logits_t, router_bias.reshape(E, 1), tri)
    return idx, rank, w, cnt[:, 0].astype(I32)


def _sorted_tile_rows(n_exp):
    return -(-(MOE_TILE * TOP_K + n_exp * (ROW_ALIGN - 1)) // 16) * 16


def _run_copies(cnt_ref, src_ref, dst_ref, base, n_exp, chunk, make_copy):
    shift = chunk.bit_length() - 1

    def expert_body(e, total):
        n = cnt_ref[base + e]
        src = src_ref[base + e]
        dst = dst_ref[base + e]
        nch = lax.shift_right_logical(n + (chunk - 1), shift)

        def chunk_body(c, carry):
            make_copy(pl.multiple_of(src + c * chunk, ROW_ALIGN), pl.multiple_of(dst + c * chunk, ROW_ALIGN)).start()
            return carry

        lax.fori_loop(0, nch, chunk_body, 0)
        return total + nch

    return lax.fori_loop(0, n_exp, expert_body, jnp.int32(0))


def _dispatch_kernel(cnt_ref, ls_ref, gb_ref, pos_ref, h_ref, xs_hbm, stage, pending, sem, *, n_exp):
    i = pl.program_id(0)
    slot = i % 2
    TMd = h_ref.shape[0]
    R = stage.shape[1] - DISP_CHUNK

    @pl.when(i == 0)
    def _():
        pending[0] = 0
        for s in range(2):
            stage[s, R:, :] = jnp.zeros((stage.shape[1] - R, stage.shape[2]), stage.dtype)

    rio = lax.broadcasted_iota(I32, (R, TMd), 0)
    perm = jnp.zeros((R, TMd), F32)
    for kx in range(TOP_K):
        perm = jnp.where(rio == pos_ref[kx:kx + 1, :], 1.0, perm)
    stage[slot, 0:R, :] = _pack_rows(_dot(perm.astype(BF16), _unpack_rows(h_ref[...])))

    def make_copy(src, dst):
        return pltpu.make_async_copy(stage.at[slot, pl.ds(src, DISP_CHUNK)], xs_hbm.at[pl.ds(dst, DISP_CHUNK)], sem)

    def wait_copies(n):
        def wait_body(c, carry):
            make_copy(0, 0).wait()
            return carry
        lax.fori_loop(0, n, wait_body, 0)

    wait_copies(pending[0])
    pending[0] = _run_copies(cnt_ref, ls_ref, gb_ref, i * n_exp, n_exp, DISP_CHUNK, make_copy)

    @pl.when(i == pl.num_programs(0) - 1)
    def _():
        wait_copies(pending[0])


def _dispatch(h2, plan, t0, n_slots, n_exp):
    cnt, ls, gb, pos = plan
    T, DP = h2.shape
    TMd = MOE_TILE
    Tm = T - t0
    assert Tm % TMd == 0 and t0 % TMd == 0
    off = t0 // TMd
    grid_spec = pltpu.PrefetchScalarGridSpec(
        num_scalar_prefetch=3,
        grid=(Tm // TMd,),
        in_specs=[pl.BlockSpec((8, TMd), lambda i, *_: (0, i)),
                  pl.BlockSpec((TMd, DP), lambda i, *_: (i + off, 0))],
        out_specs=pl.BlockSpec(memory_space=pl.ANY),
        scratch_shapes=[pltpu.VMEM((2, _sorted_tile_rows(n_exp) + DISP_CHUNK, DP), h2.dtype),
                        pltpu.SMEM((1,), I32), pltpu.SemaphoreType.DMA],
    )
    return pl.pallas_call(
        functools.partial(_dispatch_kernel, n_exp=n_exp),
        grid_spec=grid_spec,
        out_shape=jax.ShapeDtypeStruct((n_slots, DP), h2.dtype),
        compiler_params=_cparams(("arbitrary",)),
        name="moe_dispatch",
    )(cnt, ls, gb, pos, h2)


def _gmm_kernel(bexp_ref, bvalid_ref, x_ref, wg_ref, wu_ref, wd_ref, y_ref):
    del bexp_ref
    i = pl.program_id(0)
    nvalid = bvalid_ref[i]

    @pl.when(nvalid > 0)
    def _():
        row = lax.broadcasted_iota(I32, x_ref.shape, 0)
        xb = _unpack_rows(jnp.where(row < nvalid, x_ref[...], jnp.uint32(0)))
        act = _silu(_dot(xb, wg_ref[...].astype(BF16))) * _dot(xb, wu_ref[...].astype(BF16))
        y_ref[...] = _pack_rows(_dot(act.astype(BF16), wd_ref[...].astype(BF16)))

    @pl.when(nvalid == 0)
    def _():
        y_ref[...] = jnp.zeros_like(y_ref)


def _grouped_experts(xsorted, block_exp, block_valid, wg, wu, wd, layer):
    NS, DP = xsorted.shape
    nb = NS // MOE_BLOCK
    _, _, D, hid = wg.shape
    grid_spec = pltpu.PrefetchScalarGridSpec(
        num_scalar_prefetch=2,
        grid=(nb,),
        in_specs=[pl.BlockSpec((MOE_BLOCK, DP), lambda i, be, bv: (i, 0)),
                  pl.BlockSpec((None, None, D, hid), lambda i, be, bv: (layer, be[i], 0, 0)),
                  pl.BlockSpec((None, None, D, hid), lambda i, be, bv: (layer, be[i], 0, 0)),
                  pl.BlockSpec((None, None, hid, D), lambda i, be, bv: (layer, be[i], 0, 0))],
        out_specs=pl.BlockSpec((MOE_BLOCK, DP), lambda i, be, bv: (i, 0)),
    )
    return pl.pallas_call(
        _gmm_kernel,
        grid_spec=grid_spec,
        out_shape=jax.ShapeDtypeStruct((NS, DP), jnp.uint32),
        compiler_params=_cparams(("arbitrary",)),
        name="moe_grouped_experts",
    )(block_exp, block_valid, xsorted, wg, wu, wd)


def _combine_kernel(cnt_ref, so_ref, gb_ref, y_hbm, qpos_ref, w_ref, h_ref, sgu_ref, sd_ref, x_ref, m5_ref,
                    lng_ref, lnb_ref, o_ref, stage, pending, sems, *, alpha, n_exp):
    i = pl.program_id(0)
    slot = i % 2

    def make_copy(s):
        def f(dst, src):
            return pltpu.make_async_copy(y_hbm.at[pl.ds(src, COMB_CHUNK)], stage.at[s, pl.ds(dst, COMB_CHUNK)],
                                         sems.at[s])
        return f

    def fetch(tile, s):
        pending[s] = _run_copies(cnt_ref, so_ref, gb_ref, tile * n_exp, n_exp, COMB_CHUNK, make_copy(s))

    @pl.when(i == 0)
    def _():
        stage[...] = jnp.zeros_like(stage)
        fetch(0, 0)

    @pl.when(i + 1 < pl.num_programs(0))
    def _():
        fetch(i + 1, 1 - slot)

    hid = sd_ref.shape[0]
    gu = _dot(_unpack_rows(h_ref[...]), sgu_ref[...])
    y = _dot((_silu(gu[:, :hid]) * gu[:, hid:]).astype(BF16), sd_ref[...])

    def wait_body(c, carry):
        make_copy(slot)(0, 0).wait()
        return carry

    lax.fori_loop(0, pending[slot], wait_body, 0)

    TMc, S = h_ref.shape[0], stage.shape[1]
    sio = lax.broadcasted_iota(I32, (TMc, S), 1)
    q = jnp.zeros((TMc, S), F32)
    for kx in range(TOP_K):
        q = jnp.where(sio == qpos_ref[:, kx:kx + 1], w_ref[:, kx:kx + 1], q)
    y = y + _dot(q.astype(BF16), _unpack_rows(stage[slot]))
    z = alpha * x_ref[...] + m5_ref[...] * y
    o_ref[...] = _layer_norm(z, lng_ref[...], lnb_ref[...])


def _combine(dm, plan, ysorted, qpos_tm, w_tm, h2, sgu, sd, xs, m5, lng, lnb, alpha, t0, n_exp):
    cnt, so, gb = plan
    T, D = xs.shape
    TMc = MOE_TILE
    Tm = T - t0
    off = t0 // TMc
    const = lambda a: pl.BlockSpec(a.shape, lambda i, *_: (0,) * a.ndim)
    row = lambda a: pl.BlockSpec((TMc, a.shape[1]), lambda i, *_: (i + off, 0))
    grid_spec = pltpu.PrefetchScalarGridSpec(
        num_scalar_prefetch=3,
        grid=(Tm // TMc,),
        in_specs=[pl.BlockSpec(memory_space=pl.ANY),
                  pl.BlockSpec((TMc, 8), lambda i, *_: (i, 0)), pl.BlockSpec((TMc, 8), lambda i, *_: (i, 0)),
                  row(h2), const(sgu), const(sd), row(xs),
                  pl.BlockSpec((None, 1, D), lambda i, *_: (dm.group(i + off, TMc), 0, 0)),
                  const(lng), const(lnb)],
        out_specs=pl.BlockSpec((TMc, D), lambda i, *_: (i, 0)),
        scratch_shapes=[pltpu.VMEM((2, _sorted_tile_rows(n_exp), ysorted.shape[1]), ysorted.dtype),
                        pltpu.SMEM((2,), I32), pltpu.SemaphoreType.DMA((2,))],
    )
    return pl.pallas_call(
        functools.partial(_combine_kernel, alpha=alpha, n_exp=n_exp),
        grid_spec=grid_spec,
        out_shape=jax.ShapeDtypeStruct((Tm, D), F32),
        compiler_params=_cparams(("arbitrary",)),
        name="moe_combine",
    )(cnt, so, gb, ysorted, qpos_tm, w_tm, h2, sgu, sd, xs, m5, lng, lnb)


def _moe(dm, h2, logits_t, xs, m5, lng, lnb, router_bias, wg, wu, wd, layer, sgu, sd, alpha, t0):
    T = h2.shape[0]
    E = wg.shape[1]
    Tm = T - t0
    nt = Tm // MOE_TILE
    idx, rank, w, _ = _route(logits_t, router_bias, t0)
    idx, rank, w = idx[:TOP_K], rank[:TOP_K], w[:TOP_K]
    eids = jnp.arange(E, dtype=I32)
    onehot = (idx[:, :, None] == eids).reshape(TOP_K, nt, MOE_TILE, E)
    cnt = jnp.sum(onehot, axis=(0, 2), dtype=I32)
    rank_base = jnp.cumsum(cnt, axis=0) - cnt
    run = (cnt + ROW_ALIGN - 1) // ROW_ALIGN * ROW_ALIGN
    tile_base = jnp.cumsum(run, axis=0) - run
    local_start = jnp.cumsum(run, axis=1) - run
    counts = jnp.sum(run, axis=0)
    nb = -(-(Tm * TOP_K + E * (nt * (ROW_ALIGN - 1) + MOE_SLACK)) // MOE_BLOCK) + E
    padded = (counts + MOE_SLACK + MOE_BLOCK - 1) // MOE_BLOCK * MOE_BLOCK
    pends = jnp.cumsum(padded)
    pstarts = pends - padded
    global_base = pstarts[None, :] + tile_base
    bstart = jnp.arange(nb, dtype=I32) * MOE_BLOCK
    block_exp = jnp.minimum(jnp.sum((pends[None, :] <= bstart[:, None]).astype(I32), axis=1), E - 1)
    of_block = block_exp[:, None] == eids[None, :]
    seg_end = jnp.sum(jnp.where(of_block, (pstarts + counts)[None, :], 0), axis=1)
    block_valid = jnp.clip(seg_end - bstart, 0, MOE_BLOCK).astype(I32)
    look = lambda tab: jnp.sum(jnp.where(onehot, tab[None, :, None, :], 0), axis=3).reshape(TOP_K, Tm)
    pos = rank - look(rank_base) + look(local_start)
    pos8 = jnp.pad(pos, ((0, 8 - TOP_K), (0, 0)), constant_values=-1).astype(I32)
    pos_tm = jnp.pad(pos.T, ((0, 0), (0, 8 - TOP_K)), constant_values=-1).astype(I32)
    w_tm = jnp.pad(w.T, ((0, 0), (0, 8 - TOP_K)))
    flat = lambda a: a.reshape(-1).astype(I32)
    plan = (flat(run), flat(local_start), flat(global_base))
    xsorted = _dispatch(h2, plan + (pos8,), t0, nb * MOE_BLOCK, E)
    ysorted = _grouped_experts(xsorted, block_exp, block_valid, wg, wu, wd, layer)
    return _combine(dm, plan, ysorted, pos_tm, w_tm, h2, sgu, sd, xs, m5, lng, lnb, alpha, t0, E)


def kernel(x, c, ctx, c_ctx, ada_w, ada_b, ln_g, ln_b, ev_w_in, ev_w_out, na_rpb, hg_lb_raw, hg_norm_g, od_w_in, od_conv_w, od_conv_b, lru_w_a, lru_b_a, lru_w_x, lru_b_x, lru_lam, od_w_out, router_w, router_bias, exp_w_gate, exp_w_up, exp_w_down, sh_w_gate, sh_w_up, sh_w_down):
    B, N, D = x.shape
    C = ctx.shape[1]
    depth = ada_w.shape[0]
    dm = _Dims(B, N, C, D)
    alpha = float((2 * depth) ** 0.25)
    assert B + 1 <= 8

    cvec = jnp.zeros((8, D), F32).at[:B].set(c).at[B].set(c_ctx)
    mod = _modulation(cvec, ada_w, ada_b).reshape(depth, 8, 6, 1, D)

    p_lb = jax.nn.softmax(hg_lb_raw.astype(F32), axis=1)
    hg_lb = jnp.cumsum(p_lb, axis=1) - p_lb[:, :1]

    xs = jnp.concatenate([ctx.reshape(B * C, D), x.reshape(B * N, D)], axis=0)
    for l in range(depth):
        jl = l // 2
        last = l == depth - 1
        m = [mod[l, :, t] for t in range(6)]
        rw_t = router_w[l].T
        rwh, rwl = _split(rw_t)
        lng = ln_g[l][:, None, :]
        lnb = ln_b[l][:, None, :]
        if l % 2 == 0:
            q, k, v, hq, ff, fb, hi, hg = _inproj(dm, xs, m[1], m[0], ev_w_in[jl].astype(BF16), True)
            bias = _na_bias_tables(na_rpb[jl], N // GRID_W)
            na_lat = _neighbourhood_attention(dm, q, k, v, bias)
            na_ctx = _context_attention(dm, q, k, v)
            o_f, o_b = _hgrn2(dm, hq, ff, fb, hi, hg_lb[:, jl])
            ng = jnp.tile(hg_norm_g[jl], hq.shape[1] // HEAD_DIM)[None, :]
            mixer_inputs = (na_ctx, na_lat, o_f, o_b, hg, ng)
            w_out = ev_w_out[jl].astype(BF16)
        else:
            y, u = _inproj(dm, xs, m[1], m[0], od_w_in[jl].astype(BF16), False)
            h_f, h_b = _rglru(dm, u, od_conv_w[jl], od_conv_b[jl], lru_w_a[jl], lru_b_a[jl],
                              lru_w_x[jl], lru_b_x[jl], lru_lam[jl])
            mixer_inputs = (h_f, h_b, y)
            w_out = od_w_out[jl].astype(BF16)
        xs, h2, logits_t = _post_mixer(dm, mixer_inputs, w_out, xs, m[2], m[3], m[4], lng[0], lnb[0],
                                       rwh, rwl, alpha, l % 2 == 0)
        sgu = jnp.concatenate([sh_w_gate[l], sh_w_up[l]], axis=-1).astype(BF16)
        sd = sh_w_down[l].astype(BF16)
        t0 = dm.T_ctx if last else 0
        xs = _moe(dm, h2, logits_t, xs, m[5], lng[1], lnb[1], router_bias[l], exp_w_gate, exp_w_up,
                  exp_w_down, l, sgu, sd, alpha, t0)
    return xs.reshape(B, N, D)
```

```python
import functools

import numpy as np
import jax
import jax.numpy as jnp
from jax import lax
from jax.experimental import pallas as pl
from jax.experimental.pallas import tpu as pltpu

F32 = jnp.float32
BF16 = jnp.bfloat16
I32 = jnp.int32

HEAD_DIM = 64
GRID_W = 64
NA_WIN_ROWS = 8
NA_WIN_COLS = 16
NA_QROWS = 8
NA_KROWS = 16
HG_CHUNK = 64
HG_HEADS_PER_GROUP = 4
FORGET_FLOOR = 1e-30
HG_SAFE_EXP = 80.0
HG_CLIP_EXP = 85.0
LRU_C = 8.0
LRU_BLOCKS = 16
CONV_W = 4
TOP_K = 6
ROUTED_SCALE = 2.5
LN_EPS = 1e-5
RMS_EPS = 1e-6
SEQ_BLOCK = 256
MOE_BLOCK = 512
MOE_TILE = 256
ROW_ALIGN = 8
DISP_CHUNK = 32
COMB_CHUNK = ROW_ALIGN
MOE_SLACK = DISP_CHUNK
MASK_VALUE = -1e30
VMEM_LIMIT = 56 * 1024 * 1024


def _cparams(sem):
    return pltpu.CompilerParams(dimension_semantics=sem, vmem_limit_bytes=VMEM_LIMIT)


def _split(a):
    hi = a.astype(BF16)
    lo = (a - hi.astype(F32)).astype(BF16)
    return hi, lo


def _dot(a, b):
    return jnp.dot(a, b, preferred_element_type=F32)


def _dot_nt(a, b):
    return lax.dot_general(a, b, (((1,), (1,)), ((), ())), preferred_element_type=F32)


def _dot_tn(a, b):
    return lax.dot_general(a, b, (((0,), (0,)), ((), ())), preferred_element_type=F32)


def _dot3(a, b):
    ah, al = _split(a)
    bh, bl = _split(b)
    return _dot(ah, bh) + _dot(al, bh) + _dot(ah, bl)


def _pack_rows(h):
    half = h.shape[1] // 2
    bits = pltpu.bitcast(h.astype(BF16).astype(F32), jnp.uint32)
    return (bits[:, :half] >> 16) | (bits[:, half:] & jnp.uint32(0xFFFF0000))


def _unpack_rows(p):
    lo = pltpu.bitcast(p << 16, F32)
    hi = pltpu.bitcast(p & jnp.uint32(0xFFFF0000), F32)
    return jnp.concatenate([lo, hi], axis=1).astype(BF16)


def _silu(v):
    return v * jax.nn.sigmoid(v)


def _gelu_tanh(v):
    return 0.5 * v * (1.0 + jnp.tanh(0.7978845608028654 * (v + 0.044715 * v * v * v)))


def _layer_norm(z, g, b):
    mu = jnp.mean(z, axis=-1, keepdims=True)
    zc = z - mu
    var = jnp.mean(zc * zc, axis=-1, keepdims=True)
    return zc * lax.rsqrt(var + LN_EPS) * g + b


def _pick_tile(n, cands):
    for c in cands:
        if n % c == 0:
            return c
    raise ValueError(f"no tile for {n}")


class _Dims:
    def __init__(self, B, N, C, D):
        self.B, self.N, self.C, self.D = B, N, C, D
        self.T_ctx = B * C
        self.T = B * C + B * N
        assert C == SEQ_BLOCK and N % SEQ_BLOCK == 0
        assert N % GRID_W == 0 and (N // GRID_W) % NA_QROWS == 0 and N // GRID_W >= NA_KROWS
        assert self.T_ctx % 512 == 0
        self.TM = _pick_tile(self.T_ctx, (512, 256))
        assert N % self.TM == 0

    def group(self, i, tile):
        start = i * tile
        return jnp.where(start < self.T_ctx, self.B, (start - self.T_ctx) // self.N)


def _mod_kernel(c_ref, w_ref, b_ref, o_ref):
    o_ref[...] = _dot3(_silu(c_ref[...]), w_ref[...]) + b_ref[...]


def _modulation(cvec, ada_w, ada_b):
    L, D, W6 = ada_w.shape
    nc = 1536
    return pl.pallas_call(
        _mod_kernel,
        grid=(L, W6 // nc),
        in_specs=[
            pl.BlockSpec((8, D), lambda l, j: (0, 0)),
            pl.BlockSpec((None, D, nc), lambda l, j: (l, 0, j)),
            pl.BlockSpec((None, 1, nc), lambda l, j: (l, 0, j)),
        ],
        out_specs=pl.BlockSpec((None, 8, nc), lambda l, j: (l, 0, j)),
        out_shape=jax.ShapeDtypeStruct((L, 8, W6), F32),
        compiler_params=_cparams(("arbitrary", "arbitrary")),
        name="adaln_modulation",
    )(cvec, ada_w, ada_b.reshape(L, 1, W6))


def _inproj_even_kernel(x_ref, sc_ref, sh_ref, w_ref, q_ref, k_ref, v_ref, hq_ref, ff_ref, fb_ref, hi_ref, hg_ref):
    h = (x_ref[...] * (1.0 + sc_ref[...]) + sh_ref[...]).astype(BF16)
    wd = q_ref.shape[1]

    def part(j):
        return _dot(h, w_ref[:, j * wd:(j + 1) * wd])

    scale = HEAD_DIM ** -0.5
    q_ref[...] = (part(0) * scale).astype(BF16)
    k_ref[...] = part(1).astype(BF16)
    v_ref[...] = part(2).astype(BF16)
    hq_ref[...] = _silu(part(3)) * scale
    ff_ref[...] = part(4)
    fb_ref[...] = part(5)
    hi_ref[...] = part(6)
    hg_ref[...] = part(7)


def _inproj_odd_kernel(x_ref, sc_ref, sh_ref, w_ref, y_ref, u_ref):
    h = (x_ref[...] * (1.0 + sc_ref[...]) + sh_ref[...]).astype(BF16)
    wd = y_ref.shape[1]
    y_ref[...] = _dot(h, w_ref[:, :wd])
    u_ref[...] = _dot(h, w_ref[:, wd:])


def _inproj(dm, xs, sc, sh, w, even):
    T, D, TM = dm.T, dm.D, dm.TM
    wtot = w.shape[1]
    mod_spec = pl.BlockSpec((None, 1, D), lambda i: (dm.group(i, TM), 0, 0))
    in_specs = [pl.BlockSpec((TM, D), lambda i: (i, 0)), mod_spec, mod_spec,
                pl.BlockSpec((D, wtot), lambda i: (0, 0))]
    if even:
        wd = wtot // 8
        dts = [BF16] * 3 + [F32] * 5
        kern = _inproj_even_kernel
    else:
        wd = wtot // 2
        dts = [F32] * 2
        kern = _inproj_odd_kernel
    return pl.pallas_call(
        kern,
        grid=(T // TM,),
        in_specs=in_specs,
        out_specs=[pl.BlockSpec((TM, wd), lambda i: (i, 0)) for _ in dts],
        out_shape=[jax.ShapeDtypeStruct((T, wd), dt) for dt in dts],
        compiler_params=_cparams(("parallel",)),
        name="inproj_even" if even else "inproj_odd",
    )(xs, sc, sh, w)


def _na_bias_tables(rpb, rows):
    W = GRID_W
    H = rpb.shape[0]
    nr, nc = 2 * NA_WIN_ROWS - 1, 2 * NA_WIN_COLS - 1
    hp = lax.Precision.HIGHEST
    c = np.arange(W)[:, None]
    kc = np.arange(W)[None, :]
    c0 = np.clip(c - NA_WIN_COLS // 2, 0, W - NA_WIN_COLS)
    valid_c = (kc >= c0) & (kc < c0 + NA_WIN_COLS)
    dc = np.clip(kc - c + NA_WIN_COLS - 1, 0, nc - 1)
    oh_c = (dc.reshape(-1, 1) == np.arange(nc)[None, :]).astype(np.float32)
    by_col = jnp.einsum('hrs,xs->hrx', rpb.astype(F32), jnp.asarray(oh_c), precision=hp)
    tabs = []
    for rbase, kb in ((0, 0), (NA_QROWS, NA_QROWS - NA_WIN_ROWS // 2), (rows - NA_QROWS, rows - NA_KROWS)):
        r = rbase + np.arange(NA_QROWS)[:, None]
        kr = kb + np.arange(NA_KROWS)[None, :]
        r0 = np.clip(r - NA_WIN_ROWS // 2, 0, rows - NA_WIN_ROWS)
        valid_r = (kr >= r0) & (kr < r0 + NA_WIN_ROWS)
        dr = np.clip(kr - r + NA_WIN_ROWS - 1, 0, nr - 1)
        oh_r = (dr.reshape(-1, 1) == np.arange(nr)[None, :]).astype(np.float32)
        t = jnp.einsum('yr,hrx->hyx', jnp.asarray(oh_r), by_col, precision=hp)
        t = t.reshape(H, NA_QROWS, NA_KROWS, W, W).transpose(0, 1, 3, 2, 4)
        valid = valid_r[:, None, :, None] & valid_c[None, :, None, :]
        t = jnp.where(jnp.asarray(valid)[None], t, MASK_VALUE)
        tabs.append(t.reshape(H, NA_QROWS * W, NA_KROWS * W))
    return jnp.stack(tabs).astype(F32)


def _na_kernel(q_ref, k0, k1, k2, k3, v0, v1, v2, v3, kc_ref, vc_ref, bias_ref, o_ref):
    q2 = q_ref[...]
    lane = lax.broadcasted_iota(I32, (1, q2.shape[1]), 1)
    first = lane < HEAD_DIM
    ks = [r[...] for r in (k0, k1, k2, k3)]
    vs = [r[...] for r in (v0, v1, v2, v3)]
    kc = kc_ref[...]
    vc = vc_ref[...]
    kb = ks[0].shape[0]
    outs = []
    for hh in range(2):
        sel = first if hh == 0 else jnp.logical_not(first)
        qh = jnp.where(sel, q2, jnp.zeros_like(q2))
        s_loc = jnp.concatenate([_dot_nt(qh, kj) for kj in ks], axis=1) + bias_ref[hh]
        s_ctx = _dot_nt(qh, kc)
        m = jnp.maximum(jnp.max(s_loc, axis=1, keepdims=True), jnp.max(s_ctx, axis=1, keepdims=True))
        p_loc = jnp.exp(s_loc - m)
        p_ctx = jnp.exp(s_ctx - m)
        denom = jnp.sum(p_loc, axis=1, keepdims=True) + jnp.sum(p_ctx, axis=1, keepdims=True)
        o = _dot(p_ctx.astype(BF16), vc)
        for j in range(4):
            o = o + _dot(p_loc[:, j * kb:(j + 1) * kb].astype(BF16), vs[j])
        outs.append(o / denom)
    o_ref[...] = jnp.where(first, outs[0], outs[1]).astype(o_ref.dtype)


def _neighbourhood_attention(dm, q, k, v, bias):
    B, N, C, T = dm.B, dm.N, dm.C, dm.T
    H2 = q.shape[1] // (2 * HEAD_DIM)
    rows = N // GRID_W
    nqb = rows // NA_QROWS
    QB = NA_QROWS * GRID_W
    KB = QB // 2
    ngroups = N // KB
    lat_q0 = dm.T_ctx // QB
    lat_k0 = dm.T_ctx // KB

    def q_map(p, i, b):
        return (lat_q0 + b * (N // QB) + i, p)

    def o_map(p, i, b):
        return (b * (N // QB) + i, p)

    def kv_map(j):
        def f(p, i, b):
            gs = jnp.clip(2 * i - 1, 0, ngroups - 4)
            return (lat_k0 + b * ngroups + gs + j, p)
        return f

    def ctx_map(p, i, b):
        return (b, p)

    def bias_map(p, i, b):
        var = jnp.where(i == 0, 0, jnp.where(i == nqb - 1, 2, 1))
        return (var, p, 0, 0)

    lanes = 2 * HEAD_DIM
    kv_specs = [pl.BlockSpec((KB, lanes), kv_map(j)) for j in range(4)]
    return pl.pallas_call(
        _na_kernel,
        grid=(H2, nqb, B),
        in_specs=[pl.BlockSpec((QB, lanes), q_map)] + kv_specs + kv_specs
        + [pl.BlockSpec((C, lanes), ctx_map), pl.BlockSpec((C, lanes), ctx_map),
           pl.BlockSpec((None, 2, QB, NA_KROWS * GRID_W), bias_map)],
        out_specs=pl.BlockSpec((QB, lanes), o_map),
        out_shape=jax.ShapeDtypeStruct((B * N, q.shape[1]), BF16),
        compiler_params=_cparams(("arbitrary", "arbitrary", "arbitrary")),
        name="neighbourhood_attention",
    )(q, k, k, k, k, v, v, v, v, k, v, bias)


def _ctx_attn_kernel(q_ref, k_ref, v_ref, o_ref):
    q2 = q_ref[...]
    k2 = k_ref[...]
    v2 = v_ref[...]
    lane = lax.broadcasted_iota(I32, (1, q2.shape[1]), 1)
    first = lane < HEAD_DIM
    outs = []
    for hh in range(2):
        sel = first if hh == 0 else jnp.logical_not(first)
        s = _dot_nt(jnp.where(sel, q2, jnp.zeros_like(q2)), k2)
        p = jnp.exp(s - jnp.max(s, axis=1, keepdims=True))
        outs.append(_dot(p.astype(BF16), v2) / jnp.sum(p, axis=1, keepdims=True))
    o_ref[...] = jnp.where(first, outs[0], outs[1]).astype(o_ref.dtype)


def _context_attention(dm, q, k, v):
    lanes = 2 * HEAD_DIM
    spec = pl.BlockSpec((dm.C, lanes), lambda p, b: (b, p))
    return pl.pallas_call(
        _ctx_attn_kernel,
        grid=(q.shape[1] // lanes, dm.B),
        in_specs=[spec, spec, spec],
        out_specs=spec,
        out_shape=jax.ShapeDtypeStruct((dm.T_ctx, q.shape[1]), BF16),
        compiler_params=_cparams(("arbitrary", "arbitrary")),
        name="context_attention",
    )(q, k, v)


def _seq_block_maps(dm):
    nl = dm.N // SEQ_BLOCK
    base = dm.T_ctx // SEQ_BLOCK

    def fwd(b, j):
        return jnp.where(j == 0, b, base + b * nl + j - 1)

    def bwd(b, j):
        return jnp.where(j == 0, b, base + b * nl + nl - j)

    return fwd, bwd


def _hgrn_prepare(hq, z, lb, tri, last_row, mid_row):
    f = lb + (1.0 - lb) * jax.nn.sigmoid(z)
    fm = jnp.maximum(f, FORGET_FLOOR)
    g = jnp.log(fm)
    kk = 1.0 - fm
    gh, gl = _split(g)
    cum = _dot(tri, gh) + _dot(tri, gl)
    tail = cum[last_row:last_row + 1, :]
    e = cum - cum[mid_row:mid_row + 1, :]
    emax = jnp.max(jnp.max(jnp.abs(e), axis=1, keepdims=True), axis=0, keepdims=True)
    ec = jnp.clip(e, -HG_CLIP_EXP, HG_CLIP_EXP)
    qe = (hq * jnp.exp(ec)).astype(BF16)
    ke = (kk * jnp.exp(-ec)).astype(BF16)
    return kk, cum, tail, emax, qe, ke


def _hgrn_kernel(hqf_ref, zf_ref, vf_ref, hqb_ref, zb_ref, vb_ref, lbf_ref, lbb_ref,
                 of_ref, ob_ref, st_ref, a_ref, cum_ref, k_ref, qs_ref, k2_ref, et_ref):
    j = pl.program_id(1)

    @pl.when(j == 0)
    def _():
        st_ref[...] = jnp.zeros_like(st_ref)

    CH = HG_CHUNK
    GW = HG_HEADS_PER_GROUP * HEAD_DIM
    ngroups = hqf_ref.shape[1] // GW
    nchunks = hqf_ref.shape[0] // CH
    t_i = lax.broadcasted_iota(I32, (CH, CH), 0)
    u_i = lax.broadcasted_iota(I32, (CH, CH), 1)
    tris = ((u_i <= t_i).astype(BF16), (u_i >= t_i).astype(BF16))
    t_w = lax.broadcasted_iota(I32, (CH, GW), 0)
    s_w = lax.broadcasted_iota(I32, (CH, GW), 1) % CH
    cmasks = (s_w <= t_w, s_w >= t_w)
    r_b = lax.broadcasted_iota(I32, (GW, GW), 0)
    c_b = lax.broadcasted_iota(I32, (GW, GW), 1)
    bm = (r_b // HEAD_DIM) == (c_b // HEAD_DIM)
    hsel_base = (r_b // HEAD_DIM) * HEAD_DIM
    dirs = ((hqf_ref, zf_ref, vf_ref, lbf_ref, of_ref), (hqb_ref, zb_ref, vb_ref, lbb_ref, ob_ref))
    chains = [(d, g) for d in range(2) for g in range(ngroups)]

    def expand(m):
        return jnp.where(bm, jnp.concatenate([m] * HG_HEADS_PER_GROUP, axis=0), jnp.zeros((GW, GW), m.dtype))

    nchain = len(chains)

    def rows_of(c):
        return pl.multiple_of(c * CH, CH), pl.multiple_of((nchunks - 1 - c) * CH, CH)

    def prep_body(c, worst):
        rows = rows_of(c)
        for n, (d, g) in enumerate(chains):
            hq_r, z_r, _, lb_r, _ = dirs[d]
            ls = slice(g * GW, (g + 1) * GW)
            hq = hq_r[pl.ds(rows[d], CH), ls]
            z = z_r[pl.ds(rows[d], CH), ls]
            last_row = CH - 1 if d == 0 else 0
            kk, cum, tail, emax, qe, ke = _hgrn_prepare(hq, z, lb_r[:, ls], tris[d], last_row, CH // 2)
            m = c * nchain + n
            a_ref[m] = _dot_nt(qe, expand(ke))
            cum_ref[m] = cum
            k_ref[m] = kk
            qs_ref[m] = hq * jnp.exp(cum)
            k2_ref[m] = kk * jnp.exp(tail - cum)
            et_ref[m] = jnp.broadcast_to(jnp.exp(tail), et_ref.shape[1:])
            worst = jnp.maximum(worst, emax)
        return worst

    worst = lax.fori_loop(0, nchunks, prep_body, jnp.zeros((1, 1), F32), unroll=True)

    @pl.when(worst[0, 0] > HG_SAFE_EXP)
    def _():
        def slow_body(c, carry):
            rows = rows_of(c)
            for n, (d, g) in enumerate(chains):
                m = c * nchain + n
                hq = dirs[d][0][pl.ds(rows[d], CH), g * GW:(g + 1) * GW]
                cum = cum_ref[m]

                def key_body(s, acc):
                    cs = cum_ref[m, pl.ds(s, 1), :]
                    ksr = k_ref[m, pl.ds(s, 1), :]
                    p = hq * ksr * jnp.exp(jnp.minimum(cum - cs, 0.0))
                    hsel = jnp.where(c_b == hsel_base + s, 1.0, 0.0).astype(BF16)
                    return acc + _dot(p.astype(BF16), hsel)

                a_ref[m] = lax.fori_loop(0, CH, key_body, jnp.zeros((CH, GW), F32))
            return carry

        lax.fori_loop(0, nchunks, slow_body, 0)

    def state_body(c, carry):
        rows = rows_of(c)
        for n, (d, g) in enumerate(chains):
            m = c * nchain + n
            ls = slice(g * GW, (g + 1) * GW)
            vb = dirs[d][2][pl.ds(rows[d], CH), ls].astype(BF16)
            st = st_ref[n]
            a = jnp.where(cmasks[d], a_ref[m], 0.0).astype(BF16)
            o = _dot(a, expand(vb)) + _dot_nt(qs_ref[m].astype(BF16), st.astype(BF16))
            dirs[d][4][pl.ds(rows[d], CH), ls] = o
            st_ref[n] = et_ref[m, 0:1, :] * st + jnp.where(bm, _dot_tn(vb, k2_ref[m].astype(BF16)), 0.0)
        return carry

    lax.fori_loop(0, nchunks, state_body, 0, unroll=True)


def _hgrn2(dm, hq, ff, fb, hi, lb):
    B, T = dm.B, dm.T
    Wd = hq.shape[1]
    GW = HG_HEADS_PER_GROUP * HEAD_DIM
    nchain = 2 * (Wd // GW)
    nsteps = nchain * (SEQ_BLOCK // HG_CHUNK)
    nblk = 1 + dm.N // SEQ_BLOCK
    fwd, bwd = _seq_block_maps(dm)
    fspec = pl.BlockSpec((SEQ_BLOCK, Wd), lambda b, j: (fwd(b, j), 0))
    bspec = pl.BlockSpec((SEQ_BLOCK, Wd), lambda b, j: (bwd(b, j), 0))
    lbspec = pl.BlockSpec((1, Wd), lambda b, j: (0, 0))
    return pl.pallas_call(
        _hgrn_kernel,
        grid=(B, nblk),
        in_specs=[fspec, fspec, fspec, bspec, bspec, bspec, lbspec, lbspec],
        out_specs=[fspec, bspec],
        out_shape=[jax.ShapeDtypeStruct((T, Wd), F32)] * 2,
        scratch_shapes=[pltpu.VMEM((nchain, GW, GW), F32)]
        + [pltpu.VMEM((nsteps, HG_CHUNK, GW), F32)] * 5 + [pltpu.VMEM((nsteps, 8, GW), F32)],
        compiler_params=_cparams(("arbitrary", "arbitrary")),
        name="hgrn2_bidirectional",
    )(hq, ff, hi, hq, fb, hi, lb[0:1], lb[1:2])


LRU_GROUP = 8


def _lru_group_scan(a, x, reverse):
    n, w = a.shape
    a = a.reshape(n // LRU_GROUP, LRU_GROUP, w)
    x = x.reshape(n // LRU_GROUP, LRU_GROUP, w)
    row = lax.broadcasted_iota(I32, a.shape, 1)
    s = 1
    while s < LRU_GROUP:
        shift = (LRU_GROUP - s) if reverse else s
        a_sh = pltpu.roll(a, shift, 1)
        x_sh = pltpu.roll(x, shift, 1)
        valid = (row < LRU_GROUP - s) if reverse else (row >= s)
        x = jnp.where(valid, a * x_sh + x, x)
        a = jnp.where(valid, a * a_sh, a)
        s *= 2
    return a.reshape(n, w), x.reshape(n, w)


def _lru_kernel(ucf_ref, upf_ref, unf_ref, ucb_ref, upb_ref, unb_ref, cw_ref, cb_ref, wbd_ref,
                ba_ref, bx_ref, lam_ref, hf_ref, hb_ref, ext_ref, carry_ref, sa_ref, sx_ref):
    j = pl.program_id(1)
    nblk = pl.num_programs(1)

    @pl.when(j == 0)
    def _():
        carry_ref[...] = jnp.zeros_like(carry_ref)

    TT, W = ucf_ref.shape
    GW = wbd_ref.shape[2]
    halo = upf_ref.shape[0]
    dirs = ((ucf_ref, upf_ref, unf_ref, hf_ref), (ucb_ref, upb_ref, unb_ref, hb_ref))
    for d, (uc_r, up_r, un_r, out_r) in enumerate(dirs):
        pos = j if d == 0 else jnp.where(j == 0, 0, nblk - j)
        keep_prev = jnp.where((pos == 0) | (pos == 1), 0.0, 1.0)
        keep_next = jnp.where((pos == 0) | (pos == nblk - 1), 0.0, 1.0)
        ext_ref[d, 0:halo, :] = up_r[...] * keep_prev
        ext_ref[d, halo:halo + TT, :] = uc_r[...]
        ext_ref[d, halo + TT:halo + TT + halo, :] = un_r[...] * keep_next
        for g in range(W // GW):
            ls = slice(g * GW, (g + 1) * GW)
            u = cb_ref[:, ls]
            for tap in range(CONV_W):
                off = halo - 2 + tap
                u = u + cw_ref[tap:tap + 1, ls] * ext_ref[d, off:off + TT, ls]
            gates = _dot(u.astype(BF16), wbd_ref[d, g])
            r = jax.nn.sigmoid(gates[:, :GW] + ba_ref[d:d + 1, ls])
            ig = jax.nn.sigmoid(gates[:, GW:] + bx_ref[d:d + 1, ls])
            nl = -lam_ref[d:d + 1, ls]
            softplus = jnp.maximum(nl, 0.0) + jnp.log1p(jnp.exp(-jnp.abs(nl)))
            log_a = -LRU_C * r * softplus
            a = jnp.exp(log_a)
            gain2 = 1.0 - jnp.exp(2.0 * log_a)
            gain = jnp.where(gain2 > 0.0, gain2 * lax.rsqrt(gain2), 0.0)
            x_in = gain * ig * u
            a_acc, h = _lru_group_scan(a, x_in, d == 1)
            sa_ref[d, :, ls] = a_acc
            sx_ref[d, :, ls] = h

    ngroups = TT // LRU_GROUP

    def group_body(gi, carry):
        hf, hb = carry
        rf = pl.multiple_of(gi * LRU_GROUP, LRU_GROUP)
        rb = pl.multiple_of((ngroups - 1 - gi) * LRU_GROUP, LRU_GROUP)
        of = sx_ref[0, pl.ds(rf, LRU_GROUP), :] + sa_ref[0, pl.ds(rf, LRU_GROUP), :] * hf
        ob = sx_ref[1, pl.ds(rb, LRU_GROUP), :] + sa_ref[1, pl.ds(rb, LRU_GROUP), :] * hb
        hf_ref[pl.ds(rf, LRU_GROUP), :] = of
        hb_ref[pl.ds(rb, LRU_GROUP), :] = ob
        return of[LRU_GROUP - 1:LRU_GROUP, :], ob[0:1, :]

    hf, hb = lax.fori_loop(0, ngroups, group_body, (carry_ref[0, 0:1, :], carry_ref[1, 0:1, :]), unroll=4)
    carry_ref[0, 0:1, :] = hf
    carry_ref[1, 0:1, :] = hb


def _block_diag_gates(w_a, w_x, group):
    ndir, K, d, _ = w_a.shape
    per = group // d
    eye = jnp.eye(per, dtype=w_a.dtype)

    def bd(w):
        w = w.reshape(ndir, K // per, per, d, d)
        full = jnp.einsum('ngkde,kl->ngkdle', w, eye)
        return full.reshape(ndir, K // per, group, group)

    return jnp.concatenate([bd(w_a), bd(w_x)], axis=-1)


def _rglru(dm, u, conv_w, conv_b, w_a, b_a, w_x, b_x, lam):
    B, T = dm.B, dm.T
    W = u.shape[1]
    GW = 256
    halo = 8
    nblk = 1 + dm.N // SEQ_BLOCK
    fwd, bwd = _seq_block_maps(dm)
    hb = SEQ_BLOCK // halo
    nh = T // halo
    wbd = _block_diag_gates(w_a, w_x, GW).astype(BF16)

    def cur(m):
        return pl.BlockSpec((SEQ_BLOCK, W), lambda b, j: (m(b, j), 0))

    def prev(m):
        return pl.BlockSpec((halo, W), lambda b, j: (jnp.maximum(m(b, j) * hb - 1, 0), 0))

    def nxt(m):
        return pl.BlockSpec((halo, W), lambda b, j: (jnp.minimum((m(b, j) + 1) * hb, nh - 1), 0))

    def full(a):
        nd = a.ndim
        return pl.BlockSpec(a.shape, lambda b, j: (0,) * nd)

    consts = (conv_w, conv_b.reshape(1, W), wbd, b_a, b_x, lam)
    return pl.pallas_call(
        _lru_kernel,
        grid=(B, nblk),
        in_specs=[cur(fwd), prev(fwd), nxt(fwd), cur(bwd), prev(bwd), nxt(bwd)] + [full(a) for a in consts],
        out_specs=[cur(fwd), cur(bwd)],
        out_shape=[jax.ShapeDtypeStruct((T, W), F32)] * 2,
        scratch_shapes=[pltpu.VMEM((2, SEQ_BLOCK + 2 * halo, W), F32), pltpu.VMEM((2, 8, W), F32),
                        pltpu.VMEM((2, SEQ_BLOCK, W), F32), pltpu.VMEM((2, SEQ_BLOCK, W), F32)],
        compiler_params=_cparams(("arbitrary", "arbitrary")),
        name="rglru_bidirectional",
    )(u, u, u, u, u, u, *consts)


def _post_mixer_epilogue(o, x_ref, m2_ref, m3_ref, m4_ref, lng_ref, lnb_ref, rwh_ref, rwl_ref,
                         xo_ref, h2_ref, lg_ref, alpha):
    z = alpha * x_ref[...] + m2_ref[...] * o
    xn = _layer_norm(z, lng_ref[...], lnb_ref[...])
    xo_ref[...] = xn
    h2 = xn * (1.0 + m4_ref[...]) + m3_ref[...]
    h2_ref[...] = _pack_rows(h2)
    hh, hl = _split(h2)
    lg_ref[...] = _dot_nt(rwh_ref[...], hh) + _dot_nt(rwh_ref[...], hl) + _dot_nt(rwl_ref[...], hh)


def _post_even_kernel(nac_ref, nal_ref, of_ref, ob_ref, hg_ref, ng_ref, w_ref, *rest, alpha, ctx_tiles):
    na = jnp.where(pl.program_id(0) < ctx_tiles, nac_ref[...], nal_ref[...])
    o = of_ref[...] + ob_ref[...]
    wd = o.shape[1]
    r_i = lax.broadcasted_iota(I32, (wd, wd), 0) // HEAD_DIM
    c_i = lax.broadcasted_iota(I32, (wd, wd), 1) // HEAD_DIM
    avg = jnp.where(r_i == c_i, 1.0 / HEAD_DIM, 0.0).astype(BF16)
    sh, sl = _split(o * o)
    ms = _dot(sh, avg) + _dot(sl, avg)
    r = o * lax.rsqrt(ms + RMS_EPS) * ng_ref[...] * _silu(hg_ref[...])
    mix = _dot(na, w_ref[:wd, :]) + _dot(r.astype(BF16), w_ref[wd:, :])
    _post_mixer_epilogue(mix, *rest, alpha=alpha)


def _post_odd_kernel(hf_ref, hb_ref, y_ref, w_ref, *rest, alpha):
    m = ((hf_ref[...] + hb_ref[...]) * _gelu_tanh(y_ref[...])).astype(BF16)
    _post_mixer_epilogue(_dot(m, w_ref[...]), *rest, alpha=alpha)


def _post_mixer(dm, mixer_inputs, w_out, xs, m2, m3, m4, lng, lnb, rwh, rwl, alpha, even):
    T, D, TM = dm.T, dm.D, dm.TM
    E = rwh.shape[0]
    row = lambda a: pl.BlockSpec((TM, a.shape[1]), lambda i: (i, 0))
    const = lambda a: pl.BlockSpec(a.shape, lambda i: (0,) * a.ndim)
    mod_spec = pl.BlockSpec((None, 1, D), lambda i: (dm.group(i, TM), 0, 0))
    if even:
        nac, nal, of, ob, hg, ng = mixer_inputs
        nct = dm.T_ctx // TM
        ins = [nac, nal, of, ob, hg, ng, w_out]
        specs = [pl.BlockSpec((TM, nac.shape[1]), lambda i: (jnp.minimum(i, nct - 1), 0)),
                 pl.BlockSpec((TM, nal.shape[1]), lambda i: (jnp.maximum(i - nct, 0), 0)),
                 row(of), row(ob), row(hg), const(ng), const(w_out)]
        kern = functools.partial(_post_even_kernel, alpha=alpha, ctx_tiles=nct)
    else:
        hf, hb, y = mixer_inputs
        ins = [hf, hb, y, w_out]
        specs = [row(hf), row(hb), row(y), const(w_out)]
        kern = functools.partial(_post_odd_kernel, alpha=alpha)
    ins += [xs, m2, m3, m4, lng, lnb, rwh, rwl]
    specs += [row(xs), mod_spec, mod_spec, mod_spec, const(lng), const(lnb), const(rwh), const(rwl)]
    return pl.pallas_call(
        kern,
        grid=(T // TM,),
        in_specs=specs,
        out_specs=[pl.BlockSpec((TM, D), lambda i: (i, 0)), pl.BlockSpec((TM, D // 2), lambda i: (i, 0)),
                   pl.BlockSpec((E, TM), lambda i: (0, i))],
        out_shape=[jax.ShapeDtypeStruct((T, D), F32), jax.ShapeDtypeStruct((T, D // 2), jnp.uint32),
                   jax.ShapeDtypeStruct((E, T), F32)],
        compiler_params=_cparams(("parallel",)),
        name="post_mixer_even" if even else "post_mixer_odd",
    )(*ins)


def _route_kernel(lg_ref, bias_ref, tri_ref, idx_ref, rank_ref, w_ref, cnt_ref, carry_ref):
    i = pl.program_id(0)

    @pl.when(i == 0)
    def _():
        carry_ref[...] = jnp.zeros_like(carry_ref)

    E, TK = lg_ref.shape
    s = jax.nn.sigmoid(lg_ref[...])
    work = s + bias_ref[...]
    eio = lax.broadcasted_iota(I32, (E, TK), 0).astype(F32)
    picked = jnp.zeros((E, TK), F32)
    sels, idxs = [], []
    for _ in range(TOP_K):
        m = jnp.max(work, axis=0, keepdims=True)
        ik = jnp.min(jnp.where(work == m, eio, float(E)), axis=0, keepdims=True)
        oh = eio == ik
        sels.append(jnp.sum(jnp.where(oh, s, 0.0), axis=0, keepdims=True))
        idxs.append(ik)
        picked = jnp.where(oh, 1.0, picked)
        work = jnp.where(oh, -jnp.inf, work)
    total = sels[0]
    for sk in sels[1:]:
        total = total + sk
    carry = carry_ref[:, 0:1]
    ranks = carry + _dot(picked.astype(BF16), tri_ref[...])
    out_rows = idx_ref.shape[0]
    rio = lax.broadcasted_iota(I32, (out_rows, TK), 0)
    idx_o = jnp.zeros((out_rows, TK), I32)
    rank_o = jnp.zeros((out_rows, TK), I32)
    w_o = jnp.zeros((out_rows, TK), F32)
    for kx in range(TOP_K):
        rk = jnp.sum(jnp.where(eio == idxs[kx], ranks, 0.0), axis=0, keepdims=True).astype(I32)
        idx_o = jnp.where(rio == kx, idxs[kx].astype(I32), idx_o)
        rank_o = jnp.where(rio == kx, rk, rank_o)
        w_o = jnp.where(rio == kx, sels[kx] / total * ROUTED_SCALE, w_o)
    idx_ref[...] = idx_o
    rank_ref[...] = rank_o
    w_ref[...] = w_o
    new_carry = carry + jnp.sum(picked, axis=1, keepdims=True)
    carry_ref[...] = jnp.broadcast_to(new_carry, carry_ref.shape)
    cnt_ref[...] = jnp.broadcast_to(new_carry, cnt_ref.shape)


def _route(logits_t, router_bias, t0):
    E, T = logits_t.shape
    Tm = T - t0
    TK = _pick_tile(int(np.gcd(Tm, t0)), (1024, 512, 256))
    off = t0 // TK
    tri = jnp.asarray(np.triu(np.ones((TK, TK), np.float32), 1), BF16)
    tok = lambda: pl.BlockSpec((8, TK), lambda i: (0, i))
    idx, rank, w, cnt = pl.pallas_call(
        _route_kernel,
        grid=(Tm // TK,),
        in_specs=[pl.BlockSpec((E, TK), lambda i: (0, i + off)),
                  pl.BlockSpec((E, 1), lambda i: (0, 0)),
                  pl.BlockSpec((TK, TK), lambda i: (0, 0))],
        out_specs=[tok(), tok(), tok(), pl.BlockSpec((E, 128), lambda i: (0, 0))],
        out_shape=[jax.ShapeDtypeStruct((8, Tm), I32), jax.ShapeDtypeStruct((8, Tm), I32),
                   jax.ShapeDtypeStruct((8, Tm), F32), jax.ShapeDtypeStruct((E, 128), F32)],
        scratch_shapes=[pltpu.VMEM((E, 128), F32)],
        compiler_params=_cparams(("arbitrary",)),
        name="moe_route",
    )(logits_t, router_bias.reshape(E, 1), tri)
    return idx, rank, w, cnt[:, 0].astype(I32)


def _sorted_tile_rows(n_exp):
    return -(-(MOE_TILE * TOP_K + n_exp * (ROW_ALIGN - 1)) // 16) * 16


def _run_copies(cnt_ref, src_ref, dst_ref, base, n_exp, chunk, make_copy):
    shift = chunk.bit_length() - 1

    def expert_body(e, total):
        n = cnt_ref[base + e]
        src = src_ref[base + e]
        dst = dst_ref[base + e]
        nch = lax.shift_right_logical(n + (chunk - 1), shift)

        def chunk_body(c, carry):
            make_copy(pl.multiple_of(src + c * chunk, ROW_ALIGN), pl.multiple_of(dst + c * chunk, ROW_ALIGN)).start()
            return carry

        lax.fori_loop(0, nch, chunk_body, 0)
        return total + nch

    return lax.fori_loop(0, n_exp, expert_body, jnp.int32(0), unroll=4)


def _dispatch_kernel(cnt_ref, ls_ref, gb_ref, pos_ref, h_ref, xs_hbm, stage, pending, sem, *, n_exp):
    i = pl.program_id(0)
    slot = i % 2
    TMd = h_ref.shape[0]
    R = stage.shape[1] - DISP_CHUNK

    @pl.when(i == 0)
    def _():
        pending[0] = 0
        for s in range(2):
            stage[s, R:, :] = jnp.zeros((stage.shape[1] - R, stage.shape[2]), stage.dtype)

    rio = lax.broadcasted_iota(I32, (R, TMd), 0)
    perm = jnp.zeros((R, TMd), F32)
    for kx in range(TOP_K):
        perm = jnp.where(rio == pos_ref[kx:kx + 1, :], 1.0, perm)
    stage[slot, 0:R, :] = _pack_rows(_dot(perm.astype(BF16), _unpack_rows(h_ref[...])))

    def make_copy(src, dst):
        return pltpu.make_async_copy(stage.at[slot, pl.ds(src, DISP_CHUNK)], xs_hbm.at[pl.ds(dst, DISP_CHUNK)], sem)

    def wait_copies(n):
        def wait_body(c, carry):
            make_copy(0, 0).wait()
            return carry
        lax.fori_loop(0, n, wait_body, 0)

    wait_copies(pending[0])
    pending[0] = _run_copies(cnt_ref, ls_ref, gb_ref, i * n_exp, n_exp, DISP_CHUNK, make_copy)

    @pl.when(i == pl.num_programs(0) - 1)
    def _():
        wait_copies(pending[0])


def _dispatch(h2, plan, t0, n_slots, n_exp):
    cnt, ls, gb, pos = plan
    T, DP = h2.shape
    TMd = MOE_TILE
    Tm = T - t0
    assert Tm % TMd == 0 and t0 % TMd == 0
    off = t0 // TMd
    grid_spec = pltpu.PrefetchScalarGridSpec(
        num_scalar_prefetch=3,
        grid=(Tm // TMd,),
        in_specs=[pl.BlockSpec((8, TMd), lambda i, *_: (0, i)),
                  pl.BlockSpec((TMd, DP), lambda i, *_: (i + off, 0))],
        out_specs=pl.BlockSpec(memory_space=pl.ANY),
        scratch_shapes=[pltpu.VMEM((2, _sorted_tile_rows(n_exp) + DISP_CHUNK, DP), h2.dtype),
                        pltpu.SMEM((1,), I32), pltpu.SemaphoreType.DMA],
    )
    return pl.pallas_call(
        functools.partial(_dispatch_kernel, n_exp=n_exp),
        grid_spec=grid_spec,
        out_shape=jax.ShapeDtypeStruct((n_slots, DP), h2.dtype),
        compiler_params=_cparams(("arbitrary",)),
        name="moe_dispatch",
    )(cnt, ls, gb, pos, h2)


def _gmm_kernel(bexp_ref, bvalid_ref, x_ref, wg_ref, wu_ref, wd_ref, y_ref):
    del bexp_ref
    i = pl.program_id(0)
    nvalid = bvalid_ref[i]

    @pl.when(nvalid > 0)
    def _():
        row = lax.broadcasted_iota(I32, x_ref.shape, 0)
        xb = _unpack_rows(jnp.where(row < nvalid, x_ref[...], jnp.uint32(0)))
        act = _silu(_dot(xb, wg_ref[...].astype(BF16))) * _dot(xb, wu_ref[...].astype(BF16))
        y_ref[...] = _pack_rows(_dot(act.astype(BF16), wd_ref[...].astype(BF16)))

    @pl.when(nvalid == 0)
    def _():
        y_ref[...] = jnp.zeros_like(y_ref)


def _grouped_experts(xsorted, block_exp, block_valid, wg, wu, wd, layer):
    NS, DP = xsorted.shape
    nb = NS // MOE_BLOCK
    _, _, D, hid = wg.shape
    grid_spec = pltpu.PrefetchScalarGridSpec(
        num_scalar_prefetch=2,
        grid=(nb,),
        in_specs=[pl.BlockSpec((MOE_BLOCK, DP), lambda i, be, bv: (i, 0)),
                  pl.BlockSpec((None, None, D, hid), lambda i, be, bv: (layer, be[i], 0, 0)),
                  pl.BlockSpec((None, None, D, hid), lambda i, be, bv: (layer, be[i], 0, 0)),
                  pl.BlockSpec((None, None, hid, D), lambda i, be, bv: (layer, be[i], 0, 0))],
        out_specs=pl.BlockSpec((MOE_BLOCK, DP), lambda i, be, bv: (i, 0)),
    )
    return pl.pallas_call(
        _gmm_kernel,
        grid_spec=grid_spec,
        out_shape=jax.ShapeDtypeStruct((NS, DP), jnp.uint32),
        compiler_params=_cparams(("arbitrary",)),
        name="moe_grouped_experts",
    )(block_exp, block_valid, xsorted, wg, wu, wd)


def _combine_kernel(cnt_ref, so_ref, gb_ref, y_hbm, qpos_ref, w_ref, h_ref, sgu_ref, sd_ref, x_ref, m5_ref,
                    lng_ref, lnb_ref, o_ref, stage, pending, sems, *, alpha, n_exp):
    i = pl.program_id(0)
    slot = i % 2

    def make_copy(s):
        def f(dst, src):
            return pltpu.make_async_copy(y_hbm.at[pl.ds(src, COMB_CHUNK)], stage.at[s, pl.ds(dst, COMB_CHUNK)],
                                         sems.at[s])
        return f

    def fetch(tile, s):
        pending[s] = _run_copies(cnt_ref, so_ref, gb_ref, tile * n_exp, n_exp, COMB_CHUNK, make_copy(s))

    @pl.when(i == 0)
    def _():
        stage[...] = jnp.zeros_like(stage)
        fetch(0, 0)

    @pl.when(i + 1 < pl.num_programs(0))
    def _():
        fetch(i + 1, 1 - slot)

    hid = sd_ref.shape[0]
    gu = _dot(_unpack_rows(h_ref[...]), sgu_ref[...])
    y = _dot((_silu(gu[:, :hid]) * gu[:, hid:]).astype(BF16), sd_ref[...])

    def wait_body(c, carry):
        make_copy(slot)(0, 0).wait()
        return carry

    lax.fori_loop(0, pending[slot], wait_body, 0)

    TMc, S = h_ref.shape[0], stage.shape[1]
    sio = lax.broadcasted_iota(I32, (TMc, S), 1)
    q = jnp.zeros((TMc, S), F32)
    for kx in range(TOP_K):
        q = jnp.where(sio == qpos_ref[:, kx:kx + 1], w_ref[:, kx:kx + 1], q)
    y = y + _dot(q.astype(BF16), _unpack_rows(stage[slot]))
    z = alpha * x_ref[...] + m5_ref[...] * y
    o_ref[...] = _layer_norm(z, lng_ref[...], lnb_ref[...])


def _combine(dm, plan, ysorted, qpos_tm, w_tm, h2, sgu, sd, xs, m5, lng, lnb, alpha, t0, n_exp):
    cnt, so, gb = plan
    T, D = xs.shape
    TMc = MOE_TILE
    Tm = T - t0
    off = t0 // TMc
    const = lambda a: pl.BlockSpec(a.shape, lambda i, *_: (0,) * a.ndim)
    row = lambda a: pl.BlockSpec((TMc, a.shape[1]), lambda i, *_: (i + off, 0))
    grid_spec = pltpu.PrefetchScalarGridSpec(
        num_scalar_prefetch=3,
        grid=(Tm // TMc,),
        in_specs=[pl.BlockSpec(memory_space=pl.ANY),
                  pl.BlockSpec((TMc, 8), lambda i, *_: (i, 0)), pl.BlockSpec((TMc, 8), lambda i, *_: (i, 0)),
                  row(h2), const(sgu), const(sd), row(xs),
                  pl.BlockSpec((None, 1, D), lambda i, *_: (dm.group(i + off, TMc), 0, 0)),
                  const(lng), const(lnb)],
        out_specs=pl.BlockSpec((TMc, D), lambda i, *_: (i, 0)),
        scratch_shapes=[pltpu.VMEM((2, _sorted_tile_rows(n_exp), ysorted.shape[1]), ysorted.dtype),
                        pltpu.SMEM((2,), I32), pltpu.SemaphoreType.DMA((2,))],
    )
    return pl.pallas_call(
        functools.partial(_combine_kernel, alpha=alpha, n_exp=n_exp),
        grid_spec=grid_spec,
        out_shape=jax.ShapeDtypeStruct((Tm, D), F32),
        compiler_params=_cparams(("arbitrary",)),
        name="moe_combine",
    )(cnt, so, gb, ysorted, qpos_tm, w_tm, h2, sgu, sd, xs, m5, lng, lnb)


def _moe(dm, h2, logits_t, xs, m5, lng, lnb, router_bias, wg, wu, wd, layer, sgu, sd, alpha, t0):
    T = h2.shape[0]
    E = wg.shape[1]
    Tm = T - t0
    nt = Tm // MOE_TILE
    idx, rank, w, _ = _route(logits_t, router_bias, t0)
    idx, rank, w = idx[:TOP_K], rank[:TOP_K], w[:TOP_K]
    eids = jnp.arange(E, dtype=I32)
    onehot = (idx[:, :, None] == eids).reshape(TOP_K, nt, MOE_TILE, E)
    cnt = jnp.sum(onehot, axis=(0, 2), dtype=I32)
    rank_base = jnp.cumsum(cnt, axis=0) - cnt
    run = (cnt + ROW_ALIGN - 1) // ROW_ALIGN * ROW_ALIGN
    tile_base = jnp.cumsum(run, axis=0) - run
    local_start = jnp.cumsum(run, axis=1) - run
    counts = jnp.sum(run, axis=0)
    nb = -(-(Tm * TOP_K + E * (nt * (ROW_ALIGN - 1) + MOE_SLACK)) // MOE_BLOCK) + E
    padded = (counts + MOE_SLACK + MOE_BLOCK - 1) // MOE_BLOCK * MOE_BLOCK
    pends = jnp.cumsum(padded)
    pstarts = pends - padded
    global_base = pstarts[None, :] + tile_base
    bstart = jnp.arange(nb, dtype=I32) * MOE_BLOCK
    block_exp = jnp.minimum(jnp.sum((pends[None, :] <= bstart[:, None]).astype(I32), axis=1), E - 1)
    of_block = block_exp[:, None] == eids[None, :]
    seg_end = jnp.sum(jnp.where(of_block, (pstarts + counts)[None, :], 0), axis=1)
    block_valid = jnp.clip(seg_end - bstart, 0, MOE_BLOCK).astype(I32)
    look = lambda tab: jnp.sum(jnp.where(onehot, tab[None, :, None, :], 0), axis=3).reshape(TOP_K, Tm)
    pos = rank - look(rank_base) + look(local_start)
    pos8 = jnp.pad(pos, ((0, 8 - TOP_K), (0, 0)), constant_values=-1).astype(I32)
    pos_tm = jnp.pad(pos.T, ((0, 0), (0, 8 - TOP_K)), constant_values=-1).astype(I32)
    w_tm = jnp.pad(w.T, ((0, 0), (0, 8 - TOP_K)))
    flat = lambda a: a.reshape(-1).astype(I32)
    plan = (flat(run), flat(local_start), flat(global_base))
    xsorted = _dispatch(h2, plan + (pos8,), t0, nb * MOE_BLOCK, E)
    ysorted = _grouped_experts(xsorted, block_exp, block_valid, wg, wu, wd, layer)
    return _combine(dm, plan, ysorted, pos_tm, w_tm, h2, sgu, sd, xs, m5, lng, lnb, alpha, t0, E)


def kernel(x, c, ctx, c_ctx, ada_w, ada_b, ln_g, ln_b, ev_w_in, ev_w_out, na_rpb, hg_lb_raw, hg_norm_g, od_w_in, od_conv_w, od_conv_b, lru_w_a, lru_b_a, lru_w_x, lru_b_x, lru_lam, od_w_out, router_w, router_bias, exp_w_gate, exp_w_up, exp_w_down, sh_w_gate, sh_w_up, sh_w_down):
    B, N, D = x.shape
    C = ctx.shape[1]
    depth = ada_w.shape[0]
    dm = _Dims(B, N, C, D)
    alpha = float((2 * depth) ** 0.25)
    assert B + 1 <= 8

    cvec = jnp.zeros((8, D), F32).at[:B].set(c).at[B].set(c_ctx)
    mod = _modulation(cvec, ada_w, ada_b).reshape(depth, 8, 6, 1, D)

    p_lb = jax.nn.softmax(hg_lb_raw.astype(F32), axis=1)
    hg_lb = jnp.cumsum(p_lb, axis=1) - p_lb[:, :1]

    xs = jnp.concatenate([ctx.reshape(B * C, D), x.reshape(B * N, D)], axis=0)
    for l in range(depth):
        jl = l // 2
        last = l == depth - 1
        m = [mod[l, :, t] for t in range(6)]
        rw_t = router_w[l].T
        rwh, rwl = _split(rw_t)
        lng = ln_g[l][:, None, :]
        lnb = ln_b[l][:, None, :]
        if l % 2 == 0:
            q, k, v, hq, ff, fb, hi, hg = _inproj(dm, xs, m[1], m[0], ev_w_in[jl].astype(BF16), True)
            bias = _na_bias_tables(na_rpb[jl], N // GRID_W)
            na_lat = _neighbourhood_attention(dm, q, k, v, bias)
            na_ctx = _context_attention(dm, q, k, v)
            o_f, o_b = _hgrn2(dm, hq, ff, fb, hi, hg_lb[:, jl])
            ng = jnp.tile(hg_norm_g[jl], hq.shape[1] // HEAD_DIM)[None, :]
            mixer_inputs = (na_ctx, na_lat, o_f, o_b, hg, ng)
            w_out = ev_w_out[jl].astype(BF16)
        else:
            y, u = _inproj(dm, xs, m[1], m[0], od_w_in[jl].astype(BF16), False)
            h_f, h_b = _rglru(dm, u, od_conv_w[jl], od_conv_b[jl], lru_w_a[jl], lru_b_a[jl],
                              lru_w_x[jl], lru_b_x[jl], lru_lam[jl])
            mixer_inputs = (h_f, h_b, y)
            w_out = od_w_out[jl].astype(BF16)
        xs, h2, logits_t = _post_mixer(dm, mixer_inputs, w_out, xs, m[2], m[3], m[4], lng[0], lnb[0],
                                       rwh, rwl, alpha, l % 2 == 0)
        sgu = jnp.concatenate([sh_w_gate[l], sh_w_up[l]], axis=-1).astype(BF16)
        sd = sh_w_down[l].astype(BF16)
        t0 = dm.T_ctx if last else 0
        xs = _moe(dm, h2, logits_t, xs, m[5], lng[1], lnb[1], router_bias[l], exp_w_gate, exp_w_up,
                  exp_w_down, l, sgu, sd, alpha, t0)
    return xs.reshape(B, N, D)
```

```python
import functools

import numpy as np
import jax
import jax.numpy as jnp
from jax import lax
from jax.experimental import pallas as pl
from jax.experimental.pallas import tpu as pltpu

F32 = jnp.float32
BF16 = jnp.bfloat16
I32 = jnp.int32

HEAD_DIM = 64
GRID_W = 64
NA_WIN_ROWS = 8
NA_WIN_COLS = 16
NA_QROWS = 8
NA_KROWS = 16
HG_CHUNK = 64
HG_HEADS_PER_GROUP = 4
FORGET_FLOOR = 1e-30
HG_SAFE_EXP = 80.0
HG_CLIP_EXP = 85.0
LRU_C = 8.0
LRU_BLOCKS = 16
CONV_W = 4
TOP_K = 6
ROUTED_SCALE = 2.5
LN_EPS = 1e-5
RMS_EPS = 1e-6
SEQ_BLOCK = 256
MOE_BLOCK = 512
MOE_TILE = 256
ROW_ALIGN = 8
DISP_CHUNK = 32
COMB_CHUNK = ROW_ALIGN
MOE_SLACK = DISP_CHUNK
MASK_VALUE = -1e30
VMEM_LIMIT = 56 * 1024 * 1024


def _cparams(sem):
    return pltpu.CompilerParams(dimension_semantics=sem, vmem_limit_bytes=VMEM_LIMIT)


def _split(a):
    hi = a.astype(BF16)
    lo = (a - hi.astype(F32)).astype(BF16)
    return hi, lo


def _dot(a, b):
    return jnp.dot(a, b, preferred_element_type=F32)


def _dot_nt(a, b):
    return lax.dot_general(a, b, (((1,), (1,)), ((), ())), preferred_element_type=F32)


def _dot_tn(a, b):
    return lax.dot_general(a, b, (((0,), (0,)), ((), ())), preferred_element_type=F32)


def _dot3(a, b):
    ah, al = _split(a)
    bh, bl = _split(b)
    return _dot(ah, bh) + _dot(al, bh) + _dot(ah, bl)


def _pack_rows(h, is_bf16_valued=False):
    half = h.shape[1] // 2
    if not is_bf16_valued:
        h = h.astype(BF16).astype(F32)
    bits = pltpu.bitcast(h, jnp.uint32)
    return (bits[:, :half] >> 16) | (bits[:, half:] & jnp.uint32(0xFFFF0000))


def _unpack_rows(p):
    lo = pltpu.bitcast(p << 16, F32)
    hi = pltpu.bitcast(p & jnp.uint32(0xFFFF0000), F32)
    return jnp.concatenate([lo, hi], axis=1).astype(BF16)


def _silu(v):
    return v * jax.nn.sigmoid(v)


def _gelu_tanh(v):
    return 0.5 * v * (1.0 + jnp.tanh(0.7978845608028654 * (v + 0.044715 * v * v * v)))


def _layer_norm(z, g, b):
    mu = jnp.mean(z, axis=-1, keepdims=True)
    zc = z - mu
    var = jnp.mean(zc * zc, axis=-1, keepdims=True)
    return zc * lax.rsqrt(var + LN_EPS) * g + b


def _pick_tile(n, cands):
    for c in cands:
        if n % c == 0:
            return c
    raise ValueError(f"no tile for {n}")


class _Dims:
    def __init__(self, B, N, C, D):
        self.B, self.N, self.C, self.D = B, N, C, D
        self.T_ctx = B * C
        self.T = B * C + B * N
        assert C == SEQ_BLOCK and N % SEQ_BLOCK == 0
        assert N % GRID_W == 0 and (N // GRID_W) % NA_QROWS == 0 and N // GRID_W >= NA_KROWS
        assert self.T_ctx % 512 == 0
        self.TM = _pick_tile(self.T_ctx, (512, 256))
        assert N % self.TM == 0

    def group(self, i, tile):
        start = i * tile
        return jnp.where(start < self.T_ctx, self.B, (start - self.T_ctx) // self.N)


def _mod_kernel(c_ref, w_ref, b_ref, o_ref):
    o_ref[...] = _dot3(_silu(c_ref[...]), w_ref[...]) + b_ref[...]


def _modulation(cvec, ada_w, ada_b):
    L, D, W6 = ada_w.shape
    nc = 1536
    return pl.pallas_call(
        _mod_kernel,
        grid=(L, W6 // nc),
        in_specs=[
            pl.BlockSpec((8, D), lambda l, j: (0, 0)),
            pl.BlockSpec((None, D, nc), lambda l, j: (l, 0, j)),
            pl.BlockSpec((None, 1, nc), lambda l, j: (l, 0, j)),
        ],
        out_specs=pl.BlockSpec((None, 8, nc), lambda l, j: (l, 0, j)),
        out_shape=jax.ShapeDtypeStruct((L, 8, W6), F32),
        compiler_params=_cparams(("arbitrary", "arbitrary")),
        name="adaln_modulation",
    )(cvec, ada_w, ada_b.reshape(L, 1, W6))


def _inproj_even_kernel(x_ref, sc_ref, sh_ref, w_ref, q_ref, k_ref, v_ref, hq_ref, ff_ref, fb_ref, hi_ref, hg_ref):
    h = (x_ref[...] * (1.0 + sc_ref[...]) + sh_ref[...]).astype(BF16)
    wd = q_ref.shape[1]

    def part(j):
        return _dot(h, w_ref[:, j * wd:(j + 1) * wd])

    scale = HEAD_DIM ** -0.5
    q_ref[...] = (part(0) * scale).astype(BF16)
    k_ref[...] = part(1).astype(BF16)
    v_ref[...] = part(2).astype(BF16)
    hq_ref[...] = _silu(part(3)) * scale
    ff_ref[...] = part(4)
    fb_ref[...] = part(5)
    hi_ref[...] = part(6)
    hg_ref[...] = part(7)


def _inproj_odd_kernel(x_ref, sc_ref, sh_ref, w_ref, y_ref, u_ref):
    h = (x_ref[...] * (1.0 + sc_ref[...]) + sh_ref[...]).astype(BF16)
    wd = y_ref.shape[1]
    y_ref[...] = _dot(h, w_ref[:, :wd])
    u_ref[...] = _dot(h, w_ref[:, wd:])


def _inproj(dm, xs, sc, sh, w, even):
    T, D, TM = dm.T, dm.D, dm.TM
    wtot = w.shape[1]
    mod_spec = pl.BlockSpec((None, 1, D), lambda i: (dm.group(i, TM), 0, 0))
    in_specs = [pl.BlockSpec((TM, D), lambda i: (i, 0)), mod_spec, mod_spec,
                pl.BlockSpec((D, wtot), lambda i: (0, 0))]
    if even:
        wd = wtot // 8
        dts = [BF16] * 3 + [F32] * 5
        kern = _inproj_even_kernel
    else:
        wd = wtot // 2
        dts = [F32] * 2
        kern = _inproj_odd_kernel
    return pl.pallas_call(
        kern,
        grid=(T // TM,),
        in_specs=in_specs,
        out_specs=[pl.BlockSpec((TM, wd), lambda i: (i, 0)) for _ in dts],
        out_shape=[jax.ShapeDtypeStruct((T, wd), dt) for dt in dts],
        compiler_params=_cparams(("parallel",)),
        name="inproj_even" if even else "inproj_odd",
    )(xs, sc, sh, w)


def _na_bias_tables(rpb, rows):
    W = GRID_W
    H = rpb.shape[0]
    nr, nc = 2 * NA_WIN_ROWS - 1, 2 * NA_WIN_COLS - 1
    hp = lax.Precision.HIGHEST
    c = np.arange(W)[:, None]
    kc = np.arange(W)[None, :]
    c0 = np.clip(c - NA_WIN_COLS // 2, 0, W - NA_WIN_COLS)
    valid_c = (kc >= c0) & (kc < c0 + NA_WIN_COLS)
    dc = np.clip(kc - c + NA_WIN_COLS - 1, 0, nc - 1)
    oh_c = (dc.reshape(-1, 1) == np.arange(nc)[None, :]).astype(np.float32)
    by_col = jnp.einsum('hrs,xs->hrx', rpb.astype(F32), jnp.asarray(oh_c), precision=hp)
    tabs = []
    for rbase, kb in ((0, 0), (NA_QROWS, NA_QROWS - NA_WIN_ROWS // 2), (rows - NA_QROWS, rows - NA_KROWS)):
        r = rbase + np.arange(NA_QROWS)[:, None]
        kr = kb + np.arange(NA_KROWS)[None, :]
        r0 = np.clip(r - NA_WIN_ROWS // 2, 0, rows - NA_WIN_ROWS)
        valid_r = (kr >= r0) & (kr < r0 + NA_WIN_ROWS)
        dr = np.clip(kr - r + NA_WIN_ROWS - 1, 0, nr - 1)
        oh_r = (dr.reshape(-1, 1) == np.arange(nr)[None, :]).astype(np.float32)
        t = jnp.einsum('yr,hrx->hyx', jnp.asarray(oh_r), by_col, precision=hp)
        t = t.reshape(H, NA_QROWS, NA_KROWS, W, W).transpose(0, 1, 3, 2, 4)
        valid = valid_r[:, None, :, None] & valid_c[None, :, None, :]
        t = jnp.where(jnp.asarray(valid)[None], t, MASK_VALUE)
        tabs.append(t.reshape(H, NA_QROWS * W, NA_KROWS * W))
    return jnp.stack(tabs).astype(F32)


def _na_kernel(q_ref, k0, k1, k2, k3, v0, v1, v2, v3, kc_ref, vc_ref, bias_ref, o_ref):
    q2 = q_ref[...]
    lane = lax.broadcasted_iota(I32, (1, q2.shape[1]), 1)
    first = lane < HEAD_DIM
    ks = [r[...] for r in (k0, k1, k2, k3)]
    vs = [r[...] for r in (v0, v1, v2, v3)]
    kc = kc_ref[...]
    vc = vc_ref[...]
    kb = ks[0].shape[0]
    outs = []
    for hh in range(2):
        sel = first if hh == 0 else jnp.logical_not(first)
        qh = jnp.where(sel, q2, jnp.zeros_like(q2))
        s_loc = jnp.concatenate([_dot_nt(qh, kj) for kj in ks], axis=1) + bias_ref[hh]
        s_ctx = _dot_nt(qh, kc)
        m = jnp.maximum(jnp.max(s_loc, axis=1, keepdims=True), jnp.max(s_ctx, axis=1, keepdims=True))
        p_loc = jnp.exp(s_loc - m)
        p_ctx = jnp.exp(s_ctx - m)
        denom = jnp.sum(p_loc, axis=1, keepdims=True) + jnp.sum(p_ctx, axis=1, keepdims=True)
        o = _dot(p_ctx.astype(BF16), vc)
        for j in range(4):
            o = o + _dot(p_loc[:, j * kb:(j + 1) * kb].astype(BF16), vs[j])
        outs.append(o / denom)
    o_ref[...] = jnp.where(first, outs[0], outs[1]).astype(o_ref.dtype)


def _neighbourhood_attention(dm, q, k, v, bias):
    B, N, C, T = dm.B, dm.N, dm.C, dm.T
    H2 = q.shape[1] // (2 * HEAD_DIM)
    rows = N // GRID_W
    nqb = rows // NA_QROWS
    QB = NA_QROWS * GRID_W
    KB = QB // 2
    ngroups = N // KB
    lat_q0 = dm.T_ctx // QB
    lat_k0 = dm.T_ctx // KB

    def q_map(p, i, b):
        return (lat_q0 + b * (N // QB) + i, p)

    def o_map(p, i, b):
        return (b * (N // QB) + i, p)

    def kv_map(j):
        def f(p, i, b):
            gs = jnp.clip(2 * i - 1, 0, ngroups - 4)
            return (lat_k0 + b * ngroups + gs + j, p)
        return f

    def ctx_map(p, i, b):
        return (b, p)

    def bias_map(p, i, b):
        var = jnp.where(i == 0, 0, jnp.where(i == nqb - 1, 2, 1))
        return (var, p, 0, 0)

    lanes = 2 * HEAD_DIM
    kv_specs = [pl.BlockSpec((KB, lanes), kv_map(j)) for j in range(4)]
    return pl.pallas_call(
        _na_kernel,
        grid=(H2, nqb, B),
        in_specs=[pl.BlockSpec((QB, lanes), q_map)] + kv_specs + kv_specs
        + [pl.BlockSpec((C, lanes), ctx_map), pl.BlockSpec((C, lanes), ctx_map),
           pl.BlockSpec((None, 2, QB, NA_KROWS * GRID_W), bias_map)],
        out_specs=pl.BlockSpec((QB, lanes), o_map),
        out_shape=jax.ShapeDtypeStruct((B * N, q.shape[1]), BF16),
        compiler_params=_cparams(("arbitrary", "arbitrary", "arbitrary")),
        name="neighbourhood_attention",
    )(q, k, k, k, k, v, v, v, v, k, v, bias)


def _ctx_attn_kernel(q_ref, k_ref, v_ref, o_ref):
    q2 = q_ref[...]
    k2 = k_ref[...]
    v2 = v_ref[...]
    lane = lax.broadcasted_iota(I32, (1, q2.shape[1]), 1)
    first = lane < HEAD_DIM
    outs = []
    for hh in range(2):
        sel = first if hh == 0 else jnp.logical_not(first)
        s = _dot_nt(jnp.where(sel, q2, jnp.zeros_like(q2)), k2)
        p = jnp.exp(s - jnp.max(s, axis=1, keepdims=True))
        outs.append(_dot(p.astype(BF16), v2) / jnp.sum(p, axis=1, keepdims=True))
    o_ref[...] = jnp.where(first, outs[0], outs[1]).astype(o_ref.dtype)


def _context_attention(dm, q, k, v):
    lanes = 2 * HEAD_DIM
    spec = pl.BlockSpec((dm.C, lanes), lambda p, b: (b, p))
    return pl.pallas_call(
        _ctx_attn_kernel,
        grid=(q.shape[1] // lanes, dm.B),
        in_specs=[spec, spec, spec],
        out_specs=spec,
        out_shape=jax.ShapeDtypeStruct((dm.T_ctx, q.shape[1]), BF16),
        compiler_params=_cparams(("arbitrary", "arbitrary")),
        name="context_attention",
    )(q, k, v)


def _seq_block_maps(dm):
    nl = dm.N // SEQ_BLOCK
    base = dm.T_ctx // SEQ_BLOCK

    def fwd(b, j):
        return jnp.where(j == 0, b, base + b * nl + j - 1)

    def bwd(b, j):
        return jnp.where(j == 0, b, base + b * nl + nl - j)

    return fwd, bwd


def _hgrn_prepare(hq, z, lb, tri, last_row, mid_row):
    f = lb + (1.0 - lb) * jax.nn.sigmoid(z)
    fm = jnp.maximum(f, FORGET_FLOOR)
    g = jnp.log(fm)
    kk = 1.0 - fm
    gh, gl = _split(g)
    cum = _dot(tri, gh) + _dot(tri, gl)
    tail = cum[last_row:last_row + 1, :]
    e = cum - cum[mid_row:mid_row + 1, :]
    emax = jnp.max(jnp.max(jnp.abs(e), axis=1, keepdims=True), axis=0, keepdims=True)
    ec = jnp.clip(e, -HG_CLIP_EXP, HG_CLIP_EXP)
    qe = (hq * jnp.exp(ec)).astype(BF16)
    ke = (kk * jnp.exp(-ec)).astype(BF16)
    return kk, cum, tail, emax, qe, ke


def _hgrn_kernel(hqf_ref, zf_ref, vf_ref, hqb_ref, zb_ref, vb_ref, lbf_ref, lbb_ref,
                 of_ref, ob_ref, st_ref, a_ref, cum_ref, k_ref, qs_ref, k2_ref, et_ref):
    j = pl.program_id(1)

    @pl.when(j == 0)
    def _():
        st_ref[...] = jnp.zeros_like(st_ref)

    CH = HG_CHUNK
    GW = HG_HEADS_PER_GROUP * HEAD_DIM
    ngroups = hqf_ref.shape[1] // GW
    nchunks = hqf_ref.shape[0] // CH
    t_i = lax.broadcasted_iota(I32, (CH, CH), 0)
    u_i = lax.broadcasted_iota(I32, (CH, CH), 1)
    tris = ((u_i <= t_i).astype(BF16), (u_i >= t_i).astype(BF16))
    t_w = lax.broadcasted_iota(I32, (CH, GW), 0)
    s_w = lax.broadcasted_iota(I32, (CH, GW), 1) % CH
    cmasks = (s_w <= t_w, s_w >= t_w)
    r_b = lax.broadcasted_iota(I32, (GW, GW), 0)
    c_b = lax.broadcasted_iota(I32, (GW, GW), 1)
    bm = (r_b // HEAD_DIM) == (c_b // HEAD_DIM)
    hsel_base = (r_b // HEAD_DIM) * HEAD_DIM
    dirs = ((hqf_ref, zf_ref, vf_ref, lbf_ref, of_ref), (hqb_ref, zb_ref, vb_ref, lbb_ref, ob_ref))
    chains = [(d, g) for d in range(2) for g in range(ngroups)]

    def expand(m):
        return jnp.where(bm, jnp.concatenate([m] * HG_HEADS_PER_GROUP, axis=0), jnp.zeros((GW, GW), m.dtype))

    nchain = len(chains)

    def rows_of(c):
        return pl.multiple_of(c * CH, CH), pl.multiple_of((nchunks - 1 - c) * CH, CH)

    def prep_body(c, worst):
        rows = rows_of(c)
        for n, (d, g) in enumerate(chains):
            hq_r, z_r, _, lb_r, _ = dirs[d]
            ls = slice(g * GW, (g + 1) * GW)
            hq = hq_r[pl.ds(rows[d], CH), ls]
            z = z_r[pl.ds(rows[d], CH), ls]
            last_row = CH - 1 if d == 0 else 0
            kk, cum, tail, emax, qe, ke = _hgrn_prepare(hq, z, lb_r[:, ls], tris[d], last_row, CH // 2)
            m = c * nchain + n
            a_ref[m] = _dot_nt(qe, expand(ke))
            cum_ref[m] = cum
            k_ref[m] = kk
            qs_ref[m] = hq * jnp.exp(cum)
            k2_ref[m] = kk * jnp.exp(tail - cum)
            et_ref[m] = jnp.broadcast_to(jnp.exp(tail), et_ref.shape[1:])
            worst = jnp.maximum(worst, emax)
        return worst

    worst = lax.fori_loop(0, nchunks, prep_body, jnp.zeros((1, 1), F32), unroll=True)

    @pl.when(worst[0, 0] > HG_SAFE_EXP)
    def _():
        def slow_body(c, carry):
            rows = rows_of(c)
            for n, (d, g) in enumerate(chains):
                m = c * nchain + n
                hq = dirs[d][0][pl.ds(rows[d], CH), g * GW:(g + 1) * GW]
                cum = cum_ref[m]

                def key_body(s, acc):
                    cs = cum_ref[m, pl.ds(s, 1), :]
                    ksr = k_ref[m, pl.ds(s, 1), :]
                    p = hq * ksr * jnp.exp(jnp.minimum(cum - cs, 0.0))
                    hsel = jnp.where(c_b == hsel_base + s, 1.0, 0.0).astype(BF16)
                    return acc + _dot(p.astype(BF16), hsel)

                a_ref[m] = lax.fori_loop(0, CH, key_body, jnp.zeros((CH, GW), F32))
            return carry

        lax.fori_loop(0, nchunks, slow_body, 0)

    def state_body(c, carry):
        rows = rows_of(c)
        for n, (d, g) in enumerate(chains):
            m = c * nchain + n
            ls = slice(g * GW, (g + 1) * GW)
            vb = dirs[d][2][pl.ds(rows[d], CH), ls].astype(BF16)
            st = st_ref[n]
            a = jnp.where(cmasks[d], a_ref[m], 0.0).astype(BF16)
            o = _dot(a, expand(vb)) + _dot_nt(qs_ref[m].astype(BF16), st.astype(BF16))
            dirs[d][4][pl.ds(rows[d], CH), ls] = o
            st_ref[n] = et_ref[m, 0:1, :] * st + jnp.where(bm, _dot_tn(vb, k2_ref[m].astype(BF16)), 0.0)
        return carry

    lax.fori_loop(0, nchunks, state_body, 0, unroll=True)


def _hgrn2(dm, hq, ff, fb, hi, lb):
    B, T = dm.B, dm.T
    Wd = hq.shape[1]
    GW = HG_HEADS_PER_GROUP * HEAD_DIM
    nchain = 2 * (Wd // GW)
    nsteps = nchain * (SEQ_BLOCK // HG_CHUNK)
    nblk = 1 + dm.N // SEQ_BLOCK
    fwd, bwd = _seq_block_maps(dm)
    fspec = pl.BlockSpec((SEQ_BLOCK, Wd), lambda b, j: (fwd(b, j), 0))
    bspec = pl.BlockSpec((SEQ_BLOCK, Wd), lambda b, j: (bwd(b, j), 0))
    lbspec = pl.BlockSpec((1, Wd), lambda b, j: (0, 0))
    return pl.pallas_call(
        _hgrn_kernel,
        grid=(B, nblk),
        in_specs=[fspec, fspec, fspec, bspec, bspec, bspec, lbspec, lbspec],
        out_specs=[fspec, bspec],
        out_shape=[jax.ShapeDtypeStruct((T, Wd), F32)] * 2,
        scratch_shapes=[pltpu.VMEM((nchain, GW, GW), F32)]
        + [pltpu.VMEM((nsteps, HG_CHUNK, GW), F32)] * 5 + [pltpu.VMEM((nsteps, 8, GW), F32)],
        compiler_params=_cparams(("arbitrary", "arbitrary")),
        name="hgrn2_bidirectional",
    )(hq, ff, hi, hq, fb, hi, lb[0:1], lb[1:2])


LRU_GROUP = 8


def _lru_group_scan(a, x, reverse):
    n, w = a.shape
    a = a.reshape(n // LRU_GROUP, LRU_GROUP, w)
    x = x.reshape(n // LRU_GROUP, LRU_GROUP, w)
    row = lax.broadcasted_iota(I32, a.shape, 1)
    s = 1
    while s < LRU_GROUP:
        shift = (LRU_GROUP - s) if reverse else s
        a_sh = pltpu.roll(a, shift, 1)
        x_sh = pltpu.roll(x, shift, 1)
        valid = (row < LRU_GROUP - s) if reverse else (row >= s)
        x = jnp.where(valid, a * x_sh + x, x)
        a = jnp.where(valid, a * a_sh, a)
        s *= 2
    return a.reshape(n, w), x.reshape(n, w)


def _lru_kernel(ucf_ref, upf_ref, unf_ref, ucb_ref, upb_ref, unb_ref, cw_ref, cb_ref, wbd_ref,
                ba_ref, bx_ref, lam_ref, hf_ref, hb_ref, ext_ref, carry_ref, sa_ref, sx_ref):
    j = pl.program_id(1)
    nblk = pl.num_programs(1)

    @pl.when(j == 0)
    def _():
        carry_ref[...] = jnp.zeros_like(carry_ref)

    TT, W = ucf_ref.shape
    GW = wbd_ref.shape[2]
    halo = upf_ref.shape[0]
    dirs = ((ucf_ref, upf_ref, unf_ref, hf_ref), (ucb_ref, upb_ref, unb_ref, hb_ref))
    for d, (uc_r, up_r, un_r, out_r) in enumerate(dirs):
        pos = j if d == 0 else jnp.where(j == 0, 0, nblk - j)
        keep_prev = jnp.where((pos == 0) | (pos == 1), 0.0, 1.0)
        keep_next = jnp.where((pos == 0) | (pos == nblk - 1), 0.0, 1.0)
        ext_ref[d, 0:halo, :] = up_r[...] * keep_prev
        ext_ref[d, halo:halo + TT, :] = uc_r[...]
        ext_ref[d, halo + TT:halo + TT + halo, :] = un_r[...] * keep_next
        for g in range(W // GW):
            ls = slice(g * GW, (g + 1) * GW)
            u = cb_ref[:, ls]
            for tap in range(CONV_W):
                off = halo - 2 + tap
                u = u + cw_ref[tap:tap + 1, ls] * ext_ref[d, off:off + TT, ls]
            gates = _dot(u.astype(BF16), wbd_ref[d, g])
            r = jax.nn.sigmoid(gates[:, :GW] + ba_ref[d:d + 1, ls])
            ig = jax.nn.sigmoid(gates[:, GW:] + bx_ref[d:d + 1, ls])
            nl = -lam_ref[d:d + 1, ls]
            softplus = jnp.maximum(nl, 0.0) + jnp.log1p(jnp.exp(-jnp.abs(nl)))
            log_a = -LRU_C * r * softplus
            a = jnp.exp(log_a)
            gain2 = 1.0 - a * a
            gain = jnp.where(gain2 > 0.0, gain2 * lax.rsqrt(gain2), 0.0)
            x_in = gain * ig * u
            a_acc, h = _lru_group_scan(a, x_in, d == 1)
            sa_ref[d, :, ls] = a_acc
            sx_ref[d, :, ls] = h

    ngroups = TT // LRU_GROUP

    def group_body(gi, carry):
        hf, hb = carry
        rf = pl.multiple_of(gi * LRU_GROUP, LRU_GROUP)
        rb = pl.multiple_of((ngroups - 1 - gi) * LRU_GROUP, LRU_GROUP)
        of = sx_ref[0, pl.ds(rf, LRU_GROUP), :] + sa_ref[0, pl.ds(rf, LRU_GROUP), :] * hf
        ob = sx_ref[1, pl.ds(rb, LRU_GROUP), :] + sa_ref[1, pl.ds(rb, LRU_GROUP), :] * hb
        hf_ref[pl.ds(rf, LRU_GROUP), :] = of
        hb_ref[pl.ds(rb, LRU_GROUP), :] = ob
        return of[LRU_GROUP - 1:LRU_GROUP, :], ob[0:1, :]

    hf, hb = lax.fori_loop(0, ngroups, group_body, (carry_ref[0, 0:1, :], carry_ref[1, 0:1, :]), unroll=4)
    carry_ref[0, 0:1, :] = hf
    carry_ref[1, 0:1, :] = hb


def _block_diag_gates(w_a, w_x, group):
    ndir, K, d, _ = w_a.shape
    per = group // d
    eye = jnp.eye(per, dtype=w_a.dtype)

    def bd(w):
        w = w.reshape(ndir, K // per, per, d, d)
        full = jnp.einsum('ngkde,kl->ngkdle', w, eye)
        return full.reshape(ndir, K // per, group, group)

    return jnp.concatenate([bd(w_a), bd(w_x)], axis=-1)


def _rglru(dm, u, conv_w, conv_b, w_a, b_a, w_x, b_x, lam):
    B, T = dm.B, dm.T
    W = u.shape[1]
    GW = 256
    halo = 8
    nblk = 1 + dm.N // SEQ_BLOCK
    fwd, bwd = _seq_block_maps(dm)
    hb = SEQ_BLOCK // halo
    nh = T // halo
    wbd = _block_diag_gates(w_a, w_x, GW).astype(BF16)

    def cur(m):
        return pl.BlockSpec((SEQ_BLOCK, W), lambda b, j: (m(b, j), 0))

    def prev(m):
        return pl.BlockSpec((halo, W), lambda b, j: (jnp.maximum(m(b, j) * hb - 1, 0), 0))

    def nxt(m):
        return pl.BlockSpec((halo, W), lambda b, j: (jnp.minimum((m(b, j) + 1) * hb, nh - 1), 0))

    def full(a):
        nd = a.ndim
        return pl.BlockSpec(a.shape, lambda b, j: (0,) * nd)

    consts = (conv_w, conv_b.reshape(1, W), wbd, b_a, b_x, lam)
    return pl.pallas_call(
        _lru_kernel,
        grid=(B, nblk),
        in_specs=[cur(fwd), prev(fwd), nxt(fwd), cur(bwd), prev(bwd), nxt(bwd)] + [full(a) for a in consts],
        out_specs=[cur(fwd), cur(bwd)],
        out_shape=[jax.ShapeDtypeStruct((T, W), F32)] * 2,
        scratch_shapes=[pltpu.VMEM((2, SEQ_BLOCK + 2 * halo, W), F32), pltpu.VMEM((2, 8, W), F32),
                        pltpu.VMEM((2, SEQ_BLOCK, W), F32), pltpu.VMEM((2, SEQ_BLOCK, W), F32)],
        compiler_params=_cparams(("arbitrary", "arbitrary")),
        name="rglru_bidirectional",
    )(u, u, u, u, u, u, *consts)


def _post_mixer_epilogue(o, x_ref, m2_ref, m3_ref, m4_ref, lng_ref, lnb_ref, rwh_ref, rwl_ref,
                         xo_ref, h2_ref, lg_ref, alpha):
    z = alpha * x_ref[...] + m2_ref[...] * o
    xn = _layer_norm(z, lng_ref[...], lnb_ref[...])
    xo_ref[...] = xn
    h2 = xn * (1.0 + m4_ref[...]) + m3_ref[...]
    h2_ref[...] = _pack_rows(h2)
    hh, hl = _split(h2)
    lg_ref[...] = _dot_nt(rwh_ref[...], hh) + _dot_nt(rwh_ref[...], hl) + _dot_nt(rwl_ref[...], hh)


def _post_even_kernel(nac_ref, nal_ref, of_ref, ob_ref, hg_ref, ng_ref, w_ref, *rest, alpha, ctx_tiles):
    na = jnp.where(pl.program_id(0) < ctx_tiles, nac_ref[...], nal_ref[...])
    o = of_ref[...] + ob_ref[...]
    wd = o.shape[1]
    r_i = lax.broadcasted_iota(I32, (wd, wd), 0) // HEAD_DIM
    c_i = lax.broadcasted_iota(I32, (wd, wd), 1) // HEAD_DIM
    avg = jnp.where(r_i == c_i, 1.0 / HEAD_DIM, 0.0).astype(BF16)
    sh, sl = _split(o * o)
    ms = _dot(sh, avg) + _dot(sl, avg)
    r = o * lax.rsqrt(ms + RMS_EPS) * ng_ref[...] * _silu(hg_ref[...])
    mix = _dot(na, w_ref[:wd, :]) + _dot(r.astype(BF16), w_ref[wd:, :])
    _post_mixer_epilogue(mix, *rest, alpha=alpha)


def _post_odd_kernel(hf_ref, hb_ref, y_ref, w_ref, *rest, alpha):
    m = ((hf_ref[...] + hb_ref[...]) * _gelu_tanh(y_ref[...])).astype(BF16)
    _post_mixer_epilogue(_dot(m, w_ref[...]), *rest, alpha=alpha)


def _post_mixer(dm, mixer_inputs, w_out, xs, m2, m3, m4, lng, lnb, rwh, rwl, alpha, even):
    T, D, TM = dm.T, dm.D, dm.TM
    E = rwh.shape[0]
    row = lambda a: pl.BlockSpec((TM, a.shape[1]), lambda i: (i, 0))
    const = lambda a: pl.BlockSpec(a.shape, lambda i: (0,) * a.ndim)
    mod_spec = pl.BlockSpec((None, 1, D), lambda i: (dm.group(i, TM), 0, 0))
    if even:
        nac, nal, of, ob, hg, ng = mixer_inputs
        nct = dm.T_ctx // TM
        ins = [nac, nal, of, ob, hg, ng, w_out]
        specs = [pl.BlockSpec((TM, nac.shape[1]), lambda i: (jnp.minimum(i, nct - 1), 0)),
                 pl.BlockSpec((TM, nal.shape[1]), lambda i: (jnp.maximum(i - nct, 0), 0)),
                 row(of), row(ob), row(hg), const(ng), const(w_out)]
        kern = functools.partial(_post_even_kernel, alpha=alpha, ctx_tiles=nct)
    else:
        hf, hb, y = mixer_inputs
        ins = [hf, hb, y, w_out]
        specs = [row(hf), row(hb), row(y), const(w_out)]
        kern = functools.partial(_post_odd_kernel, alpha=alpha)
    ins += [xs, m2, m3, m4, lng, lnb, rwh, rwl]
    specs += [row(xs), mod_spec, mod_spec, mod_spec, const(lng), const(lnb), const(rwh), const(rwl)]
    return pl.pallas_call(
        kern,
        grid=(T // TM,),
        in_specs=specs,
        out_specs=[pl.BlockSpec((TM, D), lambda i: (i, 0)), pl.BlockSpec((TM, D // 2), lambda i: (i, 0)),
                   pl.BlockSpec((E, TM), lambda i: (0, i))],
        out_shape=[jax.ShapeDtypeStruct((T, D), F32), jax.ShapeDtypeStruct((T, D // 2), jnp.uint32),
                   jax.ShapeDtypeStruct((E, T), F32)],
        compiler_params=_cparams(("parallel",)),
        name="post_mixer_even" if even else "post_mixer_odd",
    )(*ins)


def _route_kernel(lg_ref, bias_ref, tri_ref, idx_ref, rank_ref, w_ref, cnt_ref, carry_ref):
    i = pl.program_id(0)

    @pl.when(i == 0)
    def _():
        carry_ref[...] = jnp.zeros_like(carry_ref)

    E, TK = lg_ref.shape
    s = jax.nn.sigmoid(lg_ref[...])
    work = s + bias_ref[...]
    eio = lax.broadcasted_iota(I32, (E, TK), 0).astype(F32)
    picked = jnp.zeros((E, TK), F32)
    sels, idxs = [], []
    for _ in range(TOP_K):
        m = jnp.max(work, axis=0, keepdims=True)
        ik = jnp.min(jnp.where(work == m, eio, float(E)), axis=0, keepdims=True)
        oh = eio == ik
        sels.append(jnp.sum(jnp.where(oh, s, 0.0), axis=0, keepdims=True))
        idxs.append(ik)
        picked = jnp.where(oh, 1.0, picked)
        work = jnp.where(oh, -jnp.inf, work)
    total = sels[0]
    for sk in sels[1:]:
        total = total + sk
    carry = carry_ref[:, 0:1]
    ranks = carry + _dot(picked.astype(BF16), tri_ref[...])
    out_rows = idx_ref.shape[0]
    rio = lax.broadcasted_iota(I32, (out_rows, TK), 0)
    idx_o = jnp.zeros((out_rows, TK), I32)
    rank_o = jnp.zeros((out_rows, TK), I32)
    w_o = jnp.zeros((out_rows, TK), F32)
    for kx in range(TOP_K):
        rk = jnp.sum(jnp.where(eio == idxs[kx], ranks, 0.0), axis=0, keepdims=True).astype(I32)
        idx_o = jnp.where(rio == kx, idxs[kx].astype(I32), idx_o)
        rank_o = jnp.where(rio == kx, rk, rank_o)
        w_o = jnp.where(rio == kx, sels[kx] / total * ROUTED_SCALE, w_o)
    idx_ref[...] = idx_o
    rank_ref[...] = rank_o
    w_ref[...] = w_o
    new_carry = carry + jnp.sum(picked, axis=1, keepdims=True)
    carry_ref[...] = jnp.broadcast_to(new_carry, carry_ref.shape)
    cnt_ref[...] = jnp.broadcast_to(new_carry, cnt_ref.shape)


def _route(logits_t, router_bias, t0):
    E, T = logits_t.shape
    Tm = T - t0
    TK = _pick_tile(int(np.gcd(Tm, t0)), (1024, 512, 256))
    off = t0 // TK
    tri = jnp.asarray(np.triu(np.ones((TK, TK), np.float32), 1), BF16)
    tok = lambda: pl.BlockSpec((8, TK), lambda i: (0, i))
    idx, rank, w, cnt = pl.pallas_call(
        _route_kernel,
        grid=(Tm // TK,),
        in_specs=[pl.BlockSpec((E, TK), lambda i: (0, i + off)),
                  pl.BlockSpec((E, 1), lambda i: (0, 0)),
                  pl.BlockSpec((TK, TK), lambda i: (0, 0))],
        out_specs=[tok(), tok(), tok(), pl.BlockSpec((E, 128), lambda i: (0, 0))],
        out_shape=[jax.ShapeDtypeStruct((8, Tm), I32), jax.ShapeDtypeStruct((8, Tm), I32),
                   jax.ShapeDtypeStruct((8, Tm), F32), jax.ShapeDtypeStruct((E, 128), F32)],
        scratch_shapes=[pltpu.VMEM((E, 128), F32)],
        compiler_params=_cparams(("arbitrary",)),
        name="moe_route",
    )(logits_t, router_bias.reshape(E, 1), tri)
    return idx, rank, w, cnt[:, 0].astype(I32)


def _sorted_tile_rows(n_exp):
    return -(-(MOE_TILE * TOP_K + n_exp * (ROW_ALIGN - 1)) // 16) * 16


def _run_copies(cnt_ref, src_ref, dst_ref, base, n_exp, chunk, make_copy):
    shift = chunk.bit_length() - 1

    def expert_body(e, total):
        n = cnt_ref[base + e]
        src = src_ref[base + e]
        dst = dst_ref[base + e]
        nch = lax.shift_right_logical(n + (chunk - 1), shift)

        def chunk_body(c, carry):
            make_copy(pl.multiple_of(src + c * chunk, ROW_ALIGN), pl.multiple_of(dst + c * chunk, ROW_ALIGN)).start()
            return carry

        lax.fori_loop(0, nch, chunk_body, 0)
        return total + nch

    return lax.fori_loop(0, n_exp, expert_body, jnp.int32(0), unroll=4)


def _dispatch_kernel(cnt_ref, ls_ref, gb_ref, pos_ref, h_ref, xs_hbm, stage, pending, sem, *, n_exp):
    i = pl.program_id(0)
    slot = i % 2
    TMd = h_ref.shape[0]
    R = stage.shape[1] - DISP_CHUNK

    @pl.when(i == 0)
    def _():
        pending[0] = 0
        for s in range(2):
            stage[s, R:, :] = jnp.zeros((stage.shape[1] - R, stage.shape[2]), stage.dtype)

    rio = lax.broadcasted_iota(I32, (R, TMd), 0)
    perm = jnp.zeros((R, TMd), F32)
    for kx in range(TOP_K):
        perm = jnp.where(rio == pos_ref[kx:kx + 1, :], 1.0, perm)
    stage[slot, 0:R, :] = _pack_rows(_dot(perm.astype(BF16), _unpack_rows(h_ref[...])), is_bf16_valued=True)

    def make_copy(src, dst):
        return pltpu.make_async_copy(stage.at[slot, pl.ds(src, DISP_CHUNK)], xs_hbm.at[pl.ds(dst, DISP_CHUNK)], sem)

    def wait_copies(n):
        def wait_body(c, carry):
            make_copy(0, 0).wait()
            return carry
        lax.fori_loop(0, n, wait_body, 0)

    wait_copies(pending[0])
    pending[0] = _run_copies(cnt_ref, ls_ref, gb_ref, i * n_exp, n_exp, DISP_CHUNK, make_copy)

    @pl.when(i == pl.num_programs(0) - 1)
    def _():
        wait_copies(pending[0])


def _dispatch(h2, plan, t0, n_slots, n_exp):
    cnt, ls, gb, pos = plan
    T, DP = h2.shape
    TMd = MOE_TILE
    Tm = T - t0
    assert Tm % TMd == 0 and t0 % TMd == 0
    off = t0 // TMd
    grid_spec = pltpu.PrefetchScalarGridSpec(
        num_scalar_prefetch=3,
        grid=(Tm // TMd,),
        in_specs=[pl.BlockSpec((8, TMd), lambda i, *_: (0, i)),
                  pl.BlockSpec((TMd, DP), lambda i, *_: (i + off, 0))],
        out_specs=pl.BlockSpec(memory_space=pl.ANY),
        scratch_shapes=[pltpu.VMEM((2, _sorted_tile_rows(n_exp) + DISP_CHUNK, DP), h2.dtype),
                        pltpu.SMEM((1,), I32), pltpu.SemaphoreType.DMA],
    )
    return pl.pallas_call(
        functools.partial(_dispatch_kernel, n_exp=n_exp),
        grid_spec=grid_spec,
        out_shape=jax.ShapeDtypeStruct((n_slots, DP), h2.dtype),
        compiler_params=_cparams(("arbitrary",)),
        name="moe_dispatch",
    )(cnt, ls, gb, pos, h2)


def _gmm_kernel(bexp_ref, bvalid_ref, x_ref, wg_ref, wu_ref, wd_ref, y_ref):
    del bexp_ref
    i = pl.program_id(0)
    nvalid = bvalid_ref[i]

    @pl.when(nvalid > 0)
    def _():
        row = lax.broadcasted_iota(I32, x_ref.shape, 0)
        xb = _unpack_rows(jnp.where(row < nvalid, x_ref[...], jnp.uint32(0)))
        act = _silu(_dot(xb, wg_ref[...].astype(BF16))) * _dot(xb, wu_ref[...].astype(BF16))
        y_ref[...] = _pack_rows(_dot(act.astype(BF16), wd_ref[...].astype(BF16)))

    @pl.when(nvalid == 0)
    def _():
        y_ref[...] = jnp.zeros_like(y_ref)


def _grouped_experts(xsorted, block_exp, block_valid, wg, wu, wd, layer):
    NS, DP = xsorted.shape
    nb = NS // MOE_BLOCK
    _, _, D, hid = wg.shape
    grid_spec = pltpu.PrefetchScalarGridSpec(
        num_scalar_prefetch=2,
        grid=(nb,),
        in_specs=[pl.BlockSpec((MOE_BLOCK, DP), lambda i, be, bv: (i, 0)),
                  pl.BlockSpec((None, None, D, hid), lambda i, be, bv: (layer, be[i], 0, 0)),
                  pl.BlockSpec((None, None, D, hid), lambda i, be, bv: (layer, be[i], 0, 0)),
                  pl.BlockSpec((None, None, hid, D), lambda i, be, bv: (layer, be[i], 0, 0))],
        out_specs=pl.BlockSpec((MOE_BLOCK, DP), lambda i, be, bv: (i, 0)),
    )
    return pl.pallas_call(
        _gmm_kernel,
        grid_spec=grid_spec,
        out_shape=jax.ShapeDtypeStruct((NS, DP), jnp.uint32),
        compiler_params=_cparams(("arbitrary",)),
        name="moe_grouped_experts",
    )(block_exp, block_valid, xsorted, wg, wu, wd)


def _combine_kernel(cnt_ref, so_ref, gb_ref, y_hbm, qpos_ref, w_ref, h_ref, sgu_ref, sd_ref, x_ref, m5_ref,
                    lng_ref, lnb_ref, o_ref, stage, pending, sems, *, alpha, n_exp):
    i = pl.program_id(0)
    slot = i % 2

    def make_copy(s):
        def f(dst, src):
            return pltpu.make_async_copy(y_hbm.at[pl.ds(src, COMB_CHUNK)], stage.at[s, pl.ds(dst, COMB_CHUNK)],
                                         sems.at[s])
        return f

    def fetch(tile, s):
        pending[s] = _run_copies(cnt_ref, so_ref, gb_ref, tile * n_exp, n_exp, COMB_CHUNK, make_copy(s))

    @pl.when(i == 0)
    def _():
        stage[...] = jnp.zeros_like(stage)
        fetch(0, 0)

    @pl.when(i + 1 < pl.num_programs(0))
    def _():
        fetch(i + 1, 1 - slot)

    hid = sd_ref.shape[0]
    gu = _dot(_unpack_rows(h_ref[...]), sgu_ref[...])
    y = _dot((_silu(gu[:, :hid]) * gu[:, hid:]).astype(BF16), sd_ref[...])

    def wait_body(c, carry):
        make_copy(slot)(0, 0).wait()
        return carry

    lax.fori_loop(0, pending[slot], wait_body, 0)

    TMc, S = h_ref.shape[0], stage.shape[1]
    sio = lax.broadcasted_iota(I32, (TMc, S), 1)
    q = jnp.zeros((TMc, S), F32)
    for kx in range(TOP_K):
        q = jnp.where(sio == qpos_ref[:, kx:kx + 1], w_ref[:, kx:kx + 1], q)
    y = y + _dot(q.astype(BF16), _unpack_rows(stage[slot]))
    z = alpha * x_ref[...] + m5_ref[...] * y
    o_ref[...] = _layer_norm(z, lng_ref[...], lnb_ref[...])


def _combine(dm, plan, ysorted, qpos_tm, w_tm, h2, sgu, sd, xs, m5, lng, lnb, alpha, t0, n_exp):
    cnt, so, gb = plan
    T, D = xs.shape
    TMc = MOE_TILE
    Tm = T - t0
    off = t0 // TMc
    const = lambda a: pl.BlockSpec(a.shape, lambda i, *_: (0,) * a.ndim)
    row = lambda a: pl.BlockSpec((TMc, a.shape[1]), lambda i, *_: (i + off, 0))
    grid_spec = pltpu.PrefetchScalarGridSpec(
        num_scalar_prefetch=3,
        grid=(Tm // TMc,),
        in_specs=[pl.BlockSpec(memory_space=pl.ANY),
                  pl.BlockSpec((TMc, 8), lambda i, *_: (i, 0)), pl.BlockSpec((TMc, 8), lambda i, *_: (i, 0)),
                  row(h2), const(sgu), const(sd), row(xs),
                  pl.BlockSpec((None, 1, D), lambda i, *_: (dm.group(i + off, TMc), 0, 0)),
                  const(lng), const(lnb)],
        out_specs=pl.BlockSpec((TMc, D), lambda i, *_: (i, 0)),
        scratch_shapes=[pltpu.VMEM((2, _sorted_tile_rows(n_exp), ysorted.shape[1]), ysorted.dtype),
                        pltpu.SMEM((2,), I32), pltpu.SemaphoreType.DMA((2,))],
    )
    return pl.pallas_call(
        functools.partial(_combine_kernel, alpha=alpha, n_exp=n_exp),
        grid_spec=grid_spec,
        out_shape=jax.ShapeDtypeStruct((Tm, D), F32),
        compiler_params=_cparams(("arbitrary",)),
        name="moe_combine",
    )(cnt, so, gb, ysorted, qpos_tm, w_tm, h2, sgu, sd, xs, m5, lng, lnb)


def _moe(dm, h2, logits_t, xs, m5, lng, lnb, router_bias, wg, wu, wd, layer, sgu, sd, alpha, t0):
    T = h2.shape[0]
    E = wg.shape[1]
    Tm = T - t0
    nt = Tm // MOE_TILE
    idx, rank, w, _ = _route(logits_t, router_bias, t0)
    idx, rank, w = idx[:TOP_K], rank[:TOP_K], w[:TOP_K]
    eids = jnp.arange(E, dtype=I32)
    onehot = (idx[:, :, None] == eids).reshape(TOP_K, nt, MOE_TILE, E)
    cnt = jnp.sum(onehot, axis=(0, 2), dtype=I32)
    rank_base = jnp.cumsum(cnt, axis=0) - cnt
    run = (cnt + ROW_ALIGN - 1) // ROW_ALIGN * ROW_ALIGN
    tile_base = jnp.cumsum(run, axis=0) - run
    local_start = jnp.cumsum(run, axis=1) - run
    counts = jnp.sum(run, axis=0)
    nb = -(-(Tm * TOP_K + E * (nt * (ROW_ALIGN - 1) + MOE_SLACK)) // MOE_BLOCK) + E
    padded = (counts + MOE_SLACK + MOE_BLOCK - 1) // MOE_BLOCK * MOE_BLOCK
    pends = jnp.cumsum(padded)
    pstarts = pends - padded
    global_base = pstarts[None, :] + tile_base
    bstart = jnp.arange(nb, dtype=I32) * MOE_BLOCK
    block_exp = jnp.minimum(jnp.sum((pends[None, :] <= bstart[:, None]).astype(I32), axis=1), E - 1)
    of_block = block_exp[:, None] == eids[None, :]
    seg_end = jnp.sum(jnp.where(of_block, (pstarts + counts)[None, :], 0), axis=1)
    block_valid = jnp.clip(seg_end - bstart, 0, MOE_BLOCK).astype(I32)
    look = lambda tab: jnp.sum(jnp.where(onehot, tab[None, :, None, :], 0), axis=3).reshape(TOP_K, Tm)
    pos = rank - look(rank_base) + look(local_start)
    pos8 = jnp.pad(pos, ((0, 8 - TOP_K), (0, 0)), constant_values=-1).astype(I32)
    pos_tm = jnp.pad(pos.T, ((0, 0), (0, 8 - TOP_K)), constant_values=-1).astype(I32)
    w_tm = jnp.pad(w.T, ((0, 0), (0, 8 - TOP_K)))
    flat = lambda a: a.reshape(-1).astype(I32)
    plan = (flat(run), flat(local_start), flat(global_base))
    xsorted = _dispatch(h2, plan + (pos8,), t0, nb * MOE_BLOCK, E)
    ysorted = _grouped_experts(xsorted, block_exp, block_valid, wg, wu, wd, layer)
    return _combine(dm, plan, ysorted, pos_tm, w_tm, h2, sgu, sd, xs, m5, lng, lnb, alpha, t0, E)


def kernel(x, c, ctx, c_ctx, ada_w, ada_b, ln_g, ln_b, ev_w_in, ev_w_out, na_rpb, hg_lb_raw, hg_norm_g, od_w_in, od_conv_w, od_conv_b, lru_w_a, lru_b_a, lru_w_x, lru_b_x, lru_lam, od_w_out, router_w, router_bias, exp_w_gate, exp_w_up, exp_w_down, sh_w_gate, sh_w_up, sh_w_down):
    B, N, D = x.shape
    C = ctx.shape[1]
    depth = ada_w.shape[0]
    dm = _Dims(B, N, C, D)
    alpha = float((2 * depth) ** 0.25)
    assert B + 1 <= 8

    cvec = jnp.zeros((8, D), F32).at[:B].set(c).at[B].set(c_ctx)
    mod = _modulation(cvec, ada_w, ada_b).reshape(depth, 8, 6, 1, D)

    p_lb = jax.nn.softmax(hg_lb_raw.astype(F32), axis=1)
    hg_lb = jnp.cumsum(p_lb, axis=1) - p_lb[:, :1]

    xs = jnp.concatenate([ctx.reshape(B * C, D), x.reshape(B * N, D)], axis=0)
    for l in range(depth):
        jl = l // 2
        last = l == depth - 1
        m = [mod[l, :, t] for t in range(6)]
        rw_t = router_w[l].T
        rwh, rwl = _split(rw_t)
        lng = ln_g[l][:, None, :]
        lnb = ln_b[l][:, None, :]
        if l % 2 == 0:
            q, k, v, hq, ff, fb, hi, hg = _inproj(dm, xs, m[1], m[0], ev_w_in[jl].astype(BF16), True)
            bias = _na_bias_tables(na_rpb[jl], N // GRID_W)
            na_lat = _neighbourhood_attention(dm, q, k, v, bias)
            na_ctx = _context_attention(dm, q, k, v)
            o_f, o_b = _hgrn2(dm, hq, ff, fb, hi, hg_lb[:, jl])
            ng = jnp.tile(hg_norm_g[jl], hq.shape[1] // HEAD_DIM)[None, :]
            mixer_inputs = (na_ctx, na_lat, o_f, o_b, hg, ng)
            w_out = ev_w_out[jl].astype(BF16)
        else:
            y, u = _inproj(dm, xs, m[1], m[0], od_w_in[jl].astype(BF16), False)
            h_f, h_b = _rglru(dm, u, od_conv_w[jl], od_conv_b[jl], lru_w_a[jl], lru_b_a[jl],
                              lru_w_x[jl], lru_b_x[jl], lru_lam[jl])
            mixer_inputs = (h_f, h_b, y)
            w_out = od_w_out[jl].astype(BF16)
        xs, h2, logits_t = _post_mixer(dm, mixer_inputs, w_out, xs, m[2], m[3], m[4], lng[0], lnb[0],
                                       rwh, rwl, alpha, l % 2 == 0)
        sgu = jnp.concatenate([sh_w_gate[l], sh_w_up[l]], axis=-1).astype(BF16)
        sd = sh_w_down[l].astype(BF16)
        t0 = dm.T_ctx if last else 0
        xs = _moe(dm, h2, logits_t, xs, m[5], lng[1], lnb[1], router_bias[l], exp_w_gate, exp_w_up,
                  exp_w_down, l, sgu, sd, alpha, t0)
    return xs.reshape(B, N, D)
```

```python
import functools

import numpy as np
import jax
import jax.numpy as jnp
from jax import lax
from jax.experimental import pallas as pl
from jax.experimental.pallas import tpu as pltpu

F32 = jnp.float32
BF16 = jnp.bfloat16
I32 = jnp.int32

HEAD_DIM = 64
GRID_W = 64
NA_WIN_ROWS = 8
NA_WIN_COLS = 16
NA_QROWS = 4
NA_KGROUP_ROWS = 4
NA_KROWS = 12
HG_CHUNK = 64
HG_HEADS_PER_GROUP = 4
FORGET_FLOOR = 1e-30
HG_SAFE_EXP = 80.0
HG_CLIP_EXP = 85.0
LRU_C = 8.0
LRU_BLOCKS = 16
CONV_W = 4
TOP_K = 6
ROUTED_SCALE = 2.5
LN_EPS = 1e-5
RMS_EPS = 1e-6
SEQ_BLOCK = 256
MOE_BLOCK = 512
MOE_TILE = 256
ROW_ALIGN = 8
DISP_CHUNK = 32
COMB_CHUNK = ROW_ALIGN
MOE_SLACK = DISP_CHUNK
MASK_VALUE = -1e30
VMEM_LIMIT = 56 * 1024 * 1024


def _cparams(sem):
    return pltpu.CompilerParams(dimension_semantics=sem, vmem_limit_bytes=VMEM_LIMIT)


def _split(a):
    hi = a.astype(BF16)
    lo = (a - hi.astype(F32)).astype(BF16)
    return hi, lo


def _dot(a, b):
    return jnp.dot(a, b, preferred_element_type=F32)


def _dot_nt(a, b):
    return lax.dot_general(a, b, (((1,), (1,)), ((), ())), preferred_element_type=F32)


def _dot_tn(a, b):
    return lax.dot_general(a, b, (((0,), (0,)), ((), ())), preferred_element_type=F32)


def _dot3(a, b):
    ah, al = _split(a)
    bh, bl = _split(b)
    return _dot(ah, bh) + _dot(al, bh) + _dot(ah, bl)


def _pack_rows(h, is_bf16_valued=False):
    half = h.shape[1] // 2
    if not is_bf16_valued:
        h = h.astype(BF16).astype(F32)
    bits = pltpu.bitcast(h, jnp.uint32)
    return (bits[:, :half] >> 16) | (bits[:, half:] & jnp.uint32(0xFFFF0000))


def _unpack_rows(p):
    lo = pltpu.bitcast(p << 16, F32)
    hi = pltpu.bitcast(p & jnp.uint32(0xFFFF0000), F32)
    return jnp.concatenate([lo, hi], axis=1).astype(BF16)


def _silu(v):
    return v * jax.nn.sigmoid(v)


def _gelu_tanh(v):
    return 0.5 * v * (1.0 + jnp.tanh(0.7978845608028654 * (v + 0.044715 * v * v * v)))


def _layer_norm(z, g, b):
    mu = jnp.mean(z, axis=-1, keepdims=True)
    zc = z - mu
    var = jnp.mean(zc * zc, axis=-1, keepdims=True)
    return zc * lax.rsqrt(var + LN_EPS) * g + b


def _pick_tile(n, cands):
    for c in cands:
        if n % c == 0:
            return c
    raise ValueError(f"no tile for {n}")


class _Dims:
    def __init__(self, B, N, C, D):
        self.B, self.N, self.C, self.D = B, N, C, D
        self.T_ctx = B * C
        self.T = B * C + B * N
        assert C == SEQ_BLOCK and N % SEQ_BLOCK == 0
        assert N % GRID_W == 0 and (N // GRID_W) % NA_QROWS == 0 and N // GRID_W >= NA_KROWS
        assert self.T_ctx % 512 == 0
        self.TM = _pick_tile(self.T_ctx, (512, 256))
        assert N % self.TM == 0

    def group(self, i, tile):
        start = i * tile
        return jnp.where(start < self.T_ctx, self.B, (start - self.T_ctx) // self.N)


def _mod_kernel(c_ref, w_ref, b_ref, o_ref):
    o_ref[...] = _dot3(_silu(c_ref[...]), w_ref[...]) + b_ref[...]


def _modulation(cvec, ada_w, ada_b):
    L, D, W6 = ada_w.shape
    nc = 1536
    return pl.pallas_call(
        _mod_kernel,
        grid=(L, W6 // nc),
        in_specs=[
            pl.BlockSpec((8, D), lambda l, j: (0, 0)),
            pl.BlockSpec((None, D, nc), lambda l, j: (l, 0, j)),
            pl.BlockSpec((None, 1, nc), lambda l, j: (l, 0, j)),
        ],
        out_specs=pl.BlockSpec((None, 8, nc), lambda l, j: (l, 0, j)),
        out_shape=jax.ShapeDtypeStruct((L, 8, W6), F32),
        compiler_params=_cparams(("arbitrary", "arbitrary")),
        name="adaln_modulation",
    )(cvec, ada_w, ada_b.reshape(L, 1, W6))


def _inproj_even_kernel(x_ref, sc_ref, sh_ref, w_ref, q_ref, k_ref, v_ref, hq_ref, ff_ref, fb_ref, hi_ref, hg_ref):
    h = (x_ref[...] * (1.0 + sc_ref[...]) + sh_ref[...]).astype(BF16)
    wd = q_ref.shape[1]

    def part(j):
        return _dot(h, w_ref[:, j * wd:(j + 1) * wd])

    scale = HEAD_DIM ** -0.5
    q_ref[...] = (part(0) * scale).astype(BF16)
    k_ref[...] = part(1).astype(BF16)
    v_ref[...] = part(2).astype(BF16)
    hq_ref[...] = _silu(part(3)) * scale
    ff_ref[...] = part(4)
    fb_ref[...] = part(5)
    hi_ref[...] = part(6)
    hg_ref[...] = part(7)


def _inproj_odd_kernel(x_ref, sc_ref, sh_ref, w_ref, y_ref, u_ref):
    h = (x_ref[...] * (1.0 + sc_ref[...]) + sh_ref[...]).astype(BF16)
    wd = y_ref.shape[1]
    y_ref[...] = _dot(h, w_ref[:, :wd])
    u_ref[...] = _dot(h, w_ref[:, wd:])


def _inproj(dm, xs, sc, sh, w, even):
    T, D, TM = dm.T, dm.D, dm.TM
    wtot = w.shape[1]
    mod_spec = pl.BlockSpec((None, 1, D), lambda i: (dm.group(i, TM), 0, 0))
    in_specs = [pl.BlockSpec((TM, D), lambda i: (i, 0)), mod_spec, mod_spec,
                pl.BlockSpec((D, wtot), lambda i: (0, 0))]
    if even:
        wd = wtot // 8
        dts = [BF16] * 3 + [F32] * 5
        kern = _inproj_even_kernel
    else:
        wd = wtot // 2
        dts = [F32] * 2
        kern = _inproj_odd_kernel
    return pl.pallas_call(
        kern,
        grid=(T // TM,),
        in_specs=in_specs,
        out_specs=[pl.BlockSpec((TM, wd), lambda i: (i, 0)) for _ in dts],
        out_shape=[jax.ShapeDtypeStruct((T, wd), dt) for dt in dts],
        compiler_params=_cparams(("parallel",)),
        name="inproj_even" if even else "inproj_odd",
    )(xs, sc, sh, w)


def _na_bias_tables(rpb, rows):
    W = GRID_W
    H = rpb.shape[0]
    nr, nc = 2 * NA_WIN_ROWS - 1, 2 * NA_WIN_COLS - 1
    hp = lax.Precision.HIGHEST
    c = np.arange(W)[:, None]
    kc = np.arange(W)[None, :]
    c0 = np.clip(c - NA_WIN_COLS // 2, 0, W - NA_WIN_COLS)
    valid_c = (kc >= c0) & (kc < c0 + NA_WIN_COLS)
    dc = np.clip(kc - c + NA_WIN_COLS - 1, 0, nc - 1)
    oh_c = (dc.reshape(-1, 1) == np.arange(nc)[None, :]).astype(np.float32)
    by_col = jnp.einsum('hrs,xs->hrx', rpb.astype(F32), jnp.asarray(oh_c), precision=hp)
    tabs = []
    for rbase, kb in ((0, 0), (NA_QROWS, NA_QROWS - NA_WIN_ROWS // 2), (rows - NA_QROWS, rows - NA_KROWS)):
        r = rbase + np.arange(NA_QROWS)[:, None]
        kr = kb + np.arange(NA_KROWS)[None, :]
        r0 = np.clip(r - NA_WIN_ROWS // 2, 0, rows - NA_WIN_ROWS)
        valid_r = (kr >= r0) & (kr < r0 + NA_WIN_ROWS)
        dr = np.clip(kr - r + NA_WIN_ROWS - 1, 0, nr - 1)
        oh_r = (dr.reshape(-1, 1) == np.arange(nr)[None, :]).astype(np.float32)
        t = jnp.einsum('yr,hrx->hyx', jnp.asarray(oh_r), by_col, precision=hp)
        t = t.reshape(H, NA_QROWS, NA_KROWS, W, W).transpose(0, 1, 3, 2, 4)
        valid = valid_r[:, None, :, None] & valid_c[None, :, None, :]
        t = jnp.where(jnp.asarray(valid)[None], t, MASK_VALUE)
        tabs.append(t.reshape(H, NA_QROWS * W, NA_KROWS * W))
    return jnp.stack(tabs).astype(F32)


def _na_kernel(q_ref, *refs):
    nkb = NA_KROWS // NA_KGROUP_ROWS
    k_refs, v_refs = refs[:nkb], refs[nkb:2 * nkb]
    kc_ref, vc_ref, bias_ref, o_ref = refs[2 * nkb:]
    q2 = q_ref[...]
    lane = lax.broadcasted_iota(I32, (1, q2.shape[1]), 1)
    first = lane < HEAD_DIM
    ks = [r[...] for r in k_refs]
    vs = [r[...] for r in v_refs]
    kc = kc_ref[...]
    vc = vc_ref[...]
    kb = ks[0].shape[0]
    outs = []
    for hh in range(2):
        sel = first if hh == 0 else jnp.logical_not(first)
        qh = jnp.where(sel, q2, jnp.zeros_like(q2))
        s_loc = jnp.concatenate([_dot_nt(qh, kj) for kj in ks], axis=1) + bias_ref[hh]
        s_ctx = _dot_nt(qh, kc)
        m = jnp.maximum(jnp.max(s_loc, axis=1, keepdims=True), jnp.max(s_ctx, axis=1, keepdims=True))
        p_loc = jnp.exp(s_loc - m)
        p_ctx = jnp.exp(s_ctx - m)
        denom = jnp.sum(p_loc, axis=1, keepdims=True) + jnp.sum(p_ctx, axis=1, keepdims=True)
        o = _dot(p_ctx.astype(BF16), vc)
        for j in range(nkb):
            o = o + _dot(p_loc[:, j * kb:(j + 1) * kb].astype(BF16), vs[j])
        outs.append(o / denom)
    o_ref[...] = jnp.where(first, outs[0], outs[1]).astype(o_ref.dtype)


def _neighbourhood_attention(dm, q, k, v, bias):
    B, N, C, T = dm.B, dm.N, dm.C, dm.T
    H2 = q.shape[1] // (2 * HEAD_DIM)
    rows = N // GRID_W
    nqb = rows // NA_QROWS
    QB = NA_QROWS * GRID_W
    KB = NA_KGROUP_ROWS * GRID_W
    nkb = NA_KROWS // NA_KGROUP_ROWS
    ngroups = N // KB
    lat_q0 = dm.T_ctx // QB
    lat_k0 = dm.T_ctx // KB

    def q_map(p, i, b):
        return (lat_q0 + b * (N // QB) + i, p)

    def o_map(p, i, b):
        return (b * (N // QB) + i, p)

    def kv_map(j):
        def f(p, i, b):
            gs = jnp.clip(i * (QB // KB) - 1, 0, ngroups - nkb)
            return (lat_k0 + b * ngroups + gs + j, p)
        return f

    def ctx_map(p, i, b):
        return (b, p)

    def bias_map(p, i, b):
        var = jnp.where(i == 0, 0, jnp.where(i == nqb - 1, 2, 1))
        return (var, p, 0, 0)

    lanes = 2 * HEAD_DIM
    kv_specs = [pl.BlockSpec((KB, lanes), kv_map(j)) for j in range(nkb)]
    return pl.pallas_call(
        _na_kernel,
        grid=(H2, nqb, B),
        in_specs=[pl.BlockSpec((QB, lanes), q_map)] + kv_specs + kv_specs
        + [pl.BlockSpec((C, lanes), ctx_map), pl.BlockSpec((C, lanes), ctx_map),
           pl.BlockSpec((None, 2, QB, NA_KROWS * GRID_W), bias_map)],
        out_specs=pl.BlockSpec((QB, lanes), o_map),
        out_shape=jax.ShapeDtypeStruct((B * N, q.shape[1]), BF16),
        compiler_params=_cparams(("arbitrary", "arbitrary", "arbitrary")),
        name="neighbourhood_attention",
    )(q, *([k] * nkb), *([v] * nkb), k, v, bias)


def _ctx_attn_kernel(q_ref, k_ref, v_ref, o_ref):
    q2 = q_ref[...]
    k2 = k_ref[...]
    v2 = v_ref[...]
    lane = lax.broadcasted_iota(I32, (1, q2.shape[1]), 1)
    first = lane < HEAD_DIM
    outs = []
    for hh in range(2):
        sel = first if hh == 0 else jnp.logical_not(first)
        s = _dot_nt(jnp.where(sel, q2, jnp.zeros_like(q2)), k2)
        p = jnp.exp(s - jnp.max(s, axis=1, keepdims=True))
        outs.append(_dot(p.astype(BF16), v2) / jnp.sum(p, axis=1, keepdims=True))
    o_ref[...] = jnp.where(first, outs[0], outs[1]).astype(o_ref.dtype)


def _context_attention(dm, q, k, v):
    lanes = 2 * HEAD_DIM
    spec = pl.BlockSpec((dm.C, lanes), lambda p, b: (b, p))
    return pl.pallas_call(
        _ctx_attn_kernel,
        grid=(q.shape[1] // lanes, dm.B),
        in_specs=[spec, spec, spec],
        out_specs=spec,
        out_shape=jax.ShapeDtypeStruct((dm.T_ctx, q.shape[1]), BF16),
        compiler_params=_cparams(("arbitrary", "arbitrary")),
        name="context_attention",
    )(q, k, v)


def _seq_block_maps(dm):
    nl = dm.N // SEQ_BLOCK
    base = dm.T_ctx // SEQ_BLOCK

    def fwd(b, j):
        return jnp.where(j == 0, b, base + b * nl + j - 1)

    def bwd(b, j):
        return jnp.where(j == 0, b, base + b * nl + nl - j)

    return fwd, bwd


def _hgrn_prepare(hq, z, lb, tri, last_row, mid_row):
    f = lb + (1.0 - lb) * jax.nn.sigmoid(z)
    fm = jnp.maximum(f, FORGET_FLOOR)
    g = jnp.log(fm)
    kk = 1.0 - fm
    gh, gl = _split(g)
    cum = _dot(tri, gh) + _dot(tri, gl)
    tail = cum[last_row:last_row + 1, :]
    e = cum - cum[mid_row:mid_row + 1, :]
    emax = jnp.max(jnp.max(jnp.abs(e), axis=1, keepdims=True), axis=0, keepdims=True)
    ec = jnp.clip(e, -HG_CLIP_EXP, HG_CLIP_EXP)
    qe = (hq * jnp.exp(ec)).astype(BF16)
    ke = (kk * jnp.exp(-ec)).astype(BF16)
    return kk, cum, tail, emax, qe, ke


def _hgrn_kernel(hqf_ref, zf_ref, vf_ref, hqb_ref, zb_ref, vb_ref, lbf_ref, lbb_ref,
                 of_ref, ob_ref, st_ref, a_ref, cum_ref, k_ref, qs_ref, k2_ref, et_ref):
    j = pl.program_id(1)

    @pl.when(j == 0)
    def _():
        st_ref[...] = jnp.zeros_like(st_ref)

    CH = HG_CHUNK
    GW = HG_HEADS_PER_GROUP * HEAD_DIM
    ngroups = hqf_ref.shape[1] // GW
    nchunks = hqf_ref.shape[0] // CH
    t_i = lax.broadcasted_iota(I32, (CH, CH), 0)
    u_i = lax.broadcasted_iota(I32, (CH, CH), 1)
    tris = ((u_i <= t_i).astype(BF16), (u_i >= t_i).astype(BF16))
    t_w = lax.broadcasted_iota(I32, (CH, GW), 0)
    s_w = lax.broadcasted_iota(I32, (CH, GW), 1) % CH
    cmasks = (s_w <= t_w, s_w >= t_w)
    r_b = lax.broadcasted_iota(I32, (GW, GW), 0)
    c_b = lax.broadcasted_iota(I32, (GW, GW), 1)
    bm = (r_b // HEAD_DIM) == (c_b // HEAD_DIM)
    hsel_base = (r_b // HEAD_DIM) * HEAD_DIM
    dirs = ((hqf_ref, zf_ref, vf_ref, lbf_ref, of_ref), (hqb_ref, zb_ref, vb_ref, lbb_ref, ob_ref))
    chains = [(d, g) for d in range(2) for g in range(ngroups)]

    def expand(m):
        return jnp.where(bm, jnp.concatenate([m] * HG_HEADS_PER_GROUP, axis=0), jnp.zeros((GW, GW), m.dtype))

    nchain = len(chains)

    def rows_of(c):
        return pl.multiple_of(c * CH, CH), pl.multiple_of((nchunks - 1 - c) * CH, CH)

    def prep_body(c, worst):
        rows = rows_of(c)
        for n, (d, g) in enumerate(chains):
            hq_r, z_r, _, lb_r, _ = dirs[d]
            ls = slice(g * GW, (g + 1) * GW)
            hq = hq_r[pl.ds(rows[d], CH), ls]
            z = z_r[pl.ds(rows[d], CH), ls]
            last_row = CH - 1 if d == 0 else 0
            kk, cum, tail, emax, qe, ke = _hgrn_prepare(hq, z, lb_r[:, ls], tris[d], last_row, CH // 2)
            m = c * nchain + n
            a_ref[m] = _dot_nt(qe, expand(ke))
            cum_ref[m] = cum
            k_ref[m] = kk
            qs_ref[m] = hq * jnp.exp(cum)
            k2_ref[m] = kk * jnp.exp(tail - cum)
            et_ref[m] = jnp.broadcast_to(jnp.exp(tail), et_ref.shape[1:])
            worst = jnp.maximum(worst, emax)
        return worst

    worst = lax.fori_loop(0, nchunks, prep_body, jnp.zeros((1, 1), F32), unroll=True)

    @pl.when(worst[0, 0] > HG_SAFE_EXP)
    def _():
        def slow_body(c, carry):
            rows = rows_of(c)
            for n, (d, g) in enumerate(chains):
                m = c * nchain + n
                hq = dirs[d][0][pl.ds(rows[d], CH), g * GW:(g + 1) * GW]
                cum = cum_ref[m]

                def key_body(s, acc):
                    cs = cum_ref[m, pl.ds(s, 1), :]
                    ksr = k_ref[m, pl.ds(s, 1), :]
                    p = hq * ksr * jnp.exp(jnp.minimum(cum - cs, 0.0))
                    hsel = jnp.where(c_b == hsel_base + s, 1.0, 0.0).astype(BF16)
                    return acc + _dot(p.astype(BF16), hsel)

                a_ref[m] = lax.fori_loop(0, CH, key_body, jnp.zeros((CH, GW), F32))
            return carry

        lax.fori_loop(0, nchunks, slow_body, 0)

    def state_body(c, carry):
        rows = rows_of(c)
        for n, (d, g) in enumerate(chains):
            m = c * nchain + n
            ls = slice(g * GW, (g + 1) * GW)
            vb = dirs[d][2][pl.ds(rows[d], CH), ls].astype(BF16)
            st = st_ref[n]
            a = jnp.where(cmasks[d], a_ref[m], 0.0).astype(BF16)
            o = _dot(a, expand(vb)) + _dot_nt(qs_ref[m].astype(BF16), st.astype(BF16))
            dirs[d][4][pl.ds(rows[d], CH), ls] = o
            st_ref[n] = et_ref[m, 0:1, :] * st + jnp.where(bm, _dot_tn(vb, k2_ref[m].astype(BF16)), 0.0)
        return carry

    lax.fori_loop(0, nchunks, state_body, 0, unroll=True)


def _hgrn2(dm, hq, ff, fb, hi, lb):
    B, T = dm.B, dm.T
    Wd = hq.shape[1]
    GW = HG_HEADS_PER_GROUP * HEAD_DIM
    nchain = 2 * (Wd // GW)
    nsteps = nchain * (SEQ_BLOCK // HG_CHUNK)
    nblk = 1 + dm.N // SEQ_BLOCK
    fwd, bwd = _seq_block_maps(dm)
    fspec = pl.BlockSpec((SEQ_BLOCK, Wd), lambda b, j: (fwd(b, j), 0))
    bspec = pl.BlockSpec((SEQ_BLOCK, Wd), lambda b, j: (bwd(b, j), 0))
    lbspec = pl.BlockSpec((1, Wd), lambda b, j: (0, 0))
    return pl.pallas_call(
        _hgrn_kernel,
        grid=(B, nblk),
        in_specs=[fspec, fspec, fspec, bspec, bspec, bspec, lbspec, lbspec],
        out_specs=[fspec, bspec],
        out_shape=[jax.ShapeDtypeStruct((T, Wd), F32)] * 2,
        scratch_shapes=[pltpu.VMEM((nchain, GW, GW), F32)]
        + [pltpu.VMEM((nsteps, HG_CHUNK, GW), F32)] * 5 + [pltpu.VMEM((nsteps, 8, GW), F32)],
        compiler_params=_cparams(("arbitrary", "arbitrary")),
        name="hgrn2_bidirectional",
    )(hq, ff, hi, hq, fb, hi, lb[0:1], lb[1:2])


LRU_GROUP = 8


def _lru_group_scan(a, x, reverse):
    n, w = a.shape
    a = a.reshape(n // LRU_GROUP, LRU_GROUP, w)
    x = x.reshape(n // LRU_GROUP, LRU_GROUP, w)
    row = lax.broadcasted_iota(I32, a.shape, 1)
    s = 1
    while s < LRU_GROUP:
        shift = (LRU_GROUP - s) if reverse else s
        a_sh = pltpu.roll(a, shift, 1)
        x_sh = pltpu.roll(x, shift, 1)
        valid = (row < LRU_GROUP - s) if reverse else (row >= s)
        x = jnp.where(valid, a * x_sh + x, x)
        a = jnp.where(valid, a * a_sh, a)
        s *= 2
    return a.reshape(n, w), x.reshape(n, w)


def _lru_kernel(ucf_ref, upf_ref, unf_ref, ucb_ref, upb_ref, unb_ref, cw_ref, cb_ref, wbd_ref,
                ba_ref, bx_ref, lam_ref, hf_ref, hb_ref, ext_ref, carry_ref, sa_ref, sx_ref):
    j = pl.program_id(1)
    nblk = pl.num_programs(1)

    @pl.when(j == 0)
    def _():
        carry_ref[...] = jnp.zeros_like(carry_ref)

    TT, W = ucf_ref.shape
    GW = wbd_ref.shape[2]
    halo = upf_ref.shape[0]
    dirs = ((ucf_ref, upf_ref, unf_ref, hf_ref), (ucb_ref, upb_ref, unb_ref, hb_ref))
    for d, (uc_r, up_r, un_r, out_r) in enumerate(dirs):
        pos = j if d == 0 else jnp.where(j == 0, 0, nblk - j)
        keep_prev = jnp.where((pos == 0) | (pos == 1), 0.0, 1.0)
        keep_next = jnp.where((pos == 0) | (pos == nblk - 1), 0.0, 1.0)
        ext_ref[d, 0:halo, :] = up_r[...] * keep_prev
        ext_ref[d, halo:halo + TT, :] = uc_r[...]
        ext_ref[d, halo + TT:halo + TT + halo, :] = un_r[...] * keep_next
        for g in range(W // GW):
            ls = slice(g * GW, (g + 1) * GW)
            u = cb_ref[:, ls]
            for tap in range(CONV_W):
                off = halo - 2 + tap
                u = u + cw_ref[tap:tap + 1, ls] * ext_ref[d, off:off + TT, ls]
            gates = _dot(u.astype(BF16), wbd_ref[d, g])
            r = jax.nn.sigmoid(gates[:, :GW] + ba_ref[d:d + 1, ls])
            ig = jax.nn.sigmoid(gates[:, GW:] + bx_ref[d:d + 1, ls])
            nl = -lam_ref[d:d + 1, ls]
            softplus = jnp.maximum(nl, 0.0) + jnp.log1p(jnp.exp(-jnp.abs(nl)))
            log_a = -LRU_C * r * softplus
            a = jnp.exp(log_a)
            gain2 = 1.0 - a * a
            gain = jnp.where(gain2 > 0.0, gain2 * lax.rsqrt(gain2), 0.0)
            x_in = gain * ig * u
            a_acc, h = _lru_group_scan(a, x_in, d == 1)
            sa_ref[d, :, ls] = a_acc
            sx_ref[d, :, ls] = h

    ngroups = TT // LRU_GROUP

    def group_body(gi, carry):
        hf, hb = carry
        rf = pl.multiple_of(gi * LRU_GROUP, LRU_GROUP)
        rb = pl.multiple_of((ngroups - 1 - gi) * LRU_GROUP, LRU_GROUP)
        of = sx_ref[0, pl.ds(rf, LRU_GROUP), :] + sa_ref[0, pl.ds(rf, LRU_GROUP), :] * hf
        ob = sx_ref[1, pl.ds(rb, LRU_GROUP), :] + sa_ref[1, pl.ds(rb, LRU_GROUP), :] * hb
        hf_ref[pl.ds(rf, LRU_GROUP), :] = of
        hb_ref[pl.ds(rb, LRU_GROUP), :] = ob
        return of[LRU_GROUP - 1:LRU_GROUP, :], ob[0:1, :]

    hf, hb = lax.fori_loop(0, ngroups, group_body, (carry_ref[0, 0:1, :], carry_ref[1, 0:1, :]), unroll=4)
    carry_ref[0, 0:1, :] = hf
    carry_ref[1, 0:1, :] = hb


def _block_diag_gates(w_a, w_x, group):
    ndir, K, d, _ = w_a.shape
    per = group // d
    eye = jnp.eye(per, dtype=w_a.dtype)

    def bd(w):
        w = w.reshape(ndir, K // per, per, d, d)
        full = jnp.einsum('ngkde,kl->ngkdle', w, eye)
        return full.reshape(ndir, K // per, group, group)

    return jnp.concatenate([bd(w_a), bd(w_x)], axis=-1)


def _rglru(dm, u, conv_w, conv_b, w_a, b_a, w_x, b_x, lam):
    B, T = dm.B, dm.T
    W = u.shape[1]
    GW = 256
    halo = 8
    nblk = 1 + dm.N // SEQ_BLOCK
    fwd, bwd = _seq_block_maps(dm)
    hb = SEQ_BLOCK // halo
    nh = T // halo
    wbd = _block_diag_gates(w_a, w_x, GW).astype(BF16)

    def cur(m):
        return pl.BlockSpec((SEQ_BLOCK, W), lambda b, j: (m(b, j), 0))

    def prev(m):
        return pl.BlockSpec((halo, W), lambda b, j: (jnp.maximum(m(b, j) * hb - 1, 0), 0))

    def nxt(m):
        return pl.BlockSpec((halo, W), lambda b, j: (jnp.minimum((m(b, j) + 1) * hb, nh - 1), 0))

    def full(a):
        nd = a.ndim
        return pl.BlockSpec(a.shape, lambda b, j: (0,) * nd)

    consts = (conv_w, conv_b.reshape(1, W), wbd, b_a, b_x, lam)
    return pl.pallas_call(
        _lru_kernel,
        grid=(B, nblk),
        in_specs=[cur(fwd), prev(fwd), nxt(fwd), cur(bwd), prev(bwd), nxt(bwd)] + [full(a) for a in consts],
        out_specs=[cur(fwd), cur(bwd)],
        out_shape=[jax.ShapeDtypeStruct((T, W), F32)] * 2,
        scratch_shapes=[pltpu.VMEM((2, SEQ_BLOCK + 2 * halo, W), F32), pltpu.VMEM((2, 8, W), F32),
                        pltpu.VMEM((2, SEQ_BLOCK, W), F32), pltpu.VMEM((2, SEQ_BLOCK, W), F32)],
        compiler_params=_cparams(("arbitrary", "arbitrary")),
        name="rglru_bidirectional",
    )(u, u, u, u, u, u, *consts)


def _post_mixer_epilogue(o, x_ref, m2_ref, m3_ref, m4_ref, lng_ref, lnb_ref, rwh_ref, rwl_ref,
                         xo_ref, h2_ref, lg_ref, alpha):
    z = alpha * x_ref[...] + m2_ref[...] * o
    xn = _layer_norm(z, lng_ref[...], lnb_ref[...])
    xo_ref[...] = xn
    h2 = xn * (1.0 + m4_ref[...]) + m3_ref[...]
    h2_ref[...] = _pack_rows(h2)
    hh, hl = _split(h2)
    lg_ref[...] = _dot_nt(rwh_ref[...], hh) + _dot_nt(rwh_ref[...], hl) + _dot_nt(rwl_ref[...], hh)


def _post_even_kernel(nac_ref, nal_ref, of_ref, ob_ref, hg_ref, ng_ref, w_ref, *rest, alpha, ctx_tiles):
    na = jnp.where(pl.program_id(0) < ctx_tiles, nac_ref[...], nal_ref[...])
    o = of_ref[...] + ob_ref[...]
    wd = o.shape[1]
    r_i = lax.broadcasted_iota(I32, (wd, wd), 0) // HEAD_DIM
    c_i = lax.broadcasted_iota(I32, (wd, wd), 1) // HEAD_DIM
    avg = jnp.where(r_i == c_i, 1.0 / HEAD_DIM, 0.0).astype(BF16)
    sh, sl = _split(o * o)
    ms = _dot(sh, avg) + _dot(sl, avg)
    r = o * lax.rsqrt(ms + RMS_EPS) * ng_ref[...] * _silu(hg_ref[...])
    mix = _dot(na, w_ref[:wd, :]) + _dot(r.astype(BF16), w_ref[wd:, :])
    _post_mixer_epilogue(mix, *rest, alpha=alpha)


def _post_odd_kernel(hf_ref, hb_ref, y_ref, w_ref, *rest, alpha):
    m = ((hf_ref[...] + hb_ref[...]) * _gelu_tanh(y_ref[...])).astype(BF16)
    _post_mixer_epilogue(_dot(m, w_ref[...]), *rest, alpha=alpha)


def _post_mixer(dm, mixer_inputs, w_out, xs, m2, m3, m4, lng, lnb, rwh, rwl, alpha, even):
    T, D, TM = dm.T, dm.D, dm.TM
    E = rwh.shape[0]
    row = lambda a: pl.BlockSpec((TM, a.shape[1]), lambda i: (i, 0))
    const = lambda a: pl.BlockSpec(a.shape, lambda i: (0,) * a.ndim)
    mod_spec = pl.BlockSpec((None, 1, D), lambda i: (dm.group(i, TM), 0, 0))
    if even:
        nac, nal, of, ob, hg, ng = mixer_inputs
        nct = dm.T_ctx // TM
        ins = [nac, nal, of, ob, hg, ng, w_out]
        specs = [pl.BlockSpec((TM, nac.shape[1]), lambda i: (jnp.minimum(i, nct - 1), 0)),
                 pl.BlockSpec((TM, nal.shape[1]), lambda i: (jnp.maximum(i - nct, 0), 0)),
                 row(of), row(ob), row(hg), const(ng), const(w_out)]
        kern = functools.partial(_post_even_kernel, alpha=alpha, ctx_tiles=nct)
    else:
        hf, hb, y = mixer_inputs
        ins = [hf, hb, y, w_out]
        specs = [row(hf), row(hb), row(y), const(w_out)]
        kern = functools.partial(_post_odd_kernel, alpha=alpha)
    ins += [xs, m2, m3, m4, lng, lnb, rwh, rwl]
    specs += [row(xs), mod_spec, mod_spec, mod_spec, const(lng), const(lnb), const(rwh), const(rwl)]
    return pl.pallas_call(
        kern,
        grid=(T // TM,),
        in_specs=specs,
        out_specs=[pl.BlockSpec((TM, D), lambda i: (i, 0)), pl.BlockSpec((TM, D // 2), lambda i: (i, 0)),
                   pl.BlockSpec((E, TM), lambda i: (0, i))],
        out_shape=[jax.ShapeDtypeStruct((T, D), F32), jax.ShapeDtypeStruct((T, D // 2), jnp.uint32),
                   jax.ShapeDtypeStruct((E, T), F32)],
        compiler_params=_cparams(("parallel",)),
        name="post_mixer_even" if even else "post_mixer_odd",
    )(*ins)


def _route_kernel(lg_ref, bias_ref, tri_ref, idx_ref, rank_ref, w_ref, cnt_ref, carry_ref):
    i = pl.program_id(0)

    @pl.when(i == 0)
    def _():
        carry_ref[...] = jnp.zeros_like(carry_ref)

    E, TK = lg_ref.shape
    s = jax.nn.sigmoid(lg_ref[...])
    work = s + bias_ref[...]
    eio = lax.broadcasted_iota(I32, (E, TK), 0).astype(F32)
    picked = jnp.zeros((E, TK), F32)
    sels, idxs = [], []
    for _ in range(TOP_K):
        m = jnp.max(work, axis=0, keepdims=True)
        ik = jnp.min(jnp.where(work == m, eio, float(E)), axis=0, keepdims=True)
        oh = eio == ik
        sels.append(jnp.sum(jnp.where(oh, s, 0.0), axis=0, keepdims=True))
        idxs.append(ik)
        picked = jnp.where(oh, 1.0, picked)
        work = jnp.where(oh, -jnp.inf, work)
    total = sels[0]
    for sk in sels[1:]:
        total = total + sk
    carry = carry_ref[:, 0:1]
    ranks = carry + _dot(picked.astype(BF16), tri_ref[...])
    out_rows = idx_ref.shape[0]
    rio = lax.broadcasted_iota(I32, (out_rows, TK), 0)
    idx_o = jnp.zeros((out_rows, TK), I32)
    rank_o = jnp.zeros((out_rows, TK), I32)
    w_o = jnp.zeros((out_rows, TK), F32)
    for kx in range(TOP_K):
        rk = jnp.sum(jnp.where(eio == idxs[kx], ranks, 0.0), axis=0, keepdims=True).astype(I32)
        idx_o = jnp.where(rio == kx, idxs[kx].astype(I32), idx_o)
        rank_o = jnp.where(rio == kx, rk, rank_o)
        w_o = jnp.where(rio == kx, sels[kx] / total * ROUTED_SCALE, w_o)
    idx_ref[...] = idx_o
    rank_ref[...] = rank_o
    w_ref[...] = w_o
    new_carry = carry + jnp.sum(picked, axis=1, keepdims=True)
    carry_ref[...] = jnp.broadcast_to(new_carry, carry_ref.shape)
    cnt_ref[...] = jnp.broadcast_to(new_carry, cnt_ref.shape)


def _route(logits_t, router_bias, t0):
    E, T = logits_t.shape
    Tm = T - t0
    TK = _pick_tile(int(np.gcd(Tm, t0)), (1024, 512, 256))
    off = t0 // TK
    tri = jnp.asarray(np.triu(np.ones((TK, TK), np.float32), 1), BF16)
    tok = lambda: pl.BlockSpec((8, TK), lambda i: (0, i))
    idx, rank, w, cnt = pl.pallas_call(
        _route_kernel,
        grid=(Tm // TK,),
        in_specs=[pl.BlockSpec((E, TK), lambda i: (0, i + off)),
                  pl.BlockSpec((E, 1), lambda i: (0, 0)),
                  pl.BlockSpec((TK, TK), lambda i: (0, 0))],
        out_specs=[tok(), tok(), tok(), pl.BlockSpec((E, 128), lambda i: (0, 0))],
        out_shape=[jax.ShapeDtypeStruct((8, Tm), I32), jax.ShapeDtypeStruct((8, Tm), I32),
                   jax.ShapeDtypeStruct((8, Tm), F32), jax.ShapeDtypeStruct((E, 128), F32)],
        scratch_shapes=[pltpu.VMEM((E, 128), F32)],
        compiler_params=_cparams(("arbitrary",)),
        name="moe_route",
    )(logits_t, router_bias.reshape(E, 1), tri)
    return idx, rank, w, cnt[:, 0].astype(I32)


def _sorted_tile_rows(n_exp):
    return -(-(MOE_TILE * TOP_K + n_exp * (ROW_ALIGN - 1)) // 16) * 16


def _run_copies(cnt_ref, src_ref, dst_ref, base, n_exp, chunk, make_copy):
    shift = chunk.bit_length() - 1

    def expert_body(e, total):
        n = cnt_ref[base + e]
        src = src_ref[base + e]
        dst = dst_ref[base + e]
        nch = lax.shift_right_logical(n + (chunk - 1), shift)

        def chunk_body(c, carry):
            make_copy(pl.multiple_of(src + c * chunk, ROW_ALIGN), pl.multiple_of(dst + c * chunk, ROW_ALIGN)).start()
            return carry

        lax.fori_loop(0, nch, chunk_body, 0)
        return total + nch

    return lax.fori_loop(0, n_exp, expert_body, jnp.int32(0), unroll=4)


def _dispatch_kernel(cnt_ref, ls_ref, gb_ref, pos_ref, h_ref, xs_hbm, stage, pending, sem, *, n_exp):
    i = pl.program_id(0)
    slot = i % 2
    TMd = h_ref.shape[0]
    R = stage.shape[1] - DISP_CHUNK

    @pl.when(i == 0)
    def _():
        pending[0] = 0
        for s in range(2):
            stage[s, R:, :] = jnp.zeros((stage.shape[1] - R, stage.shape[2]), stage.dtype)

    rio = lax.broadcasted_iota(I32, (R, TMd), 0)
    perm = jnp.zeros((R, TMd), F32)
    for kx in range(TOP_K):
        perm = jnp.where(rio == pos_ref[kx:kx + 1, :], 1.0, perm)
    stage[slot, 0:R, :] = _pack_rows(_dot(perm.astype(BF16), _unpack_rows(h_ref[...])), is_bf16_valued=True)

    def make_copy(src, dst):
        return pltpu.make_async_copy(stage.at[slot, pl.ds(src, DISP_CHUNK)], xs_hbm.at[pl.ds(dst, DISP_CHUNK)], sem)

    def wait_copies(n):
        def wait_body(c, carry):
            make_copy(0, 0).wait()
            return carry
        lax.fori_loop(0, n, wait_body, 0)

    wait_copies(pending[0])
    pending[0] = _run_copies(cnt_ref, ls_ref, gb_ref, i * n_exp, n_exp, DISP_CHUNK, make_copy)

    @pl.when(i == pl.num_programs(0) - 1)
    def _():
        wait_copies(pending[0])


def _dispatch(h2, plan, t0, n_slots, n_exp):
    cnt, ls, gb, pos = plan
    T, DP = h2.shape
    TMd = MOE_TILE
    Tm = T - t0
    assert Tm % TMd == 0 and t0 % TMd == 0
    off = t0 // TMd
    grid_spec = pltpu.PrefetchScalarGridSpec(
        num_scalar_prefetch=3,
        grid=(Tm // TMd,),
        in_specs=[pl.BlockSpec((8, TMd), lambda i, *_: (0, i)),
                  pl.BlockSpec((TMd, DP), lambda i, *_: (i + off, 0))],
        out_specs=pl.BlockSpec(memory_space=pl.ANY),
        scratch_shapes=[pltpu.VMEM((2, _sorted_tile_rows(n_exp) + DISP_CHUNK, DP), h2.dtype),
                        pltpu.SMEM((1,), I32), pltpu.SemaphoreType.DMA],
    )
    return pl.pallas_call(
        functools.partial(_dispatch_kernel, n_exp=n_exp),
        grid_spec=grid_spec,
        out_shape=jax.ShapeDtypeStruct((n_slots, DP), h2.dtype),
        compiler_params=_cparams(("arbitrary",)),
        name="moe_dispatch",
    )(cnt, ls, gb, pos, h2)


def _gmm_kernel(bexp_ref, bvalid_ref, x_ref, wg_ref, wu_ref, wd_ref, y_ref):
    del bexp_ref
    i = pl.program_id(0)
    nvalid = bvalid_ref[i]

    @pl.when(nvalid > 0)
    def _():
        row = lax.broadcasted_iota(I32, x_ref.shape, 0)
        xb = _unpack_rows(jnp.where(row < nvalid, x_ref[...], jnp.uint32(0)))
        act = _silu(_dot(xb, wg_ref[...].astype(BF16))) * _dot(xb, wu_ref[...].astype(BF16))
        y_ref[...] = _pack_rows(_dot(act.astype(BF16), wd_ref[...].astype(BF16)))

    @pl.when(nvalid == 0)
    def _():
        y_ref[...] = jnp.zeros_like(y_ref)


def _grouped_experts(xsorted, block_exp, block_valid, wg, wu, wd, layer):
    NS, DP = xsorted.shape
    nb = NS // MOE_BLOCK
    _, _, D, hid = wg.shape
    grid_spec = pltpu.PrefetchScalarGridSpec(
        num_scalar_prefetch=2,
        grid=(nb,),
        in_specs=[pl.BlockSpec((MOE_BLOCK, DP), lambda i, be, bv: (i, 0)),
                  pl.BlockSpec((None, None, D, hid), lambda i, be, bv: (layer, be[i], 0, 0)),
                  pl.BlockSpec((None, None, D, hid), lambda i, be, bv: (layer, be[i], 0, 0)),
                  pl.BlockSpec((None, None, hid, D), lambda i, be, bv: (layer, be[i], 0, 0))],
        out_specs=pl.BlockSpec((MOE_BLOCK, DP), lambda i, be, bv: (i, 0)),
    )
    return pl.pallas_call(
        _gmm_kernel,
        grid_spec=grid_spec,
        out_shape=jax.ShapeDtypeStruct((NS, DP), jnp.uint32),
        compiler_params=_cparams(("arbitrary",)),
        name="moe_grouped_experts",
    )(block_exp, block_valid, xsorted, wg, wu, wd)


def _combine_kernel(cnt_ref, so_ref, gb_ref, y_hbm, qpos_ref, w_ref, h_ref, sgu_ref, sd_ref, x_ref, m5_ref,
                    lng_ref, lnb_ref, o_ref, stage, pending, sems, *, alpha, n_exp):
    i = pl.program_id(0)
    slot = i % 2

    def make_copy(s):
        def f(dst, src):
            return pltpu.make_async_copy(y_hbm.at[pl.ds(src, COMB_CHUNK)], stage.at[s, pl.ds(dst, COMB_CHUNK)],
                                         sems.at[s])
        return f

    def fetch(tile, s):
        pending[s] = _run_copies(cnt_ref, so_ref, gb_ref, tile * n_exp, n_exp, COMB_CHUNK, make_copy(s))

    @pl.when(i == 0)
    def _():
        stage[...] = jnp.zeros_like(stage)
        fetch(0, 0)

    @pl.when(i + 1 < pl.num_programs(0))
    def _():
        fetch(i + 1, 1 - slot)

    hid = sd_ref.shape[0]
    gu = _dot(_unpack_rows(h_ref[...]), sgu_ref[...])
    y = _dot((_silu(gu[:, :hid]) * gu[:, hid:]).astype(BF16), sd_ref[...])

    def wait_body(c, carry):
        make_copy(slot)(0, 0).wait()
        return carry

    lax.fori_loop(0, pending[slot], wait_body, 0)

    TMc, S = h_ref.shape[0], stage.shape[1]
    sio = lax.broadcasted_iota(I32, (TMc, S), 1)
    q = jnp.zeros((TMc, S), F32)
    for kx in range(TOP_K):
        q = jnp.where(sio == qpos_ref[:, kx:kx + 1], w_ref[:, kx:kx + 1], q)
    y = y + _dot(q.astype(BF16), _unpack_rows(stage[slot]))
    z = alpha * x_ref[...] + m5_ref[...] * y
    o_ref[...] = _layer_norm(z, lng_ref[...], lnb_ref[...])


def _combine(dm, plan, ysorted, qpos_tm, w_tm, h2, sgu, sd, xs, m5, lng, lnb, alpha, t0, n_exp):
    cnt, so, gb = plan
    T, D = xs.shape
    TMc = MOE_TILE
    Tm = T - t0
    off = t0 // TMc
    const = lambda a: pl.BlockSpec(a.shape, lambda i, *_: (0,) * a.ndim)
    row = lambda a: pl.BlockSpec((TMc, a.shape[1]), lambda i, *_: (i + off, 0))
    grid_spec = pltpu.PrefetchScalarGridSpec(
        num_scalar_prefetch=3,
        grid=(Tm // TMc,),
        in_specs=[pl.BlockSpec(memory_space=pl.ANY),
                  pl.BlockSpec((TMc, 8), lambda i, *_: (i, 0)), pl.BlockSpec((TMc, 8), lambda i, *_: (i, 0)),
                  row(h2), const(sgu), const(sd), row(xs),
                  pl.BlockSpec((None, 1, D), lambda i, *_: (dm.group(i + off, TMc), 0, 0)),
                  const(lng), const(lnb)],
        out_specs=pl.BlockSpec((TMc, D), lambda i, *_: (i, 0)),
        scratch_shapes=[pltpu.VMEM((2, _sorted_tile_rows(n_exp), ysorted.shape[1]), ysorted.dtype),
                        pltpu.SMEM((2,), I32), pltpu.SemaphoreType.DMA((2,))],
    )
    return pl.pallas_call(
        functools.partial(_combine_kernel, alpha=alpha, n_exp=n_exp),
        grid_spec=grid_spec,
        out_shape=jax.ShapeDtypeStruct((Tm, D), F32),
        compiler_params=_cparams(("arbitrary",)),
        name="moe_combine",
    )(cnt, so, gb, ysorted, qpos_tm, w_tm, h2, sgu, sd, xs, m5, lng, lnb)


def _moe(dm, h2, logits_t, xs, m5, lng, lnb, router_bias, wg, wu, wd, layer, sgu, sd, alpha, t0):
    T = h2.shape[0]
    E = wg.shape[1]
    Tm = T - t0
    nt = Tm // MOE_TILE
    idx, rank, w, _ = _route(logits_t, router_bias, t0)
    idx, rank, w = idx[:TOP_K], rank[:TOP_K], w[:TOP_K]
    eids = jnp.arange(E, dtype=I32)
    onehot = (idx[:, :, None] == eids).reshape(TOP_K, nt, MOE_TILE, E)
    cnt = jnp.sum(onehot, axis=(0, 2), dtype=I32)
    rank_base = jnp.cumsum(cnt, axis=0) - cnt
    run = (cnt + ROW_ALIGN - 1) // ROW_ALIGN * ROW_ALIGN
    tile_base = jnp.cumsum(run, axis=0) - run
    local_start = jnp.cumsum(run, axis=1) - run
    counts = jnp.sum(run, axis=0)
    nb = -(-(Tm * TOP_K + E * (nt * (ROW_ALIGN - 1) + MOE_SLACK)) // MOE_BLOCK) + E
    padded = (counts + MOE_SLACK + MOE_BLOCK - 1) // MOE_BLOCK * MOE_BLOCK
    pends = jnp.cumsum(padded)
    pstarts = pends - padded
    global_base = pstarts[None, :] + tile_base
    bstart = jnp.arange(nb, dtype=I32) * MOE_BLOCK
    block_exp = jnp.minimum(jnp.sum((pends[None, :] <= bstart[:, None]).astype(I32), axis=1), E - 1)
    of_block = block_exp[:, None] == eids[None, :]
    seg_end = jnp.sum(jnp.where(of_block, (pstarts + counts)[None, :], 0), axis=1)
    block_valid = jnp.clip(seg_end - bstart, 0, MOE_BLOCK).astype(I32)
    look = lambda tab: jnp.sum(jnp.where(onehot, tab[None, :, None, :], 0), axis=3).reshape(TOP_K, Tm)
    pos = rank - look(rank_base) + look(local_start)
    pos8 = jnp.pad(pos, ((0, 8 - TOP_K), (0, 0)), constant_values=-1).astype(I32)
    pos_tm = jnp.pad(pos.T, ((0, 0), (0, 8 - TOP_K)), constant_values=-1).astype(I32)
    w_tm = jnp.pad(w.T, ((0, 0), (0, 8 - TOP_K)))
    flat = lambda a: a.reshape(-1).astype(I32)
    plan = (flat(run), flat(local_start), flat(global_base))
    xsorted = _dispatch(h2, plan + (pos8,), t0, nb * MOE_BLOCK, E)
    ysorted = _grouped_experts(xsorted, block_exp, block_valid, wg, wu, wd, layer)
    return _combine(dm, plan, ysorted, pos_tm, w_tm, h2, sgu, sd, xs, m5, lng, lnb, alpha, t0, E)


def kernel(x, c, ctx, c_ctx, ada_w, ada_b, ln_g, ln_b, ev_w_in, ev_w_out, na_rpb, hg_lb_raw, hg_norm_g, od_w_in, od_conv_w, od_conv_b, lru_w_a, lru_b_a, lru_w_x, lru_b_x, lru_lam, od_w_out, router_w, router_bias, exp_w_gate, exp_w_up, exp_w_down, sh_w_gate, sh_w_up, sh_w_down):
    B, N, D = x.shape
    C = ctx.shape[1]
    depth = ada_w.shape[0]
    dm = _Dims(B, N, C, D)
    alpha = float((2 * depth) ** 0.25)
    assert B + 1 <= 8

    cvec = jnp.zeros((8, D), F32).at[:B].set(c).at[B].set(c_ctx)
    mod = _modulation(cvec, ada_w, ada_b).reshape(depth, 8, 6, 1, D)

    p_lb = jax.nn.softmax(hg_lb_raw.astype(F32), axis=1)
    hg_lb = jnp.cumsum(p_lb, axis=1) - p_lb[:, :1]

    xs = jnp.concatenate([ctx.reshape(B * C, D), x.reshape(B * N, D)], axis=0)
    for l in range(depth):
        jl = l // 2
        last = l == depth - 1
        m = [mod[l, :, t] for t in range(6)]
        rw_t = router_w[l].T
        rwh, rwl = _split(rw_t)
        lng = ln_g[l][:, None, :]
        lnb = ln_b[l][:, None, :]
        if l % 2 == 0:
            q, k, v, hq, ff, fb, hi, hg = _inproj(dm, xs, m[1], m[0], ev_w_in[jl].astype(BF16), True)
            bias = _na_bias_tables(na_rpb[jl], N // GRID_W)
            na_lat = _neighbourhood_attention(dm, q, k, v, bias)
            na_ctx = _context_attention(dm, q, k, v)
            o_f, o_b = _hgrn2(dm, hq, ff, fb, hi, hg_lb[:, jl])
            ng = jnp.tile(hg_norm_g[jl], hq.shape[1] // HEAD_DIM)[None, :]
            mixer_inputs = (na_ctx, na_lat, o_f, o_b, hg, ng)
            w_out = ev_w_out[jl].astype(BF16)
        else:
            y, u = _inproj(dm, xs, m[1], m[0], od_w_in[jl].astype(BF16), False)
            h_f, h_b = _rglru(dm, u, od_conv_w[jl], od_conv_b[jl], lru_w_a[jl], lru_b_a[jl],
                              lru_w_x[jl], lru_b_x[jl], lru_lam[jl])
            mixer_inputs = (h_f, h_b, y)
            w_out = od_w_out[jl].astype(BF16)
        xs, h2, logits_t = _post_mixer(dm, mixer_inputs, w_out, xs, m[2], m[3], m[4], lng[0], lnb[0],
                                       rwh, rwl, alpha, l % 2 == 0)
        sgu = jnp.concatenate([sh_w_gate[l], sh_w_up[l]], axis=-1).astype(BF16)
        sd = sh_w_down[l].astype(BF16)
        t0 = dm.T_ctx if last else 0
        xs = _moe(dm, h2, logits_t, xs, m[5], lng[1], lnb[1], router_bias[l], exp_w_gate, exp_w_up,
                  exp_w_down, l, sgu, sd, alpha, t0)
    return xs.reshape(B, N, D)
```

```python
import functools

import numpy as np
import jax
import jax.numpy as jnp
from jax import lax
from jax.experimental import pallas as pl
from jax.experimental.pallas import tpu as pltpu

F32 = jnp.float32
BF16 = jnp.bfloat16
I32 = jnp.int32

HEAD_DIM = 64
GRID_W = 64
NA_WIN_ROWS = 8
NA_WIN_COLS = 16
NA_QROWS = 4
NA_KGROUP_ROWS = 4
NA_KROWS = 12
HG_CHUNK = 64
HG_HEADS_PER_GROUP = 4
FORGET_FLOOR = 1e-30
HG_SAFE_EXP = 80.0
HG_CLIP_EXP = 85.0
LRU_C = 8.0
LRU_BLOCKS = 16
CONV_W = 4
TOP_K = 6
ROUTED_SCALE = 2.5
LN_EPS = 1e-5
RMS_EPS = 1e-6
SEQ_BLOCK = 256
MOE_BLOCK = 1024
MOE_TILE = 256
ROW_ALIGN = 8
DISP_CHUNK = 32
COMB_CHUNK = ROW_ALIGN
MOE_SLACK = DISP_CHUNK
MASK_VALUE = -1e30
VMEM_LIMIT = 56 * 1024 * 1024


def _cparams(sem):
    return pltpu.CompilerParams(dimension_semantics=sem, vmem_limit_bytes=VMEM_LIMIT)


def _split(a):
    hi = a.astype(BF16)
    lo = (a - hi.astype(F32)).astype(BF16)
    return hi, lo


def _dot(a, b):
    return jnp.dot(a, b, preferred_element_type=F32)


def _dot_nt(a, b):
    return lax.dot_general(a, b, (((1,), (1,)), ((), ())), preferred_element_type=F32)


def _dot_tn(a, b):
    return lax.dot_general(a, b, (((0,), (0,)), ((), ())), preferred_element_type=F32)


def _dot3(a, b):
    ah, al = _split(a)
    bh, bl = _split(b)
    return _dot(ah, bh) + _dot(al, bh) + _dot(ah, bl)


def _pack_rows(h, is_bf16_valued=False):
    half = h.shape[1] // 2
    if not is_bf16_valued:
        h = h.astype(BF16).astype(F32)
    bits = pltpu.bitcast(h, jnp.uint32)
    return (bits[:, :half] >> 16) | (bits[:, half:] & jnp.uint32(0xFFFF0000))


def _unpack_rows(p):
    lo = pltpu.bitcast(p << 16, F32)
    hi = pltpu.bitcast(p & jnp.uint32(0xFFFF0000), F32)
    return jnp.concatenate([lo, hi], axis=1).astype(BF16)


def _silu(v):
    return v * jax.nn.sigmoid(v)


def _gelu_tanh(v):
    return 0.5 * v * (1.0 + jnp.tanh(0.7978845608028654 * (v + 0.044715 * v * v * v)))


def _layer_norm(z, g, b):
    mu = jnp.mean(z, axis=-1, keepdims=True)
    zc = z - mu
    var = jnp.mean(zc * zc, axis=-1, keepdims=True)
    return zc * lax.rsqrt(var + LN_EPS) * g + b


def _pick_tile(n, cands):
    for c in cands:
        if n % c == 0:
            return c
    raise ValueError(f"no tile for {n}")


class _Dims:
    def __init__(self, B, N, C, D):
        self.B, self.N, self.C, self.D = B, N, C, D
        self.T_ctx = B * C
        self.T = B * C + B * N
        assert C == SEQ_BLOCK and N % SEQ_BLOCK == 0
        assert N % GRID_W == 0 and (N // GRID_W) % NA_QROWS == 0 and N // GRID_W >= NA_KROWS
        assert self.T_ctx % 512 == 0
        self.TM = _pick_tile(self.T_ctx, (512, 256))
        assert N % self.TM == 0

    def group(self, i, tile):
        start = i * tile
        return jnp.where(start < self.T_ctx, self.B, (start - self.T_ctx) // self.N)


def _mod_kernel(c_ref, w_ref, b_ref, o_ref):
    o_ref[...] = _dot3(_silu(c_ref[...]), w_ref[...]) + b_ref[...]


def _modulation(cvec, ada_w, ada_b):
    L, D, W6 = ada_w.shape
    nc = 1536
    return pl.pallas_call(
        _mod_kernel,
        grid=(L, W6 // nc),
        in_specs=[
            pl.BlockSpec((8, D), lambda l, j: (0, 0)),
            pl.BlockSpec((None, D, nc), lambda l, j: (l, 0, j)),
            pl.BlockSpec((None, 1, nc), lambda l, j: (l, 0, j)),
        ],
        out_specs=pl.BlockSpec((None, 8, nc), lambda l, j: (l, 0, j)),
        out_shape=jax.ShapeDtypeStruct((L, 8, W6), F32),
        compiler_params=_cparams(("arbitrary", "arbitrary")),
        name="adaln_modulation",
    )(cvec, ada_w, ada_b.reshape(L, 1, W6))


def _inproj_even_kernel(x_ref, sc_ref, sh_ref, w_ref, q_ref, k_ref, v_ref, hq_ref, ff_ref, fb_ref, hi_ref, hg_ref):
    h = (x_ref[...] * (1.0 + sc_ref[...]) + sh_ref[...]).astype(BF16)
    wd = q_ref.shape[1]

    def part(j):
        return _dot(h, w_ref[:, j * wd:(j + 1) * wd])

    scale = HEAD_DIM ** -0.5
    q_ref[...] = (part(0) * scale).astype(BF16)
    k_ref[...] = part(1).astype(BF16)
    v_ref[...] = part(2).astype(BF16)
    hq_ref[...] = _silu(part(3)) * scale
    ff_ref[...] = part(4)
    fb_ref[...] = part(5)
    hi_ref[...] = part(6)
    hg_ref[...] = part(7)


def _inproj_odd_kernel(x_ref, sc_ref, sh_ref, w_ref, y_ref, u_ref):
    h = (x_ref[...] * (1.0 + sc_ref[...]) + sh_ref[...]).astype(BF16)
    wd = y_ref.shape[1]
    y_ref[...] = _dot(h, w_ref[:, :wd])
    u_ref[...] = _dot(h, w_ref[:, wd:])


def _inproj(dm, xs, sc, sh, w, even):
    T, D, TM = dm.T, dm.D, dm.TM
    wtot = w.shape[1]
    mod_spec = pl.BlockSpec((None, 1, D), lambda i: (dm.group(i, TM), 0, 0))
    in_specs = [pl.BlockSpec((TM, D), lambda i: (i, 0)), mod_spec, mod_spec,
                pl.BlockSpec((D, wtot), lambda i: (0, 0))]
    if even:
        wd = wtot // 8
        dts = [BF16] * 3 + [F32] * 5
        kern = _inproj_even_kernel
    else:
        wd = wtot // 2
        dts = [F32] * 2
        kern = _inproj_odd_kernel
    return pl.pallas_call(
        kern,
        grid=(T // TM,),
        in_specs=in_specs,
        out_specs=[pl.BlockSpec((TM, wd), lambda i: (i, 0)) for _ in dts],
        out_shape=[jax.ShapeDtypeStruct((T, wd), dt) for dt in dts],
        compiler_params=_cparams(("parallel",)),
        name="inproj_even" if even else "inproj_odd",
    )(xs, sc, sh, w)


def _na_bias_tables(rpb, rows):
    W = GRID_W
    H = rpb.shape[0]
    nr, nc = 2 * NA_WIN_ROWS - 1, 2 * NA_WIN_COLS - 1
    hp = lax.Precision.HIGHEST
    c = np.arange(W)[:, None]
    kc = np.arange(W)[None, :]
    c0 = np.clip(c - NA_WIN_COLS // 2, 0, W - NA_WIN_COLS)
    valid_c = (kc >= c0) & (kc < c0 + NA_WIN_COLS)
    dc = np.clip(kc - c + NA_WIN_COLS - 1, 0, nc - 1)
    oh_c = (dc.reshape(-1, 1) == np.arange(nc)[None, :]).astype(np.float32)
    by_col = jnp.einsum('hrs,xs->hrx', rpb.astype(F32), jnp.asarray(oh_c), precision=hp)
    tabs = []
    for rbase, kb in ((0, 0), (NA_QROWS, NA_QROWS - NA_WIN_ROWS // 2), (rows - NA_QROWS, rows - NA_KROWS)):
        r = rbase + np.arange(NA_QROWS)[:, None]
        kr = kb + np.arange(NA_KROWS)[None, :]
        r0 = np.clip(r - NA_WIN_ROWS // 2, 0, rows - NA_WIN_ROWS)
        valid_r = (kr >= r0) & (kr < r0 + NA_WIN_ROWS)
        dr = np.clip(kr - r + NA_WIN_ROWS - 1, 0, nr - 1)
        oh_r = (dr.reshape(-1, 1) == np.arange(nr)[None, :]).astype(np.float32)
        t = jnp.einsum('yr,hrx->hyx', jnp.asarray(oh_r), by_col, precision=hp)
        t = t.reshape(H, NA_QROWS, NA_KROWS, W, W).transpose(0, 1, 3, 2, 4)
        valid = valid_r[:, None, :, None] & valid_c[None, :, None, :]
        t = jnp.where(jnp.asarray(valid)[None], t, MASK_VALUE)
        tabs.append(t.reshape(H, NA_QROWS * W, NA_KROWS * W))
    return jnp.stack(tabs).astype(F32)


def _na_kernel(q_ref, *refs):
    nkb = NA_KROWS // NA_KGROUP_ROWS
    k_refs, v_refs = refs[:nkb], refs[nkb:2 * nkb]
    kc_ref, vc_ref, bias_ref, o_ref = refs[2 * nkb:]
    q2 = q_ref[...]
    lane = lax.broadcasted_iota(I32, (1, q2.shape[1]), 1)
    first = lane < HEAD_DIM
    ks = [r[...] for r in k_refs]
    vs = [r[...] for r in v_refs]
    kc = kc_ref[...]
    vc = vc_ref[...]
    kb = ks[0].shape[0]
    outs = []
    for hh in range(2):
        sel = first if hh == 0 else jnp.logical_not(first)
        qh = jnp.where(sel, q2, jnp.zeros_like(q2))
        s_loc = jnp.concatenate([_dot_nt(qh, kj) for kj in ks], axis=1) + bias_ref[hh]
        s_ctx = _dot_nt(qh, kc)
        m = jnp.maximum(jnp.max(s_loc, axis=1, keepdims=True), jnp.max(s_ctx, axis=1, keepdims=True))
        p_loc = jnp.exp(s_loc - m)
        p_ctx = jnp.exp(s_ctx - m)
        denom = jnp.sum(p_loc, axis=1, keepdims=True) + jnp.sum(p_ctx, axis=1, keepdims=True)
        o = _dot(p_ctx.astype(BF16), vc)
        for j in range(nkb):
            o = o + _dot(p_loc[:, j * kb:(j + 1) * kb].astype(BF16), vs[j])
        outs.append(o / denom)
    o_ref[...] = jnp.where(first, outs[0], outs[1]).astype(o_ref.dtype)


def _neighbourhood_attention(dm, q, k, v, bias):
    B, N, C, T = dm.B, dm.N, dm.C, dm.T
    H2 = q.shape[1] // (2 * HEAD_DIM)
    rows = N // GRID_W
    nqb = rows // NA_QROWS
    QB = NA_QROWS * GRID_W
    KB = NA_KGROUP_ROWS * GRID_W
    nkb = NA_KROWS // NA_KGROUP_ROWS
    ngroups = N // KB
    lat_q0 = dm.T_ctx // QB
    lat_k0 = dm.T_ctx // KB

    def q_map(p, i, b):
        return (lat_q0 + b * (N // QB) + i, p)

    def o_map(p, i, b):
        return (b * (N // QB) + i, p)

    def kv_map(j):
        def f(p, i, b):
            gs = jnp.clip(i * (QB // KB) - 1, 0, ngroups - nkb)
            return (lat_k0 + b * ngroups + gs + j, p)
        return f

    def ctx_map(p, i, b):
        return (b, p)

    def bias_map(p, i, b):
        var = jnp.where(i == 0, 0, jnp.where(i == nqb - 1, 2, 1))
        return (var, p, 0, 0)

    lanes = 2 * HEAD_DIM
    kv_specs = [pl.BlockSpec((KB, lanes), kv_map(j)) for j in range(nkb)]
    return pl.pallas_call(
        _na_kernel,
        grid=(H2, nqb, B),
        in_specs=[pl.BlockSpec((QB, lanes), q_map)] + kv_specs + kv_specs
        + [pl.BlockSpec((C, lanes), ctx_map), pl.BlockSpec((C, lanes), ctx_map),
           pl.BlockSpec((None, 2, QB, NA_KROWS * GRID_W), bias_map)],
        out_specs=pl.BlockSpec((QB, lanes), o_map),
        out_shape=jax.ShapeDtypeStruct((B * N, q.shape[1]), BF16),
        compiler_params=_cparams(("arbitrary", "arbitrary", "arbitrary")),
        name="neighbourhood_attention",
    )(q, *([k] * nkb), *([v] * nkb), k, v, bias)


def _ctx_attn_kernel(q_ref, k_ref, v_ref, o_ref):
    q2 = q_ref[...]
    k2 = k_ref[...]
    v2 = v_ref[...]
    lane = lax.broadcasted_iota(I32, (1, q2.shape[1]), 1)
    first = lane < HEAD_DIM
    outs = []
    for hh in range(2):
        sel = first if hh == 0 else jnp.logical_not(first)
        s = _dot_nt(jnp.where(sel, q2, jnp.zeros_like(q2)), k2)
        p = jnp.exp(s - jnp.max(s, axis=1, keepdims=True))
        outs.append(_dot(p.astype(BF16), v2) / jnp.sum(p, axis=1, keepdims=True))
    o_ref[...] = jnp.where(first, outs[0], outs[1]).astype(o_ref.dtype)


def _context_attention(dm, q, k, v):
    lanes = 2 * HEAD_DIM
    spec = pl.BlockSpec((dm.C, lanes), lambda p, b: (b, p))
    return pl.pallas_call(
        _ctx_attn_kernel,
        grid=(q.shape[1] // lanes, dm.B),
        in_specs=[spec, spec, spec],
        out_specs=spec,
        out_shape=jax.ShapeDtypeStruct((dm.T_ctx, q.shape[1]), BF16),
        compiler_params=_cparams(("arbitrary", "arbitrary")),
        name="context_attention",
    )(q, k, v)


def _seq_block_maps(dm):
    nl = dm.N // SEQ_BLOCK
    base = dm.T_ctx // SEQ_BLOCK

    def fwd(b, j):
        return jnp.where(j == 0, b, base + b * nl + j - 1)

    def bwd(b, j):
        return jnp.where(j == 0, b, base + b * nl + nl - j)

    return fwd, bwd


def _hgrn_prepare(hq, z, lb, tri, last_row, mid_row):
    f = lb + (1.0 - lb) * jax.nn.sigmoid(z)
    fm = jnp.maximum(f, FORGET_FLOOR)
    g = jnp.log(fm)
    kk = 1.0 - fm
    gh, gl = _split(g)
    cum = _dot(tri, gh) + _dot(tri, gl)
    tail = cum[last_row:last_row + 1, :]
    e = cum - cum[mid_row:mid_row + 1, :]
    emax = jnp.max(jnp.max(jnp.abs(e), axis=1, keepdims=True), axis=0, keepdims=True)
    ec = jnp.clip(e, -HG_CLIP_EXP, HG_CLIP_EXP)
    qe = (hq * jnp.exp(ec)).astype(BF16)
    ke = (kk * jnp.exp(-ec)).astype(BF16)
    return kk, cum, tail, emax, qe, ke


def _hgrn_kernel(hqf_ref, zf_ref, vf_ref, hqb_ref, zb_ref, vb_ref, lbf_ref, lbb_ref,
                 of_ref, ob_ref, st_ref, a_ref, cum_ref, k_ref, qs_ref, k2_ref, et_ref):
    j = pl.program_id(1)

    @pl.when(j == 0)
    def _():
        st_ref[...] = jnp.zeros_like(st_ref)

    CH = HG_CHUNK
    GW = HG_HEADS_PER_GROUP * HEAD_DIM
    ngroups = hqf_ref.shape[1] // GW
    nchunks = hqf_ref.shape[0] // CH
    t_i = lax.broadcasted_iota(I32, (CH, CH), 0)
    u_i = lax.broadcasted_iota(I32, (CH, CH), 1)
    tris = ((u_i <= t_i).astype(BF16), (u_i >= t_i).astype(BF16))
    t_w = lax.broadcasted_iota(I32, (CH, GW), 0)
    s_w = lax.broadcasted_iota(I32, (CH, GW), 1) % CH
    cmasks = (s_w <= t_w, s_w >= t_w)
    r_b = lax.broadcasted_iota(I32, (GW, GW), 0)
    c_b = lax.broadcasted_iota(I32, (GW, GW), 1)
    bm = (r_b // HEAD_DIM) == (c_b // HEAD_DIM)
    hsel_base = (r_b // HEAD_DIM) * HEAD_DIM
    dirs = ((hqf_ref, zf_ref, vf_ref, lbf_ref, of_ref), (hqb_ref, zb_ref, vb_ref, lbb_ref, ob_ref))
    chains = [(d, g) for d in range(2) for g in range(ngroups)]

    def expand(m):
        return jnp.where(bm, jnp.concatenate([m] * HG_HEADS_PER_GROUP, axis=0), jnp.zeros((GW, GW), m.dtype))

    nchain = len(chains)

    def rows_of(c):
        return pl.multiple_of(c * CH, CH), pl.multiple_of((nchunks - 1 - c) * CH, CH)

    def prep_body(c, worst):
        rows = rows_of(c)
        for n, (d, g) in enumerate(chains):
            hq_r, z_r, _, lb_r, _ = dirs[d]
            ls = slice(g * GW, (g + 1) * GW)
            hq = hq_r[pl.ds(rows[d], CH), ls]
            z = z_r[pl.ds(rows[d], CH), ls]
            last_row = CH - 1 if d == 0 else 0
            kk, cum, tail, emax, qe, ke = _hgrn_prepare(hq, z, lb_r[:, ls], tris[d], last_row, CH // 2)
            m = c * nchain + n
            a_ref[m] = _dot_nt(qe, expand(ke))
            cum_ref[m] = cum
            k_ref[m] = kk
            qs_ref[m] = hq * jnp.exp(cum)
            k2_ref[m] = kk * jnp.exp(tail - cum)
            et_ref[m] = jnp.broadcast_to(jnp.exp(tail), et_ref.shape[1:])
            worst = jnp.maximum(worst, emax)
        return worst

    worst = lax.fori_loop(0, nchunks, prep_body, jnp.zeros((1, 1), F32), unroll=True)

    @pl.when(worst[0, 0] > HG_SAFE_EXP)
    def _():
        def slow_body(c, carry):
            rows = rows_of(c)
            for n, (d, g) in enumerate(chains):
                m = c * nchain + n
                hq = dirs[d][0][pl.ds(rows[d], CH), g * GW:(g + 1) * GW]
                cum = cum_ref[m]

                def key_body(s, acc):
                    cs = cum_ref[m, pl.ds(s, 1), :]
                    ksr = k_ref[m, pl.ds(s, 1), :]
                    p = hq * ksr * jnp.exp(jnp.minimum(cum - cs, 0.0))
                    hsel = jnp.where(c_b == hsel_base + s, 1.0, 0.0).astype(BF16)
                    return acc + _dot(p.astype(BF16), hsel)

                a_ref[m] = lax.fori_loop(0, CH, key_body, jnp.zeros((CH, GW), F32))
            return carry

        lax.fori_loop(0, nchunks, slow_body, 0)

    def state_body(c, carry):
        rows = rows_of(c)
        for n, (d, g) in enumerate(chains):
            m = c * nchain + n
            ls = slice(g * GW, (g + 1) * GW)
            vb = dirs[d][2][pl.ds(rows[d], CH), ls].astype(BF16)
            st = st_ref[n]
            a = jnp.where(cmasks[d], a_ref[m], 0.0).astype(BF16)
            o = _dot(a, expand(vb)) + _dot_nt(qs_ref[m].astype(BF16), st.astype(BF16))
            dirs[d][4][pl.ds(rows[d], CH), ls] = o
            st_ref[n] = et_ref[m, 0:1, :] * st + jnp.where(bm, _dot_tn(vb, k2_ref[m].astype(BF16)), 0.0)
        return carry

    lax.fori_loop(0, nchunks, state_body, 0, unroll=True)


def _hgrn2(dm, hq, ff, fb, hi, lb):
    B, T = dm.B, dm.T
    Wd = hq.shape[1]
    GW = HG_HEADS_PER_GROUP * HEAD_DIM
    nchain = 2 * (Wd // GW)
    nsteps = nchain * (SEQ_BLOCK // HG_CHUNK)
    nblk = 1 + dm.N // SEQ_BLOCK
    fwd, bwd = _seq_block_maps(dm)
    fspec = pl.BlockSpec((SEQ_BLOCK, Wd), lambda b, j: (fwd(b, j), 0))
    bspec = pl.BlockSpec((SEQ_BLOCK, Wd), lambda b, j: (bwd(b, j), 0))
    lbspec = pl.BlockSpec((1, Wd), lambda b, j: (0, 0))
    return pl.pallas_call(
        _hgrn_kernel,
        grid=(B, nblk),
        in_specs=[fspec, fspec, fspec, bspec, bspec, bspec, lbspec, lbspec],
        out_specs=[fspec, bspec],
        out_shape=[jax.ShapeDtypeStruct((T, Wd), F32)] * 2,
        scratch_shapes=[pltpu.VMEM((nchain, GW, GW), F32)]
        + [pltpu.VMEM((nsteps, HG_CHUNK, GW), F32)] * 5 + [pltpu.VMEM((nsteps, 8, GW), F32)],
        compiler_params=_cparams(("arbitrary", "arbitrary")),
        name="hgrn2_bidirectional",
    )(hq, ff, hi, hq, fb, hi, lb[0:1], lb[1:2])


LRU_GROUP = 8


def _lru_group_scan(a, x, reverse):
    n, w = a.shape
    a = a.reshape(n // LRU_GROUP, LRU_GROUP, w)
    x = x.reshape(n // LRU_GROUP, LRU_GROUP, w)
    row = lax.broadcasted_iota(I32, a.shape, 1)
    s = 1
    while s < LRU_GROUP:
        shift = (LRU_GROUP - s) if reverse else s
        a_sh = pltpu.roll(a, shift, 1)
        x_sh = pltpu.roll(x, shift, 1)
        valid = (row < LRU_GROUP - s) if reverse else (row >= s)
        x = jnp.where(valid, a * x_sh + x, x)
        a = jnp.where(valid, a * a_sh, a)
        s *= 2
    return a.reshape(n, w), x.reshape(n, w)


def _lru_kernel(ucf_ref, upf_ref, unf_ref, ucb_ref, upb_ref, unb_ref, cw_ref, cb_ref, wbd_ref,
                ba_ref, bx_ref, lam_ref, hf_ref, hb_ref, ext_ref, carry_ref, sa_ref, sx_ref):
    j = pl.program_id(1)
    nblk = pl.num_programs(1)

    @pl.when(j == 0)
    def _():
        carry_ref[...] = jnp.zeros_like(carry_ref)

    TT, W = ucf_ref.shape
    GW = wbd_ref.shape[2]
    halo = upf_ref.shape[0]
    dirs = ((ucf_ref, upf_ref, unf_ref, hf_ref), (ucb_ref, upb_ref, unb_ref, hb_ref))
    for d, (uc_r, up_r, un_r, out_r) in enumerate(dirs):
        pos = j if d == 0 else jnp.where(j == 0, 0, nblk - j)
        keep_prev = jnp.where((pos == 0) | (pos == 1), 0.0, 1.0)
        keep_next = jnp.where((pos == 0) | (pos == nblk - 1), 0.0, 1.0)
        ext_ref[d, 0:halo, :] = up_r[...] * keep_prev
        ext_ref[d, halo:halo + TT, :] = uc_r[...]
        ext_ref[d, halo + TT:halo + TT + halo, :] = un_r[...] * keep_next
        for g in range(W // GW):
            ls = slice(g * GW, (g + 1) * GW)
            u = cb_ref[:, ls]
            for tap in range(CONV_W):
                off = halo - 2 + tap
                u = u + cw_ref[tap:tap + 1, ls] * ext_ref[d, off:off + TT, ls]
            gates = _dot(u.astype(BF16), wbd_ref[d, g])
            r = jax.nn.sigmoid(gates[:, :GW] + ba_ref[d:d + 1, ls])
            ig = jax.nn.sigmoid(gates[:, GW:] + bx_ref[d:d + 1, ls])
            nl = -lam_ref[d:d + 1, ls]
            softplus = jnp.maximum(nl, 0.0) + jnp.log1p(jnp.exp(-jnp.abs(nl)))
            log_a = -LRU_C * r * softplus
            a = jnp.exp(log_a)
            gain2 = 1.0 - a * a
            gain = jnp.where(gain2 > 0.0, gain2 * lax.rsqrt(gain2), 0.0)
            x_in = gain * ig * u
            a_acc, h = _lru_group_scan(a, x_in, d == 1)
            sa_ref[d, :, ls] = a_acc
            sx_ref[d, :, ls] = h

    ngroups = TT // LRU_GROUP

    def group_body(gi, carry):
        hf, hb = carry
        rf = pl.multiple_of(gi * LRU_GROUP, LRU_GROUP)
        rb = pl.multiple_of((ngroups - 1 - gi) * LRU_GROUP, LRU_GROUP)
        of = sx_ref[0, pl.ds(rf, LRU_GROUP), :] + sa_ref[0, pl.ds(rf, LRU_GROUP), :] * hf
        ob = sx_ref[1, pl.ds(rb, LRU_GROUP), :] + sa_ref[1, pl.ds(rb, LRU_GROUP), :] * hb
        hf_ref[pl.ds(rf, LRU_GROUP), :] = of
        hb_ref[pl.ds(rb, LRU_GROUP), :] = ob
        return of[LRU_GROUP - 1:LRU_GROUP, :], ob[0:1, :]

    hf, hb = lax.fori_loop(0, ngroups, group_body, (carry_ref[0, 0:1, :], carry_ref[1, 0:1, :]), unroll=4)
    carry_ref[0, 0:1, :] = hf
    carry_ref[1, 0:1, :] = hb


def _block_diag_gates(w_a, w_x, group):
    ndir, K, d, _ = w_a.shape
    per = group // d
    eye = jnp.eye(per, dtype=w_a.dtype)

    def bd(w):
        w = w.reshape(ndir, K // per, per, d, d)
        full = jnp.einsum('ngkde,kl->ngkdle', w, eye)
        return full.reshape(ndir, K // per, group, group)

    return jnp.concatenate([bd(w_a), bd(w_x)], axis=-1)


def _rglru(dm, u, conv_w, conv_b, w_a, b_a, w_x, b_x, lam):
    B, T = dm.B, dm.T
    W = u.shape[1]
    GW = 256
    halo = 8
    nblk = 1 + dm.N // SEQ_BLOCK
    fwd, bwd = _seq_block_maps(dm)
    hb = SEQ_BLOCK // halo
    nh = T // halo
    wbd = _block_diag_gates(w_a, w_x, GW).astype(BF16)

    def cur(m):
        return pl.BlockSpec((SEQ_BLOCK, W), lambda b, j: (m(b, j), 0))

    def prev(m):
        return pl.BlockSpec((halo, W), lambda b, j: (jnp.maximum(m(b, j) * hb - 1, 0), 0))

    def nxt(m):
        return pl.BlockSpec((halo, W), lambda b, j: (jnp.minimum((m(b, j) + 1) * hb, nh - 1), 0))

    def full(a):
        nd = a.ndim
        return pl.BlockSpec(a.shape, lambda b, j: (0,) * nd)

    consts = (conv_w, conv_b.reshape(1, W), wbd, b_a, b_x, lam)
    return pl.pallas_call(
        _lru_kernel,
        grid=(B, nblk),
        in_specs=[cur(fwd), prev(fwd), nxt(fwd), cur(bwd), prev(bwd), nxt(bwd)] + [full(a) for a in consts],
        out_specs=[cur(fwd), cur(bwd)],
        out_shape=[jax.ShapeDtypeStruct((T, W), F32)] * 2,
        scratch_shapes=[pltpu.VMEM((2, SEQ_BLOCK + 2 * halo, W), F32), pltpu.VMEM((2, 8, W), F32),
                        pltpu.VMEM((2, SEQ_BLOCK, W), F32), pltpu.VMEM((2, SEQ_BLOCK, W), F32)],
        compiler_params=_cparams(("arbitrary", "arbitrary")),
        name="rglru_bidirectional",
    )(u, u, u, u, u, u, *consts)


def _post_mixer_epilogue(o, x_ref, m2_ref, m3_ref, m4_ref, lng_ref, lnb_ref, rwh_ref, rwl_ref,
                         xo_ref, h2_ref, lg_ref, alpha):
    z = alpha * x_ref[...] + m2_ref[...] * o
    xn = _layer_norm(z, lng_ref[...], lnb_ref[...])
    xo_ref[...] = xn
    h2 = xn * (1.0 + m4_ref[...]) + m3_ref[...]
    h2_ref[...] = _pack_rows(h2)
    hh, hl = _split(h2)
    lg_ref[...] = _dot_nt(rwh_ref[...], hh) + _dot_nt(rwh_ref[...], hl) + _dot_nt(rwl_ref[...], hh)


def _post_even_kernel(nac_ref, nal_ref, of_ref, ob_ref, hg_ref, ng_ref, w_ref, *rest, alpha, ctx_tiles):
    na = jnp.where(pl.program_id(0) < ctx_tiles, nac_ref[...], nal_ref[...])
    o = of_ref[...] + ob_ref[...]
    wd = o.shape[1]
    r_i = lax.broadcasted_iota(I32, (wd, wd), 0) // HEAD_DIM
    c_i = lax.broadcasted_iota(I32, (wd, wd), 1) // HEAD_DIM
    avg = jnp.where(r_i == c_i, 1.0 / HEAD_DIM, 0.0).astype(BF16)
    sh, sl = _split(o * o)
    ms = _dot(sh, avg) + _dot(sl, avg)
    r = o * lax.rsqrt(ms + RMS_EPS) * ng_ref[...] * _silu(hg_ref[...])
    mix = _dot(na, w_ref[:wd, :]) + _dot(r.astype(BF16), w_ref[wd:, :])
    _post_mixer_epilogue(mix, *rest, alpha=alpha)


def _post_odd_kernel(hf_ref, hb_ref, y_ref, w_ref, *rest, alpha):
    m = ((hf_ref[...] + hb_ref[...]) * _gelu_tanh(y_ref[...])).astype(BF16)
    _post_mixer_epilogue(_dot(m, w_ref[...]), *rest, alpha=alpha)


def _post_mixer(dm, mixer_inputs, w_out, xs, m2, m3, m4, lng, lnb, rwh, rwl, alpha, even):
    T, D, TM = dm.T, dm.D, dm.TM
    E = rwh.shape[0]
    row = lambda a: pl.BlockSpec((TM, a.shape[1]), lambda i: (i, 0))
    const = lambda a: pl.BlockSpec(a.shape, lambda i: (0,) * a.ndim)
    mod_spec = pl.BlockSpec((None, 1, D), lambda i: (dm.group(i, TM), 0, 0))
    if even:
        nac, nal, of, ob, hg, ng = mixer_inputs
        nct = dm.T_ctx // TM
        ins = [nac, nal, of, ob, hg, ng, w_out]
        specs = [pl.BlockSpec((TM, nac.shape[1]), lambda i: (jnp.minimum(i, nct - 1), 0)),
                 pl.BlockSpec((TM, nal.shape[1]), lambda i: (jnp.maximum(i - nct, 0), 0)),
                 row(of), row(ob), row(hg), const(ng), const(w_out)]
        kern = functools.partial(_post_even_kernel, alpha=alpha, ctx_tiles=nct)
    else:
        hf, hb, y = mixer_inputs
        ins = [hf, hb, y, w_out]
        specs = [row(hf), row(hb), row(y), const(w_out)]
        kern = functools.partial(_post_odd_kernel, alpha=alpha)
    ins += [xs, m2, m3, m4, lng, lnb, rwh, rwl]
    specs += [row(xs), mod_spec, mod_spec, mod_spec, const(lng), const(lnb), const(rwh), const(rwl)]
    return pl.pallas_call(
        kern,
        grid=(T // TM,),
        in_specs=specs,
        out_specs=[pl.BlockSpec((TM, D), lambda i: (i, 0)), pl.BlockSpec((TM, D // 2), lambda i: (i, 0)),
                   pl.BlockSpec((E, TM), lambda i: (0, i))],
        out_shape=[jax.ShapeDtypeStruct((T, D), F32), jax.ShapeDtypeStruct((T, D // 2), jnp.uint32),
                   jax.ShapeDtypeStruct((E, T), F32)],
        compiler_params=_cparams(("parallel",)),
        name="post_mixer_even" if even else "post_mixer_odd",
    )(*ins)


def _route_kernel(lg_ref, bias_ref, tri_ref, idx_ref, rank_ref, w_ref, cnt_ref, carry_ref):
    i = pl.program_id(0)

    @pl.when(i == 0)
    def _():
        carry_ref[...] = jnp.zeros_like(carry_ref)

    E, TK = lg_ref.shape
    s = jax.nn.sigmoid(lg_ref[...])
    work = s + bias_ref[...]
    eio = lax.broadcasted_iota(I32, (E, TK), 0).astype(F32)
    picked = jnp.zeros((E, TK), F32)
    sels, idxs = [], []
    for _ in range(TOP_K):
        m = jnp.max(work, axis=0, keepdims=True)
        ik = jnp.min(jnp.where(work == m, eio, float(E)), axis=0, keepdims=True)
        oh = eio == ik
        sels.append(jnp.sum(jnp.where(oh, s, 0.0), axis=0, keepdims=True))
        idxs.append(ik)
        picked = jnp.where(oh, 1.0, picked)
        work = jnp.where(oh, -jnp.inf, work)
    total = sels[0]
    for sk in sels[1:]:
        total = total + sk
    carry = carry_ref[:, 0:1]
    ranks = carry + _dot(picked.astype(BF16), tri_ref[...])
    out_rows = idx_ref.shape[0]
    rio = lax.broadcasted_iota(I32, (out_rows, TK), 0)
    idx_o = jnp.zeros((out_rows, TK), I32)
    rank_o = jnp.zeros((out_rows, TK), I32)
    w_o = jnp.zeros((out_rows, TK), F32)
    for kx in range(TOP_K):
        rk = jnp.sum(jnp.where(eio == idxs[kx], ranks, 0.0), axis=0, keepdims=True).astype(I32)
        idx_o = jnp.where(rio == kx, idxs[kx].astype(I32), idx_o)
        rank_o = jnp.where(rio == kx, rk, rank_o)
        w_o = jnp.where(rio == kx, sels[kx] / total * ROUTED_SCALE, w_o)
    idx_ref[...] = idx_o
    rank_ref[...] = rank_o
    w_ref[...] = w_o
    new_carry = carry + jnp.sum(picked, axis=1, keepdims=True)
    carry_ref[...] = jnp.broadcast_to(new_carry, carry_ref.shape)
    cnt_ref[...] = jnp.broadcast_to(new_carry, cnt_ref.shape)


def _route(logits_t, router_bias, t0):
    E, T = logits_t.shape
    Tm = T - t0
    TK = _pick_tile(int(np.gcd(Tm, t0)), (1024, 512, 256))
    off = t0 // TK
    tri = jnp.asarray(np.triu(np.ones((TK, TK), np.float32), 1), BF16)
    tok = lambda: pl.BlockSpec((8, TK), lambda i: (0, i))
    idx, rank, w, cnt = pl.pallas_call(
        _route_kernel,
        grid=(Tm // TK,),
        in_specs=[pl.BlockSpec((E, TK), lambda i: (0, i + off)),
                  pl.BlockSpec((E, 1), lambda i: (0, 0)),
                  pl.BlockSpec((TK, TK), lambda i: (0, 0))],
        out_specs=[tok(), tok(), tok(), pl.BlockSpec((E, 128), lambda i: (0, 0))],
        out_shape=[jax.ShapeDtypeStruct((8, Tm), I32), jax.ShapeDtypeStruct((8, Tm), I32),
                   jax.ShapeDtypeStruct((8, Tm), F32), jax.ShapeDtypeStruct((E, 128), F32)],
        scratch_shapes=[pltpu.VMEM((E, 128), F32)],
        compiler_params=_cparams(("arbitrary",)),
        name="moe_route",
    )(logits_t, router_bias.reshape(E, 1), tri)
    return idx, rank, w, cnt[:, 0].astype(I32)


def _sorted_tile_rows(n_exp):
    return -(-(MOE_TILE * TOP_K + n_exp * (ROW_ALIGN - 1)) // 16) * 16


def _run_copies(cnt_ref, src_ref, dst_ref, base, n_exp, chunk, make_copy):
    shift = chunk.bit_length() - 1

    def expert_body(e, total):
        n = cnt_ref[base + e]
        src = src_ref[base + e]
        dst = dst_ref[base + e]
        nch = lax.shift_right_logical(n + (chunk - 1), shift)

        def chunk_body(c, carry):
            make_copy(pl.multiple_of(src + c * chunk, ROW_ALIGN), pl.multiple_of(dst + c * chunk, ROW_ALIGN)).start()
            return carry

        lax.fori_loop(0, nch, chunk_body, 0)
        return total + nch

    return lax.fori_loop(0, n_exp, expert_body, jnp.int32(0), unroll=4)


def _dispatch_kernel(cnt_ref, ls_ref, gb_ref, pos_ref, h_ref, xs_hbm, stage, pending, sem, *, n_exp):
    i = pl.program_id(0)
    slot = i % 2
    TMd = h_ref.shape[0]
    R = stage.shape[1] - DISP_CHUNK

    @pl.when(i == 0)
    def _():
        pending[0] = 0
        for s in range(2):
            stage[s, R:, :] = jnp.zeros((stage.shape[1] - R, stage.shape[2]), stage.dtype)

    rio = lax.broadcasted_iota(I32, (R, TMd), 0)
    perm = jnp.zeros((R, TMd), F32)
    for kx in range(TOP_K):
        perm = jnp.where(rio == pos_ref[kx:kx + 1, :], 1.0, perm)
    stage[slot, 0:R, :] = _pack_rows(_dot(perm.astype(BF16), _unpack_rows(h_ref[...])), is_bf16_valued=True)

    def make_copy(src, dst):
        return pltpu.make_async_copy(stage.at[slot, pl.ds(src, DISP_CHUNK)], xs_hbm.at[pl.ds(dst, DISP_CHUNK)], sem)

    def wait_copies(n):
        def wait_body(c, carry):
            make_copy(0, 0).wait()
            return carry
        lax.fori_loop(0, n, wait_body, 0)

    wait_copies(pending[0])
    pending[0] = _run_copies(cnt_ref, ls_ref, gb_ref, i * n_exp, n_exp, DISP_CHUNK, make_copy)

    @pl.when(i == pl.num_programs(0) - 1)
    def _():
        wait_copies(pending[0])


def _dispatch(h2, plan, t0, n_slots, n_exp):
    cnt, ls, gb, pos = plan
    T, DP = h2.shape
    TMd = MOE_TILE
    Tm = T - t0
    assert Tm % TMd == 0 and t0 % TMd == 0
    off = t0 // TMd
    grid_spec = pltpu.PrefetchScalarGridSpec(
        num_scalar_prefetch=3,
        grid=(Tm // TMd,),
        in_specs=[pl.BlockSpec((8, TMd), lambda i, *_: (0, i)),
                  pl.BlockSpec((TMd, DP), lambda i, *_: (i + off, 0))],
        out_specs=pl.BlockSpec(memory_space=pl.ANY),
        scratch_shapes=[pltpu.VMEM((2, _sorted_tile_rows(n_exp) + DISP_CHUNK, DP), h2.dtype),
                        pltpu.SMEM((1,), I32), pltpu.SemaphoreType.DMA],
    )
    return pl.pallas_call(
        functools.partial(_dispatch_kernel, n_exp=n_exp),
        grid_spec=grid_spec,
        out_shape=jax.ShapeDtypeStruct((n_slots, DP), h2.dtype),
        compiler_params=_cparams(("arbitrary",)),
        name="moe_dispatch",
    )(cnt, ls, gb, pos, h2)


def _gmm_kernel(bexp_ref, bvalid_ref, x_ref, wg_ref, wu_ref, wd_ref, y_ref):
    del bexp_ref
    i = pl.program_id(0)
    nvalid = bvalid_ref[i]

    @pl.when(nvalid > 0)
    def _():
        row = lax.broadcasted_iota(I32, x_ref.shape, 0)
        xb = _unpack_rows(jnp.where(row < nvalid, x_ref[...], jnp.uint32(0)))
        act = _silu(_dot(xb, wg_ref[...].astype(BF16))) * _dot(xb, wu_ref[...].astype(BF16))
        y_ref[...] = _pack_rows(_dot(act.astype(BF16), wd_ref[...].astype(BF16)))

    @pl.when(nvalid == 0)
    def _():
        y_ref[...] = jnp.zeros_like(y_ref)


def _grouped_experts(xsorted, block_exp, block_valid, wg, wu, wd, layer):
    NS, DP = xsorted.shape
    nb = NS // MOE_BLOCK
    _, _, D, hid = wg.shape
    grid_spec = pltpu.PrefetchScalarGridSpec(
        num_scalar_prefetch=2,
        grid=(nb,),
        in_specs=[pl.BlockSpec((MOE_BLOCK, DP), lambda i, be, bv: (i, 0)),
                  pl.BlockSpec((None, None, D, hid), lambda i, be, bv: (layer, be[i], 0, 0)),
                  pl.BlockSpec((None, None, D, hid), lambda i, be, bv: (layer, be[i], 0, 0)),
                  pl.BlockSpec((None, None, hid, D), lambda i, be, bv: (layer, be[i], 0, 0))],
        out_specs=pl.BlockSpec((MOE_BLOCK, DP), lambda i, be, bv: (i, 0)),
    )
    return pl.pallas_call(
        _gmm_kernel,
        grid_spec=grid_spec,
        out_shape=jax.ShapeDtypeStruct((NS, DP), jnp.uint32),
        compiler_params=_cparams(("arbitrary",)),
        name="moe_grouped_experts",
    )(block_exp, block_valid, xsorted, wg, wu, wd)


def _combine_kernel(cnt_ref, so_ref, gb_ref, y_hbm, qpos_ref, w_ref, h_ref, sgu_ref, sd_ref, x_ref, m5_ref,
                    lng_ref, lnb_ref, o_ref, stage, pending, sems, *, alpha, n_exp):
    i = pl.program_id(0)
    slot = i % 2

    def make_copy(s):
        def f(dst, src):
            return pltpu.make_async_copy(y_hbm.at[pl.ds(src, COMB_CHUNK)], stage.at[s, pl.ds(dst, COMB_CHUNK)],
                                         sems.at[s])
        return f

    def fetch(tile, s):
        pending[s] = _run_copies(cnt_ref, so_ref, gb_ref, tile * n_exp, n_exp, COMB_CHUNK, make_copy(s))

    @pl.when(i == 0)
    def _():
        stage[...] = jnp.zeros_like(stage)
        fetch(0, 0)

    @pl.when(i + 1 < pl.num_programs(0))
    def _():
        fetch(i + 1, 1 - slot)

    hid = sd_ref.shape[0]
    gu = _dot(_unpack_rows(h_ref[...]), sgu_ref[...])
    y = _dot((_silu(gu[:, :hid]) * gu[:, hid:]).astype(BF16), sd_ref[...])

    def wait_body(c, carry):
        make_copy(slot)(0, 0).wait()
        return carry

    lax.fori_loop(0, pending[slot], wait_body, 0)

    TMc, S = h_ref.shape[0], stage.shape[1]
    sio = lax.broadcasted_iota(I32, (TMc, S), 1)
    q = jnp.zeros((TMc, S), F32)
    for kx in range(TOP_K):
        q = jnp.where(sio == qpos_ref[:, kx:kx + 1], w_ref[:, kx:kx + 1], q)
    y = y + _dot(q.astype(BF16), _unpack_rows(stage[slot]))
    z = alpha * x_ref[...] + m5_ref[...] * y
    o_ref[...] = _layer_norm(z, lng_ref[...], lnb_ref[...])


def _combine(dm, plan, ysorted, qpos_tm, w_tm, h2, sgu, sd, xs, m5, lng, lnb, alpha, t0, n_exp):
    cnt, so, gb = plan
    T, D = xs.shape
    TMc = MOE_TILE
    Tm = T - t0
    off = t0 // TMc
    const = lambda a: pl.BlockSpec(a.shape, lambda i, *_: (0,) * a.ndim)
    row = lambda a: pl.BlockSpec((TMc, a.shape[1]), lambda i, *_: (i + off, 0))
    grid_spec = pltpu.PrefetchScalarGridSpec(
        num_scalar_prefetch=3,
        grid=(Tm // TMc,),
        in_specs=[pl.BlockSpec(memory_space=pl.ANY),
                  pl.BlockSpec((TMc, 8), lambda i, *_: (i, 0)), pl.BlockSpec((TMc, 8), lambda i, *_: (i, 0)),
                  row(h2), const(sgu), const(sd), row(xs),
                  pl.BlockSpec((None, 1, D), lambda i, *_: (dm.group(i + off, TMc), 0, 0)),
                  const(lng), const(lnb)],
        out_specs=pl.BlockSpec((TMc, D), lambda i, *_: (i, 0)),
        scratch_shapes=[pltpu.VMEM((2, _sorted_tile_rows(n_exp), ysorted.shape[1]), ysorted.dtype),
                        pltpu.SMEM((2,), I32), pltpu.SemaphoreType.DMA((2,))],
    )
    return pl.pallas_call(
        functools.partial(_combine_kernel, alpha=alpha, n_exp=n_exp),
        grid_spec=grid_spec,
        out_shape=jax.ShapeDtypeStruct((Tm, D), F32),
        compiler_params=_cparams(("arbitrary",)),
        name="moe_combine",
    )(cnt, so, gb, ysorted, qpos_tm, w_tm, h2, sgu, sd, xs, m5, lng, lnb)


def _moe(dm, h2, logits_t, xs, m5, lng, lnb, router_bias, wg, wu, wd, layer, sgu, sd, alpha, t0):
    T = h2.shape[0]
    E = wg.shape[1]
    Tm = T - t0
    nt = Tm // MOE_TILE
    idx, rank, w, _ = _route(logits_t, router_bias, t0)
    idx, rank, w = idx[:TOP_K], rank[:TOP_K], w[:TOP_K]
    eids = jnp.arange(E, dtype=I32)
    onehot = (idx[:, :, None] == eids).reshape(TOP_K, nt, MOE_TILE, E)
    cnt = jnp.sum(onehot, axis=(0, 2), dtype=I32)
    rank_base = jnp.cumsum(cnt, axis=0) - cnt
    run = (cnt + ROW_ALIGN - 1) // ROW_ALIGN * ROW_ALIGN
    tile_base = jnp.cumsum(run, axis=0) - run
    local_start = jnp.cumsum(run, axis=1) - run
    counts = jnp.sum(run, axis=0)
    nb = -(-(Tm * TOP_K + E * (nt * (ROW_ALIGN - 1) + MOE_SLACK)) // MOE_BLOCK) + E
    padded = (counts + MOE_SLACK + MOE_BLOCK - 1) // MOE_BLOCK * MOE_BLOCK
    pends = jnp.cumsum(padded)
    pstarts = pends - padded
    global_base = pstarts[None, :] + tile_base
    bstart = jnp.arange(nb, dtype=I32) * MOE_BLOCK
    block_exp = jnp.minimum(jnp.sum((pends[None, :] <= bstart[:, None]).astype(I32), axis=1), E - 1)
    of_block = block_exp[:, None] == eids[None, :]
    seg_end = jnp.sum(jnp.where(of_block, (pstarts + counts)[None, :], 0), axis=1)
    block_valid = jnp.clip(seg_end - bstart, 0, MOE_BLOCK).astype(I32)
    look = lambda tab: jnp.sum(jnp.where(onehot, tab[None, :, None, :], 0), axis=3).reshape(TOP_K, Tm)
    pos = rank - look(rank_base) + look(local_start)
    pos8 = jnp.pad(pos, ((0, 8 - TOP_K), (0, 0)), constant_values=-1).astype(I32)
    pos_tm = jnp.pad(pos.T, ((0, 0), (0, 8 - TOP_K)), constant_values=-1).astype(I32)
    w_tm = jnp.pad(w.T, ((0, 0), (0, 8 - TOP_K)))
    flat = lambda a: a.reshape(-1).astype(I32)
    plan = (flat(run), flat(local_start), flat(global_base))
    xsorted = _dispatch(h2, plan + (pos8,), t0, nb * MOE_BLOCK, E)
    ysorted = _grouped_experts(xsorted, block_exp, block_valid, wg, wu, wd, layer)
    return _combine(dm, plan, ysorted, pos_tm, w_tm, h2, sgu, sd, xs, m5, lng, lnb, alpha, t0, E)


def kernel(x, c, ctx, c_ctx, ada_w, ada_b, ln_g, ln_b, ev_w_in, ev_w_out, na_rpb, hg_lb_raw, hg_norm_g, od_w_in, od_conv_w, od_conv_b, lru_w_a, lru_b_a, lru_w_x, lru_b_x, lru_lam, od_w_out, router_w, router_bias, exp_w_gate, exp_w_up, exp_w_down, sh_w_gate, sh_w_up, sh_w_down):
    B, N, D = x.shape
    C = ctx.shape[1]
    depth = ada_w.shape[0]
    dm = _Dims(B, N, C, D)
    alpha = float((2 * depth) ** 0.25)
    assert B + 1 <= 8

    cvec = jnp.zeros((8, D), F32).at[:B].set(c).at[B].set(c_ctx)
    mod = _modulation(cvec, ada_w, ada_b).reshape(depth, 8, 6, 1, D)

    p_lb = jax.nn.softmax(hg_lb_raw.astype(F32), axis=1)
    hg_lb = jnp.cumsum(p_lb, axis=1) - p_lb[:, :1]

    xs = jnp.concatenate([ctx.reshape(B * C, D), x.reshape(B * N, D)], axis=0)
    for l in range(depth):
        jl = l // 2
        last = l == depth - 1
        m = [mod[l, :, t] for t in range(6)]
        rw_t = router_w[l].T
        rwh, rwl = _split(rw_t)
        lng = ln_g[l][:, None, :]
        lnb = ln_b[l][:, None, :]
        if l % 2 == 0:
            q, k, v, hq, ff, fb, hi, hg = _inproj(dm, xs, m[1], m[0], ev_w_in[jl].astype(BF16), True)
            bias = _na_bias_tables(na_rpb[jl], N // GRID_W)
            na_lat = _neighbourhood_attention(dm, q, k, v, bias)
            na_ctx = _context_attention(dm, q, k, v)
            o_f, o_b = _hgrn2(dm, hq, ff, fb, hi, hg_lb[:, jl])
            ng = jnp.tile(hg_norm_g[jl], hq.shape[1] // HEAD_DIM)[None, :]
            mixer_inputs = (na_ctx, na_lat, o_f, o_b, hg, ng)
            w_out = ev_w_out[jl].astype(BF16)
        else:
            y, u = _inproj(dm, xs, m[1], m[0], od_w_in[jl].astype(BF16), False)
            h_f, h_b = _rglru(dm, u, od_conv_w[jl], od_conv_b[jl], lru_w_a[jl], lru_b_a[jl],
                              lru_w_x[jl], lru_b_x[jl], lru_lam[jl])
            mixer_inputs = (h_f, h_b, y)
            w_out = od_w_out[jl].astype(BF16)
        xs, h2, logits_t = _post_mixer(dm, mixer_inputs, w_out, xs, m[2], m[3], m[4], lng[0], lnb[0],
                                       rwh, rwl, alpha, l % 2 == 0)
        sgu = jnp.concatenate([sh_w_gate[l], sh_w_up[l]], axis=-1).astype(BF16)
        sd = sh_w_down[l].astype(BF16)
        t0 = dm.T_ctx if last else 0
        xs = _moe(dm, h2, logits_t, xs, m[5], lng[1], lnb[1], router_bias[l], exp_w_gate, exp_w_up,
                  exp_w_down, l, sgu, sd, alpha, t0)
    return xs.reshape(B, N, D)
```

```python
import functools

import numpy as np
import jax
import jax.numpy as jnp
from jax import lax
from jax.experimental import pallas as pl
from jax.experimental.pallas import tpu as pltpu

F32 = jnp.float32
BF16 = jnp.bfloat16
I32 = jnp.int32

HEAD_DIM = 64
GRID_W = 64
NA_WIN_ROWS = 8
NA_WIN_COLS = 16
NA_QROWS = 4
NA_KGROUP_ROWS = 4
NA_KROWS = 12
HG_CHUNK = 64
HG_HEADS_PER_GROUP = 4
FORGET_FLOOR = 1e-30
HG_SAFE_EXP = 80.0
HG_CLIP_EXP = 85.0
LRU_C = 8.0
LRU_BLOCKS = 16
CONV_W = 4
TOP_K = 6
ROUTED_SCALE = 2.5
LN_EPS = 1e-5
RMS_EPS = 1e-6
SEQ_BLOCK = 256
MOE_BLOCK = 1024
MOE_TILE = 256
ROW_ALIGN = 8
DISP_CHUNK = 32
COMB_CHUNK = ROW_ALIGN
MOE_SLACK = DISP_CHUNK
MASK_VALUE = -1e30
VMEM_LIMIT = 56 * 1024 * 1024


def _cparams(sem):
    return pltpu.CompilerParams(dimension_semantics=sem, vmem_limit_bytes=VMEM_LIMIT)


def _split(a):
    hi = a.astype(BF16)
    lo = (a - hi.astype(F32)).astype(BF16)
    return hi, lo


def _dot(a, b):
    return jnp.dot(a, b, preferred_element_type=F32)


def _dot_nt(a, b):
    return lax.dot_general(a, b, (((1,), (1,)), ((), ())), preferred_element_type=F32)


def _dot_tn(a, b):
    return lax.dot_general(a, b, (((0,), (0,)), ((), ())), preferred_element_type=F32)


def _dot3(a, b):
    ah, al = _split(a)
    bh, bl = _split(b)
    return _dot(ah, bh) + _dot(al, bh) + _dot(ah, bl)


def _pack_rows(h, is_bf16_valued=False):
    half = h.shape[1] // 2
    if not is_bf16_valued:
        h = h.astype(BF16).astype(F32)
    bits = pltpu.bitcast(h, jnp.uint32)
    return (bits[:, :half] >> 16) | (bits[:, half:] & jnp.uint32(0xFFFF0000))


def _unpack_rows(p):
    lo = pltpu.bitcast(p << 16, F32)
    hi = pltpu.bitcast(p & jnp.uint32(0xFFFF0000), F32)
    return jnp.concatenate([lo, hi], axis=1).astype(BF16)


def _silu(v):
    return v * jax.nn.sigmoid(v)


def _gelu_tanh(v):
    return 0.5 * v * (1.0 + jnp.tanh(0.7978845608028654 * (v + 0.044715 * v * v * v)))


def _layer_norm(z, g, b):
    mu = jnp.mean(z, axis=-1, keepdims=True)
    zc = z - mu
    var = jnp.mean(zc * zc, axis=-1, keepdims=True)
    return zc * lax.rsqrt(var + LN_EPS) * g + b


def _pick_tile(n, cands):
    for c in cands:
        if n % c == 0:
            return c
    raise ValueError(f"no tile for {n}")


class _Dims:
    def __init__(self, B, N, C, D):
        self.B, self.N, self.C, self.D = B, N, C, D
        self.T_ctx = B * C
        self.T = B * C + B * N
        assert C == SEQ_BLOCK and N % SEQ_BLOCK == 0
        assert N % GRID_W == 0 and (N // GRID_W) % NA_QROWS == 0 and N // GRID_W >= NA_KROWS
        assert self.T_ctx % 512 == 0
        self.TM = _pick_tile(self.T_ctx, (512, 256))
        assert N % self.TM == 0

    def group(self, i, tile):
        start = i * tile
        return jnp.where(start < self.T_ctx, self.B, (start - self.T_ctx) // self.N)


def _mod_kernel(c_ref, w_ref, b_ref, o_ref):
    o_ref[...] = _dot3(_silu(c_ref[...]), w_ref[...]) + b_ref[...]


def _modulation(cvec, ada_w, ada_b):
    L, D, W6 = ada_w.shape
    nc = 1536
    return pl.pallas_call(
        _mod_kernel,
        grid=(L, W6 // nc),
        in_specs=[
            pl.BlockSpec((8, D), lambda l, j: (0, 0)),
            pl.BlockSpec((None, D, nc), lambda l, j: (l, 0, j)),
            pl.BlockSpec((None, 1, nc), lambda l, j: (l, 0, j)),
        ],
        out_specs=pl.BlockSpec((None, 8, nc), lambda l, j: (l, 0, j)),
        out_shape=jax.ShapeDtypeStruct((L, 8, W6), F32),
        compiler_params=_cparams(("arbitrary", "arbitrary")),
        name="adaln_modulation",
    )(cvec, ada_w, ada_b.reshape(L, 1, W6))


def _inproj_even_kernel(x_ref, sc_ref, sh_ref, w_ref, q_ref, k_ref, v_ref, hq_ref, ff_ref, fb_ref, hi_ref, hg_ref):
    h = (x_ref[...] * (1.0 + sc_ref[...]) + sh_ref[...]).astype(BF16)
    wd = q_ref.shape[1]

    def part(j):
        return _dot(h, w_ref[:, j * wd:(j + 1) * wd])

    scale = HEAD_DIM ** -0.5
    q_ref[...] = (part(0) * scale).astype(BF16)
    k_ref[...] = part(1).astype(BF16)
    v_ref[...] = part(2).astype(BF16)
    hq_ref[...] = _silu(part(3)) * scale
    ff_ref[...] = part(4)
    fb_ref[...] = part(5)
    hi_ref[...] = part(6)
    hg_ref[...] = part(7)


def _inproj_odd_kernel(x_ref, sc_ref, sh_ref, w_ref, y_ref, u_ref):
    h = (x_ref[...] * (1.0 + sc_ref[...]) + sh_ref[...]).astype(BF16)
    wd = y_ref.shape[1]
    y_ref[...] = _dot(h, w_ref[:, :wd])
    u_ref[...] = _dot(h, w_ref[:, wd:])


def _inproj(dm, xs, sc, sh, w, even):
    T, D, TM = dm.T, dm.D, dm.TM
    wtot = w.shape[1]
    mod_spec = pl.BlockSpec((None, 1, D), lambda i: (dm.group(i, TM), 0, 0))
    in_specs = [pl.BlockSpec((TM, D), lambda i: (i, 0)), mod_spec, mod_spec,
                pl.BlockSpec((D, wtot), lambda i: (0, 0))]
    if even:
        wd = wtot // 8
        dts = [BF16] * 3 + [F32] * 5
        kern = _inproj_even_kernel
    else:
        wd = wtot // 2
        dts = [F32] * 2
        kern = _inproj_odd_kernel
    return pl.pallas_call(
        kern,
        grid=(T // TM,),
        in_specs=in_specs,
        out_specs=[pl.BlockSpec((TM, wd), lambda i: (i, 0)) for _ in dts],
        out_shape=[jax.ShapeDtypeStruct((T, wd), dt) for dt in dts],
        compiler_params=_cparams(("parallel",)),
        name="inproj_even" if even else "inproj_odd",
    )(xs, sc, sh, w)


def _na_bias_tables(rpb, rows):
    W = GRID_W
    H = rpb.shape[0]
    nr, nc = 2 * NA_WIN_ROWS - 1, 2 * NA_WIN_COLS - 1
    hp = lax.Precision.HIGHEST
    c = np.arange(W)[:, None]
    kc = np.arange(W)[None, :]
    c0 = np.clip(c - NA_WIN_COLS // 2, 0, W - NA_WIN_COLS)
    valid_c = (kc >= c0) & (kc < c0 + NA_WIN_COLS)
    dc = np.clip(kc - c + NA_WIN_COLS - 1, 0, nc - 1)
    oh_c = (dc.reshape(-1, 1) == np.arange(nc)[None, :]).astype(np.float32)
    by_col = jnp.einsum('hrs,xs->hrx', rpb.astype(F32), jnp.asarray(oh_c), precision=hp)
    tabs = []
    for rbase, kb in ((0, 0), (NA_QROWS, NA_QROWS - NA_WIN_ROWS // 2), (rows - NA_QROWS, rows - NA_KROWS)):
        r = rbase + np.arange(NA_QROWS)[:, None]
        kr = kb + np.arange(NA_KROWS)[None, :]
        r0 = np.clip(r - NA_WIN_ROWS // 2, 0, rows - NA_WIN_ROWS)
        valid_r = (kr >= r0) & (kr < r0 + NA_WIN_ROWS)
        dr = np.clip(kr - r + NA_WIN_ROWS - 1, 0, nr - 1)
        oh_r = (dr.reshape(-1, 1) == np.arange(nr)[None, :]).astype(np.float32)
        t = jnp.einsum('yr,hrx->hyx', jnp.asarray(oh_r), by_col, precision=hp)
        t = t.reshape(H, NA_QROWS, NA_KROWS, W, W).transpose(0, 1, 3, 2, 4)
        valid = valid_r[:, None, :, None] & valid_c[None, :, None, :]
        t = jnp.where(jnp.asarray(valid)[None], t, MASK_VALUE)
        tabs.append(t.reshape(H, NA_QROWS * W, NA_KROWS * W))
    return jnp.stack(tabs).astype(F32)


def _na_kernel(q_ref, *refs):
    nkb = NA_KROWS // NA_KGROUP_ROWS
    k_refs, v_refs = refs[:nkb], refs[nkb:2 * nkb]
    kc_ref, vc_ref, bias_ref, o_ref = refs[2 * nkb:]
    q2 = q_ref[...]
    lane = lax.broadcasted_iota(I32, (1, q2.shape[1]), 1)
    first = lane < HEAD_DIM
    ks = [r[...] for r in k_refs]
    vs = [r[...] for r in v_refs]
    kc = kc_ref[...]
    vc = vc_ref[...]
    kb = ks[0].shape[0]
    outs = []
    for hh in range(2):
        sel = first if hh == 0 else jnp.logical_not(first)
        qh = jnp.where(sel, q2, jnp.zeros_like(q2))
        s_loc = jnp.concatenate([_dot_nt(qh, kj) for kj in ks], axis=1) + bias_ref[hh]
        s_ctx = _dot_nt(qh, kc)
        m = jnp.maximum(jnp.max(s_loc, axis=1, keepdims=True), jnp.max(s_ctx, axis=1, keepdims=True))
        p_loc = jnp.exp(s_loc - m)
        p_ctx = jnp.exp(s_ctx - m)
        denom = jnp.sum(p_loc, axis=1, keepdims=True) + jnp.sum(p_ctx, axis=1, keepdims=True)
        o = _dot(p_ctx.astype(BF16), vc)
        for j in range(nkb):
            o = o + _dot(p_loc[:, j * kb:(j + 1) * kb].astype(BF16), vs[j])
        outs.append(o / denom)
    o_ref[...] = jnp.where(first, outs[0], outs[1]).astype(o_ref.dtype)


def _neighbourhood_attention(dm, q, k, v, bias):
    B, N, C, T = dm.B, dm.N, dm.C, dm.T
    H2 = q.shape[1] // (2 * HEAD_DIM)
    rows = N // GRID_W
    nqb = rows // NA_QROWS
    QB = NA_QROWS * GRID_W
    KB = NA_KGROUP_ROWS * GRID_W
    nkb = NA_KROWS // NA_KGROUP_ROWS
    ngroups = N // KB
    lat_q0 = dm.T_ctx // QB
    lat_k0 = dm.T_ctx // KB

    def q_map(p, i, b):
        return (lat_q0 + b * (N // QB) + i, p)

    def o_map(p, i, b):
        return (b * (N // QB) + i, p)

    def kv_map(j):
        def f(p, i, b):
            gs = jnp.clip(i * (QB // KB) - 1, 0, ngroups - nkb)
            return (lat_k0 + b * ngroups + gs + j, p)
        return f

    def ctx_map(p, i, b):
        return (b, p)

    def bias_map(p, i, b):
        var = jnp.where(i == 0, 0, jnp.where(i == nqb - 1, 2, 1))
        return (var, p, 0, 0)

    lanes = 2 * HEAD_DIM
    kv_specs = [pl.BlockSpec((KB, lanes), kv_map(j)) for j in range(nkb)]
    return pl.pallas_call(
        _na_kernel,
        grid=(H2, nqb, B),
        in_specs=[pl.BlockSpec((QB, lanes), q_map)] + kv_specs + kv_specs
        + [pl.BlockSpec((C, lanes), ctx_map), pl.BlockSpec((C, lanes), ctx_map),
           pl.BlockSpec((None, 2, QB, NA_KROWS * GRID_W), bias_map)],
        out_specs=pl.BlockSpec((QB, lanes), o_map),
        out_shape=jax.ShapeDtypeStruct((B * N, q.shape[1]), BF16),
        compiler_params=_cparams(("arbitrary", "arbitrary", "arbitrary")),
        name="neighbourhood_attention",
    )(q, *([k] * nkb), *([v] * nkb), k, v, bias)


def _ctx_attn_kernel(q_ref, k_ref, v_ref, o_ref):
    q2 = q_ref[...]
    k2 = k_ref[...]
    v2 = v_ref[...]
    lane = lax.broadcasted_iota(I32, (1, q2.shape[1]), 1)
    first = lane < HEAD_DIM
    outs = []
    for hh in range(2):
        sel = first if hh == 0 else jnp.logical_not(first)
        s = _dot_nt(jnp.where(sel, q2, jnp.zeros_like(q2)), k2)
        p = jnp.exp(s - jnp.max(s, axis=1, keepdims=True))
        outs.append(_dot(p.astype(BF16), v2) / jnp.sum(p, axis=1, keepdims=True))
    o_ref[...] = jnp.where(first, outs[0], outs[1]).astype(o_ref.dtype)


def _context_attention(dm, q, k, v):
    lanes = 2 * HEAD_DIM
    spec = pl.BlockSpec((dm.C, lanes), lambda p, b: (b, p))
    return pl.pallas_call(
        _ctx_attn_kernel,
        grid=(q.shape[1] // lanes, dm.B),
        in_specs=[spec, spec, spec],
        out_specs=spec,
        out_shape=jax.ShapeDtypeStruct((dm.T_ctx, q.shape[1]), BF16),
        compiler_params=_cparams(("arbitrary", "arbitrary")),
        name="context_attention",
    )(q, k, v)


def _seq_block_maps(dm):
    nl = dm.N // SEQ_BLOCK
    base = dm.T_ctx // SEQ_BLOCK

    def fwd(b, j):
        return jnp.where(j == 0, b, base + b * nl + j - 1)

    def bwd(b, j):
        return jnp.where(j == 0, b, base + b * nl + nl - j)

    return fwd, bwd


def _hgrn_prepare(hq, z, lb, tri, last_row, mid_row):
    f = lb + (1.0 - lb) * jax.nn.sigmoid(z)
    fm = jnp.maximum(f, FORGET_FLOOR)
    g = jnp.log(fm)
    kk = 1.0 - fm
    gh, gl = _split(g)
    cum = _dot(tri, gh) + _dot(tri, gl)
    tail = cum[last_row:last_row + 1, :]
    e = cum - cum[mid_row:mid_row + 1, :]
    emax = jnp.max(jnp.max(jnp.abs(e), axis=1, keepdims=True), axis=0, keepdims=True)
    ec = jnp.clip(e, -HG_CLIP_EXP, HG_CLIP_EXP)
    qe = (hq * jnp.exp(ec)).astype(BF16)
    ke = (kk * jnp.exp(-ec)).astype(BF16)
    return kk, cum, tail, emax, qe, ke


def _hgrn_kernel(hqf_ref, zf_ref, vf_ref, hqb_ref, zb_ref, vb_ref, lbf_ref, lbb_ref,
                 of_ref, ob_ref, st_ref, a_ref, cum_ref, k_ref, qs_ref, k2_ref, et_ref):
    j = pl.program_id(1)

    @pl.when(j == 0)
    def _():
        st_ref[...] = jnp.zeros_like(st_ref)

    CH = HG_CHUNK
    GW = HG_HEADS_PER_GROUP * HEAD_DIM
    ngroups = hqf_ref.shape[1] // GW
    nchunks = hqf_ref.shape[0] // CH
    t_i = lax.broadcasted_iota(I32, (CH, CH), 0)
    u_i = lax.broadcasted_iota(I32, (CH, CH), 1)
    tris = ((u_i <= t_i).astype(BF16), (u_i >= t_i).astype(BF16))
    t_w = lax.broadcasted_iota(I32, (CH, GW), 0)
    s_w = lax.broadcasted_iota(I32, (CH, GW), 1) % CH
    cmasks = (s_w <= t_w, s_w >= t_w)
    r_b = lax.broadcasted_iota(I32, (GW, GW), 0)
    c_b = lax.broadcasted_iota(I32, (GW, GW), 1)
    bm = (r_b // HEAD_DIM) == (c_b // HEAD_DIM)
    hsel_base = (r_b // HEAD_DIM) * HEAD_DIM
    dirs = ((hqf_ref, zf_ref, vf_ref, lbf_ref, of_ref), (hqb_ref, zb_ref, vb_ref, lbb_ref, ob_ref))
    chains = [(d, g) for d in range(2) for g in range(ngroups)]

    def expand(m):
        return jnp.where(bm, jnp.concatenate([m] * HG_HEADS_PER_GROUP, axis=0), jnp.zeros((GW, GW), m.dtype))

    nchain = len(chains)

    def rows_of(c):
        return pl.multiple_of(c * CH, CH), pl.multiple_of((nchunks - 1 - c) * CH, CH)

    def prep_body(c, worst):
        rows = rows_of(c)
        for n, (d, g) in enumerate(chains):
            hq_r, z_r, _, lb_r, _ = dirs[d]
            ls = slice(g * GW, (g + 1) * GW)
            hq = hq_r[pl.ds(rows[d], CH), ls]
            z = z_r[pl.ds(rows[d], CH), ls]
            last_row = CH - 1 if d == 0 else 0
            kk, cum, tail, emax, qe, ke = _hgrn_prepare(hq, z, lb_r[:, ls], tris[d], last_row, CH // 2)
            m = c * nchain + n
            a_ref[m] = _dot_nt(qe, expand(ke))
            cum_ref[m] = cum
            k_ref[m] = kk
            qs_ref[m] = hq * jnp.exp(cum)
            k2_ref[m] = kk * jnp.exp(tail - cum)
            et_ref[m] = jnp.broadcast_to(jnp.exp(tail), et_ref.shape[1:])
            worst = jnp.maximum(worst, emax)
        return worst

    worst = lax.fori_loop(0, nchunks, prep_body, jnp.zeros((1, 1), F32), unroll=True)

    @pl.when(worst[0, 0] > HG_SAFE_EXP)
    def _():
        def slow_body(c, carry):
            rows = rows_of(c)
            for n, (d, g) in enumerate(chains):
                m = c * nchain + n
                hq = dirs[d][0][pl.ds(rows[d], CH), g * GW:(g + 1) * GW]
                cum = cum_ref[m]

                def key_body(s, acc):
                    cs = cum_ref[m, pl.ds(s, 1), :]
                    ksr = k_ref[m, pl.ds(s, 1), :]
                    p = hq * ksr * jnp.exp(jnp.minimum(cum - cs, 0.0))
                    hsel = jnp.where(c_b == hsel_base + s, 1.0, 0.0).astype(BF16)
                    return acc + _dot(p.astype(BF16), hsel)

                a_ref[m] = lax.fori_loop(0, CH, key_body, jnp.zeros((CH, GW), F32))
            return carry

        lax.fori_loop(0, nchunks, slow_body, 0)

    def state_body(c, carry):
        rows = rows_of(c)
        for n, (d, g) in enumerate(chains):
            m = c * nchain + n
            ls = slice(g * GW, (g + 1) * GW)
            vb = dirs[d][2][pl.ds(rows[d], CH), ls].astype(BF16)
            st = st_ref[n]
            a = jnp.where(cmasks[d], a_ref[m], 0.0).astype(BF16)
            o = _dot(a, expand(vb)) + _dot_nt(qs_ref[m].astype(BF16), st.astype(BF16))
            dirs[d][4][pl.ds(rows[d], CH), ls] = o
            st_ref[n] = et_ref[m, 0:1, :] * st + jnp.where(bm, _dot_tn(vb, k2_ref[m].astype(BF16)), 0.0)
        return carry

    lax.fori_loop(0, nchunks, state_body, 0, unroll=True)


def _hgrn2(dm, hq, ff, fb, hi, lb):
    B, T = dm.B, dm.T
    Wd = hq.shape[1]
    GW = HG_HEADS_PER_GROUP * HEAD_DIM
    nchain = 2 * (Wd // GW)
    nsteps = nchain * (SEQ_BLOCK // HG_CHUNK)
    nblk = 1 + dm.N // SEQ_BLOCK
    fwd, bwd = _seq_block_maps(dm)
    fspec = pl.BlockSpec((SEQ_BLOCK, Wd), lambda b, j: (fwd(b, j), 0))
    bspec = pl.BlockSpec((SEQ_BLOCK, Wd), lambda b, j: (bwd(b, j), 0))
    lbspec = pl.BlockSpec((1, Wd), lambda b, j: (0, 0))
    return pl.pallas_call(
        _hgrn_kernel,
        grid=(B, nblk),
        in_specs=[fspec, fspec, fspec, bspec, bspec, bspec, lbspec, lbspec],
        out_specs=[fspec, bspec],
        out_shape=[jax.ShapeDtypeStruct((T, Wd), F32)] * 2,
        scratch_shapes=[pltpu.VMEM((nchain, GW, GW), F32)]
        + [pltpu.VMEM((nsteps, HG_CHUNK, GW), F32)] * 5 + [pltpu.VMEM((nsteps, 8, GW), F32)],
        compiler_params=_cparams(("arbitrary", "arbitrary")),
        name="hgrn2_bidirectional",
    )(hq, ff, hi, hq, fb, hi, lb[0:1], lb[1:2])


LRU_GROUP = 8


def _lru_group_scan(a, x, reverse):
    n, w = a.shape
    a = a.reshape(n // LRU_GROUP, LRU_GROUP, w)
    x = x.reshape(n // LRU_GROUP, LRU_GROUP, w)
    row = lax.broadcasted_iota(I32, a.shape, 1)
    s = 1
    while s < LRU_GROUP:
        shift = (LRU_GROUP - s) if reverse else s
        a_sh = pltpu.roll(a, shift, 1)
        x_sh = pltpu.roll(x, shift, 1)
        valid = (row < LRU_GROUP - s) if reverse else (row >= s)
        x = jnp.where(valid, a * x_sh + x, x)
        a = jnp.where(valid, a * a_sh, a)
        s *= 2
    return a.reshape(n, w), x.reshape(n, w)


def _lru_kernel(ucf_ref, upf_ref, unf_ref, ucb_ref, upb_ref, unb_ref, cw_ref, cb_ref, wbd_ref,
                ba_ref, bx_ref, lam_ref, hf_ref, hb_ref, ext_ref, carry_ref, sa_ref, sx_ref):
    j = pl.program_id(1)
    nblk = pl.num_programs(1)

    @pl.when(j == 0)
    def _():
        carry_ref[...] = jnp.zeros_like(carry_ref)

    TT, W = ucf_ref.shape
    GW = wbd_ref.shape[2]
    halo = upf_ref.shape[0]
    dirs = ((ucf_ref, upf_ref, unf_ref, hf_ref), (ucb_ref, upb_ref, unb_ref, hb_ref))
    for d, (uc_r, up_r, un_r, out_r) in enumerate(dirs):
        pos = j if d == 0 else jnp.where(j == 0, 0, nblk - j)
        keep_prev = jnp.where((pos == 0) | (pos == 1), 0.0, 1.0)
        keep_next = jnp.where((pos == 0) | (pos == nblk - 1), 0.0, 1.0)
        ext_ref[d, 0:halo, :] = up_r[...] * keep_prev
        ext_ref[d, halo:halo + TT, :] = uc_r[...]
        ext_ref[d, halo + TT:halo + TT + halo, :] = un_r[...] * keep_next
        for g in range(W // GW):
            ls = slice(g * GW, (g + 1) * GW)
            u = cb_ref[:, ls]
            for tap in range(CONV_W):
                off = halo - 2 + tap
                u = u + cw_ref[tap:tap + 1, ls] * ext_ref[d, off:off + TT, ls]
            gates = _dot(u.astype(BF16), wbd_ref[d, g])
            r = jax.nn.sigmoid(gates[:, :GW] + ba_ref[d:d + 1, ls])
            ig = jax.nn.sigmoid(gates[:, GW:] + bx_ref[d:d + 1, ls])
            nl = -lam_ref[d:d + 1, ls]
            softplus = jnp.maximum(nl, 0.0) + jnp.log1p(jnp.exp(-jnp.abs(nl)))
            log_a = -LRU_C * r * softplus
            a = jnp.exp(log_a)
            gain2 = 1.0 - a * a
            gain = jnp.where(gain2 > 0.0, gain2 * lax.rsqrt(gain2), 0.0)
            x_in = gain * ig * u
            a_acc, h = _lru_group_scan(a, x_in, d == 1)
            sa_ref[d, :, ls] = a_acc
            sx_ref[d, :, ls] = h

    ngroups = TT // LRU_GROUP

    def group_body(gi, carry):
        hf, hb = carry
        rf = pl.multiple_of(gi * LRU_GROUP, LRU_GROUP)
        rb = pl.multiple_of((ngroups - 1 - gi) * LRU_GROUP, LRU_GROUP)
        of = sx_ref[0, pl.ds(rf, LRU_GROUP), :] + sa_ref[0, pl.ds(rf, LRU_GROUP), :] * hf
        ob = sx_ref[1, pl.ds(rb, LRU_GROUP), :] + sa_ref[1, pl.ds(rb, LRU_GROUP), :] * hb
        hf_ref[pl.ds(rf, LRU_GROUP), :] = of
        hb_ref[pl.ds(rb, LRU_GROUP), :] = ob
        return of[LRU_GROUP - 1:LRU_GROUP, :], ob[0:1, :]

    hf, hb = lax.fori_loop(0, ngroups, group_body, (carry_ref[0, 0:1, :], carry_ref[1, 0:1, :]), unroll=4)
    carry_ref[0, 0:1, :] = hf
    carry_ref[1, 0:1, :] = hb


def _block_diag_gates(w_a, w_x, group):
    ndir, K, d, _ = w_a.shape
    per = group // d
    eye = jnp.eye(per, dtype=w_a.dtype)

    def bd(w):
        w = w.reshape(ndir, K // per, per, d, d)
        full = jnp.einsum('ngkde,kl->ngkdle', w, eye)
        return full.reshape(ndir, K // per, group, group)

    return jnp.concatenate([bd(w_a), bd(w_x)], axis=-1)


def _rglru(dm, u, conv_w, conv_b, w_a, b_a, w_x, b_x, lam):
    B, T = dm.B, dm.T
    W = u.shape[1]
    GW = 256
    halo = 8
    nblk = 1 + dm.N // SEQ_BLOCK
    fwd, bwd = _seq_block_maps(dm)
    hb = SEQ_BLOCK // halo
    nh = T // halo
    wbd = _block_diag_gates(w_a, w_x, GW).astype(BF16)

    def cur(m):
        return pl.BlockSpec((SEQ_BLOCK, W), lambda b, j: (m(b, j), 0))

    def prev(m):
        return pl.BlockSpec((halo, W), lambda b, j: (jnp.maximum(m(b, j) * hb - 1, 0), 0))

    def nxt(m):
        return pl.BlockSpec((halo, W), lambda b, j: (jnp.minimum((m(b, j) + 1) * hb, nh - 1), 0))

    def full(a):
        nd = a.ndim
        return pl.BlockSpec(a.shape, lambda b, j: (0,) * nd)

    consts = (conv_w, conv_b.reshape(1, W), wbd, b_a, b_x, lam)
    return pl.pallas_call(
        _lru_kernel,
        grid=(B, nblk),
        in_specs=[cur(fwd), prev(fwd), nxt(fwd), cur(bwd), prev(bwd), nxt(bwd)] + [full(a) for a in consts],
        out_specs=[cur(fwd), cur(bwd)],
        out_shape=[jax.ShapeDtypeStruct((T, W), F32)] * 2,
        scratch_shapes=[pltpu.VMEM((2, SEQ_BLOCK + 2 * halo, W), F32), pltpu.VMEM((2, 8, W), F32),
                        pltpu.VMEM((2, SEQ_BLOCK, W), F32), pltpu.VMEM((2, SEQ_BLOCK, W), F32)],
        compiler_params=_cparams(("arbitrary", "arbitrary")),
        name="rglru_bidirectional",
    )(u, u, u, u, u, u, *consts)


def _post_mixer_epilogue(o, x_ref, m2_ref, m3_ref, m4_ref, lng_ref, lnb_ref, rwh_ref, rwl_ref,
                         xo_ref, h2_ref, lg_ref, alpha):
    z = alpha * x_ref[...] + m2_ref[...] * o
    xn = _layer_norm(z, lng_ref[...], lnb_ref[...])
    xo_ref[...] = xn
    h2 = xn * (1.0 + m4_ref[...]) + m3_ref[...]
    h2_ref[...] = _pack_rows(h2)
    hh, hl = _split(h2)
    lg_ref[...] = _dot_nt(rwh_ref[...], hh) + _dot_nt(rwh_ref[...], hl) + _dot_nt(rwl_ref[...], hh)


def _post_even_kernel(nac_ref, nal_ref, of_ref, ob_ref, hg_ref, ng_ref, w_ref, *rest, alpha, ctx_tiles):
    na = jnp.where(pl.program_id(0) < ctx_tiles, nac_ref[...], nal_ref[...])
    o = of_ref[...] + ob_ref[...]
    wd = o.shape[1]
    r_i = lax.broadcasted_iota(I32, (wd, wd), 0) // HEAD_DIM
    c_i = lax.broadcasted_iota(I32, (wd, wd), 1) // HEAD_DIM
    avg = jnp.where(r_i == c_i, 1.0 / HEAD_DIM, 0.0).astype(BF16)
    sh, sl = _split(o * o)
    ms = _dot(sh, avg) + _dot(sl, avg)
    r = o * lax.rsqrt(ms + RMS_EPS) * ng_ref[...] * _silu(hg_ref[...])
    mix = _dot(na, w_ref[:wd, :]) + _dot(r.astype(BF16), w_ref[wd:, :])
    _post_mixer_epilogue(mix, *rest, alpha=alpha)


def _post_odd_kernel(hf_ref, hb_ref, y_ref, w_ref, *rest, alpha):
    m = ((hf_ref[...] + hb_ref[...]) * _gelu_tanh(y_ref[...])).astype(BF16)
    _post_mixer_epilogue(_dot(m, w_ref[...]), *rest, alpha=alpha)


def _post_mixer(dm, mixer_inputs, w_out, xs, m2, m3, m4, lng, lnb, rwh, rwl, alpha, even):
    T, D, TM = dm.T, dm.D, dm.TM
    E = rwh.shape[0]
    row = lambda a: pl.BlockSpec((TM, a.shape[1]), lambda i: (i, 0))
    const = lambda a: pl.BlockSpec(a.shape, lambda i: (0,) * a.ndim)
    mod_spec = pl.BlockSpec((None, 1, D), lambda i: (dm.group(i, TM), 0, 0))
    if even:
        nac, nal, of, ob, hg, ng = mixer_inputs
        nct = dm.T_ctx // TM
        ins = [nac, nal, of, ob, hg, ng, w_out]
        specs = [pl.BlockSpec((TM, nac.shape[1]), lambda i: (jnp.minimum(i, nct - 1), 0)),
                 pl.BlockSpec((TM, nal.shape[1]), lambda i: (jnp.maximum(i - nct, 0), 0)),
                 row(of), row(ob), row(hg), const(ng), const(w_out)]
        kern = functools.partial(_post_even_kernel, alpha=alpha, ctx_tiles=nct)
    else:
        hf, hb, y = mixer_inputs
        ins = [hf, hb, y, w_out]
        specs = [row(hf), row(hb), row(y), const(w_out)]
        kern = functools.partial(_post_odd_kernel, alpha=alpha)
    ins += [xs, m2, m3, m4, lng, lnb, rwh, rwl]
    specs += [row(xs), mod_spec, mod_spec, mod_spec, const(lng), const(lnb), const(rwh), const(rwl)]
    return pl.pallas_call(
        kern,
        grid=(T // TM,),
        in_specs=specs,
        out_specs=[pl.BlockSpec((TM, D), lambda i: (i, 0)), pl.BlockSpec((TM, D // 2), lambda i: (i, 0)),
                   pl.BlockSpec((E, TM), lambda i: (0, i))],
        out_shape=[jax.ShapeDtypeStruct((T, D), F32), jax.ShapeDtypeStruct((T, D // 2), jnp.uint32),
                   jax.ShapeDtypeStruct((E, T), F32)],
        compiler_params=_cparams(("parallel",)),
        name="post_mixer_even" if even else "post_mixer_odd",
    )(*ins)


def _route_kernel(lg_ref, bias_ref, tri_ref, idx_ref, rank_ref, w_ref, cnt_ref, carry_ref):
    i = pl.program_id(0)

    @pl.when(i == 0)
    def _():
        carry_ref[...] = jnp.zeros_like(carry_ref)

    E, TK = lg_ref.shape
    s = jax.nn.sigmoid(lg_ref[...])
    work = s + bias_ref[...]
    eio = lax.broadcasted_iota(I32, (E, TK), 0).astype(F32)
    picked = jnp.zeros((E, TK), F32)
    sels, idxs = [], []
    for _ in range(TOP_K):
        m = jnp.max(work, axis=0, keepdims=True)
        ik = jnp.min(jnp.where(work == m, eio, float(E)), axis=0, keepdims=True)
        oh = eio == ik
        sels.append(jnp.sum(jnp.where(oh, s, 0.0), axis=0, keepdims=True))
        idxs.append(ik)
        picked = jnp.where(oh, 1.0, picked)
        work = jnp.where(oh, -jnp.inf, work)
    total = sels[0]
    for sk in sels[1:]:
        total = total + sk
    carry = carry_ref[:, 0:1]
    ranks = carry + _dot(picked.astype(BF16), tri_ref[...])
    out_rows = idx_ref.shape[0]
    rio = lax.broadcasted_iota(I32, (out_rows, TK), 0)
    idx_o = jnp.zeros((out_rows, TK), I32)
    rank_o = jnp.zeros((out_rows, TK), I32)
    w_o = jnp.zeros((out_rows, TK), F32)
    for kx in range(TOP_K):
        rk = jnp.sum(jnp.where(eio == idxs[kx], ranks, 0.0), axis=0, keepdims=True).astype(I32)
        idx_o = jnp.where(rio == kx, idxs[kx].astype(I32), idx_o)
        rank_o = jnp.where(rio == kx, rk, rank_o)
        w_o = jnp.where(rio == kx, sels[kx] / total * ROUTED_SCALE, w_o)
    idx_ref[...] = idx_o
    rank_ref[...] = rank_o
    w_ref[...] = w_o
    new_carry = carry + jnp.sum(picked, axis=1, keepdims=True)
    carry_ref[...] = jnp.broadcast_to(new_carry, carry_ref.shape)
    cnt_ref[...] = jnp.broadcast_to(new_carry, cnt_ref.shape)


def _route(logits_t, router_bias, t0):
    E, T = logits_t.shape
    Tm = T - t0
    TK = _pick_tile(int(np.gcd(Tm, t0)), (1024, 512, 256))
    off = t0 // TK
    tri = jnp.asarray(np.triu(np.ones((TK, TK), np.float32), 1), BF16)
    tok = lambda: pl.BlockSpec((8, TK), lambda i: (0, i))
    idx, rank, w, cnt = pl.pallas_call(
        _route_kernel,
        grid=(Tm // TK,),
        in_specs=[pl.BlockSpec((E, TK), lambda i: (0, i + off)),
                  pl.BlockSpec((E, 1), lambda i: (0, 0)),
                  pl.BlockSpec((TK, TK), lambda i: (0, 0))],
        out_specs=[tok(), tok(), tok(), pl.BlockSpec((E, 128), lambda i: (0, 0))],
        out_shape=[jax.ShapeDtypeStruct((8, Tm), I32), jax.ShapeDtypeStruct((8, Tm), I32),
                   jax.ShapeDtypeStruct((8, Tm), F32), jax.ShapeDtypeStruct((E, 128), F32)],
        scratch_shapes=[pltpu.VMEM((E, 128), F32)],
        compiler_params=_cparams(("arbitrary",)),
        name="moe_route",
    )(logits_t, router_bias.reshape(E, 1), tri)
    return idx, rank, w, cnt[:, 0].astype(I32)


def _sorted_tile_rows(n_exp):
    return -(-(MOE_TILE * TOP_K + n_exp * (ROW_ALIGN - 1)) // 16) * 16


def _run_copies(cnt_ref, src_ref, dst_ref, base, n_exp, chunk, make_copy):
    shift = chunk.bit_length() - 1

    def expert_body(e, total):
        n = cnt_ref[base + e]
        src = src_ref[base + e]
        dst = dst_ref[base + e]
        nch = lax.shift_right_logical(n + (chunk - 1), shift)

        def chunk_body(c, carry):
            make_copy(pl.multiple_of(src + c * chunk, ROW_ALIGN), pl.multiple_of(dst + c * chunk, ROW_ALIGN)).start()
            return carry

        lax.fori_loop(0, nch, chunk_body, 0)
        return total + nch

    return lax.fori_loop(0, n_exp, expert_body, jnp.int32(0), unroll=4)


def _dispatch_kernel(cnt_ref, ls_ref, gb_ref, pos_ref, h_ref, xs_hbm, stage, pending, sem, *, n_exp):
    i = pl.program_id(0)
    slot = i % 2
    TMd = h_ref.shape[0]
    R = stage.shape[1] - DISP_CHUNK

    @pl.when(i == 0)
    def _():
        pending[0] = 0
        for s in range(2):
            stage[s, R:, :] = jnp.zeros((stage.shape[1] - R, stage.shape[2]), stage.dtype)

    rio = lax.broadcasted_iota(I32, (R, TMd), 0)
    perm = jnp.zeros((R, TMd), F32)
    for kx in range(TOP_K):
        perm = jnp.where(rio == pos_ref[kx:kx + 1, :], 1.0, perm)
    stage[slot, 0:R, :] = _pack_rows(_dot(perm.astype(BF16), _unpack_rows(h_ref[...])), is_bf16_valued=True)

    def make_copy(src, dst):
        return pltpu.make_async_copy(stage.at[slot, pl.ds(src, DISP_CHUNK)], xs_hbm.at[pl.ds(dst, DISP_CHUNK)], sem)

    def wait_copies(n):
        def wait_body(c, carry):
            make_copy(0, 0).wait()
            return carry
        lax.fori_loop(0, n, wait_body, 0)

    wait_copies(pending[0])
    pending[0] = _run_copies(cnt_ref, ls_ref, gb_ref, i * n_exp, n_exp, DISP_CHUNK, make_copy)

    @pl.when(i == pl.num_programs(0) - 1)
    def _():
        wait_copies(pending[0])


def _dispatch(h2, plan, t0, n_slots, n_exp):
    cnt, ls, gb, pos = plan
    T, DP = h2.shape
    TMd = MOE_TILE
    Tm = T - t0
    assert Tm % TMd == 0 and t0 % TMd == 0
    off = t0 // TMd
    grid_spec = pltpu.PrefetchScalarGridSpec(
        num_scalar_prefetch=3,
        grid=(Tm // TMd,),
        in_specs=[pl.BlockSpec((8, TMd), lambda i, *_: (0, i)),
                  pl.BlockSpec((TMd, DP), lambda i, *_: (i + off, 0))],
        out_specs=pl.BlockSpec(memory_space=pl.ANY),
        scratch_shapes=[pltpu.VMEM((2, _sorted_tile_rows(n_exp) + DISP_CHUNK, DP), h2.dtype),
                        pltpu.SMEM((1,), I32), pltpu.SemaphoreType.DMA],
    )
    return pl.pallas_call(
        functools.partial(_dispatch_kernel, n_exp=n_exp),
        grid_spec=grid_spec,
        out_shape=jax.ShapeDtypeStruct((n_slots, DP), h2.dtype),
        compiler_params=_cparams(("arbitrary",)),
        name="moe_dispatch",
    )(cnt, ls, gb, pos, h2)


def _gmm_kernel(bexp_ref, bvalid_ref, x_ref, wg_ref, wu_ref, wd_ref, y_ref):
    del bexp_ref
    i = pl.program_id(0)
    nvalid = bvalid_ref[i]

    @pl.when(nvalid > 0)
    def _():
        row = lax.broadcasted_iota(I32, x_ref.shape, 0)
        xb = _unpack_rows(jnp.where(row < nvalid, x_ref[...], jnp.uint32(0)))
        act = _silu(_dot(xb, wg_ref[...].astype(BF16))) * _dot(xb, wu_ref[...].astype(BF16))
        y_ref[...] = _pack_rows(_dot(act.astype(BF16), wd_ref[...].astype(BF16)))

    @pl.when(nvalid == 0)
    def _():
        y_ref[...] = jnp.zeros_like(y_ref)


def _grouped_experts(xsorted, block_exp, block_valid, wg, wu, wd, layer):
    NS, DP = xsorted.shape
    nb = NS // MOE_BLOCK
    _, _, D, hid = wg.shape
    grid_spec = pltpu.PrefetchScalarGridSpec(
        num_scalar_prefetch=2,
        grid=(nb,),
        in_specs=[pl.BlockSpec((MOE_BLOCK, DP), lambda i, be, bv: (i, 0)),
                  pl.BlockSpec((None, None, D, hid), lambda i, be, bv: (layer, be[i], 0, 0)),
                  pl.BlockSpec((None, None, D, hid), lambda i, be, bv: (layer, be[i], 0, 0)),
                  pl.BlockSpec((None, None, hid, D), lambda i, be, bv: (layer, be[i], 0, 0))],
        out_specs=pl.BlockSpec((MOE_BLOCK, DP), lambda i, be, bv: (i, 0)),
    )
    return pl.pallas_call(
        _gmm_kernel,
        grid_spec=grid_spec,
        out_shape=jax.ShapeDtypeStruct((NS, DP), jnp.uint32),
        compiler_params=_cparams(("arbitrary",)),
        name="moe_grouped_experts",
    )(block_exp, block_valid, xsorted, wg, wu, wd)


def _combine_kernel(cnt_ref, so_ref, gb_ref, y_hbm, qpos_ref, w_ref, h_ref, sgu_ref, sd_ref, x_ref, m5_ref,
                    lng_ref, lnb_ref, o_ref, stage, pending, sems, *, alpha, n_exp):
    i = pl.program_id(0)
    slot = i % 2

    big_rows = 2 * COMB_CHUNK
    big_shift, small_shift = big_rows.bit_length() - 1, COMB_CHUNK.bit_length() - 1

    def make_copy(s, rows):
        def f(dst, src):
            return pltpu.make_async_copy(y_hbm.at[pl.ds(src, rows)], stage.at[s, pl.ds(dst, rows)], sems.at[s])
        return f

    def fetch(tile, s):
        base = tile * n_exp
        big, small = make_copy(s, big_rows), make_copy(s, COMB_CHUNK)

        def expert_body(e, units):
            n = cnt_ref[base + e]
            dst = so_ref[base + e]
            src = gb_ref[base + e]
            nbig = lax.shift_right_logical(n, big_shift)

            def chunk_body(c, carry):
                big(pl.multiple_of(dst + c * big_rows, ROW_ALIGN), pl.multiple_of(src + c * big_rows, ROW_ALIGN)).start()
                return carry

            lax.fori_loop(0, nbig, chunk_body, 0)
            tail = lax.shift_right_logical(n, small_shift) & 1

            @pl.when(tail == 1)
            def _():
                small(pl.multiple_of(dst + nbig * big_rows, ROW_ALIGN),
                      pl.multiple_of(src + nbig * big_rows, ROW_ALIGN)).start()

            return units + 2 * nbig + tail

        pending[s] = lax.fori_loop(0, n_exp, expert_body, jnp.int32(0), unroll=4)

    @pl.when(i == 0)
    def _():
        stage[...] = jnp.zeros_like(stage)
        fetch(0, 0)

    @pl.when(i + 1 < pl.num_programs(0))
    def _():
        fetch(i + 1, 1 - slot)

    hid = sd_ref.shape[0]
    gu = _dot(_unpack_rows(h_ref[...]), sgu_ref[...])
    y = _dot((_silu(gu[:, :hid]) * gu[:, hid:]).astype(BF16), sd_ref[...])

    units = pending[slot]

    def wait_body(c, carry):
        make_copy(slot, big_rows)(0, 0).wait()
        return carry

    lax.fori_loop(0, lax.shift_right_logical(units, 1), wait_body, 0)

    @pl.when((units & 1) == 1)
    def _():
        make_copy(slot, COMB_CHUNK)(0, 0).wait()

    TMc, S = h_ref.shape[0], stage.shape[1]
    sio = lax.broadcasted_iota(I32, (TMc, S), 1)
    q = jnp.zeros((TMc, S), F32)
    for kx in range(TOP_K):
        q = jnp.where(sio == qpos_ref[:, kx:kx + 1], w_ref[:, kx:kx + 1], q)
    y = y + _dot(q.astype(BF16), _unpack_rows(stage[slot]))
    z = alpha * x_ref[...] + m5_ref[...] * y
    o_ref[...] = _layer_norm(z, lng_ref[...], lnb_ref[...])


def _combine(dm, plan, ysorted, qpos_tm, w_tm, h2, sgu, sd, xs, m5, lng, lnb, alpha, t0, n_exp):
    cnt, so, gb = plan
    T, D = xs.shape
    TMc = MOE_TILE
    Tm = T - t0
    off = t0 // TMc
    const = lambda a: pl.BlockSpec(a.shape, lambda i, *_: (0,) * a.ndim)
    row = lambda a: pl.BlockSpec((TMc, a.shape[1]), lambda i, *_: (i + off, 0))
    grid_spec = pltpu.PrefetchScalarGridSpec(
        num_scalar_prefetch=3,
        grid=(Tm // TMc,),
        in_specs=[pl.BlockSpec(memory_space=pl.ANY),
                  pl.BlockSpec((TMc, 8), lambda i, *_: (i, 0)), pl.BlockSpec((TMc, 8), lambda i, *_: (i, 0)),
                  row(h2), const(sgu), const(sd), row(xs),
                  pl.BlockSpec((None, 1, D), lambda i, *_: (dm.group(i + off, TMc), 0, 0)),
                  const(lng), const(lnb)],
        out_specs=pl.BlockSpec((TMc, D), lambda i, *_: (i, 0)),
        scratch_shapes=[pltpu.VMEM((2, _sorted_tile_rows(n_exp), ysorted.shape[1]), ysorted.dtype),
                        pltpu.SMEM((2,), I32), pltpu.SemaphoreType.DMA((2,))],
    )
    return pl.pallas_call(
        functools.partial(_combine_kernel, alpha=alpha, n_exp=n_exp),
        grid_spec=grid_spec,
        out_shape=jax.ShapeDtypeStruct((Tm, D), F32),
        compiler_params=_cparams(("arbitrary",)),
        name="moe_combine",
    )(cnt, so, gb, ysorted, qpos_tm, w_tm, h2, sgu, sd, xs, m5, lng, lnb)


def _moe(dm, h2, logits_t, xs, m5, lng, lnb, router_bias, wg, wu, wd, layer, sgu, sd, alpha, t0):
    T = h2.shape[0]
    E = wg.shape[1]
    Tm = T - t0
    nt = Tm // MOE_TILE
    idx, rank, w, _ = _route(logits_t, router_bias, t0)
    idx, rank, w = idx[:TOP_K], rank[:TOP_K], w[:TOP_K]
    eids = jnp.arange(E, dtype=I32)
    onehot = (idx[:, :, None] == eids).reshape(TOP_K, nt, MOE_TILE, E)
    cnt = jnp.sum(onehot, axis=(0, 2), dtype=I32)
    rank_base = jnp.cumsum(cnt, axis=0) - cnt
    run = (cnt + ROW_ALIGN - 1) // ROW_ALIGN * ROW_ALIGN
    tile_base = jnp.cumsum(run, axis=0) - run
    local_start = jnp.cumsum(run, axis=1) - run
    counts = jnp.sum(run, axis=0)
    nb = -(-(Tm * TOP_K + E * (nt * (ROW_ALIGN - 1) + MOE_SLACK)) // MOE_BLOCK) + E
    padded = (counts + MOE_SLACK + MOE_BLOCK - 1) // MOE_BLOCK * MOE_BLOCK
    pends = jnp.cumsum(padded)
    pstarts = pends - padded
    global_base = pstarts[None, :] + tile_base
    bstart = jnp.arange(nb, dtype=I32) * MOE_BLOCK
    block_exp = jnp.minimum(jnp.sum((pends[None, :] <= bstart[:, None]).astype(I32), axis=1), E - 1)
    of_block = block_exp[:, None] == eids[None, :]
    seg_end = jnp.sum(jnp.where(of_block, (pstarts + counts)[None, :], 0), axis=1)
    block_valid = jnp.clip(seg_end - bstart, 0, MOE_BLOCK).astype(I32)
    look = lambda tab: jnp.sum(jnp.where(onehot, tab[None, :, None, :], 0), axis=3).reshape(TOP_K, Tm)
    pos = rank - look(rank_base) + look(local_start)
    pos8 = jnp.pad(pos, ((0, 8 - TOP_K), (0, 0)), constant_values=-1).astype(I32)
    pos_tm = jnp.pad(pos.T, ((0, 0), (0, 8 - TOP_K)), constant_values=-1).astype(I32)
    w_tm = jnp.pad(w.T, ((0, 0), (0, 8 - TOP_K)))
    flat = lambda a: a.reshape(-1).astype(I32)
    plan = (flat(run), flat(local_start), flat(global_base))
    xsorted = _dispatch(h2, plan + (pos8,), t0, nb * MOE_BLOCK, E)
    ysorted = _grouped_experts(xsorted, block_exp, block_valid, wg, wu, wd, layer)
    return _combine(dm, plan, ysorted, pos_tm, w_tm, h2, sgu, sd, xs, m5, lng, lnb, alpha, t0, E)


def kernel(x, c, ctx, c_ctx, ada_w, ada_b, ln_g, ln_b, ev_w_in, ev_w_out, na_rpb, hg_lb_raw, hg_norm_g, od_w_in, od_conv_w, od_conv_b, lru_w_a, lru_b_a, lru_w_x, lru_b_x, lru_lam, od_w_out, router_w, router_bias, exp_w_gate, exp_w_up, exp_w_down, sh_w_gate, sh_w_up, sh_w_down):
    B, N, D = x.shape
    C = ctx.shape[1]
    depth = ada_w.shape[0]
    dm = _Dims(B, N, C, D)
    alpha = float((2 * depth) ** 0.25)
    assert B + 1 <= 8

    cvec = jnp.zeros((8, D), F32).at[:B].set(c).at[B].set(c_ctx)
    mod = _modulation(cvec, ada_w, ada_b).reshape(depth, 8, 6, 1, D)

    p_lb = jax.nn.softmax(hg_lb_raw.astype(F32), axis=1)
    hg_lb = jnp.cumsum(p_lb, axis=1) - p_lb[:, :1]

    xs = jnp.concatenate([ctx.reshape(B * C, D), x.reshape(B * N, D)], axis=0)
    for l in range(depth):
        jl = l // 2
        last = l == depth - 1
        m = [mod[l, :, t] for t in range(6)]
        rw_t = router_w[l].T
        rwh, rwl = _split(rw_t)
        lng = ln_g[l][:, None, :]
        lnb = ln_b[l][:, None, :]
        if l % 2 == 0:
            q, k, v, hq, ff, fb, hi, hg = _inproj(dm, xs, m[1], m[0], ev_w_in[jl].astype(BF16), True)
            bias = _na_bias_tables(na_rpb[jl], N // GRID_W)
            na_lat = _neighbourhood_attention(dm, q, k, v, bias)
            na_ctx = _context_attention(dm, q, k, v)
            o_f, o_b = _hgrn2(dm, hq, ff, fb, hi, hg_lb[:, jl])
            ng = jnp.tile(hg_norm_g[jl], hq.shape[1] // HEAD_DIM)[None, :]
            mixer_inputs = (na_ctx, na_lat, o_f, o_b, hg, ng)
            w_out = ev_w_out[jl].astype(BF16)
        else:
            y, u = _inproj(dm, xs, m[1], m[0], od_w_in[jl].astype(BF16), False)
            h_f, h_b = _rglru(dm, u, od_conv_w[jl], od_conv_b[jl], lru_w_a[jl], lru_b_a[jl],
                              lru_w_x[jl], lru_b_x[jl], lru_lam[jl])
            mixer_inputs = (h_f, h_b, y)
            w_out = od_w_out[jl].astype(BF16)
        xs, h2, logits_t = _post_mixer(dm, mixer_inputs, w_out, xs, m[2], m[3], m[4], lng[0], lnb[0],
                                       rwh, rwl, alpha, l % 2 == 0)
        sgu = jnp.concatenate([sh_w_gate[l], sh_w_up[l]], axis=-1).astype(BF16)
        sd = sh_w_down[l].astype(BF16)
        t0 = dm.T_ctx if last else 0
        xs = _moe(dm, h2, logits_t, xs, m[5], lng[1], lnb[1], router_bias[l], exp_w_gate, exp_w_up,
                  exp_w_down, l, sgu, sd, alpha, t0)
    return xs.reshape(B, N, D)
```
